```python
import jax, jax.numpy as jnp
from jax import lax
import numpy as np

D_MODEL = 1024
BATCH = 8
SEQ = 2048
DEPTH = 1
DEC_BATCH = 128
DEC_SEQ = 8
PAST_LEN = 16384
PAGE_SIZE = 128

HEAD_DIM = 64
D_RWKV = D_MODEL // 2
N_RWKV_HEADS = D_RWKV // HEAD_DIM
D_CONV = D_MODEL // 4
CONV_WIDTH = 31
D_XATTN = D_MODEL // 4
N_XATTN_HEADS = 4
XATTN_HEAD_DIM = D_XATTN // N_XATTN_HEADS
N_MEM = 256
LORA_DECAY = 64
LORA_A = 64
LORA_G = 160
D_SHIFT = 3 * D_RWKV + LORA_DECAY + LORA_A + LORA_G
N_BRANCH = 3
D_IN = D_SHIFT + 2 * D_CONV + D_XATTN + N_BRANCH * D_MODEL
D_FF = 2816
RMS_EPS = 1e-6
LN_EPS = 1e-5
GN_EPS = 64e-5

kernel_name = 'hybrid_rwkv7_conformer_conv_memxattn_macaron_step'


def _rmsnorm(x, g):
    xf = x.astype(jnp.float32)
    return xf * lax.rsqrt(jnp.mean(xf * xf, axis=-1, keepdims=True) + RMS_EPS) * g


def _swiglu(x, w_up, w_down):
    hg = x @ w_up
    return (jax.nn.silu(hg[..., :D_FF]) * hg[..., D_FF:]) @ w_down


def _memory_kv(mem, w_mem_kv):
    b = mem.shape[0]
    kv = mem.astype(jnp.float32) @ w_mem_kv
    k = kv[..., :D_XATTN].reshape(b, N_MEM, N_XATTN_HEADS, XATTN_HEAD_DIM)
    v = kv[..., D_XATTN:].reshape(b, N_MEM, N_XATTN_HEADS, XATTN_HEAD_DIM)
    return k, v


def _wkv7_scan(s0, r, w, k, v, kk, kka):
    def step(s, inp):
        r_t, w_t, k_t, v_t, kk_t, kka_t = inp
        sa = jnp.einsum('bhij,bhj->bhi', s, -kk_t)
        s = (s * w_t[:, :, None, :] + sa[..., None] * kka_t[:, :, None, :]
             + v_t[..., None] * k_t[:, :, None, :])
        return s, jnp.einsum('bhij,bhj->bhi', s, r_t)
    xs = tuple(jnp.moveaxis(a, 1, 0) for a in (r, w, k, v, kk, kka))
    s, o = lax.scan(step, s0, xs)
    return s, jnp.moveaxis(o, 0, 1)


def _token_mixing(h, shift_prev, conv_prev, wkv0, mem_k, mem_v, p):
    f32 = jnp.float32
    b, t, _ = h.shape
    z = h @ p['w_in']
    o1 = D_SHIFT
    o2 = o1 + 2 * D_CONV
    o3 = o2 + D_XATTN
    zs, zc, zq, zg = z[..., :o1], z[..., o1:o2], z[..., o2:o3], z[..., o3:]

    prev = jnp.concatenate([shift_prev[:, None, :].astype(f32), zs[:, :-1]], axis=1)
    xm = zs + (prev - zs) * p['mu_shift']
    c1, c2, c3 = D_RWKV, 2 * D_RWKV, 3 * D_RWKV
    c4 = c3 + LORA_DECAY
    c5 = c4 + LORA_A
    r, k, v = xm[..., :c1], xm[..., c1:c2], xm[..., c2:c3]
    wd, ad, gd = xm[..., c3:c4], xm[..., c4:c5], xm[..., c5:]
    w_log = -jax.nn.softplus(-(p['w0'] + jnp.tanh(wd) @ p['w_decay_up'])) - 0.5
    decay = jnp.exp(-jnp.exp(w_log))
    a = jax.nn.sigmoid(p['a0'] + ad @ p['w_a_up'])
    g = jax.nn.sigmoid(gd) @ p['w_g_up']

    def hd(u):
        return u.reshape(b, t, N_RWKV_HEADS, HEAD_DIM)

    kk = hd(k * p['k_k'])
    kk = kk / jnp.maximum(jnp.sqrt(jnp.sum(kk * kk, axis=-1, keepdims=True)), 1e-12)
    a_h = hd(a)
    k_h = hd(k * (1.0 + (a - 1.0) * p['k_a']))
    r_h, v_h = hd(r), hd(v)
    wkv_new, o = _wkv7_scan(wkv0.astype(f32), r_h, hd(decay), k_h, v_h, kk, kk * a_h)
    mean = jnp.mean(o, axis=-1, keepdims=True)
    var = jnp.mean(jnp.square(o - mean), axis=-1, keepdims=True)
    gn_g = p['gn_g'].reshape(N_RWKV_HEADS, HEAD_DIM)
    gn_b = p['gn_b'].reshape(N_RWKV_HEADS, HEAD_DIM)
    o = (o - mean) * lax.rsqrt(var + GN_EPS) * gn_g + gn_b
    o = o + jnp.sum(r_h * k_h * p['r_k'], axis=-1, keepdims=True) * v_h
    y_a = (o.reshape(b, t, D_RWKV) * g) @ p['w_rwkv_out']

    zc = zc + p['glu_b']
    u = zc[..., :D_CONV] * jax.nn.sigmoid(zc[..., D_CONV:])
    u_ext = jnp.concatenate([conv_prev.astype(f32), u], axis=1)
    c = lax.conv_general_dilated(u_ext, p['conv_w'][:, None, :], (1,), 'VALID',
                                 dimension_numbers=('NWC', 'WIO', 'NWC'),
                                 feature_group_count=D_CONV) + p['conv_b']
    conv_new = u_ext[:, u_ext.shape[1] - (CONV_WIDTH - 1):]
    cm = jnp.mean(c, axis=-1, keepdims=True)
    cv = jnp.mean(jnp.square(c - cm), axis=-1, keepdims=True)
    c = (c - cm) * lax.rsqrt(cv + LN_EPS) * p['conv_ln_g'] + p['conv_ln_b']
    y_b = jax.nn.silu(c) @ p['w_conv_out']

    q = zq.reshape(b, t, N_XATTN_HEADS, XATTN_HEAD_DIM)
    s = jnp.einsum('btnd,bmnd->bntm', q, mem_k.astype(f32)) * (XATTN_HEAD_DIM ** -0.5)
    pr = jax.nn.softmax(s, axis=-1)
    o_c = jnp.einsum('bntm,bmnd->btnd', pr, mem_v.astype(f32)).reshape(b, t, D_XATTN)
    y_c = o_c @ p['w_xattn_out']

    gates = jax.nn.sigmoid(zg).reshape(b, t, N_BRANCH, D_MODEL)
    merged = gates[:, :, 0] * y_a + gates[:, :, 1] * y_b + gates[:, :, 2] * y_c
    return merged @ p['w_o'], wkv_new, zs[:, -1], conv_new


def _layer(x, shift_prev, conv_prev, wkv0, mem_k, mem_v, p):
    x = x.astype(jnp.float32)
    x = x + 0.5 * _swiglu(_rmsnorm(x, p['ffn1_norm']), p['ffn1_w_up'], p['ffn1_w_down'])
    m, wkv, shift, conv = _token_mixing(_rmsnorm(x, p['mix_norm']), shift_prev, conv_prev,
                                        wkv0, mem_k, mem_v, p)
    x = x + m
    x = x + 0.5 * _swiglu(_rmsnorm(x, p['ffn2_norm']), p['ffn2_w_up'], p['ffn2_w_down'])
    return x, wkv, shift, conv


def setup_inputs(seed: int = 0) -> dict:
    key = jax.random.key(seed)
    ks = iter(jax.random.split(key, 48))
    L = DEPTH

    def nrm(shape, scale):
        return jax.random.normal(next(ks), shape, jnp.float32) * scale

    return {
        'x_prompt': nrm((BATCH, SEQ, D_MODEL), 1.0),
        'mem_prompt': nrm((BATCH, N_MEM, D_MODEL), 1.0),
        'x_sample': nrm((DEC_BATCH, DEC_SEQ, D_MODEL), 1.0),
        'state_wkv': nrm((L, DEC_BATCH, N_RWKV_HEADS, HEAD_DIM, HEAD_DIM), 0.3),
        'state_shift': nrm((L, DEC_BATCH, D_SHIFT), 1.0),
        'state_conv': nrm((L, DEC_BATCH, CONV_WIDTH - 1, D_CONV), 0.5),
        'cache_mem_k': nrm((L, DEC_BATCH, N_MEM, N_XATTN_HEADS, XATTN_HEAD_DIM), 1.0),
        'cache_mem_v': nrm((L, DEC_BATCH, N_MEM, N_XATTN_HEADS, XATTN_HEAD_DIM), 1.0),
        'ffn1_norm': 1.0 + nrm((L, D_MODEL), 0.02),
        'ffn1_w_up': nrm((L, D_MODEL, 2 * D_FF), D_MODEL ** -0.5),
        'ffn1_w_down': nrm((L, D_FF, D_MODEL), D_FF ** -0.5),
        'mix_norm': 1.0 + nrm((L, D_MODEL), 0.02),
        'w_in': nrm((L, D_MODEL, D_IN), D_MODEL ** -0.5),
        'mu_shift': jax.random.uniform(next(ks), (L, D_SHIFT), jnp.float32),
        'w0': jax.random.uniform(next(ks), (L, D_RWKV), jnp.float32, minval=-4.0, maxval=0.0),
        'w_decay_up': nrm((L, LORA_DECAY, D_RWKV), 0.1),
        'a0': nrm((L, D_RWKV), 0.1),
        'w_a_up': nrm((L, LORA_A, D_RWKV), 0.1),
        'w_g_up': nrm((L, LORA_G, D_RWKV), LORA_G ** -0.5),
        'k_k': 0.85 + nrm((L, D_RWKV), 0.02),
        'k_a': 1.0 + nrm((L, D_RWKV), 0.02),
        'r_k': nrm((L, N_RWKV_HEADS, HEAD_DIM), 0.1),
        'gn_g': 1.0 + nrm((L, D_RWKV), 0.02),
        'gn_b': nrm((L, D_RWKV), 0.02),
        'w_rwkv_out': nrm((L, D_RWKV, D_MODEL), D_RWKV ** -0.5),
        'glu_b': nrm((L, 2 * D_CONV), 0.02),
        'conv_w': nrm((L, CONV_WIDTH, D_CONV), CONV_WIDTH ** -0.5),
        'conv_b': nrm((L, D_CONV), 0.02),
        'conv_ln_g': 1.0 + nrm((L, D_CONV), 0.02),
        'conv_ln_b': nrm((L, D_CONV), 0.02),
        'w_conv_out': nrm((L, D_CONV, D_MODEL), D_CONV ** -0.5),
        'w_mem_kv': nrm((L, D_MODEL, 2 * D_XATTN), D_MODEL ** -0.5),
        'w_xattn_out': nrm((L, D_XATTN, D_MODEL), D_XATTN ** -0.5),
        'w_o': nrm((L, D_MODEL, D_MODEL), D_MODEL ** -0.5),
        'ffn2_norm': 1.0 + nrm((L, D_MODEL), 0.02),
        'ffn2_w_up': nrm((L, D_MODEL, 2 * D_FF), D_MODEL ** -0.5),
        'ffn2_w_down': nrm((L, D_FF, D_MODEL), D_FF ** -0.5),
        'final_norm': 1.0 + nrm((D_MODEL,), 0.02),
    }


def reference(x_prompt, mem_prompt, x_sample, state_wkv, state_shift, state_conv,
              cache_mem_k, cache_mem_v, ffn1_norm, ffn1_w_up, ffn1_w_down, mix_norm, w_in,
              mu_shift, w0, w_decay_up, a0, w_a_up, w_g_up, k_k, k_a, r_k, gn_g, gn_b,
              w_rwkv_out, glu_b, conv_w, conv_b, conv_ln_g, conv_ln_b, w_conv_out, w_mem_kv,
              w_xattn_out, w_o, ffn2_norm, ffn2_w_up, ffn2_w_down, final_norm):
    f32 = jnp.float32
    layers = []
    for l in range(DEPTH):
        layers.append({
            'ffn1_norm': ffn1_norm[l].astype(f32), 'ffn1_w_up': ffn1_w_up[l].astype(f32),
            'ffn1_w_down': ffn1_w_down[l].astype(f32), 'mix_norm': mix_norm[l].astype(f32),
            'w_in': w_in[l].astype(f32), 'mu_shift': mu_shift[l].astype(f32),
            'w0': w0[l].astype(f32), 'w_decay_up': w_decay_up[l].astype(f32),
            'a0': a0[l].astype(f32), 'w_a_up': w_a_up[l].astype(f32),
            'w_g_up': w_g_up[l].astype(f32), 'k_k': k_k[l].astype(f32),
            'k_a': k_a[l].astype(f32), 'r_k': r_k[l].astype(f32),
            'gn_g': gn_g[l].astype(f32), 'gn_b': gn_b[l].astype(f32),
            'w_rwkv_out': w_rwkv_out[l].astype(f32), 'glu_b': glu_b[l].astype(f32),
            'conv_w': conv_w[l].astype(f32), 'conv_b': conv_b[l].astype(f32),
            'conv_ln_g': conv_ln_g[l].astype(f32), 'conv_ln_b': conv_ln_b[l].astype(f32),
            'w_conv_out': w_conv_out[l].astype(f32), 'w_mem_kv': w_mem_kv[l].astype(f32),
            'w_xattn_out': w_xattn_out[l].astype(f32), 'w_o': w_o[l].astype(f32),
            'ffn2_norm': ffn2_norm[l].astype(f32), 'ffn2_w_up': ffn2_w_up[l].astype(f32),
            'ffn2_w_down': ffn2_w_down[l].astype(f32),
        })
    g_final = final_norm.astype(f32)

    xp = x_prompt
    wkv_p, shift_p, conv_p, mk_p, mv_p = [], [], [], [], []
    for l in range(DEPTH):
        p = layers[l]
        mk, mv = _memory_kv(mem_prompt, p['w_mem_kv'])
        xp, wkv, sh, cv = _layer(
            xp, jnp.zeros((BATCH, D_SHIFT), f32), jnp.zeros((BATCH, CONV_WIDTH - 1, D_CONV), f32),
            jnp.zeros((BATCH, N_RWKV_HEADS, HEAD_DIM, HEAD_DIM), f32), mk, mv, p)
        wkv_p.append(wkv)
        shift_p.append(sh)
        conv_p.append(cv)
        mk_p.append(mk)
        mv_p.append(mv)
    y_prompt = _rmsnorm(xp, g_final).astype(x_prompt.dtype)

    xs = x_sample
    wkv_s, shift_s, conv_s = [], [], []
    for l in range(DEPTH):
        xs, wkv, sh, cv = _layer(xs, state_shift[l], state_conv[l], state_wkv[l],
                                 cache_mem_k[l], cache_mem_v[l], layers[l])
        wkv_s.append(wkv)
        shift_s.append(sh)
        conv_s.append(cv)
    y_sample = _rmsnorm(xs, g_final).astype(x_sample.dtype)

    return (y_prompt, y_sample, jnp.stack(wkv_p), jnp.stack(shift_p), jnp.stack(conv_p),
            jnp.stack(mk_p), jnp.stack(mv_p), jnp.stack(wkv_s), jnp.stack(shift_s),
            jnp.stack(conv_s))
```

```python
import functools
import math

import jax
import jax.numpy as jnp
from jax import lax
from jax.experimental import pallas as pl
from jax.experimental.pallas import tpu as pltpu

F32 = jnp.float32
BF16 = jnp.bfloat16
HI = lax.Precision.HIGHEST

HEAD_DIM = 64
N_XATTN_HEADS = 4
CONV_WIDTH = 31
CONV_PAD = 32
LORA_DECAY = 64
LORA_A = 64
RMS_EPS = 1e-6
LN_EPS = 1e-5
GN_EPS = 64e-5
VMEM_LIMIT = 56 * 1024 * 1024


def _params(*sem):
    return pltpu.CompilerParams(dimension_semantics=sem, vmem_limit_bytes=VMEM_LIMIT)


def _rms(x, g):
    return x * lax.rsqrt(jnp.mean(x * x, axis=-1, keepdims=True) + RMS_EPS) * g


def _mm(a, b, precision=None):
    return jnp.dot(a, b, preferred_element_type=F32, precision=precision)


def _nt(a, b, precision=None):
    return lax.dot_general(a, b, (((1,), (1,)), ((), ())), preferred_element_type=F32, precision=precision)


def _tn(a, b, precision=None):
    return lax.dot_general(a, b, (((0,), (0,)), ((), ())), preferred_element_type=F32, precision=precision)


def _const_spec(shape):
    nd = len(shape)
    return pl.BlockSpec(shape, lambda *_: (0,) * nd)


def _ffn_body(x_ref, g_ref, wg_ref, wu_ref, wd_ref, fg_ref, o_ref, xn_ref, acc_ref, *, final_norm):
    c = pl.program_id(1)

    @pl.when(c == 0)
    def _():
        xn_ref[...] = _rms(x_ref[...], g_ref[...]).astype(BF16)
        acc_ref[...] = jnp.zeros_like(acc_ref)

    xn = xn_ref[...]
    hg = _mm(xn, wg_ref[...])
    hu = _mm(xn, wu_ref[...])
    h = (hg * jax.nn.sigmoid(hg) * hu).astype(BF16)
    acc_ref[...] += _mm(h, wd_ref[...])

    @pl.when(c == pl.num_programs(1) - 1)
    def _():
        y = x_ref[...] + 0.5 * acc_ref[...]
        if final_norm:
            y = _rms(y, fg_ref[...])
        o_ref[...] = y


def _ffn(x, norm_g, w_up, w_down, final_g, *, final_norm, tm=512, fk=1408):
    n, d = x.shape
    tm = min(tm, n)
    d_ff = w_down.shape[0]
    nc = d_ff // fk
    assert n % tm == 0 and d_ff % fk == 0
    return pl.pallas_call(
        functools.partial(_ffn_body, final_norm=final_norm),
        grid=(n // tm, nc),
        in_specs=[
            pl.BlockSpec((tm, d), lambda i, c: (i, 0)),
            _const_spec((1, d)),
            pl.BlockSpec((d, fk), lambda i, c: (0, c)),
            pl.BlockSpec((d, fk), lambda i, c: (0, c + nc)),
            pl.BlockSpec((fk, d), lambda i, c: (c, 0)),
            _const_spec((1, d)),
        ],
        out_specs=pl.BlockSpec((tm, d), lambda i, c: (i, 0)),
        out_shape=jax.ShapeDtypeStruct((n, d), F32),
        scratch_shapes=[pltpu.VMEM((tm, d), BF16), pltpu.VMEM((tm, d), F32)],
        compiler_params=_params("parallel", "arbitrary"),
        name="ffn_final" if final_norm else "ffn",
    )(x, norm_g, w_up, w_up, w_down, final_g)


def _memkv_body(m_ref, w_ref, k_ref, v_ref):
    kv = _mm(m_ref[...].astype(BF16), w_ref[...])
    dk = k_ref.shape[-1]
    k_ref[...] = kv[:, :dk]
    v_ref[...] = kv[:, dk:]


def _memkv(mem, w, *, tm=512):
    n, d = mem.shape
    dk = w.shape[1] // 2
    return pl.pallas_call(
        _memkv_body,
        grid=(n // tm,),
        in_specs=[pl.BlockSpec((tm, d), lambda i: (i, 0)), _const_spec(w.shape)],
        out_specs=[pl.BlockSpec((tm, dk), lambda i: (i, 0))] * 2,
        out_shape=[jax.ShapeDtypeStruct((n, dk), F32)] * 2,
        compiler_params=_params("parallel"),
        name="memkv",
    )(mem, w)


def _wkv_head_chunk(r, kh, v, ld, kkraw, a, rk, gng, gnb, s0, consts):
    strict, incl, tri, eye, n_sq = consts
    chunk = r.shape[0]
    ss = jnp.sum(kkraw * kkraw, axis=-1, keepdims=True)
    kk = kkraw / jnp.maximum(jnp.sqrt(ss), 1e-12)
    b = kk * a
    cum = _mm(tri, ld, HI)
    tot = cum[chunk - 1:chunk, :]
    w_inv = jnp.exp(-cum)
    w_rem = jnp.exp(tot - cum)
    at = -kk * jnp.exp(cum - ld)
    rt = r * jnp.exp(cum)
    bb = b * w_inv
    kb = kh * w_inv
    a_ab = jnp.where(strict, _nt(at, bb, HI), 0.0)
    a_ak = jnp.where(strict, _nt(at, kb, HI), 0.0)
    a_rb = jnp.where(incl, _nt(rt, bb, HI), 0.0)
    a_rk = jnp.where(incl, _nt(rt, kb, HI), 0.0)
    tinv = eye + a_ab
    ap = a_ab
    for _ in range(n_sq):
        ap = _mm(ap, ap, HI)
        tinv = tinv + _mm(tinv, ap, HI)
    p = _mm(tinv, _mm(a_ak, v, HI), HI)
    q = _mm(tinv, at, HI)
    u = p + _nt(q, s0, HI)
    o = _nt(rt, s0, HI) + _mm(a_rb, u, HI) + _mm(a_rk, v, HI)
    s_new = s0 * jnp.exp(tot) + _tn(u, b * w_rem, HI) + _tn(v, kh * w_rem, HI)
    mean = jnp.mean(o, axis=-1, keepdims=True)
    oc = o - mean
    var = jnp.mean(oc * oc, axis=-1, keepdims=True)
    o = oc * lax.rsqrt(var + GN_EPS) * gng + gnb
    o = o + jnp.sum(r * kh * rk, axis=-1, keepdims=True) * v
    return o, s_new


def _rwkv_body(x_ref, mixg_ref, ws_ref, sprev_ref, mu_ref, w0_ref, wdu_ref, a0_ref, wau_ref, wgu_ref,
               kk_ref, ka_ref, rk_ref, gng_ref, gnb_ref, s0_ref,
               og_ref, sout_ref, wkv_ref,
               carry, r_s, kh_s, v_s, ld_s, kk_s, a_s, g_s, o_s, *, chunk):
    bb_n, tt, d = x_ref.shape
    m = bb_n * tt
    d_shift = ws_ref.shape[1]
    d_r = w0_ref.shape[1]
    n_heads = d_r // HEAD_DIM
    t = pl.program_id(1)

    @pl.when(t == 0)
    def _():
        carry[...] = sprev_ref[...]
        wkv_ref[...] = s0_ref[...]

    xn = _rms(x_ref[...].reshape(m, d), mixg_ref[...]).astype(BF16)
    zs = _mm(xn, ws_ref[...])
    rolled = pltpu.roll(zs, 1, axis=0)
    prev0 = jnp.broadcast_to(carry[...], (bb_n, tt, d_shift)).reshape(m, d_shift)
    row = lax.broadcasted_iota(jnp.int32, (m, d_shift), 0)
    prev = jnp.where(row % tt == 0, prev0, rolled)
    last = zs.reshape(bb_n, tt, d_shift)[:, tt - 1:tt, :]
    carry[...] = last
    sout_ref[...] = last
    xm = zs + (prev - zs) * mu_ref[...]

    c1, c2, c3 = d_r, 2 * d_r, 3 * d_r
    c4 = c3 + LORA_DECAY
    c5 = c4 + LORA_A
    r, k, v = xm[:, :c1], xm[:, c1:c2], xm[:, c2:c3]
    wd, ad, gd = xm[:, c3:c4], xm[:, c4:c5], xm[:, c5:]
    dec_in = w0_ref[...] + _mm(jnp.tanh(wd), wdu_ref[...], HI)
    neg = -dec_in
    softplus = jnp.maximum(neg, 0.0) + jnp.log(1.0 + jnp.exp(-jnp.abs(neg)))
    ld = -jnp.exp(-softplus - 0.5)
    a = jax.nn.sigmoid(a0_ref[...] + _mm(ad, wau_ref[...], HI))
    g_s[...] = _mm(jax.nn.sigmoid(gd), wgu_ref[...], HI)
    kkraw = k * kk_ref[...]
    kh = k * (1.0 + (a - 1.0) * ka_ref[...])
    for h in range(n_heads):
        sl = slice(h * HEAD_DIM, (h + 1) * HEAD_DIM)
        r_s[h] = r[:, sl]
        kh_s[h] = kh[:, sl]
        v_s[h] = v[:, sl]
        ld_s[h] = ld[:, sl]
        kk_s[h] = kkraw[:, sl]
        a_s[h] = a[:, sl]

    ti = lax.broadcasted_iota(jnp.int32, (chunk, chunk), 0)
    si = lax.broadcasted_iota(jnp.int32, (chunk, chunk), 1)
    strict = si < ti
    incl = si <= ti
    consts = (strict, incl, incl.astype(F32), (si == ti).astype(F32), int(math.log2(chunk)) - 1)
    rk_all, gng_all, gnb_all = rk_ref[...], gng_ref[...], gnb_ref[...]
    n_ck = tt // chunk

    def chunk_step(ci, carry_):
        row0 = pl.multiple_of(ci * chunk, chunk)
        b_i = ci // n_ck
        for h in range(n_heads):
            sl = slice(h * HEAD_DIM, (h + 1) * HEAD_DIM)
            rows = pl.ds(row0, chunk)
            o, s_new = _wkv_head_chunk(
                r_s[h, rows, :], kh_s[h, rows, :], v_s[h, rows, :], ld_s[h, rows, :], kk_s[h, rows, :],
                a_s[h, rows, :], rk_all[:, sl], gng_all[:, sl], gnb_all[:, sl], wkv_ref[b_i, h], consts)
            wkv_ref[b_i, h] = s_new
            o_s[h, rows, :] = o
        return carry_

    lax.fori_loop(0, bb_n * n_ck, chunk_step, 0)
    o_all = jnp.concatenate([o_s[h] for h in range(n_heads)], axis=-1)
    og_ref[...] = (o_all * g_s[...]).astype(BF16).reshape(bb_n, tt, d_r)


def _rwkv(x, mixg, ws, sprev, mu, w0, wdu, a0, wau, wgu, kk, ka, rk, gng, gnb, s0, *, bb, tt, chunk):
    b, t_len, d = x.shape
    d_shift = ws.shape[1]
    d_r = w0.shape[1]
    n_heads = d_r // HEAD_DIM
    m = bb * tt
    assert b % bb == 0 and t_len % tt == 0 and tt % chunk == 0 and chunk & (chunk - 1) == 0
    head_scr = pltpu.VMEM((n_heads, m, HEAD_DIM), F32)
    return pl.pallas_call(
        functools.partial(_rwkv_body, chunk=chunk),
        grid=(b // bb, t_len // tt),
        in_specs=[
            pl.BlockSpec((bb, tt, d), lambda i, j: (i, j, 0)),
            _const_spec(mixg.shape), _const_spec(ws.shape),
            pl.BlockSpec((bb, 1, d_shift), lambda i, j: (i, 0, 0)),
            _const_spec(mu.shape), _const_spec(w0.shape), _const_spec(wdu.shape), _const_spec(a0.shape),
            _const_spec(wau.shape), _const_spec(wgu.shape), _const_spec(kk.shape), _const_spec(ka.shape),
            _const_spec(rk.shape), _const_spec(gng.shape), _const_spec(gnb.shape),
            pl.BlockSpec((bb, n_heads, HEAD_DIM, HEAD_DIM), lambda i, j: (i, 0, 0, 0)),
        ],
        out_specs=[
            pl.BlockSpec((bb, tt, d_r), lambda i, j: (i, j, 0)),
            pl.BlockSpec((bb, 1, d_shift), lambda i, j: (i, 0, 0)),
            pl.BlockSpec((bb, n_heads, HEAD_DIM, HEAD_DIM), lambda i, j: (i, 0, 0, 0)),
        ],
        out_shape=[
            jax.ShapeDtypeStruct((b, t_len, d_r), BF16),
            jax.ShapeDtypeStruct((b, 1, d_shift), F32),
            jax.ShapeDtypeStruct((b, n_heads, HEAD_DIM, HEAD_DIM), F32),
        ],
        scratch_shapes=[pltpu.VMEM((bb, 1, d_shift), F32)] + [head_scr] * 6
        + [pltpu.VMEM((m, d_r), F32), head_scr],
        compiler_params=_params("parallel", "arbitrary"),
        name="rwkv",
    )(x, mixg, ws, sprev, mu, w0, wdu, a0, wau, wgu, kk, ka, rk, gng, gnb, s0)


def _conv_body(x_ref, mixg_ref, wc_ref, glub_ref, cprev_ref, cw_ref, cb_ref, lng_ref, lnb_ref,
               y_ref, cnew_ref, ubuf):
    bb_n, tt, d = x_ref.shape
    m = bb_n * tt
    dc = cw_ref.shape[1]
    lo = CONV_PAD - (CONV_WIDTH - 1)
    t = pl.program_id(1)

    @pl.when(t == 0)
    def _():
        ubuf[:, lo:CONV_PAD, :] = cprev_ref[...]

    xn = _rms(x_ref[...].reshape(m, d), mixg_ref[...]).astype(BF16)
    zc = _mm(xn, wc_ref[...]) + glub_ref[...]
    u = zc[:, :dc] * jax.nn.sigmoid(zc[:, dc:])
    ubuf[:, CONV_PAD:CONV_PAD + tt, :] = u.reshape(bb_n, tt, dc)
    acc = jnp.zeros((bb_n, tt, dc), F32) + cb_ref[...]
    for kx in range(CONV_WIDTH):
        acc = acc + ubuf[:, lo + kx:lo + kx + tt, :] * cw_ref[kx:kx + 1, :]
    tail = ubuf[:, lo + tt:CONV_PAD + tt, :]
    cnew_ref[...] = tail
    ubuf[:, lo:CONV_PAD, :] = tail
    c = acc.reshape(m, dc)
    cm = jnp.mean(c, axis=-1, keepdims=True)
    cc = c - cm
    cv = jnp.mean(cc * cc, axis=-1, keepdims=True)
    c = cc * lax.rsqrt(cv + LN_EPS) * lng_ref[...] + lnb_ref[...]
    y_ref[...] = (c * jax.nn.sigmoid(c)).astype(BF16).reshape(bb_n, tt, dc)


def _conv(x, mixg, wc, glub, cprev, cw, cb, lng, lnb, *, bb, tt):
    b, t_len, d = x.shape
    dc = cw.shape[1]
    return pl.pallas_call(
        _conv_body,
        grid=(b // bb, t_len // tt),
        in_specs=[
            pl.BlockSpec((bb, tt, d), lambda i, j: (i, j, 0)),
            _const_spec(mixg.shape), _const_spec(wc.shape), _const_spec(glub.shape),
            pl.BlockSpec((bb, CONV_WIDTH - 1, dc), lambda i, j: (i, 0, 0)),
            _const_spec(cw.shape), _const_spec(cb.shape), _const_spec(lng.shape), _const_spec(lnb.shape),
        ],
        out_specs=[
            pl.BlockSpec((bb, tt, dc), lambda i, j: (i, j, 0)),
            pl.BlockSpec((bb, CONV_WIDTH - 1, dc), lambda i, j: (i, 0, 0)),
        ],
        out_shape=[
            jax.ShapeDtypeStruct((b, t_len, dc), BF16),
            jax.ShapeDtypeStruct((b, CONV_WIDTH - 1, dc), F32),
        ],
        scratch_shapes=[pltpu.VMEM((bb, CONV_PAD + tt, dc), F32)],
        compiler_params=_params("parallel", "arbitrary"),
        name="conv",
    )(x, mixg, wc, glub, cprev, cw, cb, lng, lnb)


def _xattn_body(x_ref, mixg_ref, wq_ref, mk_ref, mv_ref, o_ref):
    bb_n, tt, d = x_ref.shape
    m = bb_n * tt
    dh = mk_ref.shape[-1] // N_XATTN_HEADS
    xn = _rms(x_ref[...].reshape(m, d), mixg_ref[...]).astype(BF16)
    q = (_mm(xn, wq_ref[...]) * (dh ** -0.5)).astype(BF16)
    for b_i in range(bb_n):
        outs = []
        for h in range(N_XATTN_HEADS):
            sl = slice(h * dh, (h + 1) * dh)
            s = _nt(q[b_i * tt:(b_i + 1) * tt, sl], mk_ref[b_i, :, sl].astype(BF16))
            p = jnp.exp(s - jnp.max(s, axis=-1, keepdims=True))
            l = jnp.sum(p, axis=-1, keepdims=True)
            outs.append(_mm(p.astype(BF16), mv_ref[b_i, :, sl].astype(BF16)) / l)
        o_ref[b_i] = jnp.concatenate(outs, axis=-1).astype(BF16)


def _xattn(x, mixg, wq, mk, mv, *, bb, tt):
    b, t_len, d = x.shape
    n_mem, dx = mk.shape[1:]
    return pl.pallas_call(
        _xattn_body,
        grid=(b // bb, t_len // tt),
        in_specs=[
            pl.BlockSpec((bb, tt, d), lambda i, j: (i, j, 0)),
            _const_spec(mixg.shape), _const_spec(wq.shape),
            pl.BlockSpec((bb, n_mem, dx), lambda i, j: (i, 0, 0)),
            pl.BlockSpec((bb, n_mem, dx), lambda i, j: (i, 0, 0)),
        ],
        out_specs=pl.BlockSpec((bb, tt, dx), lambda i, j: (i, j, 0)),
        out_shape=jax.ShapeDtypeStruct((b, t_len, dx), BF16),
        compiler_params=_params("parallel", "arbitrary"),
        name="xattn",
    )(x, mixg, wq, mk, mv)


def _merge_body(x_ref, mixg_ref, wg_ref, ya_ref, yb_ref, yc_ref, wro_ref, wco_ref, wxo_ref, wo_ref, o_ref):
    d = x_ref.shape[1]
    x = x_ref[...]
    xn = _rms(x, mixg_ref[...]).astype(BF16)
    merged = jnp.zeros_like(x)
    for i, (y_ref, w_ref) in enumerate(((ya_ref, wro_ref), (yb_ref, wco_ref), (yc_ref, wxo_ref))):
        gate = jax.nn.sigmoid(_mm(xn, wg_ref[:, i * d:(i + 1) * d]))
        merged = merged + gate * _mm(y_ref[...], w_ref[...])
    o_ref[...] = x + _mm(merged.astype(BF16), wo_ref[...])


def _merge(x, mixg, wg, ya, yb, yc, wro, wco, wxo, wo, *, tm=512):
    n, d = x.shape
    tm = min(tm, n)
    row = lambda w: pl.BlockSpec((tm, w), lambda i: (i, 0))
    return pl.pallas_call(
        _merge_body,
        grid=(n // tm,),
        in_specs=[row(d), _const_spec(mixg.shape), _const_spec(wg.shape), row(ya.shape[1]), row(yb.shape[1]),
                  row(yc.shape[1]), _const_spec(wro.shape), _const_spec(wco.shape), _const_spec(wxo.shape),
                  _const_spec(wo.shape)],
        out_specs=row(d),
        out_shape=jax.ShapeDtypeStruct((n, d), F32),
        compiler_params=_params("parallel"),
        name="merge",
    )(x, mixg, wg, ya, yb, yc, wro, wco, wxo, wo)


def _layer(x, shift_prev, conv_prev, wkv0, mem_k, mem_v, p, final_g, *, last, bb, tt, chunk):
    b, t_len, d = x.shape
    n = b * t_len
    x1 = _ffn(x.reshape(n, d), p['ffn1_norm'], p['ffn1_w_up'], p['ffn1_w_down'], final_g, final_norm=False)
    x1_3 = x1.reshape(b, t_len, d)
    og, shift, wkv = _rwkv(x1_3, p['mix_norm'], p['w_s'], shift_prev[:, None, :], p['mu_shift'], p['w0'],
                           p['w_decay_up'], p['a0'], p['w_a_up'], p['w_g_up'], p['k_k'], p['k_a'], p['r_k'],
                           p['gn_g'], p['gn_b'], wkv0, bb=bb, tt=tt, chunk=chunk)
    cb, conv = _conv(x1_3, p['mix_norm'], p['w_c'], p['glu_b'], conv_prev, p['conv_w'], p['conv_b'],
                     p['conv_ln_g'], p['conv_ln_b'], bb=bb, tt=tt)
    oc = _xattn(x1_3, p['mix_norm'], p['w_q'], mem_k, mem_v, bb=bb, tt=tt)
    x2 = _merge(x1, p['mix_norm'], p['w_g'], og.reshape(n, -1), cb.reshape(n, -1), oc.reshape(n, -1),
                p['w_rwkv_out'], p['w_conv_out'], p['w_xattn_out'], p['w_o'])
    x3 = _ffn(x2, p['ffn2_norm'], p['ffn2_w_up'], p['ffn2_w_down'], final_g, final_norm=last)
    return x3.reshape(b, t_len, d), wkv, shift[:, 0, :], conv


def kernel(x_prompt, mem_prompt, x_sample, state_wkv, state_shift, state_conv, cache_mem_k, cache_mem_v,
           ffn1_norm, ffn1_w_up, ffn1_w_down, mix_norm, w_in, mu_shift, w0, w_decay_up, a0, w_a_up, w_g_up,
           k_k, k_a, r_k, gn_g, gn_b, w_rwkv_out, glu_b, conv_w, conv_b, conv_ln_g, conv_ln_b, w_conv_out,
           w_mem_kv, w_xattn_out, w_o, ffn2_norm, ffn2_w_up, ffn2_w_down, final_norm):
    depth = w_in.shape[0]
    d_model = w_in.shape[1]
    d_r = w0.shape[1]
    d_shift = mu_shift.shape[1]
    d_conv = conv_w.shape[2]
    d_x = w_xattn_out.shape[1]
    n_heads = d_r // HEAD_DIM
    o1 = d_shift
    o2 = o1 + 2 * d_conv
    o3 = o2 + d_x
    row = lambda a: a.astype(F32).reshape(1, -1)
    final_g = row(final_norm)

    layers = []
    for l in range(depth):
        layers.append({
            'ffn1_norm': row(ffn1_norm[l]), 'ffn1_w_up': ffn1_w_up[l].astype(BF16),
            'ffn1_w_down': ffn1_w_down[l].astype(BF16), 'mix_norm': row(mix_norm[l]),
            'w_s': w_in[l, :, :o1].astype(BF16), 'w_c': w_in[l, :, o1:o2].astype(BF16),
            'w_q': w_in[l, :, o2:o3].astype(BF16), 'w_g': w_in[l, :, o3:].astype(BF16),
            'mu_shift': row(mu_shift[l]), 'w0': row(w0[l]), 'w_decay_up': w_decay_up[l].astype(F32),
            'a0': row(a0[l]), 'w_a_up': w_a_up[l].astype(F32), 'w_g_up': w_g_up[l].astype(F32),
            'k_k': row(k_k[l]), 'k_a': row(k_a[l]), 'r_k': row(r_k[l]), 'gn_g': row(gn_g[l]),
            'gn_b': row(gn_b[l]), 'w_rwkv_out': w_rwkv_out[l].astype(BF16), 'glu_b': row(glu_b[l]),
            'conv_w': conv_w[l].astype(F32), 'conv_b': row(conv_b[l]), 'conv_ln_g': row(conv_ln_g[l]),
            'conv_ln_b': row(conv_ln_b[l]), 'w_conv_out': w_conv_out[l].astype(BF16),
            'w_mem_kv': w_mem_kv[l].astype(BF16), 'w_xattn_out': w_xattn_out[l].astype(BF16),
            'w_o': w_o[l].astype(BF16), 'ffn2_norm': row(ffn2_norm[l]),
            'ffn2_w_up': ffn2_w_up[l].astype(BF16), 'ffn2_w_down': ffn2_w_down[l].astype(BF16),
        })

    bp, n_mem, _ = mem_prompt.shape
    xp = x_prompt.astype(F32)
    wkv_p, shift_p, conv_p, mk_p, mv_p = [], [], [], [], []
    for l in range(depth):
        p = layers[l]
        mk, mv = _memkv(mem_prompt.astype(F32).reshape(bp * n_mem, d_model), p['w_mem_kv'])
        mk = mk.reshape(bp, n_mem, d_x)
        mv = mv.reshape(bp, n_mem, d_x)
        xp, wkv, sh, cv = _layer(
            xp, jnp.zeros((bp, d_shift), F32), jnp.zeros((bp, CONV_WIDTH - 1, d_conv), F32),
            jnp.zeros((bp, n_heads, HEAD_DIM, HEAD_DIM), F32), mk, mv, p, final_g,
            last=(l == depth - 1), bb=1, tt=256, chunk=64)
        wkv_p.append(wkv)
        shift_p.append(sh)
        conv_p.append(cv)
        mk_p.append(mk.reshape(bp, n_mem, N_XATTN_HEADS, d_x // N_XATTN_HEADS))
        mv_p.append(mv.reshape(bp, n_mem, N_XATTN_HEADS, d_x // N_XATTN_HEADS))

    bs, ts, _ = x_sample.shape
    xs = x_sample.astype(F32)
    wkv_s, shift_s, conv_s = [], [], []
    for l in range(depth):
        xs, wkv, sh, cv = _layer(
            xs, state_shift[l].astype(F32), state_conv[l].astype(F32), state_wkv[l].astype(F32),
            cache_mem_k[l].astype(F32).reshape(bs, n_mem, d_x), cache_mem_v[l].astype(F32).reshape(bs, n_mem, d_x),
            layers[l], final_g, last=(l == depth - 1), bb=16, tt=ts, chunk=ts)
        wkv_s.append(wkv)
        shift_s.append(sh)
        conv_s.append(cv)

    return (xp.astype(x_prompt.dtype), xs.astype(x_sample.dtype), jnp.stack(wkv_p), jnp.stack(shift_p),
            jnp.stack(conv_p), jnp.stack(mk_p), jnp.stack(mv_p), jnp.stack(wkv_s), jnp.stack(shift_s),
            jnp.stack(conv_s))
```

```python
import functools
import math

import jax
import jax.numpy as jnp
from jax import lax
from jax.experimental import pallas as pl
from jax.experimental.pallas import tpu as pltpu

F32 = jnp.float32
BF16 = jnp.bfloat16
HI = lax.Precision.HIGHEST

HEAD_DIM = 64
PAIR = 2 * HEAD_DIM
N_XATTN_HEADS = 4
CONV_WIDTH = 31
CONV_PAD = 32
LORA_DECAY = 64
LORA_A = 64
RMS_EPS = 1e-6
LN_EPS = 1e-5
GN_EPS = 64e-5
VMEM_LIMIT = 56 * 1024 * 1024


def _params(*sem):
    return pltpu.CompilerParams(dimension_semantics=sem, vmem_limit_bytes=VMEM_LIMIT)


def _rms(x, g):
    return x * lax.rsqrt(jnp.mean(x * x, axis=-1, keepdims=True) + RMS_EPS) * g


def _mm(a, b, precision=None):
    return jnp.dot(a, b, preferred_element_type=F32, precision=precision)


def _nt(a, b):
    return lax.dot_general(a, b, (((1,), (1,)), ((), ())), preferred_element_type=F32)


def _tn(a, b):
    return lax.dot_general(a, b, (((0,), (0,)), ((), ())), preferred_element_type=F32)


def _split(x):
    hi = x.astype(BF16)
    return hi, (x - hi.astype(F32)).astype(BF16)


def _exact_lhs_dot(sel, x):
    hi, lo = _split(x)
    return _mm(sel, hi) + _mm(sel, lo)


def _exact_rhs_dot(x, sel):
    hi, lo = _split(x)
    return _mm(hi, sel) + _mm(lo, sel)


def _const_spec(shape):
    nd = len(shape)
    return pl.BlockSpec(shape, lambda *_: (0,) * nd)


def _ffn_body(x_ref, g_ref, wg_ref, wu_ref, wd_ref, fg_ref, o_ref, xn_ref, acc_ref, *, final_norm):
    c = pl.program_id(1)

    @pl.when(c == 0)
    def _():
        xn_ref[...] = _rms(x_ref[...], g_ref[...]).astype(BF16)
        acc_ref[...] = jnp.zeros_like(acc_ref)

    xn = xn_ref[...]
    hg = _mm(xn, wg_ref[...])
    hu = _mm(xn, wu_ref[...])
    h = (hg * jax.nn.sigmoid(hg) * hu).astype(BF16)
    acc_ref[...] += _mm(h, wd_ref[...])

    @pl.when(c == pl.num_programs(1) - 1)
    def _():
        y = x_ref[...] + 0.5 * acc_ref[...]
        if final_norm:
            y = _rms(y, fg_ref[...])
        o_ref[...] = y


def _ffn(x, norm_g, w_up, w_down, final_g, *, final_norm, tm=512, fk=1408):
    n, d = x.shape
    tm = min(tm, n)
    d_ff = w_down.shape[0]
    nc = d_ff // fk
    assert n % tm == 0 and d_ff % fk == 0
    return pl.pallas_call(
        functools.partial(_ffn_body, final_norm=final_norm),
        grid=(n // tm, nc),
        in_specs=[
            pl.BlockSpec((tm, d), lambda i, c: (i, 0)),
            _const_spec((1, d)),
            pl.BlockSpec((d, fk), lambda i, c: (0, c)),
            pl.BlockSpec((d, fk), lambda i, c: (0, c + nc)),
            pl.BlockSpec((fk, d), lambda i, c: (c, 0)),
            _const_spec((1, d)),
        ],
        out_specs=pl.BlockSpec((tm, d), lambda i, c: (i, 0)),
        out_shape=jax.ShapeDtypeStruct((n, d), F32),
        scratch_shapes=[pltpu.VMEM((tm, d), BF16), pltpu.VMEM((tm, d), F32)],
        compiler_params=_params("parallel", "arbitrary"),
        name="ffn_final" if final_norm else "ffn",
    )(x, norm_g, w_up, w_up, w_down, final_g)


def _memkv_body(m_ref, w_ref, k_ref, v_ref):
    kv = _mm(m_ref[...].astype(BF16), w_ref[...])
    dk = k_ref.shape[-1]
    k_ref[...] = kv[:, :dk]
    v_ref[...] = kv[:, dk:]


def _memkv(mem, w, *, tm=512):
    n, d = mem.shape
    dk = w.shape[1] // 2
    return pl.pallas_call(
        _memkv_body,
        grid=(n // tm,),
        in_specs=[pl.BlockSpec((tm, d), lambda i: (i, 0)), _const_spec(w.shape)],
        out_specs=[pl.BlockSpec((tm, dk), lambda i: (i, 0))] * 2,
        out_shape=[jax.ShapeDtypeStruct((n, dk), F32)] * 2,
        compiler_params=_params("parallel"),
        name="memkv",
    )(mem, w)


def _rwkv_body(x_ref, mixg_ref, ws_ref, sprev_ref, mu_ref, w0_ref, wdu_ref, a0_ref, wau_ref, wgu_ref,
               kk_ref, ka_ref, rk_ref, gng_ref, gnb_ref, s0_ref,
               og_ref, sout_ref, wkv_ref,
               carry, st, at_s, rt_s, bb_s, kb_s, bh_s, kh_s, v_s, wt_s, bonus_s, g_s, o_s,
               gm_s, hm_s, rp_s, oi_s, *, chunk):
    bb_n, tt, d = x_ref.shape
    m = bb_n * tt
    d_shift = ws_ref.shape[1]
    d_r = w0_ref.shape[1]
    n_pairs = d_r // PAIR
    n_ck = tt // chunk
    n_blk = m // chunk
    log_chunk = int(math.log2(chunk))
    c2 = 2 * chunk
    t = pl.program_id(1)

    @pl.when(t == 0)
    def _():
        carry[...] = sprev_ref[...]
        zero = jnp.zeros((HEAD_DIM, HEAD_DIM), F32)
        for b_i in range(bb_n):
            for p in range(n_pairs):
                top = jnp.concatenate([s0_ref[b_i, 2 * p], zero], axis=1)
                bot = jnp.concatenate([zero, s0_ref[b_i, 2 * p + 1]], axis=1)
                st[b_i, p] = jnp.concatenate([top, bot], axis=0)

    xn = _rms(x_ref[...].reshape(m, d), mixg_ref[...]).astype(BF16)
    zs = _mm(xn, ws_ref[...])
    rolled = pltpu.roll(zs, 1, axis=0)
    prev0 = jnp.broadcast_to(carry[...], (bb_n, tt, d_shift)).reshape(m, d_shift)
    row = lax.broadcasted_iota(jnp.int32, (m, d_shift), 0)
    prev = jnp.where((row & (tt - 1)) == 0, prev0, rolled)
    last = zs.reshape(bb_n, tt, d_shift)[:, tt - 1:tt, :]
    carry[...] = last
    sout_ref[...] = last
    xm = zs + (prev - zs) * mu_ref[...]

    c1, c2_, c3 = d_r, 2 * d_r, 3 * d_r
    c4 = c3 + LORA_DECAY
    c5 = c4 + LORA_A
    r, k, v = xm[:, :c1], xm[:, c1:c2_], xm[:, c2_:c3]
    wd, ad, gd = xm[:, c3:c4], xm[:, c4:c5], xm[:, c5:]
    dec_in = w0_ref[...] + _mm(jnp.tanh(wd), wdu_ref[...], HI)
    neg = -dec_in
    softplus = jnp.maximum(neg, 0.0) + jnp.log(1.0 + jnp.exp(-jnp.abs(neg)))
    ld = -jnp.exp(-softplus - 0.5)
    a = jax.nn.sigmoid(a0_ref[...] + _mm(ad, wau_ref[...], HI))
    g_s[...] = _mm(jax.nn.sigmoid(gd), wgu_ref[...], HI)
    kh = k * (1.0 + (a - 1.0) * ka_ref[...])

    hr = lax.broadcasted_iota(jnp.int32, (d_r, d_r), 0)
    hc = lax.broadcasted_iota(jnp.int32, (d_r, d_r), 1)
    head_sum = jnp.where((hr ^ hc) < HEAD_DIM, 1.0, 0.0).astype(BF16)
    kkraw = k * kk_ref[...]
    kk = kkraw / jnp.maximum(jnp.sqrt(_exact_rhs_dot(kkraw * kkraw, head_sum)), 1e-12)
    b = kk * a
    bonus_s[...] = _exact_rhs_dot(r * kh * rk_ref[...], head_sum) * v

    ri = lax.broadcasted_iota(jnp.int32, (m, m), 0)
    ci = lax.broadcasted_iota(jnp.int32, (m, m), 1)
    same_chunk = (ri ^ ci) < chunk
    in_chunk = jnp.where(same_chunk, 1.0, 0.0)
    cum = _exact_lhs_dot(jnp.where(ci <= ri, in_chunk, 0.0).astype(BF16), ld)
    tot = _exact_lhs_dot(in_chunk.astype(BF16), ld)
    w_inv = jnp.exp(-cum)
    w_rem = jnp.exp(tot - cum)
    at_s[...] = -kk * jnp.exp(cum - ld)
    rt_s[...] = r * jnp.exp(cum)
    bb_s[...] = b * w_inv
    kb_s[...] = kh * w_inv
    bh_s[...] = b * w_rem
    kh_s[...] = kh * w_rem
    v_s[...] = v
    wt_s[...] = jnp.exp(tot)

    lane_lo = lax.broadcasted_iota(jnp.int32, (chunk, PAIR), 1) < HEAD_DIM
    r2 = lax.broadcasted_iota(jnp.int32, (c2, c2), 0)
    q2 = lax.broadcasted_iota(jnp.int32, (c2, c2), 1)
    strict = (q2 & (chunk - 1)) < (r2 & (chunk - 1))
    incl = (q2 & (chunk - 1)) <= (r2 & (chunk - 1))
    eye2 = (r2 == q2).astype(F32)

    def stack(ref, rows, ls):
        x = ref[rows, ls]
        return jnp.concatenate([jnp.where(lane_lo, x, 0.0), jnp.where(lane_lo, 0.0, x)], axis=0).astype(BF16)

    def phase1_step(i, c):
        chains = [(2 * i + j, p) for j in range(2) for p in range(n_pairs)]
        n = range(len(chains))

        def stacks(ref):
            return [stack(ref, pl.ds(pl.multiple_of(blk * chunk, chunk), chunk), slice(p * PAIR, (p + 1) * PAIR))
                    for blk, p in chains]

        at, rt, bbm, kbm = stacks(at_s), stacks(rt_s), stacks(bb_s), stacks(kb_s)
        bhm, khm, vm = stacks(bh_s), stacks(kh_s), stacks(v_s)
        a_ab = [jnp.where(strict, _nt(at[c_], bbm[c_]), 0.0) for c_ in n]
        a_ak = [jnp.where(strict, _nt(at[c_], kbm[c_]), 0.0).astype(BF16) for c_ in n]
        a_rb = [jnp.where(incl, _nt(rt[c_], bbm[c_]), 0.0).astype(BF16) for c_ in n]
        a_rk = [jnp.where(incl, _nt(rt[c_], kbm[c_]), 0.0).astype(BF16) for c_ in n]
        tinv = [eye2 + a_ab[c_] for c_ in n]
        ap = [a_ab[c_].astype(BF16) for c_ in n]
        for _ in range(log_chunk - 1):
            ap = [_mm(ap[c_], ap[c_]).astype(BF16) for c_ in n]
            tinv = [tinv[c_] + _mm(tinv[c_].astype(BF16), ap[c_]) for c_ in n]
        av = [_mm(a_ak[c_], vm[c_]).astype(BF16) for c_ in n]
        pq = [_mm(tinv[c_].astype(BF16), jnp.concatenate([av[c_], at[c_]], axis=1)) for c_ in n]
        pm = [pq[c_][:, :PAIR].astype(BF16) for c_ in n]
        qm = [pq[c_][:, PAIR:].astype(BF16) for c_ in n]
        for c_, (blk, p) in enumerate(chains):
            gm_s[blk, p] = _tn(qm[c_], bhm[c_]).astype(BF16)
        for c_, (blk, p) in enumerate(chains):
            hm_s[blk, p] = _tn(jnp.concatenate([pm[c_], vm[c_]], axis=0),
                               jnp.concatenate([bhm[c_], khm[c_]], axis=0))
        for c_, (blk, p) in enumerate(chains):
            rp_s[blk, p] = (rt[c_].astype(F32) + _mm(a_rb[c_], qm[c_])).astype(BF16)
        for c_, (blk, p) in enumerate(chains):
            oi_s[blk, p] = _mm(a_rb[c_], pm[c_]) + _mm(a_rk[c_], vm[c_])
        return c

    lax.fori_loop(0, n_blk // 2, phase1_step, 0)

    def phase2_step(blk, c):
        row0 = pl.multiple_of(blk * chunk, chunk)
        b_i = blk // n_ck
        for p in range(n_pairs):
            ls = slice(p * PAIR, (p + 1) * PAIR)
            s = st[b_i, p]
            sb = s.astype(BF16)
            o_bd = oi_s[blk, p] + _nt(rp_s[blk, p], sb)
            o_s[pl.ds(row0, chunk), ls] = o_bd[:chunk] + o_bd[chunk:]
            st[b_i, p] = s * wt_s[pl.ds(row0, 1), ls] + _mm(sb, gm_s[blk, p]) + hm_s[blk, p]
        return c

    lax.fori_loop(0, n_blk, phase2_step, 0)

    o = o_s[...]
    mean = _exact_rhs_dot(o, head_sum) * (1.0 / HEAD_DIM)
    oc = o - mean
    var = _exact_rhs_dot(oc * oc, head_sum) * (1.0 / HEAD_DIM)
    o = oc * lax.rsqrt(var + GN_EPS) * gng_ref[...] + gnb_ref[...] + bonus_s[...]
    og_ref[...] = (o * g_s[...]).astype(BF16).reshape(bb_n, tt, d_r)

    @pl.when(t == pl.num_programs(1) - 1)
    def _():
        for b_i in range(bb_n):
            for p in range(n_pairs):
                s = st[b_i, p]
                wkv_ref[b_i, 2 * p] = s[:HEAD_DIM, :HEAD_DIM]
                wkv_ref[b_i, 2 * p + 1] = s[HEAD_DIM:, HEAD_DIM:]


def _rwkv(x, mixg, ws, sprev, mu, w0, wdu, a0, wau, wgu, kk, ka, rk, gng, gnb, s0, *, bb, tt, chunk):
    b, t_len, d = x.shape
    d_shift = ws.shape[1]
    d_r = w0.shape[1]
    n_heads = d_r // HEAD_DIM
    n_pairs = d_r // PAIR
    m = bb * tt
    n_blk = m // chunk
    assert b % bb == 0 and t_len % tt == 0 and tt % chunk == 0 and n_blk % 2 == 0
    assert chunk & (chunk - 1) == 0 and tt & (tt - 1) == 0 and chunk % 8 == 0
    tok = pltpu.VMEM((m, d_r), F32)
    return pl.pallas_call(
        functools.partial(_rwkv_body, chunk=chunk),
        grid=(b // bb, t_len // tt),
        in_specs=[
            pl.BlockSpec((bb, tt, d), lambda i, j: (i, j, 0)),
            _const_spec(mixg.shape), _const_spec(ws.shape),
            pl.BlockSpec((bb, 1, d_shift), lambda i, j: (i, 0, 0)),
            _const_spec(mu.shape), _const_spec(w0.shape), _const_spec(wdu.shape), _const_spec(a0.shape),
            _const_spec(wau.shape), _const_spec(wgu.shape), _const_spec(kk.shape), _const_spec(ka.shape),
            _const_spec(rk.shape), _const_spec(gng.shape), _const_spec(gnb.shape),
            pl.BlockSpec((bb, n_heads, HEAD_DIM, HEAD_DIM), lambda i, j: (i, 0, 0, 0)),
        ],
        out_specs=[
            pl.BlockSpec((bb, tt, d_r), lambda i, j: (i, j, 0)),
            pl.BlockSpec((bb, 1, d_shift), lambda i, j: (i, 0, 0)),
            pl.BlockSpec((bb, n_heads, HEAD_DIM, HEAD_DIM), lambda i, j: (i, 0, 0, 0)),
        ],
        out_shape=[
            jax.ShapeDtypeStruct((b, t_len, d_r), BF16),
            jax.ShapeDtypeStruct((b, 1, d_shift), F32),
            jax.ShapeDtypeStruct((b, n_heads, HEAD_DIM, HEAD_DIM), F32),
        ],
        scratch_shapes=[
            pltpu.VMEM((bb, 1, d_shift), F32),
            pltpu.VMEM((bb, n_pairs, PAIR, PAIR), F32),
        ] + [tok] * 11 + [
            pltpu.VMEM((n_blk, n_pairs, PAIR, PAIR), BF16),
            pltpu.VMEM((n_blk, n_pairs, PAIR, PAIR), F32),
            pltpu.VMEM((n_blk, n_pairs, 2 * chunk, PAIR), BF16),
            pltpu.VMEM((n_blk, n_pairs, 2 * chunk, PAIR), F32),
        ],
        compiler_params=_params("parallel", "arbitrary"),
        name="rwkv",
    )(x, mixg, ws, sprev, mu, w0, wdu, a0, wau, wgu, kk, ka, rk, gng, gnb, s0)


def _conv_body(x_ref, mixg_ref, wc_ref, glub_ref, cprev_ref, cw_ref, cb_ref, lng_ref, lnb_ref,
               y_ref, cnew_ref, ubuf):
    bb_n, tt, d = x_ref.shape
    m = bb_n * tt
    dc = cw_ref.shape[1]
    lo = CONV_PAD - (CONV_WIDTH - 1)
    t = pl.program_id(1)

    @pl.when(t == 0)
    def _():
        ubuf[:, lo:CONV_PAD, :] = cprev_ref[...]

    xn = _rms(x_ref[...].reshape(m, d), mixg_ref[...]).astype(BF16)
    zc = _mm(xn, wc_ref[...]) + glub_ref[...]
    u = zc[:, :dc] * jax.nn.sigmoid(zc[:, dc:])
    ubuf[:, CONV_PAD:CONV_PAD + tt, :] = u.reshape(bb_n, tt, dc)
    acc = jnp.zeros((bb_n, tt, dc), F32) + cb_ref[...]
    for kx in range(CONV_WIDTH):
        acc = acc + ubuf[:, lo + kx:lo + kx + tt, :] * cw_ref[kx:kx + 1, :]
    tail = ubuf[:, lo + tt:CONV_PAD + tt, :]
    cnew_ref[...] = tail
    ubuf[:, lo:CONV_PAD, :] = tail
    c = acc.reshape(m, dc)
    cm = jnp.mean(c, axis=-1, keepdims=True)
    cc = c - cm
    cv = jnp.mean(cc * cc, axis=-1, keepdims=True)
    c = cc * lax.rsqrt(cv + LN_EPS) * lng_ref[...] + lnb_ref[...]
    y_ref[...] = (c * jax.nn.sigmoid(c)).astype(BF16).reshape(bb_n, tt, dc)


def _conv(x, mixg, wc, glub, cprev, cw, cb, lng, lnb, *, bb, tt):
    b, t_len, d = x.shape
    dc = cw.shape[1]
    return pl.pallas_call(
        _conv_body,
        grid=(b // bb, t_len // tt),
        in_specs=[
            pl.BlockSpec((bb, tt, d), lambda i, j: (i, j, 0)),
            _const_spec(mixg.shape), _const_spec(wc.shape), _const_spec(glub.shape),
            pl.BlockSpec((bb, CONV_WIDTH - 1, dc), lambda i, j: (i, 0, 0)),
            _const_spec(cw.shape), _const_spec(cb.shape), _const_spec(lng.shape), _const_spec(lnb.shape),
        ],
        out_specs=[
            pl.BlockSpec((bb, tt, dc), lambda i, j: (i, j, 0)),
            pl.BlockSpec((bb, CONV_WIDTH - 1, dc), lambda i, j: (i, 0, 0)),
        ],
        out_shape=[
            jax.ShapeDtypeStruct((b, t_len, dc), BF16),
            jax.ShapeDtypeStruct((b, CONV_WIDTH - 1, dc), F32),
        ],
        scratch_shapes=[pltpu.VMEM((bb, CONV_PAD + tt, dc), F32)],
        compiler_params=_params("parallel", "arbitrary"),
        name="conv",
    )(x, mixg, wc, glub, cprev, cw, cb, lng, lnb)


def _xattn_body(x_ref, mixg_ref, wq_ref, mk_ref, mv_ref, o_ref):
    bb_n, tt, d = x_ref.shape
    m = bb_n * tt
    dh = mk_ref.shape[-1] // N_XATTN_HEADS
    xn = _rms(x_ref[...].reshape(m, d), mixg_ref[...]).astype(BF16)
    q = (_mm(xn, wq_ref[...]) * (dh ** -0.5)).astype(BF16)
    for b_i in range(bb_n):
        outs = []
        for h in range(N_XATTN_HEADS):
            sl = slice(h * dh, (h + 1) * dh)
            s = _nt(q[b_i * tt:(b_i + 1) * tt, sl], mk_ref[b_i, :, sl].astype(BF16))
            p = jnp.exp(s - jnp.max(s, axis=-1, keepdims=True))
            l = jnp.sum(p, axis=-1, keepdims=True)
            outs.append(_mm(p.astype(BF16), mv_ref[b_i, :, sl].astype(BF16)) / l)
        o_ref[b_i] = jnp.concatenate(outs, axis=-1).astype(BF16)


def _xattn(x, mixg, wq, mk, mv, *, bb, tt):
    b, t_len, d = x.shape
    n_mem, dx = mk.shape[1:]
    return pl.pallas_call(
        _xattn_body,
        grid=(b // bb, t_len // tt),
        in_specs=[
            pl.BlockSpec((bb, tt, d), lambda i, j: (i, j, 0)),
            _const_spec(mixg.shape), _const_spec(wq.shape),
            pl.BlockSpec((bb, n_mem, dx), lambda i, j: (i, 0, 0)),
            pl.BlockSpec((bb, n_mem, dx), lambda i, j: (i, 0, 0)),
        ],
        out_specs=pl.BlockSpec((bb, tt, dx), lambda i, j: (i, j, 0)),
        out_shape=jax.ShapeDtypeStruct((b, t_len, dx), BF16),
        compiler_params=_params("parallel", "arbitrary"),
        name="xattn",
    )(x, mixg, wq, mk, mv)


def _merge_body(x_ref, mixg_ref, wg_ref, ya_ref, yb_ref, yc_ref, wro_ref, wco_ref, wxo_ref, wo_ref, o_ref):
    d = x_ref.shape[1]
    x = x_ref[...]
    xn = _rms(x, mixg_ref[...]).astype(BF16)
    merged = jnp.zeros_like(x)
    for i, (y_ref, w_ref) in enumerate(((ya_ref, wro_ref), (yb_ref, wco_ref), (yc_ref, wxo_ref))):
        gate = jax.nn.sigmoid(_mm(xn, wg_ref[:, i * d:(i + 1) * d]))
        merged = merged + gate * _mm(y_ref[...], w_ref[...])
    o_ref[...] = x + _mm(merged.astype(BF16), wo_ref[...])


def _merge(x, mixg, wg, ya, yb, yc, wro, wco, wxo, wo, *, tm=512):
    n, d = x.shape
    tm = min(tm, n)
    row = lambda w: pl.BlockSpec((tm, w), lambda i: (i, 0))
    return pl.pallas_call(
        _merge_body,
        grid=(n // tm,),
        in_specs=[row(d), _const_spec(mixg.shape), _const_spec(wg.shape), row(ya.shape[1]), row(yb.shape[1]),
                  row(yc.shape[1]), _const_spec(wro.shape), _const_spec(wco.shape), _const_spec(wxo.shape),
                  _const_spec(wo.shape)],
        out_specs=row(d),
        out_shape=jax.ShapeDtypeStruct((n, d), F32),
        compiler_params=_params("parallel"),
        name="merge",
    )(x, mixg, wg, ya, yb, yc, wro, wco, wxo, wo)


def _layer(x, shift_prev, conv_prev, wkv0, mem_k, mem_v, p, final_g, *, last, bb, tt, chunk):
    b, t_len, d = x.shape
    n = b * t_len
    x1 = _ffn(x.reshape(n, d), p['ffn1_norm'], p['ffn1_w_up'], p['ffn1_w_down'], final_g, final_norm=False)
    x1_3 = x1.reshape(b, t_len, d)
    og, shift, wkv = _rwkv(x1_3, p['mix_norm'], p['w_s'], shift_prev[:, None, :], p['mu_shift'], p['w0'],
                           p['w_decay_up'], p['a0'], p['w_a_up'], p['w_g_up'], p['k_k'], p['k_a'], p['r_k'],
                           p['gn_g'], p['gn_b'], wkv0, bb=bb, tt=tt, chunk=chunk)
    cb, conv = _conv(x1_3, p['mix_norm'], p['w_c'], p['glu_b'], conv_prev, p['conv_w'], p['conv_b'],
                     p['conv_ln_g'], p['conv_ln_b'], bb=bb, tt=tt)
    oc = _xattn(x1_3, p['mix_norm'], p['w_q'], mem_k, mem_v, bb=bb, tt=tt)
    x2 = _merge(x1, p['mix_norm'], p['w_g'], og.reshape(n, -1), cb.reshape(n, -1), oc.reshape(n, -1),
                p['w_rwkv_out'], p['w_conv_out'], p['w_xattn_out'], p['w_o'])
    x3 = _ffn(x2, p['ffn2_norm'], p['ffn2_w_up'], p['ffn2_w_down'], final_g, final_norm=last)
    return x3.reshape(b, t_len, d), wkv, shift[:, 0, :], conv


def kernel(x_prompt, mem_prompt, x_sample, state_wkv, state_shift, state_conv, cache_mem_k, cache_mem_v,
           ffn1_norm, ffn1_w_up, ffn1_w_down, mix_norm, w_in, mu_shift, w0, w_decay_up, a0, w_a_up, w_g_up,
           k_k, k_a, r_k, gn_g, gn_b, w_rwkv_out, glu_b, conv_w, conv_b, conv_ln_g, conv_ln_b, w_conv_out,
           w_mem_kv, w_xattn_out, w_o, ffn2_norm, ffn2_w_up, ffn2_w_down, final_norm):
    depth = w_in.shape[0]
    d_model = w_in.shape[1]
    d_r = w0.shape[1]
    d_shift = mu_shift.shape[1]
    d_conv = conv_w.shape[2]
    d_x = w_xattn_out.shape[1]
    n_heads = d_r // HEAD_DIM
    o1 = d_shift
    o2 = o1 + 2 * d_conv
    o3 = o2 + d_x
    row = lambda a: a.astype(F32).reshape(1, -1)
    final_g = row(final_norm)

    layers = []
    for l in range(depth):
        layers.append({
            'ffn1_norm': row(ffn1_norm[l]), 'ffn1_w_up': ffn1_w_up[l].astype(BF16),
            'ffn1_w_down': ffn1_w_down[l].astype(BF16), 'mix_norm': row(mix_norm[l]),
            'w_s': w_in[l, :, :o1].astype(BF16), 'w_c': w_in[l, :, o1:o2].astype(BF16),
            'w_q': w_in[l, :, o2:o3].astype(BF16), 'w_g': w_in[l, :, o3:].astype(BF16),
            'mu_shift': row(mu_shift[l]), 'w0': row(w0[l]), 'w_decay_up': w_decay_up[l].astype(F32),
            'a0': row(a0[l]), 'w_a_up': w_a_up[l].astype(F32), 'w_g_up': w_g_up[l].astype(F32),
            'k_k': row(k_k[l]), 'k_a': row(k_a[l]), 'r_k': row(r_k[l]), 'gn_g': row(gn_g[l]),
            'gn_b': row(gn_b[l]), 'w_rwkv_out': w_rwkv_out[l].astype(BF16), 'glu_b': row(glu_b[l]),
            'conv_w': conv_w[l].astype(F32), 'conv_b': row(conv_b[l]), 'conv_ln_g': row(conv_ln_g[l]),
            'conv_ln_b': row(conv_ln_b[l]), 'w_conv_out': w_conv_out[l].astype(BF16),
            'w_mem_kv': w_mem_kv[l].astype(BF16), 'w_xattn_out': w_xattn_out[l].astype(BF16),
            'w_o': w_o[l].astype(BF16), 'ffn2_norm': row(ffn2_norm[l]),
            'ffn2_w_up': ffn2_w_up[l].astype(BF16), 'ffn2_w_down': ffn2_w_down[l].astype(BF16),
        })

    bp, n_mem, _ = mem_prompt.shape
    xp = x_prompt.astype(F32)
    wkv_p, shift_p, conv_p, mk_p, mv_p = [], [], [], [], []
    for l in range(depth):
        p = layers[l]
        mk, mv = _memkv(mem_prompt.astype(F32).reshape(bp * n_mem, d_model), p['w_mem_kv'])
        mk = mk.reshape(bp, n_mem, d_x)
        mv = mv.reshape(bp, n_mem, d_x)
        xp, wkv, sh, cv = _layer(
            xp, jnp.zeros((bp, d_shift), F32), jnp.zeros((bp, CONV_WIDTH - 1, d_conv), F32),
            jnp.zeros((bp, n_heads, HEAD_DIM, HEAD_DIM), F32), mk, mv, p, final_g,
            last=(l == depth - 1), bb=1, tt=256, chunk=64)
        wkv_p.append(wkv)
        shift_p.append(sh)
        conv_p.append(cv)
        mk_p.append(mk.reshape(bp, n_mem, N_XATTN_HEADS, d_x // N_XATTN_HEADS))
        mv_p.append(mv.reshape(bp, n_mem, N_XATTN_HEADS, d_x // N_XATTN_HEADS))

    bs, ts, _ = x_sample.shape
    xs = x_sample.astype(F32)
    wkv_s, shift_s, conv_s = [], [], []
    for l in range(depth):
        xs, wkv, sh, cv = _layer(
            xs, state_shift[l].astype(F32), state_conv[l].astype(F32), state_wkv[l].astype(F32),
            cache_mem_k[l].astype(F32).reshape(bs, n_mem, d_x), cache_mem_v[l].astype(F32).reshape(bs, n_mem, d_x),
            layers[l], final_g, last=(l == depth - 1), bb=16, tt=ts, chunk=ts)
        wkv_s.append(wkv)
        shift_s.append(sh)
        conv_s.append(cv)

    return (xp.astype(x_prompt.dtype), xs.astype(x_sample.dtype), jnp.stack(wkv_p), jnp.stack(shift_p),
            jnp.stack(conv_p), jnp.stack(mk_p), jnp.stack(mv_p), jnp.stack(wkv_s), jnp.stack(shift_s),
            jnp.stack(conv_s))
```

```python
import functools
import math

import jax
import jax.numpy as jnp
from jax import lax
from jax.experimental import pallas as pl
from jax.experimental.pallas import tpu as pltpu

F32 = jnp.float32
BF16 = jnp.bfloat16
HI = lax.Precision.HIGHEST

HEAD_DIM = 64
PAIR = 2 * HEAD_DIM
N_XATTN_HEADS = 4
CONV_WIDTH = 31
SUBLANES = 8
CONV_PAD = 32
LORA_DECAY = 64
LORA_A = 64
RMS_EPS = 1e-6
LN_EPS = 1e-5
GN_EPS = 64e-5
VMEM_LIMIT = 56 * 1024 * 1024


def _params(*sem):
    return pltpu.CompilerParams(dimension_semantics=sem, vmem_limit_bytes=VMEM_LIMIT)


def _rms(x, g):
    return x * lax.rsqrt(jnp.mean(x * x, axis=-1, keepdims=True) + RMS_EPS) * g


def _mm(a, b, precision=None):
    return jnp.dot(a, b, preferred_element_type=F32, precision=precision)


def _nt(a, b):
    return lax.dot_general(a, b, (((1,), (1,)), ((), ())), preferred_element_type=F32)


def _tn(a, b):
    return lax.dot_general(a, b, (((0,), (0,)), ((), ())), preferred_element_type=F32)


def _split(x):
    hi = x.astype(BF16)
    return hi, (x - hi.astype(F32)).astype(BF16)


def _exact_lhs_dot(sel, x):
    hi, lo = _split(x)
    return _mm(sel, hi) + _mm(sel, lo)


def _dot3(x, w):
    x_hi, x_lo = _split(x)
    w_hi, w_lo = _split(w)
    return _mm(x_hi, w_hi) + _mm(x_lo, w_hi) + _mm(x_hi, w_lo)


def _const_spec(shape):
    nd = len(shape)
    return pl.BlockSpec(shape, lambda *_: (0,) * nd)


def _ffn_body(x_ref, g_ref, wg_ref, wu_ref, wd_ref, fg_ref, o_ref, xn_ref, acc_ref, *, final_norm):
    c = pl.program_id(1)

    @pl.when(c == 0)
    def _():
        xn_ref[...] = _rms(x_ref[...], g_ref[...]).astype(BF16)
        acc_ref[...] = jnp.zeros_like(acc_ref)

    xn = xn_ref[...]
    hg = _mm(xn, wg_ref[...])
    hu = _mm(xn, wu_ref[...])
    h = (hg * jax.nn.sigmoid(hg) * hu).astype(BF16)
    acc_ref[...] += _mm(h, wd_ref[...])

    @pl.when(c == pl.num_programs(1) - 1)
    def _():
        y = x_ref[...] + 0.5 * acc_ref[...]
        if final_norm:
            y = _rms(y, fg_ref[...])
        o_ref[...] = y


def _ffn(x, norm_g, w_up, w_down, final_g, *, final_norm, tm=512, fk=1408):
    n, d = x.shape
    tm = min(tm, n)
    d_ff = w_down.shape[0]
    nc = d_ff // fk
    assert n % tm == 0 and d_ff % fk == 0
    return pl.pallas_call(
        functools.partial(_ffn_body, final_norm=final_norm),
        grid=(n // tm, nc),
        in_specs=[
            pl.BlockSpec((tm, d), lambda i, c: (i, 0)),
            _const_spec((1, d)),
            pl.BlockSpec((d, fk), lambda i, c: (0, c)),
            pl.BlockSpec((d, fk), lambda i, c: (0, c + nc)),
            pl.BlockSpec((fk, d), lambda i, c: (c, 0)),
            _const_spec((1, d)),
        ],
        out_specs=pl.BlockSpec((tm, d), lambda i, c: (i, 0)),
        out_shape=jax.ShapeDtypeStruct((n, d), F32),
        scratch_shapes=[pltpu.VMEM((tm, d), BF16), pltpu.VMEM((tm, d), F32)],
        compiler_params=_params("parallel", "arbitrary"),
        name="ffn_final" if final_norm else "ffn",
    )(x, norm_g, w_up, w_up, w_down, final_g)


def _memkv_body(m_ref, w_ref, wt_ref, k_ref, v_ref, kt_ref, vt_ref):
    mem = m_ref[0].astype(BF16)
    dk = k_ref.shape[-1]
    kv = _mm(mem, w_ref[...])
    k_ref[0] = kv[:, :dk]
    v_ref[0] = kv[:, dk:]
    kvt = _nt(wt_ref[...], mem)
    kt_ref[0] = kvt[:dk].astype(BF16)
    vt_ref[0] = kvt[dk:].astype(BF16)


def _memkv(mem, w, wt):
    b, n_mem, d = mem.shape
    dk = w.shape[1] // 2
    return pl.pallas_call(
        _memkv_body,
        grid=(b,),
        in_specs=[pl.BlockSpec((1, n_mem, d), lambda i: (i, 0, 0)), _const_spec(w.shape), _const_spec(wt.shape)],
        out_specs=[pl.BlockSpec((1, n_mem, dk), lambda i: (i, 0, 0))] * 2
        + [pl.BlockSpec((1, dk, n_mem), lambda i: (i, 0, 0))] * 2,
        out_shape=[jax.ShapeDtypeStruct((b, n_mem, dk), F32)] * 2
        + [jax.ShapeDtypeStruct((b, dk, n_mem), BF16)] * 2,
        compiler_params=_params("parallel"),
        name="memkv",
    )(mem, w, wt)


def _rwkv_body(x_ref, mixg_ref, ws_ref, sprev_ref, mu_ref, w0_ref, wdu_ref, a0_ref, wau_ref, wgu_ref,
               kk_ref, ka_ref, rk_ref, gng_ref, gnb_ref, s0_ref,
               og_ref, sout_ref, wkv_ref,
               carry, st, at_s, rt_s, bb_s, kb_s, bh_s, kh_s, v_s, wt_s, bonus_s, g_s, o_s,
               gm_s, hm_s, rp_s, oi_s, *, chunk):
    bb_n, tt, d = x_ref.shape
    m = bb_n * tt
    d_shift = ws_ref.shape[1]
    d_r = w0_ref.shape[1]
    n_pairs = d_r // PAIR
    n_ck = tt // chunk
    n_blk = m // chunk
    log_chunk = int(math.log2(chunk))
    c2 = 2 * chunk
    t = pl.program_id(1)

    @pl.when(t == 0)
    def _():
        carry[...] = sprev_ref[...]
        zero = jnp.zeros((HEAD_DIM, HEAD_DIM), F32)
        for b_i in range(bb_n):
            for p in range(n_pairs):
                top = jnp.concatenate([s0_ref[b_i, 2 * p], zero], axis=1)
                bot = jnp.concatenate([zero, s0_ref[b_i, 2 * p + 1]], axis=1)
                st[b_i, p] = jnp.concatenate([top, bot], axis=0)

    xn = _rms(x_ref[...].reshape(m, d), mixg_ref[...]).astype(BF16)
    zs = _mm(xn, ws_ref[...])
    rolled = pltpu.roll(zs, 1, axis=0)
    prev0 = jnp.broadcast_to(carry[...], (bb_n, tt, d_shift)).reshape(m, d_shift)
    row = lax.broadcasted_iota(jnp.int32, (m, d_shift), 0)
    prev = jnp.where((row & (tt - 1)) == 0, prev0, rolled)
    last = zs.reshape(bb_n, tt, d_shift)[:, tt - 1:tt, :]
    carry[...] = last
    sout_ref[...] = last
    xm = zs + (prev - zs) * mu_ref[...]

    c1, c2_, c3 = d_r, 2 * d_r, 3 * d_r
    c4 = c3 + LORA_DECAY
    c5 = c4 + LORA_A
    r, k, v = xm[:, :c1], xm[:, c1:c2_], xm[:, c2_:c3]
    wd, ad, gd = xm[:, c3:c4], xm[:, c4:c5], xm[:, c5:]
    dec_in = w0_ref[...] + _dot3(jnp.tanh(wd), wdu_ref[...])
    neg = -dec_in
    softplus = jnp.maximum(neg, 0.0) + jnp.log(1.0 + jnp.exp(-jnp.abs(neg)))
    ld = -jnp.exp(-softplus - 0.5)
    a = jax.nn.sigmoid(a0_ref[...] + _dot3(ad, wau_ref[...]))
    g_s[...] = _dot3(jax.nn.sigmoid(gd), wgu_ref[...])
    kh = k * (1.0 + (a - 1.0) * ka_ref[...])

    hr = lax.broadcasted_iota(jnp.int32, (d_r, d_r), 0)
    hc = lax.broadcasted_iota(jnp.int32, (d_r, d_r), 1)
    head_sum = jnp.where((hr ^ hc) < HEAD_DIM, 1.0, 0.0).astype(BF16)
    kkraw = k * kk_ref[...]
    kk = kkraw / jnp.maximum(jnp.sqrt(_mm((kkraw * kkraw).astype(BF16), head_sum)), 1e-12)
    b = kk * a
    bonus_s[...] = _mm((r * kh * rk_ref[...]).astype(BF16), head_sum) * v

    ri = lax.broadcasted_iota(jnp.int32, (m, m), 0)
    ci = lax.broadcasted_iota(jnp.int32, (m, m), 1)
    same_chunk = (ri ^ ci) < chunk
    in_chunk = jnp.where(same_chunk, 1.0, 0.0)
    cum = _exact_lhs_dot(jnp.where(ci <= ri, in_chunk, 0.0).astype(BF16), ld)
    tot = _exact_lhs_dot(in_chunk.astype(BF16), ld)
    w_inv = jnp.exp(-cum)
    w_rem = jnp.exp(tot - cum)
    at_s[...] = -kk * jnp.exp(cum - ld)
    rt_s[...] = r * jnp.exp(cum)
    bb_s[...] = b * w_inv
    kb_s[...] = kh * w_inv
    bh_s[...] = b * w_rem
    kh_s[...] = kh * w_rem
    v_s[...] = v
    wt_s[...] = jnp.exp(tot)

    lane_lo = lax.broadcasted_iota(jnp.int32, (chunk, PAIR), 1) < HEAD_DIM
    r2 = lax.broadcasted_iota(jnp.int32, (c2, c2), 0)
    q2 = lax.broadcasted_iota(jnp.int32, (c2, c2), 1)
    strict = (q2 & (chunk - 1)) < (r2 & (chunk - 1))
    incl = (q2 & (chunk - 1)) <= (r2 & (chunk - 1))
    eye2 = (r2 == q2).astype(F32)

    def stack(ref, rows, ls):
        x = ref[rows, ls]
        return jnp.concatenate([jnp.where(lane_lo, x, 0.0), jnp.where(lane_lo, 0.0, x)], axis=0).astype(BF16)

    def phase1_step(i, c):
        chains = [(2 * i + j, p) for j in range(2) for p in range(n_pairs)]
        n = range(len(chains))

        def stacks(ref):
            return [stack(ref, pl.ds(pl.multiple_of(blk * chunk, chunk), chunk), slice(p * PAIR, (p + 1) * PAIR))
                    for blk, p in chains]

        at, rt, bbm, kbm = stacks(at_s), stacks(rt_s), stacks(bb_s), stacks(kb_s)
        bhm, khm, vm = stacks(bh_s), stacks(kh_s), stacks(v_s)
        a_ab = [jnp.where(strict, _nt(at[c_], bbm[c_]), 0.0) for c_ in n]
        a_ak = [jnp.where(strict, _nt(at[c_], kbm[c_]), 0.0).astype(BF16) for c_ in n]
        a_rb = [jnp.where(incl, _nt(rt[c_], bbm[c_]), 0.0).astype(BF16) for c_ in n]
        a_rk = [jnp.where(incl, _nt(rt[c_], kbm[c_]), 0.0).astype(BF16) for c_ in n]
        tinv = [eye2 + a_ab[c_] for c_ in n]
        ap = [a_ab[c_].astype(BF16) for c_ in n]
        for _ in range(log_chunk - 1):
            ap = [_mm(ap[c_], ap[c_]).astype(BF16) for c_ in n]
            tinv = [tinv[c_] + _mm(tinv[c_].astype(BF16), ap[c_]) for c_ in n]
        av = [_mm(a_ak[c_], vm[c_]).astype(BF16) for c_ in n]
        pq = [_mm(tinv[c_].astype(BF16), jnp.concatenate([av[c_], at[c_]], axis=1)) for c_ in n]
        pm = [pq[c_][:, :PAIR].astype(BF16) for c_ in n]
        qm = [pq[c_][:, PAIR:].astype(BF16) for c_ in n]
        for c_, (blk, p) in enumerate(chains):
            gm_s[blk, p] = _tn(qm[c_], bhm[c_]).astype(BF16)
        for c_, (blk, p) in enumerate(chains):
            hm_s[blk, p] = _tn(jnp.concatenate([pm[c_], vm[c_]], axis=0),
                               jnp.concatenate([bhm[c_], khm[c_]], axis=0))
        for c_, (blk, p) in enumerate(chains):
            rp_s[blk, p] = (rt[c_].astype(F32) + _mm(a_rb[c_], qm[c_])).astype(BF16)
        for c_, (blk, p) in enumerate(chains):
            oi_s[blk, p] = _mm(a_rb[c_], pm[c_]) + _mm(a_rk[c_], vm[c_])
        return c

    lax.fori_loop(0, n_blk // 2, phase1_step, 0)

    def phase2_step(blk, c):
        row0 = pl.multiple_of(blk * chunk, chunk)
        b_i = blk // n_ck
        for p in range(n_pairs):
            ls = slice(p * PAIR, (p + 1) * PAIR)
            s = st[b_i, p]
            sb = s.astype(BF16)
            o_bd = oi_s[blk, p] + _nt(rp_s[blk, p], sb)
            o_s[pl.ds(row0, chunk), ls] = o_bd[:chunk] + o_bd[chunk:]
            st[b_i, p] = s * wt_s[pl.ds(row0, 1), ls] + _mm(sb, gm_s[blk, p]) + hm_s[blk, p]
        return c

    lax.fori_loop(0, n_blk, phase2_step, 0)

    o = o_s[...]
    mean = _mm(o.astype(BF16), head_sum) * (1.0 / HEAD_DIM)
    oc = o - mean
    var = _mm((oc * oc).astype(BF16), head_sum) * (1.0 / HEAD_DIM)
    o = oc * lax.rsqrt(var + GN_EPS) * gng_ref[...] + gnb_ref[...] + bonus_s[...]
    og_ref[...] = (o * g_s[...]).astype(BF16).reshape(bb_n, tt, d_r)

    @pl.when(t == pl.num_programs(1) - 1)
    def _():
        for b_i in range(bb_n):
            for p in range(n_pairs):
                s = st[b_i, p]
                wkv_ref[b_i, 2 * p] = s[:HEAD_DIM, :HEAD_DIM]
                wkv_ref[b_i, 2 * p + 1] = s[HEAD_DIM:, HEAD_DIM:]


def _rwkv(x, mixg, ws, sprev, mu, w0, wdu, a0, wau, wgu, kk, ka, rk, gng, gnb, s0, *, bb, tt, chunk):
    b, t_len, d = x.shape
    d_shift = ws.shape[1]
    d_r = w0.shape[1]
    n_heads = d_r // HEAD_DIM
    n_pairs = d_r // PAIR
    m = bb * tt
    n_blk = m // chunk
    assert b % bb == 0 and t_len % tt == 0 and tt % chunk == 0 and n_blk % 2 == 0
    assert chunk & (chunk - 1) == 0 and tt & (tt - 1) == 0 and chunk % 8 == 0
    tok = pltpu.VMEM((m, d_r), F32)
    return pl.pallas_call(
        functools.partial(_rwkv_body, chunk=chunk),
        grid=(b // bb, t_len // tt),
        in_specs=[
            pl.BlockSpec((bb, tt, d), lambda i, j: (i, j, 0)),
            _const_spec(mixg.shape), _const_spec(ws.shape),
            pl.BlockSpec((bb, 1, d_shift), lambda i, j: (i, 0, 0)),
            _const_spec(mu.shape), _const_spec(w0.shape), _const_spec(wdu.shape), _const_spec(a0.shape),
            _const_spec(wau.shape), _const_spec(wgu.shape), _const_spec(kk.shape), _const_spec(ka.shape),
            _const_spec(rk.shape), _const_spec(gng.shape), _const_spec(gnb.shape),
            pl.BlockSpec((bb, n_heads, HEAD_DIM, HEAD_DIM), lambda i, j: (i, 0, 0, 0)),
        ],
        out_specs=[
            pl.BlockSpec((bb, tt, d_r), lambda i, j: (i, j, 0)),
            pl.BlockSpec((bb, 1, d_shift), lambda i, j: (i, 0, 0)),
            pl.BlockSpec((bb, n_heads, HEAD_DIM, HEAD_DIM), lambda i, j: (i, 0, 0, 0)),
        ],
        out_shape=[
            jax.ShapeDtypeStruct((b, t_len, d_r), BF16),
            jax.ShapeDtypeStruct((b, 1, d_shift), F32),
            jax.ShapeDtypeStruct((b, n_heads, HEAD_DIM, HEAD_DIM), F32),
        ],
        scratch_shapes=[
            pltpu.VMEM((bb, 1, d_shift), F32),
            pltpu.VMEM((bb, n_pairs, PAIR, PAIR), F32),
        ] + [tok] * 11 + [
            pltpu.VMEM((n_blk, n_pairs, PAIR, PAIR), BF16),
            pltpu.VMEM((n_blk, n_pairs, PAIR, PAIR), F32),
            pltpu.VMEM((n_blk, n_pairs, 2 * chunk, PAIR), BF16),
            pltpu.VMEM((n_blk, n_pairs, 2 * chunk, PAIR), F32),
        ],
        compiler_params=_params("parallel", "arbitrary"),
        name="rwkv",
    )(x, mixg, ws, sprev, mu, w0, wdu, a0, wau, wgu, kk, ka, rk, gng, gnb, s0)


def _conv_body(x_ref, mixg_ref, wc_ref, glub_ref, cprev_ref, cw_ref, cb_ref, lng_ref, lnb_ref,
               y_ref, cnew_ref, ubuf, ushift):
    bb_n, tt, d = x_ref.shape
    m = bb_n * tt
    dc = cw_ref.shape[1]
    lo = CONV_PAD - (CONV_WIDTH - 1)
    t = pl.program_id(1)

    @pl.when(t == 0)
    def _():
        ubuf[:, lo:CONV_PAD, :] = cprev_ref[...]

    xn = _rms(x_ref[...].reshape(m, d), mixg_ref[...]).astype(BF16)
    zc = _mm(xn, wc_ref[...]) + glub_ref[...]
    u = zc[:, :dc] * jax.nn.sigmoid(zc[:, dc:])
    ubuf[:, CONV_PAD:CONV_PAD + tt, :] = u.reshape(bb_n, tt, dc)
    n_sh = tt + CONV_PAD - SUBLANES
    for sh in range(1, SUBLANES):
        ushift[sh - 1] = ubuf[:, sh:sh + n_sh, :]
    acc = jnp.zeros((bb_n, tt, dc), F32) + cb_ref[...]
    for kx in range(CONV_WIDTH):
        off = lo + kx
        sh, base = off % SUBLANES, off - off % SUBLANES
        win = ubuf[:, base:base + tt, :] if sh == 0 else ushift[sh - 1, :, base:base + tt, :]
        acc = acc + win * cw_ref[kx:kx + 1, :]
    tail = ubuf[:, lo + tt:CONV_PAD + tt, :]
    cnew_ref[...] = tail
    ubuf[:, lo:CONV_PAD, :] = tail
    c = acc.reshape(m, dc)
    cm = jnp.mean(c, axis=-1, keepdims=True)
    cc = c - cm
    cv = jnp.mean(cc * cc, axis=-1, keepdims=True)
    c = cc * lax.rsqrt(cv + LN_EPS) * lng_ref[...] + lnb_ref[...]
    y_ref[...] = (c * jax.nn.sigmoid(c)).astype(BF16).reshape(bb_n, tt, dc)


def _conv(x, mixg, wc, glub, cprev, cw, cb, lng, lnb, *, bb, tt):
    b, t_len, d = x.shape
    dc = cw.shape[1]
    return pl.pallas_call(
        _conv_body,
        grid=(b // bb, t_len // tt),
        in_specs=[
            pl.BlockSpec((bb, tt, d), lambda i, j: (i, j, 0)),
            _const_spec(mixg.shape), _const_spec(wc.shape), _const_spec(glub.shape),
            pl.BlockSpec((bb, CONV_WIDTH - 1, dc), lambda i, j: (i, 0, 0)),
            _const_spec(cw.shape), _const_spec(cb.shape), _const_spec(lng.shape), _const_spec(lnb.shape),
        ],
        out_specs=[
            pl.BlockSpec((bb, tt, dc), lambda i, j: (i, j, 0)),
            pl.BlockSpec((bb, CONV_WIDTH - 1, dc), lambda i, j: (i, 0, 0)),
        ],
        out_shape=[
            jax.ShapeDtypeStruct((b, t_len, dc), BF16),
            jax.ShapeDtypeStruct((b, CONV_WIDTH - 1, dc), F32),
        ],
        scratch_shapes=[pltpu.VMEM((bb, CONV_PAD + tt, dc), F32),
                        pltpu.VMEM((SUBLANES - 1, bb, CONV_PAD + tt - SUBLANES, dc), F32)],
        compiler_params=_params("parallel", "arbitrary"),
        name="conv",
    )(x, mixg, wc, glub, cprev, cw, cb, lng, lnb)


def _xattn_body(x_ref, mixg_ref, wq_ref, kt_ref, vt_ref, o_ref, *, rows, group):
    bb_n, tt, d = x_ref.shape
    m = bb_n * tt
    dh = kt_ref.shape[1] // N_XATTN_HEADS
    xn = _rms(x_ref[...].reshape(m, d), mixg_ref[...]).astype(BF16)
    q = (_mm(xn, wq_ref[...]) * (dh ** -0.5)).astype(BF16)
    n_blocks = m // rows
    for g0 in range(0, n_blocks, group):
        chains = [(r, h) for r in range(g0, min(g0 + group, n_blocks)) for h in range(N_XATTN_HEADS)]
        hs = lambda h: slice(h * dh, (h + 1) * dh)
        seq = lambda r: (r * rows) // tt
        s = [_mm(q[r * rows:(r + 1) * rows, hs(h)], kt_ref[seq(r), hs(h), :].astype(BF16)) for r, h in chains]
        p = [jnp.exp(x - jnp.max(x, axis=-1, keepdims=True)) for x in s]
        l = [jnp.sum(x, axis=-1, keepdims=True) for x in p]
        o = [_nt(p[i].astype(BF16), vt_ref[seq(r), hs(h), :].astype(BF16)) / l[i] for i, (r, h) in enumerate(chains)]
        for j, r in enumerate(range(g0, min(g0 + group, n_blocks))):
            t0 = (r * rows) % tt
            heads = o[j * N_XATTN_HEADS:(j + 1) * N_XATTN_HEADS]
            o_ref[seq(r), t0:t0 + rows, :] = jnp.concatenate(heads, axis=-1).astype(BF16)


def _xattn(x, mixg, wq, kt, vt, *, bb, tt, rows, group):
    b, t_len, d = x.shape
    dx, n_mem = kt.shape[1:]
    assert tt % rows == 0
    return pl.pallas_call(
        functools.partial(_xattn_body, rows=rows, group=group),
        grid=(b // bb, t_len // tt),
        in_specs=[
            pl.BlockSpec((bb, tt, d), lambda i, j: (i, j, 0)),
            _const_spec(mixg.shape), _const_spec(wq.shape),
            pl.BlockSpec((bb, dx, n_mem), lambda i, j: (i, 0, 0)),
            pl.BlockSpec((bb, dx, n_mem), lambda i, j: (i, 0, 0)),
        ],
        out_specs=pl.BlockSpec((bb, tt, dx), lambda i, j: (i, j, 0)),
        out_shape=jax.ShapeDtypeStruct((b, t_len, dx), BF16),
        compiler_params=_params("parallel", "arbitrary"),
        name="xattn",
    )(x, mixg, wq, kt, vt)


def _merge_body(x_ref, mixg_ref, wg_ref, ya_ref, yb_ref, yc_ref, wro_ref, wco_ref, wxo_ref, wo_ref, o_ref):
    d = x_ref.shape[1]
    x = x_ref[...]
    xn = _rms(x, mixg_ref[...]).astype(BF16)
    merged = jnp.zeros_like(x)
    for i, (y_ref, w_ref) in enumerate(((ya_ref, wro_ref), (yb_ref, wco_ref), (yc_ref, wxo_ref))):
        gate = jax.nn.sigmoid(_mm(xn, wg_ref[:, i * d:(i + 1) * d]))
        merged = merged + gate * _mm(y_ref[...], w_ref[...])
    o_ref[...] = x + _mm(merged.astype(BF16), wo_ref[...])


def _merge(x, mixg, wg, ya, yb, yc, wro, wco, wxo, wo, *, tm=512):
    n, d = x.shape
    tm = min(tm, n)
    row = lambda w: pl.BlockSpec((tm, w), lambda i: (i, 0))
    return pl.pallas_call(
        _merge_body,
        grid=(n // tm,),
        in_specs=[row(d), _const_spec(mixg.shape), _const_spec(wg.shape), row(ya.shape[1]), row(yb.shape[1]),
                  row(yc.shape[1]), _const_spec(wro.shape), _const_spec(wco.shape), _const_spec(wxo.shape),
                  _const_spec(wo.shape)],
        out_specs=row(d),
        out_shape=jax.ShapeDtypeStruct((n, d), F32),
        compiler_params=_params("parallel"),
        name="merge",
    )(x, mixg, wg, ya, yb, yc, wro, wco, wxo, wo)


def _layer(x, shift_prev, conv_prev, wkv0, mem_kt, mem_vt, p, final_g, *, last, bb, tt, chunk, xa):
    b, t_len, d = x.shape
    n = b * t_len
    x1 = _ffn(x.reshape(n, d), p['ffn1_norm'], p['ffn1_w_up'], p['ffn1_w_down'], final_g, final_norm=False)
    x1_3 = x1.reshape(b, t_len, d)
    og, shift, wkv = _rwkv(x1_3, p['mix_norm'], p['w_s'], shift_prev[:, None, :], p['mu_shift'], p['w0'],
                           p['w_decay_up'], p['a0'], p['w_a_up'], p['w_g_up'], p['k_k'], p['k_a'], p['r_k'],
                           p['gn_g'], p['gn_b'], wkv0, bb=bb, tt=tt, chunk=chunk)
    cb, conv = _conv(x1_3, p['mix_norm'], p['w_c'], p['glu_b'], conv_prev, p['conv_w'], p['conv_b'],
                     p['conv_ln_g'], p['conv_ln_b'], bb=bb, tt=tt)
    oc = _xattn(x1_3, p['mix_norm'], p['w_q'], mem_kt, mem_vt, **xa)
    x2 = _merge(x1, p['mix_norm'], p['w_g'], og.reshape(n, -1), cb.reshape(n, -1), oc.reshape(n, -1),
                p['w_rwkv_out'], p['w_conv_out'], p['w_xattn_out'], p['w_o'])
    x3 = _ffn(x2, p['ffn2_norm'], p['ffn2_w_up'], p['ffn2_w_down'], final_g, final_norm=last)
    return x3.reshape(b, t_len, d), wkv, shift[:, 0, :], conv


def kernel(x_prompt, mem_prompt, x_sample, state_wkv, state_shift, state_conv, cache_mem_k, cache_mem_v,
           ffn1_norm, ffn1_w_up, ffn1_w_down, mix_norm, w_in, mu_shift, w0, w_decay_up, a0, w_a_up, w_g_up,
           k_k, k_a, r_k, gn_g, gn_b, w_rwkv_out, glu_b, conv_w, conv_b, conv_ln_g, conv_ln_b, w_conv_out,
           w_mem_kv, w_xattn_out, w_o, ffn2_norm, ffn2_w_up, ffn2_w_down, final_norm):
    depth = w_in.shape[0]
    d_model = w_in.shape[1]
    d_r = w0.shape[1]
    d_shift = mu_shift.shape[1]
    d_conv = conv_w.shape[2]
    d_x = w_xattn_out.shape[1]
    n_heads = d_r // HEAD_DIM
    o1 = d_shift
    o2 = o1 + 2 * d_conv
    o3 = o2 + d_x
    row = lambda a: a.astype(F32).reshape(1, -1)
    final_g = row(final_norm)

    layers = []
    for l in range(depth):
        layers.append({
            'ffn1_norm': row(ffn1_norm[l]), 'ffn1_w_up': ffn1_w_up[l].astype(BF16),
            'ffn1_w_down': ffn1_w_down[l].astype(BF16), 'mix_norm': row(mix_norm[l]),
            'w_s': w_in[l, :, :o1].astype(BF16), 'w_c': w_in[l, :, o1:o2].astype(BF16),
            'w_q': w_in[l, :, o2:o3].astype(BF16), 'w_g': w_in[l, :, o3:].astype(BF16),
            'mu_shift': row(mu_shift[l]), 'w0': row(w0[l]), 'w_decay_up': w_decay_up[l].astype(F32),
            'a0': row(a0[l]), 'w_a_up': w_a_up[l].astype(F32), 'w_g_up': w_g_up[l].astype(F32),
            'k_k': row(k_k[l]), 'k_a': row(k_a[l]), 'r_k': row(r_k[l]), 'gn_g': row(gn_g[l]),
            'gn_b': row(gn_b[l]), 'w_rwkv_out': w_rwkv_out[l].astype(BF16), 'glu_b': row(glu_b[l]),
            'conv_w': conv_w[l].astype(F32), 'conv_b': row(conv_b[l]), 'conv_ln_g': row(conv_ln_g[l]),
            'conv_ln_b': row(conv_ln_b[l]), 'w_conv_out': w_conv_out[l].astype(BF16),
            'w_mem_kv': w_mem_kv[l].astype(BF16), 'w_mem_kv_t': w_mem_kv[l].T.astype(BF16), 'w_xattn_out': w_xattn_out[l].astype(BF16),
            'w_o': w_o[l].astype(BF16), 'ffn2_norm': row(ffn2_norm[l]),
            'ffn2_w_up': ffn2_w_up[l].astype(BF16), 'ffn2_w_down': ffn2_w_down[l].astype(BF16),
        })

    bp, n_mem, _ = mem_prompt.shape
    xp = x_prompt.astype(F32)
    wkv_p, shift_p, conv_p, mk_p, mv_p = [], [], [], [], []
    for l in range(depth):
        p = layers[l]
        mk, mv, mkt, mvt = _memkv(mem_prompt.astype(F32), p['w_mem_kv'], p['w_mem_kv_t'])
        xp, wkv, sh, cv = _layer(
            xp, jnp.zeros((bp, d_shift), F32), jnp.zeros((bp, CONV_WIDTH - 1, d_conv), F32),
            jnp.zeros((bp, n_heads, HEAD_DIM, HEAD_DIM), F32), mkt, mvt, p, final_g,
            last=(l == depth - 1), bb=1, tt=256, chunk=64, xa=dict(bb=1, tt=512, rows=256, group=2))
        wkv_p.append(wkv)
        shift_p.append(sh)
        conv_p.append(cv)
        mk_p.append(mk.reshape(bp, n_mem, N_XATTN_HEADS, d_x // N_XATTN_HEADS))
        mv_p.append(mv.reshape(bp, n_mem, N_XATTN_HEADS, d_x // N_XATTN_HEADS))

    bs, ts, _ = x_sample.shape
    xs = x_sample.astype(F32)
    wkv_s, shift_s, conv_s = [], [], []
    for l in range(depth):
        kt = jnp.transpose(cache_mem_k[l].astype(F32), (0, 2, 3, 1)).reshape(bs, d_x, n_mem)
        vt = jnp.transpose(cache_mem_v[l].astype(F32), (0, 2, 3, 1)).reshape(bs, d_x, n_mem)
        xs, wkv, sh, cv = _layer(
            xs, state_shift[l].astype(F32), state_conv[l].astype(F32), state_wkv[l].astype(F32), kt, vt,
            layers[l], final_g, last=(l == depth - 1), bb=16, tt=ts, chunk=ts,
            xa=dict(bb=16, tt=ts, rows=ts, group=4))
        wkv_s.append(wkv)
        shift_s.append(sh)
        conv_s.append(cv)

    return (xp.astype(x_prompt.dtype), xs.astype(x_sample.dtype), jnp.stack(wkv_p), jnp.stack(shift_p),
            jnp.stack(conv_p), jnp.stack(mk_p), jnp.stack(mv_p), jnp.stack(wkv_s), jnp.stack(shift_s),
            jnp.stack(conv_s))
```

```python
import functools
import math

import jax
import jax.numpy as jnp
from jax import lax
from jax.experimental import pallas as pl
from jax.experimental.pallas import tpu as pltpu

F32 = jnp.float32
BF16 = jnp.bfloat16
HI = lax.Precision.HIGHEST

HEAD_DIM = 64
PAIR = 2 * HEAD_DIM
N_XATTN_HEADS = 4
CONV_WIDTH = 31
SUBLANES = 8
CONV_PAD = 32
LORA_DECAY = 64
LORA_A = 64
RMS_EPS = 1e-6
LN_EPS = 1e-5
GN_EPS = 64e-5
VMEM_LIMIT = 56 * 1024 * 1024


def _params(*sem):
    return pltpu.CompilerParams(dimension_semantics=sem, vmem_limit_bytes=VMEM_LIMIT)


def _rms(x, g):
    return x * lax.rsqrt(jnp.mean(x * x, axis=-1, keepdims=True) + RMS_EPS) * g


def _mm(a, b, precision=None):
    return jnp.dot(a, b, preferred_element_type=F32, precision=precision)


def _nt(a, b):
    return lax.dot_general(a, b, (((1,), (1,)), ((), ())), preferred_element_type=F32)


def _tn(a, b):
    return lax.dot_general(a, b, (((0,), (0,)), ((), ())), preferred_element_type=F32)


def _split(x):
    hi = x.astype(BF16)
    return hi, (x - hi.astype(F32)).astype(BF16)


def _exact_lhs_dot(sel, x):
    hi, lo = _split(x)
    return _mm(sel, hi) + _mm(sel, lo)


def _dot3(x, w):
    x_hi, x_lo = _split(x)
    w_hi, w_lo = _split(w)
    return _mm(x_hi, w_hi) + _mm(x_lo, w_hi) + _mm(x_hi, w_lo)


def _dot1(x, w):
    return _mm(x.astype(BF16), w.astype(BF16))


def _const_spec(shape):
    nd = len(shape)
    return pl.BlockSpec(shape, lambda *_: (0,) * nd)


def _ffn_body(x_ref, g_ref, wg_ref, wu_ref, wd_ref, fg_ref, o_ref, xn_ref, acc_ref, *, final_norm):
    c = pl.program_id(1)

    @pl.when(c == 0)
    def _():
        xn_ref[...] = _rms(x_ref[...], g_ref[...]).astype(BF16)
        acc_ref[...] = jnp.zeros_like(acc_ref)

    xn = xn_ref[...]
    hg = _mm(xn, wg_ref[...])
    hu = _mm(xn, wu_ref[...])
    h = (hg * jax.nn.sigmoid(hg) * hu).astype(BF16)
    acc_ref[...] += _mm(h, wd_ref[...])

    @pl.when(c == pl.num_programs(1) - 1)
    def _():
        y = x_ref[...] + 0.5 * acc_ref[...]
        if final_norm:
            y = _rms(y, fg_ref[...])
        o_ref[...] = y


def _ffn(x, norm_g, w_up, w_down, final_g, *, final_norm, tm=512, fk=1408):
    n, d = x.shape
    tm = min(tm, n)
    d_ff = w_down.shape[0]
    nc = d_ff // fk
    assert n % tm == 0 and d_ff % fk == 0
    return pl.pallas_call(
        functools.partial(_ffn_body, final_norm=final_norm),
        grid=(n // tm, nc),
        in_specs=[
            pl.BlockSpec((tm, d), lambda i, c: (i, 0)),
            _const_spec((1, d)),
            pl.BlockSpec((d, fk), lambda i, c: (0, c)),
            pl.BlockSpec((d, fk), lambda i, c: (0, c + nc)),
            pl.BlockSpec((fk, d), lambda i, c: (c, 0)),
            _const_spec((1, d)),
        ],
        out_specs=pl.BlockSpec((tm, d), lambda i, c: (i, 0)),
        out_shape=jax.ShapeDtypeStruct((n, d), F32),
        scratch_shapes=[pltpu.VMEM((tm, d), BF16), pltpu.VMEM((tm, d), F32)],
        compiler_params=_params("parallel", "arbitrary"),
        name="ffn_final" if final_norm else "ffn",
    )(x, norm_g, w_up, w_up, w_down, final_g)


def _memkv_body(m_ref, w_ref, wt_ref, k_ref, v_ref, kt_ref, vt_ref):
    mem = m_ref[0].astype(BF16)
    dk = k_ref.shape[-1]
    kv = _mm(mem, w_ref[...])
    k_ref[0] = kv[:, :dk]
    v_ref[0] = kv[:, dk:]
    kvt = _nt(wt_ref[...], mem)
    kt_ref[0] = kvt[:dk].astype(BF16)
    vt_ref[0] = kvt[dk:].astype(BF16)


def _memkv(mem, w, wt):
    b, n_mem, d = mem.shape
    dk = w.shape[1] // 2
    return pl.pallas_call(
        _memkv_body,
        grid=(b,),
        in_specs=[pl.BlockSpec((1, n_mem, d), lambda i: (i, 0, 0)), _const_spec(w.shape), _const_spec(wt.shape)],
        out_specs=[pl.BlockSpec((1, n_mem, dk), lambda i: (i, 0, 0))] * 2
        + [pl.BlockSpec((1, dk, n_mem), lambda i: (i, 0, 0))] * 2,
        out_shape=[jax.ShapeDtypeStruct((b, n_mem, dk), F32)] * 2
        + [jax.ShapeDtypeStruct((b, dk, n_mem), BF16)] * 2,
        compiler_params=_params("parallel"),
        name="memkv",
    )(mem, w, wt)


def _rwkv_body(x_ref, mixg_ref, ws_ref, sprev_ref, mu_ref, w0_ref, wdu_ref, a0_ref, wau_ref, wgu_ref,
               kk_ref, ka_ref, rk_ref, gng_ref, gnb_ref, s0_ref, hsum_ref, tri_ref, blk_ref,
               og_ref, sout_ref, wkv_ref,
               carry, st, at_s, rt_s, bb_s, kb_s, bh_s, kh_s, v_s, wt_s, bonus_s, g_s, o_s,
               gm_s, hm_s, rp_s, oi_s, *, chunk):
    bb_n, tt, d = x_ref.shape
    m = bb_n * tt
    d_shift = ws_ref.shape[1]
    d_r = w0_ref.shape[1]
    n_pairs = d_r // PAIR
    n_ck = tt // chunk
    n_blk = m // chunk
    log_chunk = int(math.log2(chunk))
    c2 = 2 * chunk
    t = pl.program_id(1)

    @pl.when(t == 0)
    def _():
        carry[...] = sprev_ref[...]
        zero = jnp.zeros((HEAD_DIM, HEAD_DIM), F32)
        for b_i in range(bb_n):
            for p in range(n_pairs):
                top = jnp.concatenate([s0_ref[b_i, 2 * p], zero], axis=1)
                bot = jnp.concatenate([zero, s0_ref[b_i, 2 * p + 1]], axis=1)
                st[b_i, p] = jnp.concatenate([top, bot], axis=0)

    xn = _rms(x_ref[...].reshape(m, d), mixg_ref[...]).astype(BF16)
    zs = _mm(xn, ws_ref[...])
    rolled = pltpu.roll(zs, 1, axis=0)
    prev0 = jnp.broadcast_to(carry[...], (bb_n, tt, d_shift)).reshape(m, d_shift)
    row = lax.broadcasted_iota(jnp.int32, (m, d_shift), 0)
    prev = jnp.where((row & (tt - 1)) == 0, prev0, rolled)
    last = zs.reshape(bb_n, tt, d_shift)[:, tt - 1:tt, :]
    carry[...] = last
    sout_ref[...] = last
    xm = zs + (prev - zs) * mu_ref[...]

    c1, c2_, c3 = d_r, 2 * d_r, 3 * d_r
    c4 = c3 + LORA_DECAY
    c5 = c4 + LORA_A
    r, k, v = xm[:, :c1], xm[:, c1:c2_], xm[:, c2_:c3]
    wd, ad, gd = xm[:, c3:c4], xm[:, c4:c5], xm[:, c5:]
    dec_in = w0_ref[...] + _dot3(jnp.tanh(wd), wdu_ref[...])
    neg = -dec_in
    softplus = jnp.maximum(neg, 0.0) + jnp.log(1.0 + jnp.exp(-jnp.abs(neg)))
    ld = -jnp.exp(-softplus - 0.5)
    a = jax.nn.sigmoid(a0_ref[...] + _dot1(ad, wau_ref[...]))
    g_s[...] = _dot1(jax.nn.sigmoid(gd), wgu_ref[...])
    kh = k * (1.0 + (a - 1.0) * ka_ref[...])

    head_sum = hsum_ref[...]
    kkraw = k * kk_ref[...]
    kk = kkraw / jnp.maximum(jnp.sqrt(_mm((kkraw * kkraw).astype(BF16), head_sum)), 1e-12)
    b = kk * a
    bonus_s[...] = _mm((r * kh * rk_ref[...]).astype(BF16), head_sum) * v

    tdt = at_s.dtype
    v_s[...] = v.astype(tdt)
    slab = tri_ref.shape[0]
    for s0 in range(0, m, slab):
        sl = slice(s0, s0 + slab)
        ld_c = ld[sl]
        cum = _exact_lhs_dot(tri_ref[...], ld_c)
        tot = _exact_lhs_dot(blk_ref[...], ld_c)
        w_inv = jnp.exp(-cum)
        w_rem = jnp.exp(tot - cum)
        at_s[sl, :] = (-kk[sl] * jnp.exp(cum - ld_c)).astype(tdt)
        rt_s[sl, :] = (r[sl] * jnp.exp(cum)).astype(tdt)
        bb_s[sl, :] = (b[sl] * w_inv).astype(tdt)
        kb_s[sl, :] = (kh[sl] * w_inv).astype(tdt)
        bh_s[sl, :] = (b[sl] * w_rem).astype(tdt)
        kh_s[sl, :] = (kh[sl] * w_rem).astype(tdt)
        wt_s[sl, :] = jnp.exp(tot)

    lane_lo = lax.broadcasted_iota(jnp.int32, (chunk, PAIR), 1) < HEAD_DIM
    r2 = lax.broadcasted_iota(jnp.int32, (c2, c2), 0)
    q2 = lax.broadcasted_iota(jnp.int32, (c2, c2), 1)
    strict = (q2 & (chunk - 1)) < (r2 & (chunk - 1))
    incl = (q2 & (chunk - 1)) <= (r2 & (chunk - 1))
    eye2 = (r2 == q2).astype(F32)
    wide = c2 % PAIR == 0

    def stack(ref, rows, ls):
        x = ref[rows, ls]
        return jnp.concatenate([jnp.where(lane_lo, x, 0.0), jnp.where(lane_lo, 0.0, x)], axis=0).astype(BF16)

    def phase1_step(i, c):
        chains = [(2 * i + j, p) for j in range(2) for p in range(n_pairs)]
        n = range(len(chains))

        def stacks(ref):
            return [stack(ref, pl.ds(pl.multiple_of(blk * chunk, chunk), chunk), slice(p * PAIR, (p + 1) * PAIR))
                    for blk, p in chains]

        at, rt, bbm, kbm = stacks(at_s), stacks(rt_s), stacks(bb_s), stacks(kb_s)
        bhm, khm, vm = stacks(bh_s), stacks(kh_s), stacks(v_s)
        if wide:
            a4 = [_nt(jnp.concatenate([at[c_], rt[c_]], axis=0), jnp.concatenate([bbm[c_], kbm[c_]], axis=0))
                  for c_ in n]
            a_ab = [jnp.where(strict, a4[c_][:c2, :c2], 0.0) for c_ in n]
            a_ak = [jnp.where(strict, a4[c_][:c2, c2:], 0.0).astype(BF16) for c_ in n]
            a_rb = [jnp.where(incl, a4[c_][c2:, :c2], 0.0).astype(BF16) for c_ in n]
            a_rk = [jnp.where(incl, a4[c_][c2:, c2:], 0.0).astype(BF16) for c_ in n]
        else:
            a_ab = [jnp.where(strict, _nt(at[c_], bbm[c_]), 0.0) for c_ in n]
            a_ak = [jnp.where(strict, _nt(at[c_], kbm[c_]), 0.0).astype(BF16) for c_ in n]
            a_rb = [jnp.where(incl, _nt(rt[c_], bbm[c_]), 0.0).astype(BF16) for c_ in n]
            a_rk = [jnp.where(incl, _nt(rt[c_], kbm[c_]), 0.0).astype(BF16) for c_ in n]
        tinv = [eye2 + a_ab[c_] for c_ in n]
        ap = [a_ab[c_].astype(BF16) for c_ in n]
        ap = [_mm(ap[c_], ap[c_]).astype(BF16) for c_ in n]
        for i_sq in range(1, log_chunk):
            if i_sq == log_chunk - 1:
                tinv = [tinv[c_] + _mm(ap[c_], tinv[c_].astype(BF16)) for c_ in n]
            elif wide:
                x = [_mm(ap[c_], jnp.concatenate([ap[c_], tinv[c_].astype(BF16)], axis=1)) for c_ in n]
                ap = [x[c_][:, :c2].astype(BF16) for c_ in n]
                tinv = [tinv[c_] + x[c_][:, c2:] for c_ in n]
            else:
                tinv = [tinv[c_] + _mm(ap[c_], tinv[c_].astype(BF16)) for c_ in n]
                ap = [_mm(ap[c_], ap[c_]).astype(BF16) for c_ in n]
        av = [_mm(a_ak[c_], vm[c_]).astype(BF16) for c_ in n]
        pq = [_mm(tinv[c_].astype(BF16), jnp.concatenate([av[c_], at[c_]], axis=1)) for c_ in n]
        pm = [pq[c_][:, :PAIR].astype(BF16) for c_ in n]
        qm = [pq[c_][:, PAIR:].astype(BF16) for c_ in n]
        for c_, (blk, p) in enumerate(chains):
            gm_s[blk, p] = _tn(qm[c_], bhm[c_]).astype(BF16)
        for c_, (blk, p) in enumerate(chains):
            hm_s[blk, p] = _tn(jnp.concatenate([pm[c_], vm[c_]], axis=0),
                               jnp.concatenate([bhm[c_], khm[c_]], axis=0))
        rq = [_mm(a_rb[c_], jnp.concatenate([qm[c_], pm[c_]], axis=1)) for c_ in n]
        for c_, (blk, p) in enumerate(chains):
            rp_s[blk, p] = (rt[c_].astype(F32) + rq[c_][:, :PAIR]).astype(BF16)
        for c_, (blk, p) in enumerate(chains):
            oi_s[blk, p] = rq[c_][:, PAIR:] + _mm(a_rk[c_], vm[c_])
        return c

    lax.fori_loop(0, n_blk // 2, phase1_step, 0)

    def phase2_step(blk, c):
        row0 = pl.multiple_of(blk * chunk, chunk)
        b_i = blk // n_ck
        pairs = range(n_pairs)
        ls = [slice(p * PAIR, (p + 1) * PAIR) for p in pairs]
        s = [st[b_i, p] for p in pairs]
        sb = [s[p].astype(BF16) for p in pairs]
        s_new = [s[p] * wt_s[pl.ds(row0, 1), ls[p]] + _mm(sb[p], gm_s[blk, p]) + hm_s[blk, p] for p in pairs]
        o_bd = [oi_s[blk, p] + _nt(rp_s[blk, p], sb[p]) for p in pairs]
        for p in pairs:
            st[b_i, p] = s_new[p]
        for p in pairs:
            o_s[pl.ds(row0, chunk), ls[p]] = o_bd[p][:chunk] + o_bd[p][chunk:]
        return c

    lax.fori_loop(0, n_blk, phase2_step, 0)

    o = o_s[...]
    mean = _mm(o.astype(BF16), head_sum) * (1.0 / HEAD_DIM)
    oc = o - mean
    var = _mm((oc * oc).astype(BF16), head_sum) * (1.0 / HEAD_DIM)
    o = oc * lax.rsqrt(var + GN_EPS) * gng_ref[...] + gnb_ref[...] + bonus_s[...]
    og_ref[...] = (o * g_s[...]).astype(BF16).reshape(bb_n, tt, d_r)

    @pl.when(t == pl.num_programs(1) - 1)
    def _():
        for b_i in range(bb_n):
            for p in range(n_pairs):
                s = st[b_i, p]
                wkv_ref[b_i, 2 * p] = s[:HEAD_DIM, :HEAD_DIM]
                wkv_ref[b_i, 2 * p + 1] = s[HEAD_DIM:, HEAD_DIM:]


def _rwkv(x, mixg, ws, sprev, mu, w0, wdu, a0, wau, wgu, kk, ka, rk, gng, gnb, s0, *, bb, tt, chunk):
    b, t_len, d = x.shape
    d_shift = ws.shape[1]
    d_r = w0.shape[1]
    n_heads = d_r // HEAD_DIM
    n_pairs = d_r // PAIR
    m = bb * tt
    n_blk = m // chunk
    assert b % bb == 0 and t_len % tt == 0 and tt % chunk == 0 and n_blk % 2 == 0
    assert chunk & (chunk - 1) == 0 and tt & (tt - 1) == 0 and chunk % 8 == 0
    tok = pltpu.VMEM((m, d_r), F32)
    tok_mm = pltpu.VMEM((m, d_r), BF16 if chunk % (2 * SUBLANES) == 0 else F32)
    lane = jnp.arange(d_r)
    head_sum = (lane[:, None] // HEAD_DIM == lane[None, :] // HEAD_DIM).astype(BF16)
    rows = jnp.arange(chunk if chunk >= HEAD_DIM else m)
    same_chunk = rows[:, None] // chunk == rows[None, :] // chunk
    tri = (same_chunk & (rows[None, :] <= rows[:, None])).astype(BF16)
    blk = same_chunk.astype(BF16)
    return pl.pallas_call(
        functools.partial(_rwkv_body, chunk=chunk),
        grid=(b // bb, t_len // tt),
        in_specs=[
            pl.BlockSpec((bb, tt, d), lambda i, j: (i, j, 0)),
            _const_spec(mixg.shape), _const_spec(ws.shape),
            pl.BlockSpec((bb, 1, d_shift), lambda i, j: (i, 0, 0)),
            _const_spec(mu.shape), _const_spec(w0.shape), _const_spec(wdu.shape), _const_spec(a0.shape),
            _const_spec(wau.shape), _const_spec(wgu.shape), _const_spec(kk.shape), _const_spec(ka.shape),
            _const_spec(rk.shape), _const_spec(gng.shape), _const_spec(gnb.shape),
            pl.BlockSpec((bb, n_heads, HEAD_DIM, HEAD_DIM), lambda i, j: (i, 0, 0, 0)),
            _const_spec(head_sum.shape), _const_spec(tri.shape), _const_spec(blk.shape),
        ],
        out_specs=[
            pl.BlockSpec((bb, tt, d_r), lambda i, j: (i, j, 0)),
            pl.BlockSpec((bb, 1, d_shift), lambda i, j: (i, 0, 0)),
            pl.BlockSpec((bb, n_heads, HEAD_DIM, HEAD_DIM), lambda i, j: (i, 0, 0, 0)),
        ],
        out_shape=[
            jax.ShapeDtypeStruct((b, t_len, d_r), BF16),
            jax.ShapeDtypeStruct((b, 1, d_shift), F32),
            jax.ShapeDtypeStruct((b, n_heads, HEAD_DIM, HEAD_DIM), F32),
        ],
        scratch_shapes=[
            pltpu.VMEM((bb, 1, d_shift), F32),
            pltpu.VMEM((bb, n_pairs, PAIR, PAIR), F32),
        ] + [tok_mm] * 7 + [tok] * 4 + [
            pltpu.VMEM((n_blk, n_pairs, PAIR, PAIR), BF16),
            pltpu.VMEM((n_blk, n_pairs, PAIR, PAIR), F32),
            pltpu.VMEM((n_blk, n_pairs, 2 * chunk, PAIR), BF16),
            pltpu.VMEM((n_blk, n_pairs, 2 * chunk, PAIR), F32),
        ],
        compiler_params=_params("parallel", "arbitrary"),
        name="rwkv",
    )(x, mixg, ws, sprev, mu, w0, wdu, a0, wau, wgu, kk, ka, rk, gng, gnb, s0, head_sum, tri, blk)


def _conv_body(x_ref, mixg_ref, wc_ref, glub_ref, cprev_ref, cw_ref, cb_ref, lng_ref, lnb_ref,
               y_ref, cnew_ref, ubuf, ushift):
    bb_n, tt, d = x_ref.shape
    m = bb_n * tt
    dc = cw_ref.shape[1]
    lo = CONV_PAD - (CONV_WIDTH - 1)
    t = pl.program_id(1)

    @pl.when(t == 0)
    def _():
        ubuf[:, lo:CONV_PAD, :] = cprev_ref[...]

    xn = _rms(x_ref[...].reshape(m, d), mixg_ref[...]).astype(BF16)
    zc = _mm(xn, wc_ref[...]) + glub_ref[...]
    u = zc[:, :dc] * jax.nn.sigmoid(zc[:, dc:])
    ubuf[:, CONV_PAD:CONV_PAD + tt, :] = u.reshape(bb_n, tt, dc)
    n_sh = tt + CONV_PAD - SUBLANES
    for sh in range(1, SUBLANES):
        ushift[sh - 1] = ubuf[:, sh:sh + n_sh, :]
    acc = jnp.zeros((bb_n, tt, dc), F32) + cb_ref[...]
    for kx in range(CONV_WIDTH):
        off = lo + kx
        sh, base = off % SUBLANES, off - off % SUBLANES
        win = ubuf[:, base:base + tt, :] if sh == 0 else ushift[sh - 1, :, base:base + tt, :]
        acc = acc + win * cw_ref[kx:kx + 1, :]
    tail = ubuf[:, lo + tt:CONV_PAD + tt, :]
    cnew_ref[...] = tail
    ubuf[:, lo:CONV_PAD, :] = tail
    c = acc.reshape(m, dc)
    cm = jnp.mean(c, axis=-1, keepdims=True)
    cc = c - cm
    cv = jnp.mean(cc * cc, axis=-1, keepdims=True)
    c = cc * lax.rsqrt(cv + LN_EPS) * lng_ref[...] + lnb_ref[...]
    y_ref[...] = (c * jax.nn.sigmoid(c)).astype(BF16).reshape(bb_n, tt, dc)


def _conv(x, mixg, wc, glub, cprev, cw, cb, lng, lnb, *, bb, tt):
    b, t_len, d = x.shape
    dc = cw.shape[1]
    return pl.pallas_call(
        _conv_body,
        grid=(b // bb, t_len // tt),
        in_specs=[
            pl.BlockSpec((bb, tt, d), lambda i, j: (i, j, 0)),
            _const_spec(mixg.shape), _const_spec(wc.shape), _const_spec(glub.shape),
            pl.BlockSpec((bb, CONV_WIDTH - 1, dc), lambda i, j: (i, 0, 0)),
            _const_spec(cw.shape), _const_spec(cb.shape), _const_spec(lng.shape), _const_spec(lnb.shape),
        ],
        out_specs=[
            pl.BlockSpec((bb, tt, dc), lambda i, j: (i, j, 0)),
            pl.BlockSpec((bb, CONV_WIDTH - 1, dc), lambda i, j: (i, 0, 0)),
        ],
        out_shape=[
            jax.ShapeDtypeStruct((b, t_len, dc), BF16),
            jax.ShapeDtypeStruct((b, CONV_WIDTH - 1, dc), F32),
        ],
        scratch_shapes=[pltpu.VMEM((bb, CONV_PAD + tt, dc), F32),
                        pltpu.VMEM((SUBLANES - 1, bb, CONV_PAD + tt - SUBLANES, dc), F32)],
        compiler_params=_params("parallel", "arbitrary"),
        name="conv",
    )(x, mixg, wc, glub, cprev, cw, cb, lng, lnb)


def _xattn_body(x_ref, mixg_ref, wq_ref, kt_ref, vt_ref, o_ref, *, rows, group):
    bb_n, tt, d = x_ref.shape
    m = bb_n * tt
    dh = kt_ref.shape[1] // N_XATTN_HEADS
    xn = _rms(x_ref[...].reshape(m, d), mixg_ref[...]).astype(BF16)
    q = (_mm(xn, wq_ref[...]) * (dh ** -0.5)).astype(BF16)
    n_blocks = m // rows
    for g0 in range(0, n_blocks, group):
        chains = [(r, h) for r in range(g0, min(g0 + group, n_blocks)) for h in range(N_XATTN_HEADS)]
        hs = lambda h: slice(h * dh, (h + 1) * dh)
        seq = lambda r: (r * rows) // tt
        s = [_mm(q[r * rows:(r + 1) * rows, hs(h)], kt_ref[seq(r), hs(h), :].astype(BF16)) for r, h in chains]
        p = [jnp.exp(x - jnp.max(x, axis=-1, keepdims=True)) for x in s]
        l = [jnp.sum(x, axis=-1, keepdims=True) for x in p]
        o = [_nt(p[i].astype(BF16), vt_ref[seq(r), hs(h), :].astype(BF16)) / l[i] for i, (r, h) in enumerate(chains)]
        for j, r in enumerate(range(g0, min(g0 + group, n_blocks))):
            t0 = (r * rows) % tt
            heads = o[j * N_XATTN_HEADS:(j + 1) * N_XATTN_HEADS]
            o_ref[seq(r), t0:t0 + rows, :] = jnp.concatenate(heads, axis=-1).astype(BF16)


def _xattn(x, mixg, wq, kt, vt, *, bb, tt, rows, group):
    b, t_len, d = x.shape
    dx, n_mem = kt.shape[1:]
    assert tt % rows == 0
    return pl.pallas_call(
        functools.partial(_xattn_body, rows=rows, group=group),
        grid=(b // bb, t_len // tt),
        in_specs=[
            pl.BlockSpec((bb, tt, d), lambda i, j: (i, j, 0)),
            _const_spec(mixg.shape), _const_spec(wq.shape),
            pl.BlockSpec((bb, dx, n_mem), lambda i, j: (i, 0, 0)),
            pl.BlockSpec((bb, dx, n_mem), lambda i, j: (i, 0, 0)),
        ],
        out_specs=pl.BlockSpec((bb, tt, dx), lambda i, j: (i, j, 0)),
        out_shape=jax.ShapeDtypeStruct((b, t_len, dx), BF16),
        compiler_params=_params("parallel", "arbitrary"),
        name="xattn",
    )(x, mixg, wq, kt, vt)


def _merge_body(x_ref, mixg_ref, wg_ref, ya_ref, yb_ref, yc_ref, wro_ref, wco_ref, wxo_ref, wo_ref, o_ref):
    d = x_ref.shape[1]
    x = x_ref[...]
    xn = _rms(x, mixg_ref[...]).astype(BF16)
    merged = jnp.zeros_like(x)
    for i, (y_ref, w_ref) in enumerate(((ya_ref, wro_ref), (yb_ref, wco_ref), (yc_ref, wxo_ref))):
        gate = jax.nn.sigmoid(_mm(xn, wg_ref[:, i * d:(i + 1) * d]))
        merged = merged + gate * _mm(y_ref[...], w_ref[...])
    o_ref[...] = x + _mm(merged.astype(BF16), wo_ref[...])


def _merge(x, mixg, wg, ya, yb, yc, wro, wco, wxo, wo, *, tm=512):
    n, d = x.shape
    tm = min(tm, n)
    row = lambda w: pl.BlockSpec((tm, w), lambda i: (i, 0))
    return pl.pallas_call(
        _merge_body,
        grid=(n // tm,),
        in_specs=[row(d), _const_spec(mixg.shape), _const_spec(wg.shape), row(ya.shape[1]), row(yb.shape[1]),
                  row(yc.shape[1]), _const_spec(wro.shape), _const_spec(wco.shape), _const_spec(wxo.shape),
                  _const_spec(wo.shape)],
        out_specs=row(d),
        out_shape=jax.ShapeDtypeStruct((n, d), F32),
        compiler_params=_params("parallel"),
        name="merge",
    )(x, mixg, wg, ya, yb, yc, wro, wco, wxo, wo)


def _layer(x, shift_prev, conv_prev, wkv0, mem_kt, mem_vt, p, final_g, *, last, bb, tt, chunk, xa):
    b, t_len, d = x.shape
    n = b * t_len
    x1 = _ffn(x.reshape(n, d), p['ffn1_norm'], p['ffn1_w_up'], p['ffn1_w_down'], final_g, final_norm=False)
    x1_3 = x1.reshape(b, t_len, d)
    og, shift, wkv = _rwkv(x1_3, p['mix_norm'], p['w_s'], shift_prev[:, None, :], p['mu_shift'], p['w0'],
                           p['w_decay_up'], p['a0'], p['w_a_up'], p['w_g_up'], p['k_k'], p['k_a'], p['r_k'],
                           p['gn_g'], p['gn_b'], wkv0, bb=bb, tt=tt, chunk=chunk)
    cb, conv = _conv(x1_3, p['mix_norm'], p['w_c'], p['glu_b'], conv_prev, p['conv_w'], p['conv_b'],
                     p['conv_ln_g'], p['conv_ln_b'], bb=bb, tt=tt)
    oc = _xattn(x1_3, p['mix_norm'], p['w_q'], mem_kt, mem_vt, **xa)
    x2 = _merge(x1, p['mix_norm'], p['w_g'], og.reshape(n, -1), cb.reshape(n, -1), oc.reshape(n, -1),
                p['w_rwkv_out'], p['w_conv_out'], p['w_xattn_out'], p['w_o'])
    x3 = _ffn(x2, p['ffn2_norm'], p['ffn2_w_up'], p['ffn2_w_down'], final_g, final_norm=last)
    return x3.reshape(b, t_len, d), wkv, shift[:, 0, :], conv


def kernel(x_prompt, mem_prompt, x_sample, state_wkv, state_shift, state_conv, cache_mem_k, cache_mem_v,
           ffn1_norm, ffn1_w_up, ffn1_w_down, mix_norm, w_in, mu_shift, w0, w_decay_up, a0, w_a_up, w_g_up,
           k_k, k_a, r_k, gn_g, gn_b, w_rwkv_out, glu_b, conv_w, conv_b, conv_ln_g, conv_ln_b, w_conv_out,
           w_mem_kv, w_xattn_out, w_o, ffn2_norm, ffn2_w_up, ffn2_w_down, final_norm):
    depth = w_in.shape[0]
    d_model = w_in.shape[1]
    d_r = w0.shape[1]
    d_shift = mu_shift.shape[1]
    d_conv = conv_w.shape[2]
    d_x = w_xattn_out.shape[1]
    n_heads = d_r // HEAD_DIM
    o1 = d_shift
    o2 = o1 + 2 * d_conv
    o3 = o2 + d_x
    row = lambda a: a.astype(F32).reshape(1, -1)
    final_g = row(final_norm)

    layers = []
    for l in range(depth):
        layers.append({
            'ffn1_norm': row(ffn1_norm[l]), 'ffn1_w_up': ffn1_w_up[l].astype(BF16),
            'ffn1_w_down': ffn1_w_down[l].astype(BF16), 'mix_norm': row(mix_norm[l]),
            'w_s': w_in[l, :, :o1].astype(BF16), 'w_c': w_in[l, :, o1:o2].astype(BF16),
            'w_q': w_in[l, :, o2:o3].astype(BF16), 'w_g': w_in[l, :, o3:].astype(BF16),
            'mu_shift': row(mu_shift[l]), 'w0': row(w0[l]), 'w_decay_up': w_decay_up[l].astype(F32),
            'a0': row(a0[l]), 'w_a_up': w_a_up[l].astype(F32), 'w_g_up': w_g_up[l].astype(F32),
            'k_k': row(k_k[l]), 'k_a': row(k_a[l]), 'r_k': row(r_k[l]), 'gn_g': row(gn_g[l]),
            'gn_b': row(gn_b[l]), 'w_rwkv_out': w_rwkv_out[l].astype(BF16), 'glu_b': row(glu_b[l]),
            'conv_w': conv_w[l].astype(F32), 'conv_b': row(conv_b[l]), 'conv_ln_g': row(conv_ln_g[l]),
            'conv_ln_b': row(conv_ln_b[l]), 'w_conv_out': w_conv_out[l].astype(BF16),
            'w_mem_kv': w_mem_kv[l].astype(BF16), 'w_mem_kv_t': w_mem_kv[l].T.astype(BF16), 'w_xattn_out': w_xattn_out[l].astype(BF16),
            'w_o': w_o[l].astype(BF16), 'ffn2_norm': row(ffn2_norm[l]),
            'ffn2_w_up': ffn2_w_up[l].astype(BF16), 'ffn2_w_down': ffn2_w_down[l].astype(BF16),
        })

    bp, n_mem, _ = mem_prompt.shape
    xp = x_prompt.astype(F32)
    wkv_p, shift_p, conv_p, mk_p, mv_p = [], [], [], [], []
    for l in range(depth):
        p = layers[l]
        mk, mv, mkt, mvt = _memkv(mem_prompt.astype(F32), p['w_mem_kv'], p['w_mem_kv_t'])
        xp, wkv, sh, cv = _layer(
            xp, jnp.zeros((bp, d_shift), F32), jnp.zeros((bp, CONV_WIDTH - 1, d_conv), F32),
            jnp.zeros((bp, n_heads, HEAD_DIM, HEAD_DIM), F32), mkt, mvt, p, final_g,
            last=(l == depth - 1), bb=1, tt=512, chunk=64, xa=dict(bb=1, tt=512, rows=256, group=2))
        wkv_p.append(wkv)
        shift_p.append(sh)
        conv_p.append(cv)
        mk_p.append(mk.reshape(bp, n_mem, N_XATTN_HEADS, d_x // N_XATTN_HEADS))
        mv_p.append(mv.reshape(bp, n_mem, N_XATTN_HEADS, d_x // N_XATTN_HEADS))

    bs, ts, _ = x_sample.shape
    xs = x_sample.astype(F32)
    wkv_s, shift_s, conv_s = [], [], []
    for l in range(depth):
        kt = jnp.transpose(cache_mem_k[l].astype(F32), (0, 2, 3, 1)).reshape(bs, d_x, n_mem)
        vt = jnp.transpose(cache_mem_v[l].astype(F32), (0, 2, 3, 1)).reshape(bs, d_x, n_mem)
        xs, wkv, sh, cv = _layer(
            xs, state_shift[l].astype(F32), state_conv[l].astype(F32), state_wkv[l].astype(F32), kt, vt,
            layers[l], final_g, last=(l == depth - 1), bb=16, tt=ts, chunk=ts,
            xa=dict(bb=16, tt=ts, rows=ts, group=4))
        wkv_s.append(wkv)
        shift_s.append(sh)
        conv_s.append(cv)

    return (xp.astype(x_prompt.dtype), xs.astype(x_sample.dtype), jnp.stack(wkv_p), jnp.stack(shift_p),
            jnp.stack(conv_p), jnp.stack(mk_p), jnp.stack(mv_p), jnp.stack(wkv_s), jnp.stack(shift_s),
            jnp.stack(conv_s))
```

```python
import functools
import math

import jax
import jax.numpy as jnp
from jax import lax
from jax.experimental import pallas as pl
from jax.experimental.pallas import tpu as pltpu

F32 = jnp.float32
BF16 = jnp.bfloat16
HI = lax.Precision.HIGHEST

HEAD_DIM = 64
PAIR = 2 * HEAD_DIM
N_XATTN_HEADS = 4
CONV_WIDTH = 31
SUBLANES = 8
CONV_PAD = 32
LORA_DECAY = 64
LORA_A = 64
RMS_EPS = 1e-6
LN_EPS = 1e-5
GN_EPS = 64e-5
VMEM_LIMIT = 56 * 1024 * 1024


def _params(*sem):
    return pltpu.CompilerParams(dimension_semantics=sem, vmem_limit_bytes=VMEM_LIMIT)


def _rms(x, g):
    return x * lax.rsqrt(jnp.mean(x * x, axis=-1, keepdims=True) + RMS_EPS) * g


def _mm(a, b, precision=None):
    return jnp.dot(a, b, preferred_element_type=F32, precision=precision)


def _nt(a, b):
    return lax.dot_general(a, b, (((1,), (1,)), ((), ())), preferred_element_type=F32)


def _tn(a, b):
    return lax.dot_general(a, b, (((0,), (0,)), ((), ())), preferred_element_type=F32)


def _split(x):
    hi = x.astype(BF16)
    return hi, (x - hi.astype(F32)).astype(BF16)


def _exact_lhs_dot(sel, x):
    hi, lo = _split(x)
    return _mm(sel, hi) + _mm(sel, lo)


def _dot3(x, w):
    x_hi, x_lo = _split(x)
    w_hi, w_lo = _split(w)
    return _mm(x_hi, w_hi) + _mm(x_lo, w_hi) + _mm(x_hi, w_lo)


def _dot1(x, w):
    return _mm(x.astype(BF16), w.astype(BF16))


def _const_spec(shape):
    nd = len(shape)
    return pl.BlockSpec(shape, lambda *_: (0,) * nd)


def _ffn_body(x_ref, wg_ref, wu_ref, wd_ref, fg_ref, o_ref, *, final_norm):
    x = x_ref[...]
    xb = x.astype(BF16)
    inv = lax.rsqrt(jnp.mean(x * x, axis=-1, keepdims=True) + RMS_EPS)
    hg = _mm(xb, wg_ref[...]) * inv
    hu = _mm(xb, wu_ref[...]) * inv
    h = (hg * jax.nn.sigmoid(hg) * hu).astype(BF16)
    y = x + 0.5 * _mm(h, wd_ref[...])
    if final_norm:
        y = _rms(y, fg_ref[...])
    o_ref[...] = y


def _resident_spec(shape):
    nd = len(shape)
    return pl.BlockSpec(shape, lambda *_: (0,) * nd, pipeline_mode=pl.Buffered(1))


def _ffn(x, w_gate, w_up, w_down, final_g, *, final_norm, tm=512):
    n, d = x.shape
    tm = min(tm, n)
    assert n % tm == 0
    return pl.pallas_call(
        functools.partial(_ffn_body, final_norm=final_norm),
        grid=(n // tm,),
        in_specs=[
            pl.BlockSpec((tm, d), lambda i: (i, 0)),
            _resident_spec(w_gate.shape), _resident_spec(w_up.shape), _resident_spec(w_down.shape),
            _const_spec((1, d)),
        ],
        out_specs=pl.BlockSpec((tm, d), lambda i: (i, 0)),
        out_shape=jax.ShapeDtypeStruct((n, d), F32),
        compiler_params=_params("parallel"),
        name="ffn_final" if final_norm else "ffn",
    )(x, w_gate, w_up, w_down, final_g)


def _memkv_body(m_ref, w_ref, wt_ref, k_ref, v_ref, kt_ref, vt_ref):
    mem = m_ref[0].astype(BF16)
    dk = k_ref.shape[-1]
    kv = _mm(mem, w_ref[...])
    k_ref[0] = kv[:, :dk]
    v_ref[0] = kv[:, dk:]
    kvt = _nt(wt_ref[...], mem)
    kt_ref[0] = kvt[:dk].astype(BF16)
    vt_ref[0] = kvt[dk:].astype(BF16)


def _memkv(mem, w, wt):
    b, n_mem, d = mem.shape
    dk = w.shape[1] // 2
    return pl.pallas_call(
        _memkv_body,
        grid=(b,),
        in_specs=[pl.BlockSpec((1, n_mem, d), lambda i: (i, 0, 0)), _const_spec(w.shape), _const_spec(wt.shape)],
        out_specs=[pl.BlockSpec((1, n_mem, dk), lambda i: (i, 0, 0))] * 2
        + [pl.BlockSpec((1, dk, n_mem), lambda i: (i, 0, 0))] * 2,
        out_shape=[jax.ShapeDtypeStruct((b, n_mem, dk), F32)] * 2
        + [jax.ShapeDtypeStruct((b, dk, n_mem), BF16)] * 2,
        compiler_params=_params("parallel"),
        name="memkv",
    )(mem, w, wt)


def _rwkv_body(x_ref, mixg_ref, ws_ref, sprev_ref, mu_ref, w0_ref, wdu_ref, a0_ref, wau_ref, wgu_ref,
               kk_ref, ka_ref, rk_ref, gng_ref, gnb_ref, s0_ref, hsum_ref, tri_ref, blk_ref,
               og_ref, sout_ref, wkv_ref,
               carry, st, at_s, rt_s, bb_s, kb_s, bh_s, kh_s, v_s, wt_s, bonus_s, g_s, o_s,
               gm_s, hm_s, rp_s, oi_s, *, chunk, unroll1, unroll2):
    bb_n, tt, d = x_ref.shape
    m = bb_n * tt
    d_shift = ws_ref.shape[1]
    d_r = w0_ref.shape[1]
    n_pairs = d_r // PAIR
    n_ck = tt // chunk
    n_blk = m // chunk
    log_chunk = int(math.log2(chunk))
    c2 = 2 * chunk
    t = pl.program_id(1)

    @pl.when(t == 0)
    def _():
        carry[...] = sprev_ref[...]
        zero = jnp.zeros((HEAD_DIM, HEAD_DIM), F32)
        for b_i in range(bb_n):
            for p in range(n_pairs):
                top = jnp.concatenate([s0_ref[b_i, 2 * p], zero], axis=1)
                bot = jnp.concatenate([zero, s0_ref[b_i, 2 * p + 1]], axis=1)
                st[b_i, p] = jnp.concatenate([top, bot], axis=0)

    xn = _rms(x_ref[...].reshape(m, d), mixg_ref[...]).astype(BF16)
    zs = _mm(xn, ws_ref[...])
    rolled = pltpu.roll(zs, 1, axis=0)
    prev0 = jnp.broadcast_to(carry[...], (bb_n, tt, d_shift)).reshape(m, d_shift)
    row = lax.broadcasted_iota(jnp.int32, (m, d_shift), 0)
    prev = jnp.where((row & (tt - 1)) == 0, prev0, rolled)
    last = zs.reshape(bb_n, tt, d_shift)[:, tt - 1:tt, :]
    carry[...] = last
    sout_ref[...] = last
    xm = zs + (prev - zs) * mu_ref[...]

    c1, c2_, c3 = d_r, 2 * d_r, 3 * d_r
    c4 = c3 + LORA_DECAY
    c5 = c4 + LORA_A
    r, k, v = xm[:, :c1], xm[:, c1:c2_], xm[:, c2_:c3]
    wd, ad, gd = xm[:, c3:c4], xm[:, c4:c5], xm[:, c5:]
    dec_in = w0_ref[...] + _dot3(jnp.tanh(wd), wdu_ref[...])
    neg = -dec_in
    softplus = jnp.maximum(neg, 0.0) + jnp.log(1.0 + jnp.exp(-jnp.abs(neg)))
    ld = -jnp.exp(-softplus - 0.5)
    a = jax.nn.sigmoid(a0_ref[...] + _dot1(ad, wau_ref[...]))
    g_s[...] = _dot1(jax.nn.sigmoid(gd), wgu_ref[...])
    kh = k * (1.0 + (a - 1.0) * ka_ref[...])

    head_sum = hsum_ref[...]
    kkraw = k * kk_ref[...]
    kk = kkraw / jnp.maximum(jnp.sqrt(_mm((kkraw * kkraw).astype(BF16), head_sum)), 1e-12)
    b = kk * a
    bonus_s[...] = _mm((r * kh * rk_ref[...]).astype(BF16), head_sum) * v

    tdt = at_s.dtype
    v_s[...] = v.astype(tdt)
    slab = tri_ref.shape[0]
    for s0 in range(0, m, slab):
        sl = slice(s0, s0 + slab)
        ld_c = ld[sl]
        cum = _exact_lhs_dot(tri_ref[...], ld_c)
        tot = _exact_lhs_dot(blk_ref[...], ld_c)
        w_inv = jnp.exp(-cum)
        w_rem = jnp.exp(tot - cum)
        at_s[sl, :] = (-kk[sl] * jnp.exp(cum - ld_c)).astype(tdt)
        rt_s[sl, :] = (r[sl] * jnp.exp(cum)).astype(tdt)
        bb_s[sl, :] = (b[sl] * w_inv).astype(tdt)
        kb_s[sl, :] = (kh[sl] * w_inv).astype(tdt)
        bh_s[sl, :] = (b[sl] * w_rem).astype(tdt)
        kh_s[sl, :] = (kh[sl] * w_rem).astype(tdt)
        wt_s[sl, :] = jnp.exp(tot)

    lane_lo = lax.broadcasted_iota(jnp.int32, (chunk, PAIR), 1) < HEAD_DIM
    r2 = lax.broadcasted_iota(jnp.int32, (c2, c2), 0)
    q2 = lax.broadcasted_iota(jnp.int32, (c2, c2), 1)
    strict = (q2 & (chunk - 1)) < (r2 & (chunk - 1))
    incl = (q2 & (chunk - 1)) <= (r2 & (chunk - 1))
    eye2 = (r2 == q2).astype(F32)
    wide = c2 % PAIR == 0

    def stack(ref, rows, ls):
        x = ref[rows, ls]
        return jnp.concatenate([jnp.where(lane_lo, x, 0.0), jnp.where(lane_lo, 0.0, x)], axis=0).astype(BF16)

    def phase1_step(i, c):
        chains = [(unroll1 * i + j, p) for j in range(unroll1) for p in range(n_pairs)]
        n = range(len(chains))

        def stacks(ref):
            return [stack(ref, pl.ds(pl.multiple_of(blk * chunk, chunk), chunk), slice(p * PAIR, (p + 1) * PAIR))
                    for blk, p in chains]

        at, rt, bbm, kbm = stacks(at_s), stacks(rt_s), stacks(bb_s), stacks(kb_s)
        bhm, khm, vm = stacks(bh_s), stacks(kh_s), stacks(v_s)
        if wide:
            a4 = [_nt(jnp.concatenate([at[c_], rt[c_]], axis=0), jnp.concatenate([bbm[c_], kbm[c_]], axis=0))
                  for c_ in n]
            a_ab = [jnp.where(strict, a4[c_][:c2, :c2], 0.0) for c_ in n]
            a_ak = [jnp.where(strict, a4[c_][:c2, c2:], 0.0).astype(BF16) for c_ in n]
            a_rb = [jnp.where(incl, a4[c_][c2:, :c2], 0.0).astype(BF16) for c_ in n]
            a_rk = [jnp.where(incl, a4[c_][c2:, c2:], 0.0).astype(BF16) for c_ in n]
        else:
            a_ab = [jnp.where(strict, _nt(at[c_], bbm[c_]), 0.0) for c_ in n]
            a_ak = [jnp.where(strict, _nt(at[c_], kbm[c_]), 0.0).astype(BF16) for c_ in n]
            a_rb = [jnp.where(incl, _nt(rt[c_], bbm[c_]), 0.0).astype(BF16) for c_ in n]
            a_rk = [jnp.where(incl, _nt(rt[c_], kbm[c_]), 0.0).astype(BF16) for c_ in n]
        tinv = [eye2 + a_ab[c_] for c_ in n]
        ap = [a_ab[c_].astype(BF16) for c_ in n]
        ap = [_mm(ap[c_], ap[c_]).astype(BF16) for c_ in n]
        for i_sq in range(1, log_chunk):
            if i_sq == log_chunk - 1:
                tinv = [tinv[c_] + _mm(ap[c_], tinv[c_].astype(BF16)) for c_ in n]
            elif wide:
                x = [_mm(ap[c_], jnp.concatenate([ap[c_], tinv[c_].astype(BF16)], axis=1)) for c_ in n]
                ap = [x[c_][:, :c2].astype(BF16) for c_ in n]
                tinv = [tinv[c_] + x[c_][:, c2:] for c_ in n]
            else:
                tinv = [tinv[c_] + _mm(ap[c_], tinv[c_].astype(BF16)) for c_ in n]
                ap = [_mm(ap[c_], ap[c_]).astype(BF16) for c_ in n]
        av = [_mm(a_ak[c_], vm[c_]).astype(BF16) for c_ in n]
        pq = [_mm(tinv[c_].astype(BF16), jnp.concatenate([av[c_], at[c_]], axis=1)) for c_ in n]
        pm = [pq[c_][:, :PAIR].astype(BF16) for c_ in n]
        qm = [pq[c_][:, PAIR:].astype(BF16) for c_ in n]
        for c_, (blk, p) in enumerate(chains):
            gm_s[blk, p] = _tn(qm[c_], bhm[c_]).astype(BF16)
        for c_, (blk, p) in enumerate(chains):
            hm_s[blk, p] = _tn(jnp.concatenate([pm[c_], vm[c_]], axis=0),
                               jnp.concatenate([bhm[c_], khm[c_]], axis=0))
        rq = [_mm(a_rb[c_], jnp.concatenate([qm[c_], pm[c_]], axis=1)) for c_ in n]
        for c_, (blk, p) in enumerate(chains):
            rp_s[blk, p] = (rt[c_].astype(F32) + rq[c_][:, :PAIR]).astype(BF16)
        for c_, (blk, p) in enumerate(chains):
            oi_s[blk, p] = rq[c_][:, PAIR:] + _mm(a_rk[c_], vm[c_])
        return c

    lax.fori_loop(0, n_blk // unroll1, phase1_step, 0)

    def phase2_step(i, c):
        chains = [(unroll2 * i + j, p) for j in range(unroll2) for p in range(n_pairs)]
        n = range(len(chains))
        row0 = [pl.multiple_of(blk * chunk, chunk) for blk, _ in chains]
        ls = [slice(p * PAIR, (p + 1) * PAIR) for _, p in chains]
        s = [st[blk // n_ck, p] for blk, p in chains]
        sb = [s[c_].astype(BF16) for c_ in n]
        s_new = [s[c_] * wt_s[pl.ds(row0[c_], 1), ls[c_]] + _mm(sb[c_], gm_s[blk, p]) + hm_s[blk, p]
                 for c_, (blk, p) in enumerate(chains)]
        o_bd = [oi_s[blk, p] + _nt(rp_s[blk, p], sb[c_]) for c_, (blk, p) in enumerate(chains)]
        for c_, (blk, p) in enumerate(chains):
            st[blk // n_ck, p] = s_new[c_]
        for c_ in n:
            o_s[pl.ds(row0[c_], chunk), ls[c_]] = o_bd[c_][:chunk] + o_bd[c_][chunk:]
        return c

    lax.fori_loop(0, n_blk // unroll2, phase2_step, 0)

    o = o_s[...]
    mean = _mm(o.astype(BF16), head_sum) * (1.0 / HEAD_DIM)
    oc = o - mean
    var = _mm((oc * oc).astype(BF16), head_sum) * (1.0 / HEAD_DIM)
    o = oc * lax.rsqrt(var + GN_EPS) * gng_ref[...] + gnb_ref[...] + bonus_s[...]
    og_ref[...] = (o * g_s[...]).astype(BF16).reshape(bb_n, tt, d_r)

    @pl.when(t == pl.num_programs(1) - 1)
    def _():
        for b_i in range(bb_n):
            for p in range(n_pairs):
                s = st[b_i, p]
                wkv_ref[b_i, 2 * p] = s[:HEAD_DIM, :HEAD_DIM]
                wkv_ref[b_i, 2 * p + 1] = s[HEAD_DIM:, HEAD_DIM:]


def _rwkv(x, mixg, ws, sprev, mu, w0, wdu, a0, wau, wgu, kk, ka, rk, gng, gnb, s0, *, bb, tt, chunk,
          unroll1, unroll2):
    b, t_len, d = x.shape
    d_shift = ws.shape[1]
    d_r = w0.shape[1]
    n_heads = d_r // HEAD_DIM
    n_pairs = d_r // PAIR
    m = bb * tt
    n_blk = m // chunk
    assert b % bb == 0 and t_len % tt == 0 and tt % chunk == 0
    assert n_blk % unroll1 == 0 and n_blk % unroll2 == 0 and (unroll2 == 1 or tt == chunk)
    assert chunk & (chunk - 1) == 0 and tt & (tt - 1) == 0 and chunk % 8 == 0
    tok = pltpu.VMEM((m, d_r), F32)
    tok_mm = pltpu.VMEM((m, d_r), BF16 if chunk % (2 * SUBLANES) == 0 else F32)
    lane = jnp.arange(d_r)
    head_sum = (lane[:, None] // HEAD_DIM == lane[None, :] // HEAD_DIM).astype(BF16)
    rows = jnp.arange(chunk if chunk >= HEAD_DIM else m)
    same_chunk = rows[:, None] // chunk == rows[None, :] // chunk
    tri = (same_chunk & (rows[None, :] <= rows[:, None])).astype(BF16)
    blk = same_chunk.astype(BF16)
    return pl.pallas_call(
        functools.partial(_rwkv_body, chunk=chunk, unroll1=unroll1, unroll2=unroll2),
        grid=(b // bb, t_len // tt),
        in_specs=[
            pl.BlockSpec((bb, tt, d), lambda i, j: (i, j, 0)),
            _const_spec(mixg.shape), _const_spec(ws.shape),
            pl.BlockSpec((bb, 1, d_shift), lambda i, j: (i, 0, 0)),
            _const_spec(mu.shape), _const_spec(w0.shape), _const_spec(wdu.shape), _const_spec(a0.shape),
            _const_spec(wau.shape), _const_spec(wgu.shape), _const_spec(kk.shape), _const_spec(ka.shape),
            _const_spec(rk.shape), _const_spec(gng.shape), _const_spec(gnb.shape),
            pl.BlockSpec((bb, n_heads, HEAD_DIM, HEAD_DIM), lambda i, j: (i, 0, 0, 0)),
            _const_spec(head_sum.shape), _const_spec(tri.shape), _const_spec(blk.shape),
        ],
        out_specs=[
            pl.BlockSpec((bb, tt, d_r), lambda i, j: (i, j, 0)),
            pl.BlockSpec((bb, 1, d_shift), lambda i, j: (i, 0, 0)),
            pl.BlockSpec((bb, n_heads, HEAD_DIM, HEAD_DIM), lambda i, j: (i, 0, 0, 0)),
        ],
        out_shape=[
            jax.ShapeDtypeStruct((b, t_len, d_r), BF16),
            jax.ShapeDtypeStruct((b, 1, d_shift), F32),
            jax.ShapeDtypeStruct((b, n_heads, HEAD_DIM, HEAD_DIM), F32),
        ],
        scratch_shapes=[
            pltpu.VMEM((bb, 1, d_shift), F32),
            pltpu.VMEM((bb, n_pairs, PAIR, PAIR), F32),
        ] + [tok_mm] * 7 + [tok] * 4 + [
            pltpu.VMEM((n_blk, n_pairs, PAIR, PAIR), BF16),
            pltpu.VMEM((n_blk, n_pairs, PAIR, PAIR), F32),
            pltpu.VMEM((n_blk, n_pairs, 2 * chunk, PAIR), BF16),
            pltpu.VMEM((n_blk, n_pairs, 2 * chunk, PAIR), F32),
        ],
        compiler_params=_params("parallel", "arbitrary"),
        name="rwkv",
    )(x, mixg, ws, sprev, mu, w0, wdu, a0, wau, wgu, kk, ka, rk, gng, gnb, s0, head_sum, tri, blk)


def _conv_body(x_ref, mixg_ref, wc_ref, glub_ref, cprev_ref, cw_ref, cb_ref, lng_ref, lnb_ref,
               y_ref, cnew_ref, ubuf, ushift):
    bb_n, tt, d = x_ref.shape
    m = bb_n * tt
    dc = cw_ref.shape[1]
    lo = CONV_PAD - (CONV_WIDTH - 1)
    t = pl.program_id(1)

    @pl.when(t == 0)
    def _():
        ubuf[:, lo:CONV_PAD, :] = cprev_ref[...]

    xn = _rms(x_ref[...].reshape(m, d), mixg_ref[...]).astype(BF16)
    zc = _mm(xn, wc_ref[...]) + glub_ref[...]
    u = zc[:, :dc] * jax.nn.sigmoid(zc[:, dc:])
    ubuf[:, CONV_PAD:CONV_PAD + tt, :] = u.reshape(bb_n, tt, dc)
    n_sh = tt + CONV_PAD - SUBLANES
    for sh in range(1, SUBLANES):
        ushift[sh - 1] = ubuf[:, sh:sh + n_sh, :]
    acc = jnp.zeros((bb_n, tt, dc), F32) + cb_ref[...]
    for kx in range(CONV_WIDTH):
        off = lo + kx
        sh, base = off % SUBLANES, off - off % SUBLANES
        win = ubuf[:, base:base + tt, :] if sh == 0 else ushift[sh - 1, :, base:base + tt, :]
        acc = acc + win * cw_ref[kx:kx + 1, :]
    tail = ubuf[:, lo + tt:CONV_PAD + tt, :]
    cnew_ref[...] = tail
    ubuf[:, lo:CONV_PAD, :] = tail
    c = acc.reshape(m, dc)
    cm = jnp.mean(c, axis=-1, keepdims=True)
    cc = c - cm
    cv = jnp.mean(cc * cc, axis=-1, keepdims=True)
    c = cc * lax.rsqrt(cv + LN_EPS) * lng_ref[...] + lnb_ref[...]
    y_ref[...] = (c * jax.nn.sigmoid(c)).astype(BF16).reshape(bb_n, tt, dc)


def _conv(x, mixg, wc, glub, cprev, cw, cb, lng, lnb, *, bb, tt):
    b, t_len, d = x.shape
    dc = cw.shape[1]
    return pl.pallas_call(
        _conv_body,
        grid=(b // bb, t_len // tt),
        in_specs=[
            pl.BlockSpec((bb, tt, d), lambda i, j: (i, j, 0)),
            _const_spec(mixg.shape), _const_spec(wc.shape), _const_spec(glub.shape),
            pl.BlockSpec((bb, CONV_WIDTH - 1, dc), lambda i, j: (i, 0, 0)),
            _const_spec(cw.shape), _const_spec(cb.shape), _const_spec(lng.shape), _const_spec(lnb.shape),
        ],
        out_specs=[
            pl.BlockSpec((bb, tt, dc), lambda i, j: (i, j, 0)),
            pl.BlockSpec((bb, CONV_WIDTH - 1, dc), lambda i, j: (i, 0, 0)),
        ],
        out_shape=[
            jax.ShapeDtypeStruct((b, t_len, dc), BF16),
            jax.ShapeDtypeStruct((b, CONV_WIDTH - 1, dc), F32),
        ],
        scratch_shapes=[pltpu.VMEM((bb, CONV_PAD + tt, dc), F32),
                        pltpu.VMEM((SUBLANES - 1, bb, CONV_PAD + tt - SUBLANES, dc), F32)],
        compiler_params=_params("parallel", "arbitrary"),
        name="conv",
    )(x, mixg, wc, glub, cprev, cw, cb, lng, lnb)


def _xattn_body(x_ref, mixg_ref, wq_ref, kt_ref, vt_ref, o_ref, *, rows, group):
    bb_n, tt, d = x_ref.shape
    m = bb_n * tt
    dh = kt_ref.shape[1] // N_XATTN_HEADS
    xn = _rms(x_ref[...].reshape(m, d), mixg_ref[...]).astype(BF16)
    q = (_mm(xn, wq_ref[...]) * (dh ** -0.5)).astype(BF16)
    n_blocks = m // rows
    for g0 in range(0, n_blocks, group):
        chains = [(r, h) for r in range(g0, min(g0 + group, n_blocks)) for h in range(N_XATTN_HEADS)]
        hs = lambda h: slice(h * dh, (h + 1) * dh)
        seq = lambda r: (r * rows) // tt
        s = [_mm(q[r * rows:(r + 1) * rows, hs(h)], kt_ref[seq(r), hs(h), :].astype(BF16)) for r, h in chains]
        p = [jnp.exp(x - jnp.max(x, axis=-1, keepdims=True)) for x in s]
        l = [jnp.sum(x, axis=-1, keepdims=True) for x in p]
        o = [_nt(p[i].astype(BF16), vt_ref[seq(r), hs(h), :].astype(BF16)) / l[i] for i, (r, h) in enumerate(chains)]
        for j, r in enumerate(range(g0, min(g0 + group, n_blocks))):
            t0 = (r * rows) % tt
            heads = o[j * N_XATTN_HEADS:(j + 1) * N_XATTN_HEADS]
            o_ref[seq(r), t0:t0 + rows, :] = jnp.concatenate(heads, axis=-1).astype(BF16)


def _xattn(x, mixg, wq, kt, vt, *, bb, tt, rows, group):
    b, t_len, d = x.shape
    dx, n_mem = kt.shape[1:]
    assert tt % rows == 0
    return pl.pallas_call(
        functools.partial(_xattn_body, rows=rows, group=group),
        grid=(b // bb, t_len // tt),
        in_specs=[
            pl.BlockSpec((bb, tt, d), lambda i, j: (i, j, 0)),
            _const_spec(mixg.shape), _const_spec(wq.shape),
            pl.BlockSpec((bb, dx, n_mem), lambda i, j: (i, 0, 0)),
            pl.BlockSpec((bb, dx, n_mem), lambda i, j: (i, 0, 0)),
        ],
        out_specs=pl.BlockSpec((bb, tt, dx), lambda i, j: (i, j, 0)),
        out_shape=jax.ShapeDtypeStruct((b, t_len, dx), BF16),
        compiler_params=_params("parallel", "arbitrary"),
        name="xattn",
    )(x, mixg, wq, kt, vt)


def _merge_body(x_ref, mixg_ref, wg_ref, ya_ref, yb_ref, yc_ref, wro_ref, wco_ref, wxo_ref, wo_ref, o_ref):
    d = x_ref.shape[1]
    x = x_ref[...]
    xn = _rms(x, mixg_ref[...]).astype(BF16)
    merged = jnp.zeros_like(x)
    for i, (y_ref, w_ref) in enumerate(((ya_ref, wro_ref), (yb_ref, wco_ref), (yc_ref, wxo_ref))):
        gate = jax.nn.sigmoid(_mm(xn, wg_ref[:, i * d:(i + 1) * d]))
        merged = merged + gate * _mm(y_ref[...], w_ref[...])
    o_ref[...] = x + _mm(merged.astype(BF16), wo_ref[...])


def _merge(x, mixg, wg, ya, yb, yc, wro, wco, wxo, wo, *, tm=512):
    n, d = x.shape
    tm = min(tm, n)
    row = lambda w: pl.BlockSpec((tm, w), lambda i: (i, 0))
    return pl.pallas_call(
        _merge_body,
        grid=(n // tm,),
        in_specs=[row(d), _const_spec(mixg.shape), _const_spec(wg.shape), row(ya.shape[1]), row(yb.shape[1]),
                  row(yc.shape[1]), _const_spec(wro.shape), _const_spec(wco.shape), _const_spec(wxo.shape),
                  _const_spec(wo.shape)],
        out_specs=row(d),
        out_shape=jax.ShapeDtypeStruct((n, d), F32),
        compiler_params=_params("parallel"),
        name="merge",
    )(x, mixg, wg, ya, yb, yc, wro, wco, wxo, wo)


def _layer(x, shift_prev, conv_prev, wkv0, mem_kt, mem_vt, p, final_g, *, last, bb, tt, rw, xa):
    b, t_len, d = x.shape
    n = b * t_len
    x1 = _ffn(x.reshape(n, d), p['ffn1_w_gate'], p['ffn1_w_up'], p['ffn1_w_down'], final_g, final_norm=False)
    x1_3 = x1.reshape(b, t_len, d)
    og, shift, wkv = _rwkv(x1_3, p['mix_norm'], p['w_s'], shift_prev[:, None, :], p['mu_shift'], p['w0'],
                           p['w_decay_up'], p['a0'], p['w_a_up'], p['w_g_up'], p['k_k'], p['k_a'], p['r_k'],
                           p['gn_g'], p['gn_b'], wkv0, bb=bb, tt=tt, **rw)
    cb, conv = _conv(x1_3, p['mix_norm'], p['w_c'], p['glu_b'], conv_prev, p['conv_w'], p['conv_b'],
                     p['conv_ln_g'], p['conv_ln_b'], bb=bb, tt=tt)
    oc = _xattn(x1_3, p['mix_norm'], p['w_q'], mem_kt, mem_vt, **xa)
    x2 = _merge(x1, p['mix_norm'], p['w_g'], og.reshape(n, -1), cb.reshape(n, -1), oc.reshape(n, -1),
                p['w_rwkv_out'], p['w_conv_out'], p['w_xattn_out'], p['w_o'])
    x3 = _ffn(x2, p['ffn2_w_gate'], p['ffn2_w_up'], p['ffn2_w_down'], final_g, final_norm=last)
    return x3.reshape(b, t_len, d), wkv, shift[:, 0, :], conv


def kernel(x_prompt, mem_prompt, x_sample, state_wkv, state_shift, state_conv, cache_mem_k, cache_mem_v,
           ffn1_norm, ffn1_w_up, ffn1_w_down, mix_norm, w_in, mu_shift, w0, w_decay_up, a0, w_a_up, w_g_up,
           k_k, k_a, r_k, gn_g, gn_b, w_rwkv_out, glu_b, conv_w, conv_b, conv_ln_g, conv_ln_b, w_conv_out,
           w_mem_kv, w_xattn_out, w_o, ffn2_norm, ffn2_w_up, ffn2_w_down, final_norm):
    depth = w_in.shape[0]
    d_model = w_in.shape[1]
    d_r = w0.shape[1]
    d_shift = mu_shift.shape[1]
    d_conv = conv_w.shape[2]
    d_x = w_xattn_out.shape[1]
    n_heads = d_r // HEAD_DIM
    o1 = d_shift
    o2 = o1 + 2 * d_conv
    o3 = o2 + d_x
    d_ff = ffn1_w_down.shape[1]
    row = lambda a: a.astype(F32).reshape(1, -1)
    scaled = lambda g, w: (g.astype(F32)[:, None] * w.astype(F32)).astype(BF16)
    final_g = row(final_norm)

    layers = []
    for l in range(depth):
        layers.append({
            'ffn1_w_gate': scaled(ffn1_norm[l], ffn1_w_up[l][:, :d_ff]),
            'ffn1_w_up': scaled(ffn1_norm[l], ffn1_w_up[l][:, d_ff:]),
            'ffn1_w_down': ffn1_w_down[l].astype(BF16), 'mix_norm': row(mix_norm[l]),
            'w_s': w_in[l, :, :o1].astype(BF16), 'w_c': w_in[l, :, o1:o2].astype(BF16),
            'w_q': w_in[l, :, o2:o3].astype(BF16), 'w_g': w_in[l, :, o3:].astype(BF16),
            'mu_shift': row(mu_shift[l]), 'w0': row(w0[l]), 'w_decay_up': w_decay_up[l].astype(F32),
            'a0': row(a0[l]), 'w_a_up': w_a_up[l].astype(F32), 'w_g_up': w_g_up[l].astype(F32),
            'k_k': row(k_k[l]), 'k_a': row(k_a[l]), 'r_k': row(r_k[l]), 'gn_g': row(gn_g[l]),
            'gn_b': row(gn_b[l]), 'w_rwkv_out': w_rwkv_out[l].astype(BF16), 'glu_b': row(glu_b[l]),
            'conv_w': conv_w[l].astype(F32), 'conv_b': row(conv_b[l]), 'conv_ln_g': row(conv_ln_g[l]),
            'conv_ln_b': row(conv_ln_b[l]), 'w_conv_out': w_conv_out[l].astype(BF16),
            'w_mem_kv': w_mem_kv[l].astype(BF16), 'w_mem_kv_t': w_mem_kv[l].T.astype(BF16), 'w_xattn_out': w_xattn_out[l].astype(BF16),
            'w_o': w_o[l].astype(BF16),
            'ffn2_w_gate': scaled(ffn2_norm[l], ffn2_w_up[l][:, :d_ff]),
            'ffn2_w_up': scaled(ffn2_norm[l], ffn2_w_up[l][:, d_ff:]), 'ffn2_w_down': ffn2_w_down[l].astype(BF16),
        })

    bp, n_mem, _ = mem_prompt.shape
    xp = x_prompt.astype(F32)
    wkv_p, shift_p, conv_p, mk_p, mv_p = [], [], [], [], []
    for l in range(depth):
        p = layers[l]
        mk, mv, mkt, mvt = _memkv(mem_prompt.astype(F32), p['w_mem_kv'], p['w_mem_kv_t'])
        xp, wkv, sh, cv = _layer(
            xp, jnp.zeros((bp, d_shift), F32), jnp.zeros((bp, CONV_WIDTH - 1, d_conv), F32),
            jnp.zeros((bp, n_heads, HEAD_DIM, HEAD_DIM), F32), mkt, mvt, p, final_g,
            last=(l == depth - 1), bb=1, tt=512, rw=dict(chunk=64, unroll1=2, unroll2=1),
            xa=dict(bb=1, tt=512, rows=256, group=2))
        wkv_p.append(wkv)
        shift_p.append(sh)
        conv_p.append(cv)
        mk_p.append(mk.reshape(bp, n_mem, N_XATTN_HEADS, d_x // N_XATTN_HEADS))
        mv_p.append(mv.reshape(bp, n_mem, N_XATTN_HEADS, d_x // N_XATTN_HEADS))

    bs, ts, _ = x_sample.shape
    xs = x_sample.astype(F32)
    wkv_s, shift_s, conv_s = [], [], []
    for l in range(depth):
        kt = jnp.transpose(cache_mem_k[l].astype(F32), (0, 2, 3, 1)).reshape(bs, d_x, n_mem)
        vt = jnp.transpose(cache_mem_v[l].astype(F32), (0, 2, 3, 1)).reshape(bs, d_x, n_mem)
        xs, wkv, sh, cv = _layer(
            xs, state_shift[l].astype(F32), state_conv[l].astype(F32), state_wkv[l].astype(F32), kt, vt,
            layers[l], final_g, last=(l == depth - 1), bb=16, tt=ts, rw=dict(chunk=ts, unroll1=4, unroll2=4),
            xa=dict(bb=16, tt=ts, rows=ts, group=4))
        wkv_s.append(wkv)
        shift_s.append(sh)
        conv_s.append(cv)

    return (xp.astype(x_prompt.dtype), xs.astype(x_sample.dtype), jnp.stack(wkv_p), jnp.stack(shift_p),
            jnp.stack(conv_p), jnp.stack(mk_p), jnp.stack(mv_p), jnp.stack(wkv_s), jnp.stack(shift_s),
            jnp.stack(conv_s))
```

```python
import functools
import math

import jax
import jax.numpy as jnp
from jax import lax
from jax.experimental import pallas as pl
from jax.experimental.pallas import tpu as pltpu

F32 = jnp.float32
BF16 = jnp.bfloat16
HI = lax.Precision.HIGHEST

HEAD_DIM = 64
PAIR = 2 * HEAD_DIM
N_XATTN_HEADS = 4
CONV_WIDTH = 31
SUBLANES = 8
CONV_PAD = 32
LORA_DECAY = 64
LORA_A = 64
RMS_EPS = 1e-6
LN_EPS = 1e-5
GN_EPS = 64e-5
VMEM_LIMIT = 56 * 1024 * 1024


def _params(*sem):
    return pltpu.CompilerParams(dimension_semantics=sem, vmem_limit_bytes=VMEM_LIMIT)


def _rms(x, g):
    return x * lax.rsqrt(jnp.mean(x * x, axis=-1, keepdims=True) + RMS_EPS) * g


def _mm(a, b, precision=None):
    return jnp.dot(a, b, preferred_element_type=F32, precision=precision)


def _nt(a, b):
    return lax.dot_general(a, b, (((1,), (1,)), ((), ())), preferred_element_type=F32)


def _tn(a, b):
    return lax.dot_general(a, b, (((0,), (0,)), ((), ())), preferred_element_type=F32)


def _split(x):
    hi = x.astype(BF16)
    return hi, (x - hi.astype(F32)).astype(BF16)


def _exact_lhs_dot(sel, x):
    hi, lo = _split(x)
    return _mm(sel, hi) + _mm(sel, lo)


def _dot3(x, w):
    x_hi, x_lo = _split(x)
    w_hi, w_lo = _split(w)
    return _mm(x_hi, w_hi) + _mm(x_lo, w_hi) + _mm(x_hi, w_lo)


def _dot1(x, w):
    return _mm(x.astype(BF16), w.astype(BF16))


def _const_spec(shape):
    nd = len(shape)
    return pl.BlockSpec(shape, lambda *_: (0,) * nd)


def _ffn_body(x_ref, wg_ref, wu_ref, wd_ref, fg_ref, o_ref, *, final_norm):
    x = x_ref[...]
    xb = x.astype(BF16)
    inv = lax.rsqrt(jnp.mean(x * x, axis=-1, keepdims=True) + RMS_EPS)
    hg = _mm(xb, wg_ref[...]) * inv
    hu = _mm(xb, wu_ref[...]) * inv
    h = (hg * jax.nn.sigmoid(hg) * hu).astype(BF16)
    y = x + 0.5 * _mm(h, wd_ref[...])
    if final_norm:
        y = _rms(y, fg_ref[...])
    o_ref[...] = y


def _resident_spec(shape):
    nd = len(shape)
    return pl.BlockSpec(shape, lambda *_: (0,) * nd, pipeline_mode=pl.Buffered(1))


def _ffn(x, w_gate, w_up, w_down, final_g, *, final_norm, tm=512):
    n, d = x.shape
    tm = min(tm, n)
    assert n % tm == 0
    return pl.pallas_call(
        functools.partial(_ffn_body, final_norm=final_norm),
        grid=(n // tm,),
        in_specs=[
            pl.BlockSpec((tm, d), lambda i: (i, 0)),
            _resident_spec(w_gate.shape), _resident_spec(w_up.shape), _resident_spec(w_down.shape),
            _const_spec((1, d)),
        ],
        out_specs=pl.BlockSpec((tm, d), lambda i: (i, 0)),
        out_shape=jax.ShapeDtypeStruct((n, d), F32),
        compiler_params=_params("parallel"),
        name="ffn_final" if final_norm else "ffn",
    )(x, w_gate, w_up, w_down, final_g)


def _memkv_body(m_ref, w_ref, wt_ref, k_ref, v_ref, kt_ref, vt_ref):
    mem = m_ref[0].astype(BF16)
    dk = k_ref.shape[-1]
    kv = _mm(mem, w_ref[...])
    k_ref[0] = kv[:, :dk]
    v_ref[0] = kv[:, dk:]
    kvt = _nt(wt_ref[...], mem)
    kt_ref[0] = kvt[:dk].astype(BF16)
    vt_ref[0] = kvt[dk:].astype(BF16)


def _memkv(mem, w, wt):
    b, n_mem, d = mem.shape
    dk = w.shape[1] // 2
    return pl.pallas_call(
        _memkv_body,
        grid=(b,),
        in_specs=[pl.BlockSpec((1, n_mem, d), lambda i: (i, 0, 0)), _const_spec(w.shape), _const_spec(wt.shape)],
        out_specs=[pl.BlockSpec((1, n_mem, dk), lambda i: (i, 0, 0))] * 2
        + [pl.BlockSpec((1, dk, n_mem), lambda i: (i, 0, 0))] * 2,
        out_shape=[jax.ShapeDtypeStruct((b, n_mem, dk), F32)] * 2
        + [jax.ShapeDtypeStruct((b, dk, n_mem), BF16)] * 2,
        compiler_params=_params("parallel"),
        name="memkv",
    )(mem, w, wt)


def _rwkv_body(x_ref, mixg_ref, ws_ref, sprev_ref, mu_ref, w0_ref, wdu_ref, a0_ref, wau_ref, wgu_ref,
               kk_ref, ka_ref, rk_ref, gng_ref, gnb_ref, s0_ref, hsum_ref, tri_ref, blk_ref,
               wc_ref, glub_ref, cprev_ref, cw_ref, cb_ref, lng_ref, lnb_ref,
               og_ref, sout_ref, wkv_ref, yc_ref, cnew_ref,
               carry, st, at_s, rt_s, bb_s, kb_s, bh_s, kh_s, v_s, wt_s, bonus_s, g_s, o_s,
               gm_s, hm_s, rp_s, oi_s, ubuf, ushift, yc_s, *, chunk, unroll1, unroll2):
    bb_n, tt, d = x_ref.shape
    m = bb_n * tt
    d_shift = ws_ref.shape[1]
    d_r = w0_ref.shape[1]
    dc = cw_ref.shape[1]
    n_pairs = d_r // PAIR
    n_ck = tt // chunk
    n_blk = m // chunk
    log_chunk = int(math.log2(chunk))
    c2 = 2 * chunk
    conv_lo = CONV_PAD - (CONV_WIDTH - 1)
    t = pl.program_id(1)

    @pl.when(t == 0)
    def _():
        carry[...] = sprev_ref[...]
        ubuf[:, conv_lo:CONV_PAD, :] = cprev_ref[...]
        zero = jnp.zeros((HEAD_DIM, HEAD_DIM), F32)
        for b_i in range(bb_n):
            for p in range(n_pairs):
                top = jnp.concatenate([s0_ref[b_i, 2 * p], zero], axis=1)
                bot = jnp.concatenate([zero, s0_ref[b_i, 2 * p + 1]], axis=1)
                st[b_i, p] = jnp.concatenate([top, bot], axis=0)

    xn = _rms(x_ref[...].reshape(m, d), mixg_ref[...]).astype(BF16)
    zc = _mm(xn, wc_ref[...]) + glub_ref[...]
    ubuf[:, CONV_PAD:CONV_PAD + tt, :] = (zc[:, :dc] * jax.nn.sigmoid(zc[:, dc:])).reshape(bb_n, tt, dc)
    for sh in range(1, SUBLANES):
        ushift[sh - 1] = ubuf[:, sh:sh + tt + CONV_PAD - SUBLANES, :]

    zs = _mm(xn, ws_ref[...])
    rolled = pltpu.roll(zs, 1, axis=0)
    prev0 = jnp.broadcast_to(carry[...], (bb_n, tt, d_shift)).reshape(m, d_shift)
    row = lax.broadcasted_iota(jnp.int32, (m, d_shift), 0)
    prev = jnp.where((row & (tt - 1)) == 0, prev0, rolled)
    last = zs.reshape(bb_n, tt, d_shift)[:, tt - 1:tt, :]
    carry[...] = last
    sout_ref[...] = last
    xm = zs + (prev - zs) * mu_ref[...]

    c1, c2_, c3 = d_r, 2 * d_r, 3 * d_r
    c4 = c3 + LORA_DECAY
    c5 = c4 + LORA_A
    r, k, v = xm[:, :c1], xm[:, c1:c2_], xm[:, c2_:c3]
    wd, ad, gd = xm[:, c3:c4], xm[:, c4:c5], xm[:, c5:]
    dec_in = w0_ref[...] + _dot3(jnp.tanh(wd), wdu_ref[...])
    neg = -dec_in
    softplus = jnp.maximum(neg, 0.0) + jnp.log(1.0 + jnp.exp(-jnp.abs(neg)))
    ld = -jnp.exp(-softplus - 0.5)
    a = jax.nn.sigmoid(a0_ref[...] + _dot1(ad, wau_ref[...]))
    g_s[...] = _dot1(jax.nn.sigmoid(gd), wgu_ref[...])
    kh = k * (1.0 + (a - 1.0) * ka_ref[...])

    head_sum = hsum_ref[...]
    kkraw = k * kk_ref[...]
    kk = kkraw / jnp.maximum(jnp.sqrt(_mm((kkraw * kkraw).astype(BF16), head_sum)), 1e-12)
    b = kk * a
    bonus_s[...] = _mm((r * kh * rk_ref[...]).astype(BF16), head_sum) * v

    tdt = at_s.dtype
    v_s[...] = v.astype(tdt)
    slab = tri_ref.shape[0]
    for s0 in range(0, m, slab):
        sl = slice(s0, s0 + slab)
        ld_c = ld[sl]
        cum = _exact_lhs_dot(tri_ref[...], ld_c)
        tot = _exact_lhs_dot(blk_ref[...], ld_c)
        w_inv = jnp.exp(-cum)
        w_rem = jnp.exp(tot - cum)
        at_s[sl, :] = (-kk[sl] * jnp.exp(cum - ld_c)).astype(tdt)
        rt_s[sl, :] = (r[sl] * jnp.exp(cum)).astype(tdt)
        bb_s[sl, :] = (b[sl] * w_inv).astype(tdt)
        kb_s[sl, :] = (kh[sl] * w_inv).astype(tdt)
        bh_s[sl, :] = (b[sl] * w_rem).astype(tdt)
        kh_s[sl, :] = (kh[sl] * w_rem).astype(tdt)
        wt_s[sl, :] = jnp.exp(tot)

    lane_lo = lax.broadcasted_iota(jnp.int32, (chunk, PAIR), 1) < HEAD_DIM
    r2 = lax.broadcasted_iota(jnp.int32, (c2, c2), 0)
    q2 = lax.broadcasted_iota(jnp.int32, (c2, c2), 1)
    strict = (q2 & (chunk - 1)) < (r2 & (chunk - 1))
    incl = (q2 & (chunk - 1)) <= (r2 & (chunk - 1))
    eye2 = (r2 == q2).astype(F32)
    wide = c2 % PAIR == 0

    def stack(ref, rows, ls):
        x = ref[rows, ls]
        return jnp.concatenate([jnp.where(lane_lo, x, 0.0), jnp.where(lane_lo, 0.0, x)], axis=0).astype(BF16)

    def phase1_step(i, c):
        chains = [(unroll1 * i + j, p) for j in range(unroll1) for p in range(n_pairs)]
        n = range(len(chains))

        def stacks(ref):
            return [stack(ref, pl.ds(pl.multiple_of(blk * chunk, chunk), chunk), slice(p * PAIR, (p + 1) * PAIR))
                    for blk, p in chains]

        at, rt, bbm, kbm = stacks(at_s), stacks(rt_s), stacks(bb_s), stacks(kb_s)
        bhm, khm, vm = stacks(bh_s), stacks(kh_s), stacks(v_s)
        if wide:
            a4 = [_nt(jnp.concatenate([at[c_], rt[c_]], axis=0), jnp.concatenate([bbm[c_], kbm[c_]], axis=0))
                  for c_ in n]
            a_ab = [jnp.where(strict, a4[c_][:c2, :c2], 0.0) for c_ in n]
            a_ak = [jnp.where(strict, a4[c_][:c2, c2:], 0.0).astype(BF16) for c_ in n]
            a_rb = [jnp.where(incl, a4[c_][c2:, :c2], 0.0).astype(BF16) for c_ in n]
            a_rk = [jnp.where(incl, a4[c_][c2:, c2:], 0.0).astype(BF16) for c_ in n]
        else:
            a_ab = [jnp.where(strict, _nt(at[c_], bbm[c_]), 0.0) for c_ in n]
            a_ak = [jnp.where(strict, _nt(at[c_], kbm[c_]), 0.0).astype(BF16) for c_ in n]
            a_rb = [jnp.where(incl, _nt(rt[c_], bbm[c_]), 0.0).astype(BF16) for c_ in n]
            a_rk = [jnp.where(incl, _nt(rt[c_], kbm[c_]), 0.0).astype(BF16) for c_ in n]
        tinv = [eye2 + a_ab[c_] for c_ in n]
        ap = [a_ab[c_].astype(BF16) for c_ in n]
        ap = [_mm(ap[c_], ap[c_]).astype(BF16) for c_ in n]
        for i_sq in range(1, log_chunk):
            if i_sq == log_chunk - 1:
                tinv = [tinv[c_] + _mm(ap[c_], tinv[c_].astype(BF16)) for c_ in n]
            elif wide:
                x = [_mm(ap[c_], jnp.concatenate([ap[c_], tinv[c_].astype(BF16)], axis=1)) for c_ in n]
                ap = [x[c_][:, :c2].astype(BF16) for c_ in n]
                tinv = [tinv[c_] + x[c_][:, c2:] for c_ in n]
            else:
                tinv = [tinv[c_] + _mm(ap[c_], tinv[c_].astype(BF16)) for c_ in n]
                ap = [_mm(ap[c_], ap[c_]).astype(BF16) for c_ in n]
        av = [_mm(a_ak[c_], vm[c_]).astype(BF16) for c_ in n]
        pq = [_mm(tinv[c_].astype(BF16), jnp.concatenate([av[c_], at[c_]], axis=1)) for c_ in n]
        pm = [pq[c_][:, :PAIR].astype(BF16) for c_ in n]
        qm = [pq[c_][:, PAIR:].astype(BF16) for c_ in n]
        for c_, (blk, p) in enumerate(chains):
            gm_s[blk, p] = _tn(qm[c_], bhm[c_]).astype(BF16)
        for c_, (blk, p) in enumerate(chains):
            hm_s[blk, p] = _tn(jnp.concatenate([pm[c_], vm[c_]], axis=0),
                               jnp.concatenate([bhm[c_], khm[c_]], axis=0))
        rq = [_mm(a_rb[c_], jnp.concatenate([qm[c_], pm[c_]], axis=1)) for c_ in n]
        for c_, (blk, p) in enumerate(chains):
            rp_s[blk, p] = (rt[c_].astype(F32) + rq[c_][:, :PAIR]).astype(BF16)
        for c_, (blk, p) in enumerate(chains):
            oi_s[blk, p] = rq[c_][:, PAIR:] + _mm(a_rk[c_], vm[c_])
        return c

    lax.fori_loop(0, n_blk // unroll1, phase1_step, 0)

    def conv_block(blk):
        b_i, t0 = blk // n_ck, (blk % n_ck) * chunk
        acc = jnp.zeros((chunk, dc), F32) + cb_ref[...]
        for kx in range(CONV_WIDTH):
            off = conv_lo + kx
            sh, base = off % SUBLANES, off - off % SUBLANES
            rows = pl.ds(pl.multiple_of(t0 + base, SUBLANES), chunk)
            win = ubuf[b_i, rows, :] if sh == 0 else ushift[sh - 1, b_i, rows, :]
            acc = acc + win * cw_ref[kx:kx + 1, :]
        cm = jnp.mean(acc, axis=-1, keepdims=True)
        cc = acc - cm
        cv = jnp.mean(cc * cc, axis=-1, keepdims=True)
        cn = cc * lax.rsqrt(cv + LN_EPS) * lng_ref[...] + lnb_ref[...]
        yc_s[pl.ds(pl.multiple_of(blk * chunk, chunk), chunk), :] = cn * jax.nn.sigmoid(cn)

    def phase2_step(i, c):
        for j in range(unroll2):
            conv_block(unroll2 * i + j)
        chains = [(unroll2 * i + j, p) for j in range(unroll2) for p in range(n_pairs)]
        n = range(len(chains))
        row0 = [pl.multiple_of(blk * chunk, chunk) for blk, _ in chains]
        ls = [slice(p * PAIR, (p + 1) * PAIR) for _, p in chains]
        s = [st[blk // n_ck, p] for blk, p in chains]
        sb = [s[c_].astype(BF16) for c_ in n]
        s_new = [s[c_] * wt_s[pl.ds(row0[c_], 1), ls[c_]] + _mm(sb[c_], gm_s[blk, p]) + hm_s[blk, p]
                 for c_, (blk, p) in enumerate(chains)]
        o_bd = [oi_s[blk, p] + _nt(rp_s[blk, p], sb[c_]) for c_, (blk, p) in enumerate(chains)]
        for c_, (blk, p) in enumerate(chains):
            st[blk // n_ck, p] = s_new[c_]
        for c_ in n:
            o_s[pl.ds(row0[c_], chunk), ls[c_]] = o_bd[c_][:chunk] + o_bd[c_][chunk:]
        return c

    lax.fori_loop(0, n_blk // unroll2, phase2_step, 0)

    o = o_s[...]
    mean = _mm(o.astype(BF16), head_sum) * (1.0 / HEAD_DIM)
    oc = o - mean
    var = _mm((oc * oc).astype(BF16), head_sum) * (1.0 / HEAD_DIM)
    o = oc * lax.rsqrt(var + GN_EPS) * gng_ref[...] + gnb_ref[...] + bonus_s[...]
    og_ref[...] = (o * g_s[...]).astype(BF16).reshape(bb_n, tt, d_r)

    yc_ref[...] = yc_s[...].astype(BF16).reshape(bb_n, tt, dc)
    tail = ubuf[:, conv_lo + tt:CONV_PAD + tt, :]
    cnew_ref[...] = tail
    ubuf[:, conv_lo:CONV_PAD, :] = tail

    @pl.when(t == pl.num_programs(1) - 1)
    def _():
        for b_i in range(bb_n):
            for p in range(n_pairs):
                s = st[b_i, p]
                wkv_ref[b_i, 2 * p] = s[:HEAD_DIM, :HEAD_DIM]
                wkv_ref[b_i, 2 * p + 1] = s[HEAD_DIM:, HEAD_DIM:]


def _rwkv(x, mixg, ws, sprev, mu, w0, wdu, a0, wau, wgu, kk, ka, rk, gng, gnb, s0,
          wc, glub, cprev, cw, cb, lng, lnb, *, bb, tt, chunk, unroll1, unroll2):
    b, t_len, d = x.shape
    d_shift = ws.shape[1]
    d_r = w0.shape[1]
    dc = cw.shape[1]
    n_heads = d_r // HEAD_DIM
    n_pairs = d_r // PAIR
    m = bb * tt
    n_blk = m // chunk
    assert b % bb == 0 and t_len % tt == 0 and tt % chunk == 0
    assert n_blk % unroll1 == 0 and n_blk % unroll2 == 0 and (unroll2 == 1 or tt == chunk)
    assert chunk & (chunk - 1) == 0 and tt & (tt - 1) == 0 and chunk % 8 == 0
    tok = pltpu.VMEM((m, d_r), F32)
    tok_mm = pltpu.VMEM((m, d_r), BF16 if chunk % (2 * SUBLANES) == 0 else F32)
    lane = jnp.arange(d_r)
    head_sum = (lane[:, None] // HEAD_DIM == lane[None, :] // HEAD_DIM).astype(BF16)
    rows = jnp.arange(chunk if chunk >= HEAD_DIM else m)
    same_chunk = rows[:, None] // chunk == rows[None, :] // chunk
    tri = (same_chunk & (rows[None, :] <= rows[:, None])).astype(BF16)
    blk = same_chunk.astype(BF16)
    return pl.pallas_call(
        functools.partial(_rwkv_body, chunk=chunk, unroll1=unroll1, unroll2=unroll2),
        grid=(b // bb, t_len // tt),
        in_specs=[
            pl.BlockSpec((bb, tt, d), lambda i, j: (i, j, 0)),
            _const_spec(mixg.shape), _const_spec(ws.shape),
            pl.BlockSpec((bb, 1, d_shift), lambda i, j: (i, 0, 0)),
            _const_spec(mu.shape), _const_spec(w0.shape), _const_spec(wdu.shape), _const_spec(a0.shape),
            _const_spec(wau.shape), _const_spec(wgu.shape), _const_spec(kk.shape), _const_spec(ka.shape),
            _const_spec(rk.shape), _const_spec(gng.shape), _const_spec(gnb.shape),
            pl.BlockSpec((bb, n_heads, HEAD_DIM, HEAD_DIM), lambda i, j: (i, 0, 0, 0)),
            _const_spec(head_sum.shape), _const_spec(tri.shape), _const_spec(blk.shape),
            _const_spec(wc.shape), _const_spec(glub.shape),
            pl.BlockSpec((bb, CONV_WIDTH - 1, dc), lambda i, j: (i, 0, 0)),
            _const_spec(cw.shape), _const_spec(cb.shape), _const_spec(lng.shape), _const_spec(lnb.shape),
        ],
        out_specs=[
            pl.BlockSpec((bb, tt, d_r), lambda i, j: (i, j, 0)),
            pl.BlockSpec((bb, 1, d_shift), lambda i, j: (i, 0, 0)),
            pl.BlockSpec((bb, n_heads, HEAD_DIM, HEAD_DIM), lambda i, j: (i, 0, 0, 0)),
            pl.BlockSpec((bb, tt, dc), lambda i, j: (i, j, 0)),
            pl.BlockSpec((bb, CONV_WIDTH - 1, dc), lambda i, j: (i, 0, 0)),
        ],
        out_shape=[
            jax.ShapeDtypeStruct((b, t_len, d_r), BF16),
            jax.ShapeDtypeStruct((b, 1, d_shift), F32),
            jax.ShapeDtypeStruct((b, n_heads, HEAD_DIM, HEAD_DIM), F32),
            jax.ShapeDtypeStruct((b, t_len, dc), BF16),
            jax.ShapeDtypeStruct((b, CONV_WIDTH - 1, dc), F32),
        ],
        scratch_shapes=[
            pltpu.VMEM((bb, 1, d_shift), F32),
            pltpu.VMEM((bb, n_pairs, PAIR, PAIR), F32),
        ] + [tok_mm] * 7 + [tok] * 4 + [
            pltpu.VMEM((n_blk, n_pairs, PAIR, PAIR), BF16),
            pltpu.VMEM((n_blk, n_pairs, PAIR, PAIR), F32),
            pltpu.VMEM((n_blk, n_pairs, 2 * chunk, PAIR), BF16),
            pltpu.VMEM((n_blk, n_pairs, 2 * chunk, PAIR), F32),
            pltpu.VMEM((bb, CONV_PAD + tt, dc), F32),
            pltpu.VMEM((SUBLANES - 1, bb, CONV_PAD + tt - SUBLANES, dc), F32),
            pltpu.VMEM((m, dc), F32),
        ],
        compiler_params=_params("parallel", "arbitrary"),
        name="rwkv_conv",
    )(x, mixg, ws, sprev, mu, w0, wdu, a0, wau, wgu, kk, ka, rk, gng, gnb, s0, head_sum, tri, blk,
      wc, glub, cprev, cw, cb, lng, lnb)


def _xattn_body(x_ref, mixg_ref, wq_ref, kt_ref, vt_ref, o_ref, *, rows, group):
    bb_n, tt, d = x_ref.shape
    m = bb_n * tt
    dh = kt_ref.shape[1] // N_XATTN_HEADS
    xn = _rms(x_ref[...].reshape(m, d), mixg_ref[...]).astype(BF16)
    q = (_mm(xn, wq_ref[...]) * (dh ** -0.5)).astype(BF16)
    n_blocks = m // rows
    for g0 in range(0, n_blocks, group):
        chains = [(r, h) for r in range(g0, min(g0 + group, n_blocks)) for h in range(N_XATTN_HEADS)]
        hs = lambda h: slice(h * dh, (h + 1) * dh)
        seq = lambda r: (r * rows) // tt
        s = [_mm(q[r * rows:(r + 1) * rows, hs(h)], kt_ref[seq(r), hs(h), :].astype(BF16)) for r, h in chains]
        p = [jnp.exp(x - jnp.max(x, axis=-1, keepdims=True)) for x in s]
        l = [jnp.sum(x, axis=-1, keepdims=True) for x in p]
        o = [_nt(p[i].astype(BF16), vt_ref[seq(r), hs(h), :].astype(BF16)) / l[i] for i, (r, h) in enumerate(chains)]
        for j, r in enumerate(range(g0, min(g0 + group, n_blocks))):
            t0 = (r * rows) % tt
            heads = o[j * N_XATTN_HEADS:(j + 1) * N_XATTN_HEADS]
            o_ref[seq(r), t0:t0 + rows, :] = jnp.concatenate(heads, axis=-1).astype(BF16)


def _xattn(x, mixg, wq, kt, vt, *, bb, tt, rows, group):
    b, t_len, d = x.shape
    dx, n_mem = kt.shape[1:]
    assert tt % rows == 0
    return pl.pallas_call(
        functools.partial(_xattn_body, rows=rows, group=group),
        grid=(b // bb, t_len // tt),
        in_specs=[
            pl.BlockSpec((bb, tt, d), lambda i, j: (i, j, 0)),
            _const_spec(mixg.shape), _const_spec(wq.shape),
            pl.BlockSpec((bb, dx, n_mem), lambda i, j: (i, 0, 0)),
            pl.BlockSpec((bb, dx, n_mem), lambda i, j: (i, 0, 0)),
        ],
        out_specs=pl.BlockSpec((bb, tt, dx), lambda i, j: (i, j, 0)),
        out_shape=jax.ShapeDtypeStruct((b, t_len, dx), BF16),
        compiler_params=_params("parallel", "arbitrary"),
        name="xattn",
    )(x, mixg, wq, kt, vt)


def _merge_body(x_ref, mixg_ref, wg_ref, ya_ref, yb_ref, yc_ref, wro_ref, wco_ref, wxo_ref, wo_ref, o_ref):
    d = x_ref.shape[1]
    x = x_ref[...]
    xn = _rms(x, mixg_ref[...]).astype(BF16)
    merged = jnp.zeros_like(x)
    for i, (y_ref, w_ref) in enumerate(((ya_ref, wro_ref), (yb_ref, wco_ref), (yc_ref, wxo_ref))):
        gate = jax.nn.sigmoid(_mm(xn, wg_ref[:, i * d:(i + 1) * d]))
        merged = merged + gate * _mm(y_ref[...], w_ref[...])
    o_ref[...] = x + _mm(merged.astype(BF16), wo_ref[...])


def _merge(x, mixg, wg, ya, yb, yc, wro, wco, wxo, wo, *, tm=512):
    n, d = x.shape
    tm = min(tm, n)
    row = lambda w: pl.BlockSpec((tm, w), lambda i: (i, 0))
    return pl.pallas_call(
        _merge_body,
        grid=(n // tm,),
        in_specs=[row(d), _const_spec(mixg.shape), _const_spec(wg.shape), row(ya.shape[1]), row(yb.shape[1]),
                  row(yc.shape[1]), _const_spec(wro.shape), _const_spec(wco.shape), _const_spec(wxo.shape),
                  _const_spec(wo.shape)],
        out_specs=row(d),
        out_shape=jax.ShapeDtypeStruct((n, d), F32),
        compiler_params=_params("parallel"),
        name="merge",
    )(x, mixg, wg, ya, yb, yc, wro, wco, wxo, wo)


def _layer(x, shift_prev, conv_prev, wkv0, mem_kt, mem_vt, p, final_g, *, last, bb, tt, rw, xa):
    b, t_len, d = x.shape
    n = b * t_len
    x1 = _ffn(x.reshape(n, d), p['ffn1_w_gate'], p['ffn1_w_up'], p['ffn1_w_down'], final_g, final_norm=False)
    x1_3 = x1.reshape(b, t_len, d)
    og, shift, wkv, cb, conv = _rwkv(
        x1_3, p['mix_norm'], p['w_s'], shift_prev[:, None, :], p['mu_shift'], p['w0'], p['w_decay_up'], p['a0'],
        p['w_a_up'], p['w_g_up'], p['k_k'], p['k_a'], p['r_k'], p['gn_g'], p['gn_b'], wkv0,
        p['w_c'], p['glu_b'], conv_prev, p['conv_w'], p['conv_b'], p['conv_ln_g'], p['conv_ln_b'],
        bb=bb, tt=tt, **rw)
    oc = _xattn(x1_3, p['mix_norm'], p['w_q'], mem_kt, mem_vt, **xa)
    x2 = _merge(x1, p['mix_norm'], p['w_g'], og.reshape(n, -1), cb.reshape(n, -1), oc.reshape(n, -1),
                p['w_rwkv_out'], p['w_conv_out'], p['w_xattn_out'], p['w_o'])
    x3 = _ffn(x2, p['ffn2_w_gate'], p['ffn2_w_up'], p['ffn2_w_down'], final_g, final_norm=last)
    return x3.reshape(b, t_len, d), wkv, shift[:, 0, :], conv


def kernel(x_prompt, mem_prompt, x_sample, state_wkv, state_shift, state_conv, cache_mem_k, cache_mem_v,
           ffn1_norm, ffn1_w_up, ffn1_w_down, mix_norm, w_in, mu_shift, w0, w_decay_up, a0, w_a_up, w_g_up,
           k_k, k_a, r_k, gn_g, gn_b, w_rwkv_out, glu_b, conv_w, conv_b, conv_ln_g, conv_ln_b, w_conv_out,
           w_mem_kv, w_xattn_out, w_o, ffn2_norm, ffn2_w_up, ffn2_w_down, final_norm):
    depth = w_in.shape[0]
    d_model = w_in.shape[1]
    d_r = w0.shape[1]
    d_shift = mu_shift.shape[1]
    d_conv = conv_w.shape[2]
    d_x = w_xattn_out.shape[1]
    n_heads = d_r // HEAD_DIM
    o1 = d_shift
    o2 = o1 + 2 * d_conv
    o3 = o2 + d_x
    d_ff = ffn1_w_down.shape[1]
    row = lambda a: a.astype(F32).reshape(1, -1)
    scaled = lambda g, w: (g.astype(F32)[:, None] * w.astype(F32)).astype(BF16)
    final_g = row(final_norm)

    layers = []
    for l in range(depth):
        layers.append({
            'ffn1_w_gate': scaled(ffn1_norm[l], ffn1_w_up[l][:, :d_ff]),
            'ffn1_w_up': scaled(ffn1_norm[l], ffn1_w_up[l][:, d_ff:]),
            'ffn1_w_down': ffn1_w_down[l].astype(BF16), 'mix_norm': row(mix_norm[l]),
            'w_s': w_in[l, :, :o1].astype(BF16), 'w_c': w_in[l, :, o1:o2].astype(BF16),
            'w_q': w_in[l, :, o2:o3].astype(BF16), 'w_g': w_in[l, :, o3:].astype(BF16),
            'mu_shift': row(mu_shift[l]), 'w0': row(w0[l]), 'w_decay_up': w_decay_up[l].astype(F32),
            'a0': row(a0[l]), 'w_a_up': w_a_up[l].astype(F32), 'w_g_up': w_g_up[l].astype(F32),
            'k_k': row(k_k[l]), 'k_a': row(k_a[l]), 'r_k': row(r_k[l]), 'gn_g': row(gn_g[l]),
            'gn_b': row(gn_b[l]), 'w_rwkv_out': w_rwkv_out[l].astype(BF16), 'glu_b': row(glu_b[l]),
            'conv_w': conv_w[l].astype(F32), 'conv_b': row(conv_b[l]), 'conv_ln_g': row(conv_ln_g[l]),
            'conv_ln_b': row(conv_ln_b[l]), 'w_conv_out': w_conv_out[l].astype(BF16),
            'w_mem_kv': w_mem_kv[l].astype(BF16), 'w_mem_kv_t': w_mem_kv[l].T.astype(BF16), 'w_xattn_out': w_xattn_out[l].astype(BF16),
            'w_o': w_o[l].astype(BF16),
            'ffn2_w_gate': scaled(ffn2_norm[l], ffn2_w_up[l][:, :d_ff]),
            'ffn2_w_up': scaled(ffn2_norm[l], ffn2_w_up[l][:, d_ff:]), 'ffn2_w_down': ffn2_w_down[l].astype(BF16),
        })

    bp, n_mem, _ = mem_prompt.shape
    xp = x_prompt.astype(F32)
    wkv_p, shift_p, conv_p, mk_p, mv_p = [], [], [], [], []
    for l in range(depth):
        p = layers[l]
        mk, mv, mkt, mvt = _memkv(mem_prompt.astype(F32), p['w_mem_kv'], p['w_mem_kv_t'])
        xp, wkv, sh, cv = _layer(
            xp, jnp.zeros((bp, d_shift), F32), jnp.zeros((bp, CONV_WIDTH - 1, d_conv), F32),
            jnp.zeros((bp, n_heads, HEAD_DIM, HEAD_DIM), F32), mkt, mvt, p, final_g,
            last=(l == depth - 1), bb=1, tt=512, rw=dict(chunk=64, unroll1=4, unroll2=1),
            xa=dict(bb=1, tt=512, rows=256, group=2))
        wkv_p.append(wkv)
        shift_p.append(sh)
        conv_p.append(cv)
        mk_p.append(mk.reshape(bp, n_mem, N_XATTN_HEADS, d_x // N_XATTN_HEADS))
        mv_p.append(mv.reshape(bp, n_mem, N_XATTN_HEADS, d_x // N_XATTN_HEADS))

    bs, ts, _ = x_sample.shape
    xs = x_sample.astype(F32)
    wkv_s, shift_s, conv_s = [], [], []
    for l in range(depth):
        kt = jnp.transpose(cache_mem_k[l].astype(F32), (0, 2, 3, 1)).reshape(bs, d_x, n_mem)
        vt = jnp.transpose(cache_mem_v[l].astype(F32), (0, 2, 3, 1)).reshape(bs, d_x, n_mem)
        xs, wkv, sh, cv = _layer(
            xs, state_shift[l].astype(F32), state_conv[l].astype(F32), state_wkv[l].astype(F32), kt, vt,
            layers[l], final_g, last=(l == depth - 1), bb=16, tt=ts, rw=dict(chunk=ts, unroll1=4, unroll2=4),
            xa=dict(bb=16, tt=ts, rows=ts, group=4))
        wkv_s.append(wkv)
        shift_s.append(sh)
        conv_s.append(cv)

    return (xp.astype(x_prompt.dtype), xs.astype(x_sample.dtype), jnp.stack(wkv_p), jnp.stack(shift_p),
            jnp.stack(conv_p), jnp.stack(mk_p), jnp.stack(mv_p), jnp.stack(wkv_s), jnp.stack(shift_s),
            jnp.stack(conv_s))
```

```python
import functools
import math

import jax
import jax.numpy as jnp
from jax import lax
from jax.experimental import pallas as pl
from jax.experimental.pallas import tpu as pltpu

F32 = jnp.float32
BF16 = jnp.bfloat16
HI = lax.Precision.HIGHEST

HEAD_DIM = 64
PAIR = 2 * HEAD_DIM
N_XATTN_HEADS = 4
CONV_WIDTH = 31
SUBLANES = 8
CONV_PAD = 32
LORA_DECAY = 64
LORA_A = 64
RMS_EPS = 1e-6
LN_EPS = 1e-5
GN_EPS = 64e-5
VMEM_LIMIT = 56 * 1024 * 1024


def _params(*sem):
    return pltpu.CompilerParams(dimension_semantics=sem, vmem_limit_bytes=VMEM_LIMIT)


def _rms(x, g):
    return x * lax.rsqrt(jnp.mean(x * x, axis=-1, keepdims=True) + RMS_EPS) * g


def _mm(a, b, precision=None):
    return jnp.dot(a, b, preferred_element_type=F32, precision=precision)


def _nt(a, b):
    return lax.dot_general(a, b, (((1,), (1,)), ((), ())), preferred_element_type=F32)


def _tn(a, b):
    return lax.dot_general(a, b, (((0,), (0,)), ((), ())), preferred_element_type=F32)


def _split(x):
    hi = x.astype(BF16)
    return hi, (x - hi.astype(F32)).astype(BF16)


def _exact_lhs_dot(sel, x):
    hi, lo = _split(x)
    return _mm(sel, hi) + _mm(sel, lo)


def _dot3(x, w):
    x_hi, x_lo = _split(x)
    w_hi, w_lo = _split(w)
    return _mm(x_hi, w_hi) + _mm(x_lo, w_hi) + _mm(x_hi, w_lo)


def _dot1(x, w):
    return _mm(x.astype(BF16), w.astype(BF16))


def _const_spec(shape):
    nd = len(shape)
    return pl.BlockSpec(shape, lambda *_: (0,) * nd)


def _ffn_body(x_ref, wg_ref, wu_ref, wd_ref, fg_ref, o_ref, *, final_norm):
    x = x_ref[...]
    xb = x.astype(BF16)
    inv = lax.rsqrt(jnp.mean(x * x, axis=-1, keepdims=True) + RMS_EPS)
    hg = _mm(xb, wg_ref[...]) * inv
    hu = _mm(xb, wu_ref[...]) * inv
    h = (hg * jax.nn.sigmoid(hg) * hu).astype(BF16)
    y = x + 0.5 * _mm(h, wd_ref[...])
    if final_norm:
        y = _rms(y, fg_ref[...])
    o_ref[...] = y


def _resident_spec(shape):
    nd = len(shape)
    return pl.BlockSpec(shape, lambda *_: (0,) * nd, pipeline_mode=pl.Buffered(1))


def _ffn(x, w_gate, w_up, w_down, final_g, *, final_norm, tm=512):
    n, d = x.shape
    tm = min(tm, n)
    assert n % tm == 0
    return pl.pallas_call(
        functools.partial(_ffn_body, final_norm=final_norm),
        grid=(n // tm,),
        in_specs=[
            pl.BlockSpec((tm, d), lambda i: (i, 0)),
            _resident_spec(w_gate.shape), _resident_spec(w_up.shape), _resident_spec(w_down.shape),
            _const_spec((1, d)),
        ],
        out_specs=pl.BlockSpec((tm, d), lambda i: (i, 0)),
        out_shape=jax.ShapeDtypeStruct((n, d), F32),
        compiler_params=_params("parallel"),
        name="ffn_final" if final_norm else "ffn",
    )(x, w_gate, w_up, w_down, final_g)


def _memkv_body(m_ref, w_ref, wt_ref, k_ref, v_ref, kt_ref, vt_ref):
    mem = m_ref[0].astype(BF16)
    dk = k_ref.shape[-1]
    kv = _mm(mem, w_ref[...])
    k_ref[0] = kv[:, :dk]
    v_ref[0] = kv[:, dk:]
    kvt = _nt(wt_ref[...], mem)
    kt_ref[0] = kvt[:dk].astype(BF16)
    vt_ref[0] = kvt[dk:].astype(BF16)


def _memkv(mem, w, wt):
    b, n_mem, d = mem.shape
    dk = w.shape[1] // 2
    return pl.pallas_call(
        _memkv_body,
        grid=(b,),
        in_specs=[pl.BlockSpec((1, n_mem, d), lambda i: (i, 0, 0)), _const_spec(w.shape), _const_spec(wt.shape)],
        out_specs=[pl.BlockSpec((1, n_mem, dk), lambda i: (i, 0, 0))] * 2
        + [pl.BlockSpec((1, dk, n_mem), lambda i: (i, 0, 0))] * 2,
        out_shape=[jax.ShapeDtypeStruct((b, n_mem, dk), F32)] * 2
        + [jax.ShapeDtypeStruct((b, dk, n_mem), BF16)] * 2,
        compiler_params=_params("parallel"),
        name="memkv",
    )(mem, w, wt)


def _rwkv_body(x_ref, mixg_ref, ws_ref, sprev_ref, mu_ref, w0_ref, wdu_ref, a0_ref, wau_ref, wgu_ref,
               kk_ref, ka_ref, rk_ref, gng_ref, gnb_ref, s0_ref, hsum_ref, tri_ref, blk_ref,
               wc_ref, glub_ref, cprev_ref, cw_ref, cb_ref, lng_ref, lnb_ref,
               og_ref, sout_ref, wkv_ref, yc_ref, cnew_ref,
               carry, st, at_s, rt_s, bb_s, kb_s, bh_s, kh_s, v_s, wt_s, bonus_s, g_s, o_s,
               gm_s, hm_s, rp_s, oi_s, ubuf, ushift, yc_s, *, chunk, unroll1, unroll2):
    bb_n, tt, d = x_ref.shape
    m = bb_n * tt
    d_shift = ws_ref.shape[1]
    d_r = w0_ref.shape[1]
    dc = cw_ref.shape[1]
    n_pairs = d_r // PAIR
    n_ck = tt // chunk
    n_blk = m // chunk
    log_chunk = int(math.log2(chunk))
    c2 = 2 * chunk
    conv_lo = CONV_PAD - (CONV_WIDTH - 1)
    t = pl.program_id(1)

    @pl.when(t == 0)
    def _():
        carry[...] = sprev_ref[...]
        ubuf[:, conv_lo:CONV_PAD, :] = cprev_ref[...]
        zero = jnp.zeros((HEAD_DIM, HEAD_DIM), F32)
        for b_i in range(bb_n):
            for p in range(n_pairs):
                top = jnp.concatenate([s0_ref[b_i, 2 * p], zero], axis=1)
                bot = jnp.concatenate([zero, s0_ref[b_i, 2 * p + 1]], axis=1)
                st[b_i, p] = jnp.concatenate([top, bot], axis=0)

    xn = _rms(x_ref[...].reshape(m, d), mixg_ref[...]).astype(BF16)
    zc = _mm(xn, wc_ref[...]) + glub_ref[...]
    ubuf[:, CONV_PAD:CONV_PAD + tt, :] = (zc[:, :dc] * jax.nn.sigmoid(zc[:, dc:])).reshape(bb_n, tt, dc)
    for sh in range(1, SUBLANES):
        ushift[sh - 1] = ubuf[:, sh:sh + tt + CONV_PAD - SUBLANES, :]

    zs = _mm(xn, ws_ref[...])
    rolled = pltpu.roll(zs, 1, axis=0)
    prev0 = jnp.broadcast_to(carry[...], (bb_n, tt, d_shift)).reshape(m, d_shift)
    row = lax.broadcasted_iota(jnp.int32, (m, d_shift), 0)
    prev = jnp.where((row & (tt - 1)) == 0, prev0, rolled)
    last = zs.reshape(bb_n, tt, d_shift)[:, tt - 1:tt, :]
    carry[...] = last
    sout_ref[...] = last
    xm = zs + (prev - zs) * mu_ref[...]

    c1, c2_, c3 = d_r, 2 * d_r, 3 * d_r
    c4 = c3 + LORA_DECAY
    c5 = c4 + LORA_A
    r, k, v = xm[:, :c1], xm[:, c1:c2_], xm[:, c2_:c3]
    wd, ad, gd = xm[:, c3:c4], xm[:, c4:c5], xm[:, c5:]
    dec_in = w0_ref[...] + _dot3(jnp.tanh(wd), wdu_ref[...])
    neg = -dec_in
    softplus = jnp.maximum(neg, 0.0) + jnp.log(1.0 + jnp.exp(-jnp.abs(neg)))
    ld = -jnp.exp(-softplus - 0.5)
    a = jax.nn.sigmoid(a0_ref[...] + _dot1(ad, wau_ref[...]))
    g_s[...] = _dot1(jax.nn.sigmoid(gd), wgu_ref[...])
    kh = k * (1.0 + (a - 1.0) * ka_ref[...])

    head_sum = hsum_ref[...]
    kkraw = k * kk_ref[...]
    kk = kkraw / jnp.maximum(jnp.sqrt(_mm((kkraw * kkraw).astype(BF16), head_sum)), 1e-12)
    b = kk * a
    bonus_s[...] = _mm((r * kh * rk_ref[...]).astype(BF16), head_sum) * v

    tdt = at_s.dtype
    v_s[...] = v.astype(tdt)
    slab = tri_ref.shape[0]
    for s0 in range(0, m, slab):
        sl = slice(s0, s0 + slab)
        ld_c = ld[sl]
        cum = _exact_lhs_dot(tri_ref[...], ld_c)
        tot = _exact_lhs_dot(blk_ref[...], ld_c)
        w_inv = jnp.exp(-cum)
        w_rem = jnp.exp(tot - cum)
        at_s[sl, :] = (-kk[sl] * jnp.exp(cum - ld_c)).astype(tdt)
        rt_s[sl, :] = (r[sl] * jnp.exp(cum)).astype(tdt)
        bb_s[sl, :] = (b[sl] * w_inv).astype(tdt)
        kb_s[sl, :] = (kh[sl] * w_inv).astype(tdt)
        bh_s[sl, :] = (b[sl] * w_rem).astype(tdt)
        kh_s[sl, :] = (kh[sl] * w_rem).astype(tdt)
        wt_s[sl, :] = jnp.exp(tot)

    lane_lo = lax.broadcasted_iota(jnp.int32, (chunk, PAIR), 1) < HEAD_DIM
    r2 = lax.broadcasted_iota(jnp.int32, (c2, c2), 0)
    q2 = lax.broadcasted_iota(jnp.int32, (c2, c2), 1)
    strict = (q2 & (chunk - 1)) < (r2 & (chunk - 1))
    incl = (q2 & (chunk - 1)) <= (r2 & (chunk - 1))
    eye2 = (r2 == q2).astype(F32)
    wide = c2 % PAIR == 0

    def stack(ref, rows, ls):
        x = ref[rows, ls]
        return jnp.concatenate([jnp.where(lane_lo, x, 0.0), jnp.where(lane_lo, 0.0, x)], axis=0).astype(BF16)

    def phase1_step(i, c):
        chains = [(unroll1 * i + j, p) for j in range(unroll1) for p in range(n_pairs)]
        n = range(len(chains))

        def stacks(ref):
            return [stack(ref, pl.ds(pl.multiple_of(blk * chunk, chunk), chunk), slice(p * PAIR, (p + 1) * PAIR))
                    for blk, p in chains]

        at, rt, bbm, kbm = stacks(at_s), stacks(rt_s), stacks(bb_s), stacks(kb_s)
        bhm, khm, vm = stacks(bh_s), stacks(kh_s), stacks(v_s)
        if wide:
            a4 = [_nt(jnp.concatenate([at[c_], rt[c_]], axis=0), jnp.concatenate([bbm[c_], kbm[c_]], axis=0))
                  for c_ in n]
            a_ab = [jnp.where(strict, a4[c_][:c2, :c2], 0.0) for c_ in n]
            a_ak = [jnp.where(strict, a4[c_][:c2, c2:], 0.0).astype(BF16) for c_ in n]
            a_rb = [jnp.where(incl, a4[c_][c2:, :c2], 0.0).astype(BF16) for c_ in n]
            a_rk = [jnp.where(incl, a4[c_][c2:, c2:], 0.0).astype(BF16) for c_ in n]
        else:
            a_ab = [jnp.where(strict, _nt(at[c_], bbm[c_]), 0.0) for c_ in n]
            a_ak = [jnp.where(strict, _nt(at[c_], kbm[c_]), 0.0).astype(BF16) for c_ in n]
            a_rb = [jnp.where(incl, _nt(rt[c_], bbm[c_]), 0.0).astype(BF16) for c_ in n]
            a_rk = [jnp.where(incl, _nt(rt[c_], kbm[c_]), 0.0).astype(BF16) for c_ in n]
        tinv = [eye2 + a_ab[c_] for c_ in n]
        ap = [a_ab[c_].astype(BF16) for c_ in n]
        ap = [_mm(ap[c_], ap[c_]).astype(BF16) for c_ in n]
        for i_sq in range(1, log_chunk):
            if i_sq == log_chunk - 1:
                tinv = [tinv[c_] + _mm(ap[c_], tinv[c_].astype(BF16)) for c_ in n]
            elif wide:
                x = [_mm(ap[c_], jnp.concatenate([ap[c_], tinv[c_].astype(BF16)], axis=1)) for c_ in n]
                ap = [x[c_][:, :c2].astype(BF16) for c_ in n]
                tinv = [tinv[c_] + x[c_][:, c2:] for c_ in n]
            else:
                tinv = [tinv[c_] + _mm(ap[c_], tinv[c_].astype(BF16)) for c_ in n]
                ap = [_mm(ap[c_], ap[c_]).astype(BF16) for c_ in n]
        av = [_mm(a_ak[c_], vm[c_]).astype(BF16) for c_ in n]
        pq = [_mm(tinv[c_].astype(BF16), jnp.concatenate([av[c_], at[c_]], axis=1)) for c_ in n]
        pm = [pq[c_][:, :PAIR].astype(BF16) for c_ in n]
        qm = [pq[c_][:, PAIR:].astype(BF16) for c_ in n]
        for c_, (blk, p) in enumerate(chains):
            gm_s[blk, p] = _tn(qm[c_], bhm[c_]).astype(BF16)
        for c_, (blk, p) in enumerate(chains):
            hm_s[blk, p] = _tn(jnp.concatenate([pm[c_], vm[c_]], axis=0),
                               jnp.concatenate([bhm[c_], khm[c_]], axis=0))
        rq = [_mm(a_rb[c_], jnp.concatenate([qm[c_], pm[c_]], axis=1)) for c_ in n]
        for c_, (blk, p) in enumerate(chains):
            rp_s[blk, p] = (rt[c_].astype(F32) + rq[c_][:, :PAIR]).astype(BF16)
        for c_, (blk, p) in enumerate(chains):
            oi_s[blk, p] = rq[c_][:, PAIR:] + _mm(a_rk[c_], vm[c_])
        return c

    lax.fori_loop(0, n_blk // unroll1, phase1_step, 0)

    def conv_block(blk):
        b_i, t0 = blk // n_ck, (blk % n_ck) * chunk
        acc = jnp.zeros((chunk, dc), F32) + cb_ref[...]
        for kx in range(CONV_WIDTH):
            off = conv_lo + kx
            sh, base = off % SUBLANES, off - off % SUBLANES
            rows = pl.ds(pl.multiple_of(t0 + base, SUBLANES), chunk)
            win = ubuf[b_i, rows, :] if sh == 0 else ushift[sh - 1, b_i, rows, :]
            acc = acc + win * cw_ref[kx:kx + 1, :]
        cm = jnp.mean(acc, axis=-1, keepdims=True)
        cc = acc - cm
        cv = jnp.mean(cc * cc, axis=-1, keepdims=True)
        cn = cc * lax.rsqrt(cv + LN_EPS) * lng_ref[...] + lnb_ref[...]
        yc_s[pl.ds(pl.multiple_of(blk * chunk, chunk), chunk), :] = cn * jax.nn.sigmoid(cn)

    def phase2_step(i, c):
        for j in range(unroll2):
            conv_block(unroll2 * i + j)
        chains = [(unroll2 * i + j, p) for j in range(unroll2) for p in range(n_pairs)]
        n = range(len(chains))
        row0 = [pl.multiple_of(blk * chunk, chunk) for blk, _ in chains]
        ls = [slice(p * PAIR, (p + 1) * PAIR) for _, p in chains]
        s = [st[blk // n_ck, p] for blk, p in chains]
        sb = [s[c_].astype(BF16) for c_ in n]
        s_new = [s[c_] * wt_s[pl.ds(row0[c_], 1), ls[c_]] + _mm(sb[c_], gm_s[blk, p]) + hm_s[blk, p]
                 for c_, (blk, p) in enumerate(chains)]
        o_bd = [oi_s[blk, p] + _nt(rp_s[blk, p], sb[c_]) for c_, (blk, p) in enumerate(chains)]
        for c_, (blk, p) in enumerate(chains):
            st[blk // n_ck, p] = s_new[c_]
        for c_ in n:
            o_s[pl.ds(row0[c_], chunk), ls[c_]] = o_bd[c_][:chunk] + o_bd[c_][chunk:]
        return c

    lax.fori_loop(0, n_blk // unroll2, phase2_step, 0)

    o = o_s[...]
    mean = _mm(o.astype(BF16), head_sum) * (1.0 / HEAD_DIM)
    oc = o - mean
    var = _mm((oc * oc).astype(BF16), head_sum) * (1.0 / HEAD_DIM)
    o = oc * lax.rsqrt(var + GN_EPS) * gng_ref[...] + gnb_ref[...] + bonus_s[...]
    og_ref[...] = (o * g_s[...]).astype(BF16).reshape(bb_n, tt, d_r)

    yc_ref[...] = yc_s[...].astype(BF16).reshape(bb_n, tt, dc)
    tail = ubuf[:, conv_lo + tt:CONV_PAD + tt, :]
    cnew_ref[...] = tail
    ubuf[:, conv_lo:CONV_PAD, :] = tail

    @pl.when(t == pl.num_programs(1) - 1)
    def _():
        for b_i in range(bb_n):
            for p in range(n_pairs):
                s = st[b_i, p]
                wkv_ref[b_i, 2 * p] = s[:HEAD_DIM, :HEAD_DIM]
                wkv_ref[b_i, 2 * p + 1] = s[HEAD_DIM:, HEAD_DIM:]


def _rwkv(x, mixg, ws, sprev, mu, w0, wdu, a0, wau, wgu, kk, ka, rk, gng, gnb, s0,
          wc, glub, cprev, cw, cb, lng, lnb, *, bb, tt, chunk, unroll1, unroll2):
    b, t_len, d = x.shape
    d_shift = ws.shape[1]
    d_r = w0.shape[1]
    dc = cw.shape[1]
    n_heads = d_r // HEAD_DIM
    n_pairs = d_r // PAIR
    m = bb * tt
    n_blk = m // chunk
    assert b % bb == 0 and t_len % tt == 0 and tt % chunk == 0
    assert n_blk % unroll1 == 0 and n_blk % unroll2 == 0 and (unroll2 == 1 or tt == chunk)
    assert chunk & (chunk - 1) == 0 and tt & (tt - 1) == 0 and chunk % 8 == 0
    tok = pltpu.VMEM((m, d_r), F32)
    tok_mm = pltpu.VMEM((m, d_r), BF16 if chunk % (2 * SUBLANES) == 0 else F32)
    lane = jnp.arange(d_r)
    head_sum = (lane[:, None] // HEAD_DIM == lane[None, :] // HEAD_DIM).astype(BF16)
    rows = jnp.arange(chunk if chunk >= HEAD_DIM else m)
    same_chunk = rows[:, None] // chunk == rows[None, :] // chunk
    tri = (same_chunk & (rows[None, :] <= rows[:, None])).astype(BF16)
    blk = same_chunk.astype(BF16)
    return pl.pallas_call(
        functools.partial(_rwkv_body, chunk=chunk, unroll1=unroll1, unroll2=unroll2),
        grid=(b // bb, t_len // tt),
        in_specs=[
            pl.BlockSpec((bb, tt, d), lambda i, j: (i, j, 0)),
            _const_spec(mixg.shape), _const_spec(ws.shape),
            pl.BlockSpec((bb, 1, d_shift), lambda i, j: (i, 0, 0)),
            _const_spec(mu.shape), _const_spec(w0.shape), _const_spec(wdu.shape), _const_spec(a0.shape),
            _const_spec(wau.shape), _const_spec(wgu.shape), _const_spec(kk.shape), _const_spec(ka.shape),
            _const_spec(rk.shape), _const_spec(gng.shape), _const_spec(gnb.shape),
            pl.BlockSpec((bb, n_heads, HEAD_DIM, HEAD_DIM), lambda i, j: (i, 0, 0, 0)),
            _const_spec(head_sum.shape), _const_spec(tri.shape), _const_spec(blk.shape),
            _const_spec(wc.shape), _const_spec(glub.shape),
            pl.BlockSpec((bb, CONV_WIDTH - 1, dc), lambda i, j: (i, 0, 0)),
            _const_spec(cw.shape), _const_spec(cb.shape), _const_spec(lng.shape), _const_spec(lnb.shape),
        ],
        out_specs=[
            pl.BlockSpec((bb, tt, d_r), lambda i, j: (i, j, 0)),
            pl.BlockSpec((bb, 1, d_shift), lambda i, j: (i, 0, 0)),
            pl.BlockSpec((bb, n_heads, HEAD_DIM, HEAD_DIM), lambda i, j: (i, 0, 0, 0)),
            pl.BlockSpec((bb, tt, dc), lambda i, j: (i, j, 0)),
            pl.BlockSpec((bb, CONV_WIDTH - 1, dc), lambda i, j: (i, 0, 0)),
        ],
        out_shape=[
            jax.ShapeDtypeStruct((b, t_len, d_r), BF16),
            jax.ShapeDtypeStruct((b, 1, d_shift), F32),
            jax.ShapeDtypeStruct((b, n_heads, HEAD_DIM, HEAD_DIM), F32),
            jax.ShapeDtypeStruct((b, t_len, dc), BF16),
            jax.ShapeDtypeStruct((b, CONV_WIDTH - 1, dc), F32),
        ],
        scratch_shapes=[
            pltpu.VMEM((bb, 1, d_shift), F32),
            pltpu.VMEM((bb, n_pairs, PAIR, PAIR), F32),
        ] + [tok_mm] * 7 + [tok] * 4 + [
            pltpu.VMEM((n_blk, n_pairs, PAIR, PAIR), BF16),
            pltpu.VMEM((n_blk, n_pairs, PAIR, PAIR), F32),
            pltpu.VMEM((n_blk, n_pairs, 2 * chunk, PAIR), BF16),
            pltpu.VMEM((n_blk, n_pairs, 2 * chunk, PAIR), F32),
            pltpu.VMEM((bb, CONV_PAD + tt, dc), F32),
            pltpu.VMEM((SUBLANES - 1, bb, CONV_PAD + tt - SUBLANES, dc), F32),
            pltpu.VMEM((m, dc), F32),
        ],
        compiler_params=_params("parallel", "arbitrary"),
        name="rwkv_conv",
    )(x, mixg, ws, sprev, mu, w0, wdu, a0, wau, wgu, kk, ka, rk, gng, gnb, s0, head_sum, tri, blk,
      wc, glub, cprev, cw, cb, lng, lnb)


def _merge_body(x_ref, wq_ref, wg_ref, kt_ref, vt_ref, ya_ref, yb_ref, wro_ref, wco_ref, wxo_ref, wo_ref, o_ref,
                *, t_len, rows, group):
    tm, d = x_ref.shape
    dh = kt_ref.shape[1] // N_XATTN_HEADS
    x = x_ref[...]
    xb = x.astype(BF16)
    inv = lax.rsqrt(jnp.mean(x * x, axis=-1, keepdims=True) + RMS_EPS)

    q = (_mm(xb, wq_ref[...]) * (inv * dh ** -0.5)).astype(BF16)
    n_blocks = tm // rows
    hs = lambda h: slice(h * dh, (h + 1) * dh)
    seq = lambda r: (r * rows) // t_len
    pieces = []
    for g0 in range(0, n_blocks, group):
        blocks = range(g0, min(g0 + group, n_blocks))
        chains = [(r, h) for r in blocks for h in range(N_XATTN_HEADS)]
        s = [_mm(q[r * rows:(r + 1) * rows, hs(h)], kt_ref[seq(r), hs(h), :].astype(BF16)) for r, h in chains]
        p = [jnp.exp(v - jnp.max(v, axis=-1, keepdims=True)) for v in s]
        l = [jnp.sum(v, axis=-1, keepdims=True) for v in p]
        o = [_nt(p[i].astype(BF16), vt_ref[seq(r), hs(h), :].astype(BF16)) / l[i] for i, (r, h) in enumerate(chains)]
        for j in range(len(blocks)):
            pieces.append(jnp.concatenate(o[j * N_XATTN_HEADS:(j + 1) * N_XATTN_HEADS], axis=-1))
    oc = jnp.concatenate(pieces, axis=0).astype(BF16)

    merged = jnp.zeros_like(x)
    for i, (y, w_ref) in enumerate(((ya_ref[...], wro_ref), (yb_ref[...], wco_ref), (oc, wxo_ref))):
        gate = jax.nn.sigmoid(_mm(xb, wg_ref[:, i * d:(i + 1) * d]) * inv)
        merged = merged + gate * _mm(y, w_ref[...])
    o_ref[...] = x + _mm(merged.astype(BF16), wo_ref[...])


def _merge(x, wq, wg, kt, vt, ya, yb, wro, wco, wxo, wo, *, t_len, tm, rows, group):
    n, d = x.shape
    dx, n_mem = kt.shape[1:]
    seqs = max(tm // t_len, 1)
    assert n % tm == 0 and tm % rows == 0 and (tm % t_len == 0 or t_len % tm == 0) and rows <= t_len
    row = lambda w: pl.BlockSpec((tm, w), lambda i: (i, 0))
    mem = pl.BlockSpec((seqs, dx, n_mem), lambda i: ((i * tm) // (t_len * seqs), 0, 0))
    return pl.pallas_call(
        functools.partial(_merge_body, t_len=t_len, rows=rows, group=group),
        grid=(n // tm,),
        in_specs=[row(d), _resident_spec(wq.shape), _resident_spec(wg.shape), mem, mem, row(ya.shape[1]),
                  row(yb.shape[1]), _resident_spec(wro.shape), _resident_spec(wco.shape),
                  _resident_spec(wxo.shape), _resident_spec(wo.shape)],
        out_specs=row(d),
        out_shape=jax.ShapeDtypeStruct((n, d), F32),
        compiler_params=_params("parallel"),
        name="merge",
    )(x, wq, wg, kt, vt, ya, yb, wro, wco, wxo, wo)


def _layer(x, shift_prev, conv_prev, wkv0, mem_kt, mem_vt, p, final_g, *, last, bb, tt, rw, mg):
    b, t_len, d = x.shape
    n = b * t_len
    x1 = _ffn(x.reshape(n, d), p['ffn1_w_gate'], p['ffn1_w_up'], p['ffn1_w_down'], final_g, final_norm=False)
    x1_3 = x1.reshape(b, t_len, d)
    og, shift, wkv, cb, conv = _rwkv(
        x1_3, p['mix_norm'], p['w_s'], shift_prev[:, None, :], p['mu_shift'], p['w0'], p['w_decay_up'], p['a0'],
        p['w_a_up'], p['w_g_up'], p['k_k'], p['k_a'], p['r_k'], p['gn_g'], p['gn_b'], wkv0,
        p['w_c'], p['glu_b'], conv_prev, p['conv_w'], p['conv_b'], p['conv_ln_g'], p['conv_ln_b'],
        bb=bb, tt=tt, **rw)
    x2 = _merge(x1, p['w_q'], p['w_g'], mem_kt, mem_vt, og.reshape(n, -1), cb.reshape(n, -1),
                p['w_rwkv_out'], p['w_conv_out'], p['w_xattn_out'], p['w_o'], t_len=t_len, **mg)
    x3 = _ffn(x2, p['ffn2_w_gate'], p['ffn2_w_up'], p['ffn2_w_down'], final_g, final_norm=last)
    return x3.reshape(b, t_len, d), wkv, shift[:, 0, :], conv


def kernel(x_prompt, mem_prompt, x_sample, state_wkv, state_shift, state_conv, cache_mem_k, cache_mem_v,
           ffn1_norm, ffn1_w_up, ffn1_w_down, mix_norm, w_in, mu_shift, w0, w_decay_up, a0, w_a_up, w_g_up,
           k_k, k_a, r_k, gn_g, gn_b, w_rwkv_out, glu_b, conv_w, conv_b, conv_ln_g, conv_ln_b, w_conv_out,
           w_mem_kv, w_xattn_out, w_o, ffn2_norm, ffn2_w_up, ffn2_w_down, final_norm):
    depth = w_in.shape[0]
    d_model = w_in.shape[1]
    d_r = w0.shape[1]
    d_shift = mu_shift.shape[1]
    d_conv = conv_w.shape[2]
    d_x = w_xattn_out.shape[1]
    n_heads = d_r // HEAD_DIM
    o1 = d_shift
    o2 = o1 + 2 * d_conv
    o3 = o2 + d_x
    d_ff = ffn1_w_down.shape[1]
    row = lambda a: a.astype(F32).reshape(1, -1)
    scaled = lambda g, w: (g.astype(F32)[:, None] * w.astype(F32)).astype(BF16)
    final_g = row(final_norm)

    layers = []
    for l in range(depth):
        layers.append({
            'ffn1_w_gate': scaled(ffn1_norm[l], ffn1_w_up[l][:, :d_ff]),
            'ffn1_w_up': scaled(ffn1_norm[l], ffn1_w_up[l][:, d_ff:]),
            'ffn1_w_down': ffn1_w_down[l].astype(BF16), 'mix_norm': row(mix_norm[l]),
            'w_s': w_in[l, :, :o1].astype(BF16), 'w_c': w_in[l, :, o1:o2].astype(BF16),
            'w_q': scaled(mix_norm[l], w_in[l, :, o2:o3]), 'w_g': scaled(mix_norm[l], w_in[l, :, o3:]),
            'mu_shift': row(mu_shift[l]), 'w0': row(w0[l]), 'w_decay_up': w_decay_up[l].astype(F32),
            'a0': row(a0[l]), 'w_a_up': w_a_up[l].astype(F32), 'w_g_up': w_g_up[l].astype(F32),
            'k_k': row(k_k[l]), 'k_a': row(k_a[l]), 'r_k': row(r_k[l]), 'gn_g': row(gn_g[l]),
            'gn_b': row(gn_b[l]), 'w_rwkv_out': w_rwkv_out[l].astype(BF16), 'glu_b': row(glu_b[l]),
            'conv_w': conv_w[l].astype(F32), 'conv_b': row(conv_b[l]), 'conv_ln_g': row(conv_ln_g[l]),
            'conv_ln_b': row(conv_ln_b[l]), 'w_conv_out': w_conv_out[l].astype(BF16),
            'w_mem_kv': w_mem_kv[l].astype(BF16), 'w_mem_kv_t': w_mem_kv[l].T.astype(BF16), 'w_xattn_out': w_xattn_out[l].astype(BF16),
            'w_o': w_o[l].astype(BF16),
            'ffn2_w_gate': scaled(ffn2_norm[l], ffn2_w_up[l][:, :d_ff]),
            'ffn2_w_up': scaled(ffn2_norm[l], ffn2_w_up[l][:, d_ff:]), 'ffn2_w_down': ffn2_w_down[l].astype(BF16),
        })

    bp, n_mem, _ = mem_prompt.shape
    xp = x_prompt.astype(F32)
    wkv_p, shift_p, conv_p, mk_p, mv_p = [], [], [], [], []
    for l in range(depth):
        p = layers[l]
        mk, mv, mkt, mvt = _memkv(mem_prompt.astype(F32), p['w_mem_kv'], p['w_mem_kv_t'])
        xp, wkv, sh, cv = _layer(
            xp, jnp.zeros((bp, d_shift), F32), jnp.zeros((bp, CONV_WIDTH - 1, d_conv), F32),
            jnp.zeros((bp, n_heads, HEAD_DIM, HEAD_DIM), F32), mkt, mvt, p, final_g,
            last=(l == depth - 1), bb=1, tt=512, rw=dict(chunk=64, unroll1=4, unroll2=1),
            mg=dict(tm=512, rows=256, group=2))
        wkv_p.append(wkv)
        shift_p.append(sh)
        conv_p.append(cv)
        mk_p.append(mk.reshape(bp, n_mem, N_XATTN_HEADS, d_x // N_XATTN_HEADS))
        mv_p.append(mv.reshape(bp, n_mem, N_XATTN_HEADS, d_x // N_XATTN_HEADS))

    bs, ts, _ = x_sample.shape
    xs = x_sample.astype(F32)
    wkv_s, shift_s, conv_s = [], [], []
    for l in range(depth):
        kt = jnp.transpose(cache_mem_k[l].astype(F32), (0, 2, 3, 1)).reshape(bs, d_x, n_mem)
        vt = jnp.transpose(cache_mem_v[l].astype(F32), (0, 2, 3, 1)).reshape(bs, d_x, n_mem)
        xs, wkv, sh, cv = _layer(
            xs, state_shift[l].astype(F32), state_conv[l].astype(F32), state_wkv[l].astype(F32), kt, vt,
            layers[l], final_g, last=(l == depth - 1), bb=16, tt=ts, rw=dict(chunk=ts, unroll1=4, unroll2=4),
            mg=dict(tm=16 * ts, rows=ts, group=4))
        wkv_s.append(wkv)
        shift_s.append(sh)
        conv_s.append(cv)

    return (xp.astype(x_prompt.dtype), xs.astype(x_sample.dtype), jnp.stack(wkv_p), jnp.stack(shift_p),
            jnp.stack(conv_p), jnp.stack(mk_p), jnp.stack(mv_p), jnp.stack(wkv_s), jnp.stack(shift_s),
            jnp.stack(conv_s))
```

```python
import functools
import math

import jax
import jax.numpy as jnp
from jax import lax
from jax.experimental import pallas as pl
from jax.experimental.pallas import tpu as pltpu

F32 = jnp.float32
BF16 = jnp.bfloat16
HI = lax.Precision.HIGHEST

HEAD_DIM = 64
PAIR = 2 * HEAD_DIM
N_XATTN_HEADS = 4
CONV_WIDTH = 31
SUBLANES = 8
CONV_PAD = 32
LORA_DECAY = 64
LORA_A = 64
RMS_EPS = 1e-6
LN_EPS = 1e-5
GN_EPS = 64e-5
VMEM_LIMIT = 56 * 1024 * 1024


def _params(*sem):
    return pltpu.CompilerParams(dimension_semantics=sem, vmem_limit_bytes=VMEM_LIMIT)


def _rms(x, g):
    return x * lax.rsqrt(jnp.mean(x * x, axis=-1, keepdims=True) + RMS_EPS) * g


def _mm(a, b, precision=None):
    return jnp.dot(a, b, preferred_element_type=F32, precision=precision)


def _nt(a, b):
    return lax.dot_general(a, b, (((1,), (1,)), ((), ())), preferred_element_type=F32)


def _tn(a, b):
    return lax.dot_general(a, b, (((0,), (0,)), ((), ())), preferred_element_type=F32)


def _split(x):
    hi = x.astype(BF16)
    return hi, (x - hi.astype(F32)).astype(BF16)


def _exact_lhs_dot(sel, x):
    hi, lo = _split(x)
    return _mm(sel, hi) + _mm(sel, lo)


def _dot3(x, w):
    x_hi, x_lo = _split(x)
    w_hi, w_lo = _split(w)
    return _mm(x_hi, w_hi) + _mm(x_lo, w_hi) + _mm(x_hi, w_lo)


def _dot1(x, w):
    return _mm(x.astype(BF16), w.astype(BF16))


def _const_spec(shape):
    nd = len(shape)
    return pl.BlockSpec(shape, lambda *_: (0,) * nd)


def _ffn_body(xa_ref, xb_ref, wg_ref, wu_ref, wd_ref, fg_ref, oa_ref, ob_ref, *, final_norm, tiles_a):
    in_a = pl.program_id(0) < tiles_a
    x = jnp.where(in_a, xa_ref[...], xb_ref[...])
    xb = x.astype(BF16)
    inv = lax.rsqrt(jnp.mean(x * x, axis=-1, keepdims=True) + RMS_EPS)
    hg = _mm(xb, wg_ref[...]) * inv
    hu = _mm(xb, wu_ref[...]) * inv
    h = (hg * jax.nn.sigmoid(hg) * hu).astype(BF16)
    y = x + 0.5 * _mm(h, wd_ref[...])
    if final_norm:
        y = _rms(y, fg_ref[...])

    @pl.when(in_a)
    def _():
        oa_ref[...] = y

    @pl.when(jnp.logical_not(in_a))
    def _():
        ob_ref[...] = y


def _resident_spec(shape):
    nd = len(shape)
    return pl.BlockSpec(shape, lambda *_: (0,) * nd, pipeline_mode=pl.Buffered(1))


def _ffn(xa, xb, w_gate, w_up, w_down, final_g, *, final_norm, tm=512):
    (na, d), nb = xa.shape, xb.shape[0]
    tm = min(tm, na, nb)
    assert na % tm == 0 and nb % tm == 0
    ta, tb = na // tm, nb // tm
    spec_a = pl.BlockSpec((tm, d), lambda i: (jnp.minimum(i, ta - 1), 0))
    spec_b = pl.BlockSpec((tm, d), lambda i: (jnp.maximum(i - ta, 0), 0))
    return pl.pallas_call(
        functools.partial(_ffn_body, final_norm=final_norm, tiles_a=ta),
        grid=(ta + tb,),
        in_specs=[
            spec_a, spec_b,
            _resident_spec(w_gate.shape), _resident_spec(w_up.shape), _resident_spec(w_down.shape),
            _const_spec((1, d)),
        ],
        out_specs=[spec_a, spec_b],
        out_shape=[jax.ShapeDtypeStruct((na, d), F32), jax.ShapeDtypeStruct((nb, d), F32)],
        compiler_params=_params("arbitrary"),
        name="ffn_final" if final_norm else "ffn",
    )(xa, xb, w_gate, w_up, w_down, final_g)


def _memkv_body(m_ref, w_ref, wt_ref, k_ref, v_ref, kt_ref, vt_ref):
    mem = m_ref[0].astype(BF16)
    dk = k_ref.shape[-1]
    kv = _mm(mem, w_ref[...])
    k_ref[0] = kv[:, :dk]
    v_ref[0] = kv[:, dk:]
    kvt = _nt(wt_ref[...], mem)
    kt_ref[0] = kvt[:dk].astype(BF16)
    vt_ref[0] = kvt[dk:].astype(BF16)


def _memkv(mem, w, wt):
    b, n_mem, d = mem.shape
    dk = w.shape[1] // 2
    return pl.pallas_call(
        _memkv_body,
        grid=(b,),
        in_specs=[pl.BlockSpec((1, n_mem, d), lambda i: (i, 0, 0)), _const_spec(w.shape), _const_spec(wt.shape)],
        out_specs=[pl.BlockSpec((1, n_mem, dk), lambda i: (i, 0, 0))] * 2
        + [pl.BlockSpec((1, dk, n_mem), lambda i: (i, 0, 0))] * 2,
        out_shape=[jax.ShapeDtypeStruct((b, n_mem, dk), F32)] * 2
        + [jax.ShapeDtypeStruct((b, dk, n_mem), BF16)] * 2,
        compiler_params=_params("parallel"),
        name="memkv",
    )(mem, w, wt)


def _rwkv_body(x_ref, mixg_ref, ws_ref, sprev_ref, mu_ref, w0_ref, wdu_ref, a0_ref, wau_ref, wgu_ref,
               kk_ref, ka_ref, rk_ref, gng_ref, gnb_ref, s0_ref, hsum_ref, tri_ref, blk_ref,
               wc_ref, glub_ref, cprev_ref, cw_ref, cb_ref, lng_ref, lnb_ref,
               og_ref, sout_ref, wkv_ref, yc_ref, cnew_ref,
               carry, st, at_s, rt_s, bb_s, kb_s, bh_s, kh_s, v_s, wt_s, bonus_s, g_s, o_s,
               gm_s, hm_s, rp_s, oi_s, ubuf, ushift, yc_s, *, chunk, unroll1, unroll2):
    bb_n, tt, d = x_ref.shape
    m = bb_n * tt
    d_shift = ws_ref.shape[1]
    d_r = w0_ref.shape[1]
    dc = cw_ref.shape[1]
    n_pairs = d_r // PAIR
    n_ck = tt // chunk
    n_blk = m // chunk
    log_chunk = int(math.log2(chunk))
    c2 = 2 * chunk
    conv_lo = CONV_PAD - (CONV_WIDTH - 1)
    t = pl.program_id(1)

    @pl.when(t == 0)
    def _():
        carry[...] = sprev_ref[...]
        ubuf[:, conv_lo:CONV_PAD, :] = cprev_ref[...]
        zero = jnp.zeros((HEAD_DIM, HEAD_DIM), F32)
        for b_i in range(bb_n):
            for p in range(n_pairs):
                top = jnp.concatenate([s0_ref[b_i, 2 * p], zero], axis=1)
                bot = jnp.concatenate([zero, s0_ref[b_i, 2 * p + 1]], axis=1)
                st[b_i, p] = jnp.concatenate([top, bot], axis=0)

    n_part = 2 if (bb_n == 1 and m >= 4 * chunk) else 1
    pm_ = m // n_part
    parts = range(n_part)
    psl = [slice(i * pm_, (i + 1) * pm_) for i in parts]
    x2d = x_ref[...].reshape(m, d)
    xn = [_rms(x2d[psl[i]], mixg_ref[...]).astype(BF16) for i in parts]
    zc = [_mm(xn[i], wc_ref[...]) + glub_ref[...] for i in parts]
    zs = [_mm(xn[i], ws_ref[...]) for i in parts]
    u = [zc[i][:, :dc] * jax.nn.sigmoid(zc[i][:, dc:]) for i in parts]
    if n_part == 1:
        ubuf[:, CONV_PAD:CONV_PAD + tt, :] = u[0].reshape(bb_n, tt, dc)
    else:
        for i in parts:
            ubuf[0, CONV_PAD + i * pm_:CONV_PAD + (i + 1) * pm_, :] = u[i]
    for sh in range(1, SUBLANES):
        ushift[sh - 1] = ubuf[:, sh:sh + tt + CONV_PAD - SUBLANES, :]

    if n_part == 1:
        prev0 = jnp.broadcast_to(carry[...], (bb_n, tt, d_shift)).reshape(m, d_shift)
        row = lax.broadcasted_iota(jnp.int32, (m, d_shift), 0)
        prev = [jnp.where((row & (tt - 1)) == 0, prev0, pltpu.roll(zs[0], 1, axis=0))]
        last = zs[0].reshape(bb_n, tt, d_shift)[:, tt - 1:tt, :]
    else:
        row = lax.broadcasted_iota(jnp.int32, (pm_, d_shift), 0)
        first = [carry[0]] + [zs[i][pm_ - 1:pm_, :] for i in parts[:-1]]
        prev = [jnp.where(row == 0, first[i], pltpu.roll(zs[i], 1, axis=0)) for i in parts]
        last = zs[-1][pm_ - 1:pm_, :].reshape(1, 1, d_shift)
    carry[...] = last
    sout_ref[...] = last
    xm = [zs[i] + (prev[i] - zs[i]) * mu_ref[...] for i in parts]

    c1, c2_, c3 = d_r, 2 * d_r, 3 * d_r
    c4 = c3 + LORA_DECAY
    c5 = c4 + LORA_A
    r = [xm[i][:, :c1] for i in parts]
    k = [xm[i][:, c1:c2_] for i in parts]
    v = [xm[i][:, c2_:c3] for i in parts]
    dec_in = [w0_ref[...] + _dot3(jnp.tanh(xm[i][:, c3:c4]), wdu_ref[...]) for i in parts]
    a_in = [a0_ref[...] + _dot1(xm[i][:, c4:c5], wau_ref[...]) for i in parts]
    for i in parts:
        g_s[psl[i], :] = _dot1(jax.nn.sigmoid(xm[i][:, c5:]), wgu_ref[...])
    ld = [-jnp.exp(-(jnp.maximum(-dec_in[i], 0.0) + jnp.log(1.0 + jnp.exp(-jnp.abs(dec_in[i])))) - 0.5)
          for i in parts]
    a = [jax.nn.sigmoid(a_in[i]) for i in parts]
    kh = [k[i] * (1.0 + (a[i] - 1.0) * ka_ref[...]) for i in parts]

    head_sum = hsum_ref[...]
    kkraw = [k[i] * kk_ref[...] for i in parts]
    kk_ss = [_mm((kkraw[i] * kkraw[i]).astype(BF16), head_sum) for i in parts]
    rk_sum = [_mm((r[i] * kh[i] * rk_ref[...]).astype(BF16), head_sum) for i in parts]
    kk = [kkraw[i] / jnp.maximum(jnp.sqrt(kk_ss[i]), 1e-12) for i in parts]
    b = [kk[i] * a[i] for i in parts]
    tdt = at_s.dtype
    for i in parts:
        bonus_s[psl[i], :] = rk_sum[i] * v[i]
        v_s[psl[i], :] = v[i].astype(tdt)

    slab = tri_ref.shape[0]
    for s0 in range(0, pm_, slab):
        sl = slice(s0, s0 + slab)
        ld_c = [ld[i][sl] for i in parts]
        cum = [_exact_lhs_dot(tri_ref[...], ld_c[i]) for i in parts]
        tot = [_exact_lhs_dot(blk_ref[...], ld_c[i]) for i in parts]
        for i in parts:
            gl = slice(i * pm_ + s0, i * pm_ + s0 + slab)
            w_inv = jnp.exp(-cum[i])
            w_rem = jnp.exp(tot[i] - cum[i])
            at_s[gl, :] = (-kk[i][sl] * jnp.exp(cum[i] - ld_c[i])).astype(tdt)
            rt_s[gl, :] = (r[i][sl] * jnp.exp(cum[i])).astype(tdt)
            bb_s[gl, :] = (b[i][sl] * w_inv).astype(tdt)
            kb_s[gl, :] = (kh[i][sl] * w_inv).astype(tdt)
            bh_s[gl, :] = (b[i][sl] * w_rem).astype(tdt)
            kh_s[gl, :] = (kh[i][sl] * w_rem).astype(tdt)
            wt_s[gl, :] = jnp.exp(tot[i])

    lane_lo = lax.broadcasted_iota(jnp.int32, (chunk, PAIR), 1) < HEAD_DIM
    r2 = lax.broadcasted_iota(jnp.int32, (c2, c2), 0)
    q2 = lax.broadcasted_iota(jnp.int32, (c2, c2), 1)
    strict = (q2 & (chunk - 1)) < (r2 & (chunk - 1))
    incl = (q2 & (chunk - 1)) <= (r2 & (chunk - 1))
    eye2 = (r2 == q2).astype(F32)
    wide = c2 % PAIR == 0

    def stack(ref, rows, ls):
        x = ref[rows, ls]
        return jnp.concatenate([jnp.where(lane_lo, x, 0.0), jnp.where(lane_lo, 0.0, x)], axis=0).astype(BF16)

    def phase1_step(i, c):
        chains = [(unroll1 * i + j, p) for j in range(unroll1) for p in range(n_pairs)]
        n = range(len(chains))

        def stacks(ref):
            return [stack(ref, pl.ds(pl.multiple_of(blk * chunk, chunk), chunk), slice(p * PAIR, (p + 1) * PAIR))
                    for blk, p in chains]

        at, rt, bbm, kbm = stacks(at_s), stacks(rt_s), stacks(bb_s), stacks(kb_s)
        bhm, khm, vm = stacks(bh_s), stacks(kh_s), stacks(v_s)
        if wide:
            a4 = [_nt(jnp.concatenate([at[c_], rt[c_]], axis=0), jnp.concatenate([bbm[c_], kbm[c_]], axis=0))
                  for c_ in n]
            a_ab = [jnp.where(strict, a4[c_][:c2, :c2], 0.0) for c_ in n]
            a_ak = [jnp.where(strict, a4[c_][:c2, c2:], 0.0).astype(BF16) for c_ in n]
            a_rb = [jnp.where(incl, a4[c_][c2:, :c2], 0.0).astype(BF16) for c_ in n]
            a_rk = [jnp.where(incl, a4[c_][c2:, c2:], 0.0).astype(BF16) for c_ in n]
        else:
            a_ab = [jnp.where(strict, _nt(at[c_], bbm[c_]), 0.0) for c_ in n]
            a_ak = [jnp.where(strict, _nt(at[c_], kbm[c_]), 0.0).astype(BF16) for c_ in n]
            a_rb = [jnp.where(incl, _nt(rt[c_], bbm[c_]), 0.0).astype(BF16) for c_ in n]
            a_rk = [jnp.where(incl, _nt(rt[c_], kbm[c_]), 0.0).astype(BF16) for c_ in n]
        tinv = [eye2 + a_ab[c_] for c_ in n]
        ap = [a_ab[c_].astype(BF16) for c_ in n]
        ap = [_mm(ap[c_], ap[c_]).astype(BF16) for c_ in n]
        for i_sq in range(1, log_chunk):
            if i_sq == log_chunk - 1:
                tinv = [tinv[c_] + _mm(ap[c_], tinv[c_].astype(BF16)) for c_ in n]
            elif wide:
                x = [_mm(ap[c_], jnp.concatenate([ap[c_], tinv[c_].astype(BF16)], axis=1)) for c_ in n]
                ap = [x[c_][:, :c2].astype(BF16) for c_ in n]
                tinv = [tinv[c_] + x[c_][:, c2:] for c_ in n]
            else:
                tinv = [tinv[c_] + _mm(ap[c_], tinv[c_].astype(BF16)) for c_ in n]
                ap = [_mm(ap[c_], ap[c_]).astype(BF16) for c_ in n]
        av = [_mm(a_ak[c_], vm[c_]).astype(BF16) for c_ in n]
        pq = [_mm(tinv[c_].astype(BF16), jnp.concatenate([av[c_], at[c_]], axis=1)) for c_ in n]
        pm = [pq[c_][:, :PAIR].astype(BF16) for c_ in n]
        qm = [pq[c_][:, PAIR:].astype(BF16) for c_ in n]
        for c_, (blk, p) in enumerate(chains):
            gm_s[blk, p] = _tn(qm[c_], bhm[c_]).astype(BF16)
        for c_, (blk, p) in enumerate(chains):
            hm_s[blk, p] = _tn(jnp.concatenate([pm[c_], vm[c_]], axis=0),
                               jnp.concatenate([bhm[c_], khm[c_]], axis=0))
        rq = [_mm(a_rb[c_], jnp.concatenate([qm[c_], pm[c_]], axis=1)) for c_ in n]
        for c_, (blk, p) in enumerate(chains):
            rp_s[blk, p] = (rt[c_].astype(F32) + rq[c_][:, :PAIR]).astype(BF16)
        for c_, (blk, p) in enumerate(chains):
            oi_s[blk, p] = rq[c_][:, PAIR:] + _mm(a_rk[c_], vm[c_])
        return c

    lax.fori_loop(0, n_blk // unroll1, phase1_step, 0)

    def conv_block(blk):
        b_i, t0 = blk // n_ck, (blk % n_ck) * chunk
        acc = jnp.zeros((chunk, dc), F32) + cb_ref[...]
        for kx in range(CONV_WIDTH):
            off = conv_lo + kx
            sh, base = off % SUBLANES, off - off % SUBLANES
            rows = pl.ds(pl.multiple_of(t0 + base, SUBLANES), chunk)
            win = ubuf[b_i, rows, :] if sh == 0 else ushift[sh - 1, b_i, rows, :]
            acc = acc + win * cw_ref[kx:kx + 1, :]
        cm = jnp.mean(acc, axis=-1, keepdims=True)
        cc = acc - cm
        cv = jnp.mean(cc * cc, axis=-1, keepdims=True)
        cn = cc * lax.rsqrt(cv + LN_EPS) * lng_ref[...] + lnb_ref[...]
        yc_s[pl.ds(pl.multiple_of(blk * chunk, chunk), chunk), :] = cn * jax.nn.sigmoid(cn)

    def phase2_step(i, c):
        for j in range(unroll2):
            conv_block(unroll2 * i + j)
        chains = [(unroll2 * i + j, p) for j in range(unroll2) for p in range(n_pairs)]
        n = range(len(chains))
        row0 = [pl.multiple_of(blk * chunk, chunk) for blk, _ in chains]
        ls = [slice(p * PAIR, (p + 1) * PAIR) for _, p in chains]
        s = [st[blk // n_ck, p] for blk, p in chains]
        sb = [s[c_].astype(BF16) for c_ in n]
        s_new = [s[c_] * wt_s[pl.ds(row0[c_], 1), ls[c_]] + _mm(sb[c_], gm_s[blk, p]) + hm_s[blk, p]
                 for c_, (blk, p) in enumerate(chains)]
        o_bd = [oi_s[blk, p] + _nt(rp_s[blk, p], sb[c_]) for c_, (blk, p) in enumerate(chains)]
        for c_, (blk, p) in enumerate(chains):
            st[blk // n_ck, p] = s_new[c_]
        for c_ in n:
            o_s[pl.ds(row0[c_], chunk), ls[c_]] = o_bd[c_][:chunk] + o_bd[c_][chunk:]
        return c

    lax.fori_loop(0, n_blk // unroll2, phase2_step, 0)

    o = o_s[...]
    mean = _mm(o.astype(BF16), head_sum) * (1.0 / HEAD_DIM)
    oc = o - mean
    var = _mm((oc * oc).astype(BF16), head_sum) * (1.0 / HEAD_DIM)
    o = oc * lax.rsqrt(var + GN_EPS) * gng_ref[...] + gnb_ref[...] + bonus_s[...]
    og_ref[...] = (o * g_s[...]).astype(BF16).reshape(bb_n, tt, d_r)

    yc_ref[...] = yc_s[...].astype(BF16).reshape(bb_n, tt, dc)
    tail = ubuf[:, conv_lo + tt:CONV_PAD + tt, :]
    cnew_ref[...] = tail
    ubuf[:, conv_lo:CONV_PAD, :] = tail

    @pl.when(t == pl.num_programs(1) - 1)
    def _():
        for b_i in range(bb_n):
            for p in range(n_pairs):
                s = st[b_i, p]
                wkv_ref[b_i, 2 * p] = s[:HEAD_DIM, :HEAD_DIM]
                wkv_ref[b_i, 2 * p + 1] = s[HEAD_DIM:, HEAD_DIM:]


def _rwkv(x, mixg, ws, sprev, mu, w0, wdu, a0, wau, wgu, kk, ka, rk, gng, gnb, s0,
          wc, glub, cprev, cw, cb, lng, lnb, *, bb, tt, chunk, unroll1, unroll2):
    b, t_len, d = x.shape
    d_shift = ws.shape[1]
    d_r = w0.shape[1]
    dc = cw.shape[1]
    n_heads = d_r // HEAD_DIM
    n_pairs = d_r // PAIR
    m = bb * tt
    n_blk = m // chunk
    assert b % bb == 0 and t_len % tt == 0 and tt % chunk == 0
    assert n_blk % unroll1 == 0 and n_blk % unroll2 == 0 and (unroll2 == 1 or tt == chunk)
    assert chunk & (chunk - 1) == 0 and tt & (tt - 1) == 0 and chunk % 8 == 0
    tok = pltpu.VMEM((m, d_r), F32)
    tok_mm = pltpu.VMEM((m, d_r), BF16 if chunk % (2 * SUBLANES) == 0 else F32)
    lane = jnp.arange(d_r)
    head_sum = (lane[:, None] // HEAD_DIM == lane[None, :] // HEAD_DIM).astype(BF16)
    rows = jnp.arange(chunk if chunk >= HEAD_DIM else m)
    same_chunk = rows[:, None] // chunk == rows[None, :] // chunk
    tri = (same_chunk & (rows[None, :] <= rows[:, None])).astype(BF16)
    blk = same_chunk.astype(BF16)
    return pl.pallas_call(
        functools.partial(_rwkv_body, chunk=chunk, unroll1=unroll1, unroll2=unroll2),
        grid=(b // bb, t_len // tt),
        in_specs=[
            pl.BlockSpec((bb, tt, d), lambda i, j: (i, j, 0)),
            _const_spec(mixg.shape), _const_spec(ws.shape),
            pl.BlockSpec((bb, 1, d_shift), lambda i, j: (i, 0, 0)),
            _const_spec(mu.shape), _const_spec(w0.shape), _const_spec(wdu.shape), _const_spec(a0.shape),
            _const_spec(wau.shape), _const_spec(wgu.shape), _const_spec(kk.shape), _const_spec(ka.shape),
            _const_spec(rk.shape), _const_spec(gng.shape), _const_spec(gnb.shape),
            pl.BlockSpec((bb, n_heads, HEAD_DIM, HEAD_DIM), lambda i, j: (i, 0, 0, 0)),
            _const_spec(head_sum.shape), _const_spec(tri.shape), _const_spec(blk.shape),
            _const_spec(wc.shape), _const_spec(glub.shape),
            pl.BlockSpec((bb, CONV_WIDTH - 1, dc), lambda i, j: (i, 0, 0)),
            _const_spec(cw.shape), _const_spec(cb.shape), _const_spec(lng.shape), _const_spec(lnb.shape),
        ],
        out_specs=[
            pl.BlockSpec((bb, tt, d_r), lambda i, j: (i, j, 0)),
            pl.BlockSpec((bb, 1, d_shift), lambda i, j: (i, 0, 0)),
            pl.BlockSpec((bb, n_heads, HEAD_DIM, HEAD_DIM), lambda i, j: (i, 0, 0, 0)),
            pl.BlockSpec((bb, tt, dc), lambda i, j: (i, j, 0)),
            pl.BlockSpec((bb, CONV_WIDTH - 1, dc), lambda i, j: (i, 0, 0)),
        ],
        out_shape=[
            jax.ShapeDtypeStruct((b, t_len, d_r), BF16),
            jax.ShapeDtypeStruct((b, 1, d_shift), F32),
            jax.ShapeDtypeStruct((b, n_heads, HEAD_DIM, HEAD_DIM), F32),
            jax.ShapeDtypeStruct((b, t_len, dc), BF16),
            jax.ShapeDtypeStruct((b, CONV_WIDTH - 1, dc), F32),
        ],
        scratch_shapes=[
            pltpu.VMEM((bb, 1, d_shift), F32),
            pltpu.VMEM((bb, n_pairs, PAIR, PAIR), F32),
        ] + [tok_mm] * 7 + [tok] * 4 + [
            pltpu.VMEM((n_blk, n_pairs, PAIR, PAIR), BF16),
            pltpu.VMEM((n_blk, n_pairs, PAIR, PAIR), F32),
            pltpu.VMEM((n_blk, n_pairs, 2 * chunk, PAIR), BF16),
            pltpu.VMEM((n_blk, n_pairs, 2 * chunk, PAIR), F32),
            pltpu.VMEM((bb, CONV_PAD + tt, dc), F32),
            pltpu.VMEM((SUBLANES - 1, bb, CONV_PAD + tt - SUBLANES, dc), F32),
            pltpu.VMEM((m, dc), F32),
        ],
        compiler_params=_params("parallel", "arbitrary"),
        name="rwkv_conv",
    )(x, mixg, ws, sprev, mu, w0, wdu, a0, wau, wgu, kk, ka, rk, gng, gnb, s0, head_sum, tri, blk,
      wc, glub, cprev, cw, cb, lng, lnb)


def _merge_body(x_ref, wq_ref, wg_ref, kt_ref, vt_ref, ya_ref, yb_ref, wro_ref, wco_ref, wxo_ref, wo_ref, o_ref,
                *, t_len, rows, group):
    tm, d = x_ref.shape
    dh = kt_ref.shape[1] // N_XATTN_HEADS
    x = x_ref[...]
    xb = x.astype(BF16)
    inv = lax.rsqrt(jnp.mean(x * x, axis=-1, keepdims=True) + RMS_EPS)

    q = (_mm(xb, wq_ref[...]) * (inv * dh ** -0.5)).astype(BF16)
    n_blocks = tm // rows
    hs = lambda h: slice(h * dh, (h + 1) * dh)
    seq = lambda r: (r * rows) // t_len
    pieces = []
    for g0 in range(0, n_blocks, group):
        blocks = range(g0, min(g0 + group, n_blocks))
        chains = [(r, h) for r in blocks for h in range(N_XATTN_HEADS)]
        s = [_mm(q[r * rows:(r + 1) * rows, hs(h)], kt_ref[seq(r), hs(h), :].astype(BF16)) for r, h in chains]
        p = [jnp.exp(v - jnp.max(v, axis=-1, keepdims=True)) for v in s]
        l = [jnp.sum(v, axis=-1, keepdims=True) for v in p]
        o = [_nt(p[i].astype(BF16), vt_ref[seq(r), hs(h), :].astype(BF16)) / l[i] for i, (r, h) in enumerate(chains)]
        for j in range(len(blocks)):
            pieces.append(jnp.concatenate(o[j * N_XATTN_HEADS:(j + 1) * N_XATTN_HEADS], axis=-1))
    oc = jnp.concatenate(pieces, axis=0).astype(BF16)

    merged = jnp.zeros_like(x)
    for i, (y, w_ref) in enumerate(((ya_ref[...], wro_ref), (yb_ref[...], wco_ref), (oc, wxo_ref))):
        gate = jax.nn.sigmoid(_mm(xb, wg_ref[:, i * d:(i + 1) * d]) * inv)
        merged = merged + gate * _mm(y, w_ref[...])
    o_ref[...] = x + _mm(merged.astype(BF16), wo_ref[...])


def _merge(x, wq, wg, kt, vt, ya, yb, wro, wco, wxo, wo, *, t_len, tm, rows, group):
    n, d = x.shape
    dx, n_mem = kt.shape[1:]
    seqs = max(tm // t_len, 1)
    assert n % tm == 0 and tm % rows == 0 and (tm % t_len == 0 or t_len % tm == 0) and rows <= t_len
    row = lambda w: pl.BlockSpec((tm, w), lambda i: (i, 0))
    mem = pl.BlockSpec((seqs, dx, n_mem), lambda i: ((i * tm) // (t_len * seqs), 0, 0))
    return pl.pallas_call(
        functools.partial(_merge_body, t_len=t_len, rows=rows, group=group),
        grid=(n // tm,),
        in_specs=[row(d), _resident_spec(wq.shape), _resident_spec(wg.shape), mem, mem, row(ya.shape[1]),
                  row(yb.shape[1]), _resident_spec(wro.shape), _resident_spec(wco.shape),
                  _resident_spec(wxo.shape), _resident_spec(wo.shape)],
        out_specs=row(d),
        out_shape=jax.ShapeDtypeStruct((n, d), F32),
        compiler_params=_params("parallel"),
        name="merge",
    )(x, wq, wg, kt, vt, ya, yb, wro, wco, wxo, wo)


def _mix(x1, shift_prev, conv_prev, wkv0, mem_kt, mem_vt, p, *, bb, tt, rw, mg):
    b, t_len, d = x1.shape
    n = b * t_len
    og, shift, wkv, cb, conv = _rwkv(
        x1, p['mix_norm'], p['w_s'], shift_prev[:, None, :], p['mu_shift'], p['w0'], p['w_decay_up'], p['a0'],
        p['w_a_up'], p['w_g_up'], p['k_k'], p['k_a'], p['r_k'], p['gn_g'], p['gn_b'], wkv0,
        p['w_c'], p['glu_b'], conv_prev, p['conv_w'], p['conv_b'], p['conv_ln_g'], p['conv_ln_b'],
        bb=bb, tt=tt, **rw)
    x2 = _merge(x1.reshape(n, d), p['w_q'], p['w_g'], mem_kt, mem_vt, og.reshape(n, -1), cb.reshape(n, -1),
                p['w_rwkv_out'], p['w_conv_out'], p['w_xattn_out'], p['w_o'], t_len=t_len, **mg)
    return x2, wkv, shift[:, 0, :], conv


def kernel(x_prompt, mem_prompt, x_sample, state_wkv, state_shift, state_conv, cache_mem_k, cache_mem_v,
           ffn1_norm, ffn1_w_up, ffn1_w_down, mix_norm, w_in, mu_shift, w0, w_decay_up, a0, w_a_up, w_g_up,
           k_k, k_a, r_k, gn_g, gn_b, w_rwkv_out, glu_b, conv_w, conv_b, conv_ln_g, conv_ln_b, w_conv_out,
           w_mem_kv, w_xattn_out, w_o, ffn2_norm, ffn2_w_up, ffn2_w_down, final_norm):
    depth = w_in.shape[0]
    d_model = w_in.shape[1]
    d_r = w0.shape[1]
    d_shift = mu_shift.shape[1]
    d_conv = conv_w.shape[2]
    d_x = w_xattn_out.shape[1]
    n_heads = d_r // HEAD_DIM
    o1 = d_shift
    o2 = o1 + 2 * d_conv
    o3 = o2 + d_x
    d_ff = ffn1_w_down.shape[1]
    row = lambda a: a.astype(F32).reshape(1, -1)
    scaled = lambda g, w: (g.astype(F32)[:, None] * w.astype(F32)).astype(BF16)
    final_g = row(final_norm)

    layers = []
    for l in range(depth):
        layers.append({
            'ffn1_w_gate': scaled(ffn1_norm[l], ffn1_w_up[l][:, :d_ff]),
            'ffn1_w_up': scaled(ffn1_norm[l], ffn1_w_up[l][:, d_ff:]),
            'ffn1_w_down': ffn1_w_down[l].astype(BF16), 'mix_norm': row(mix_norm[l]),
            'w_s': w_in[l, :, :o1].astype(BF16), 'w_c': w_in[l, :, o1:o2].astype(BF16),
            'w_q': scaled(mix_norm[l], w_in[l, :, o2:o3]), 'w_g': scaled(mix_norm[l], w_in[l, :, o3:]),
            'mu_shift': row(mu_shift[l]), 'w0': row(w0[l]), 'w_decay_up': w_decay_up[l].astype(F32),
            'a0': row(a0[l]), 'w_a_up': w_a_up[l].astype(F32), 'w_g_up': w_g_up[l].astype(F32),
            'k_k': row(k_k[l]), 'k_a': row(k_a[l]), 'r_k': row(r_k[l]), 'gn_g': row(gn_g[l]),
            'gn_b': row(gn_b[l]), 'w_rwkv_out': w_rwkv_out[l].astype(BF16), 'glu_b': row(glu_b[l]),
            'conv_w': conv_w[l].astype(F32), 'conv_b': row(conv_b[l]), 'conv_ln_g': row(conv_ln_g[l]),
            'conv_ln_b': row(conv_ln_b[l]), 'w_conv_out': w_conv_out[l].astype(BF16),
            'w_mem_kv': w_mem_kv[l].astype(BF16), 'w_mem_kv_t': w_mem_kv[l].T.astype(BF16), 'w_xattn_out': w_xattn_out[l].astype(BF16),
            'w_o': w_o[l].astype(BF16),
            'ffn2_w_gate': scaled(ffn2_norm[l], ffn2_w_up[l][:, :d_ff]),
            'ffn2_w_up': scaled(ffn2_norm[l], ffn2_w_up[l][:, d_ff:]), 'ffn2_w_down': ffn2_w_down[l].astype(BF16),
        })

    bp, tp, _ = x_prompt.shape
    bs, ts, _ = x_sample.shape
    n_mem = mem_prompt.shape[1]
    xp = x_prompt.astype(F32).reshape(bp * tp, d_model)
    xs = x_sample.astype(F32).reshape(bs * ts, d_model)
    wkv_p, shift_p, conv_p, mk_p, mv_p = [], [], [], [], []
    wkv_s, shift_s, conv_s = [], [], []
    for l in range(depth):
        p = layers[l]
        x1p, x1s = _ffn(xp, xs, p['ffn1_w_gate'], p['ffn1_w_up'], p['ffn1_w_down'], final_g, final_norm=False)

        mk, mv, mkt, mvt = _memkv(mem_prompt.astype(F32), p['w_mem_kv'], p['w_mem_kv_t'])
        x2p, wkv, sh, cv = _mix(
            x1p.reshape(bp, tp, d_model), jnp.zeros((bp, d_shift), F32),
            jnp.zeros((bp, CONV_WIDTH - 1, d_conv), F32), jnp.zeros((bp, n_heads, HEAD_DIM, HEAD_DIM), F32),
            mkt, mvt, p, bb=1, tt=512, rw=dict(chunk=64, unroll1=4, unroll2=1), mg=dict(tm=512, rows=256, group=2))
        wkv_p.append(wkv)
        shift_p.append(sh)
        conv_p.append(cv)
        mk_p.append(mk.reshape(bp, n_mem, N_XATTN_HEADS, d_x // N_XATTN_HEADS))
        mv_p.append(mv.reshape(bp, n_mem, N_XATTN_HEADS, d_x // N_XATTN_HEADS))

        kt = jnp.transpose(cache_mem_k[l].astype(F32), (0, 2, 3, 1)).reshape(bs, d_x, n_mem)
        vt = jnp.transpose(cache_mem_v[l].astype(F32), (0, 2, 3, 1)).reshape(bs, d_x, n_mem)
        x2s, wkv, sh, cv = _mix(
            x1s.reshape(bs, ts, d_model), state_shift[l].astype(F32), state_conv[l].astype(F32),
            state_wkv[l].astype(F32), kt, vt, p, bb=16, tt=ts, rw=dict(chunk=ts, unroll1=4, unroll2=4),
            mg=dict(tm=16 * ts, rows=ts, group=4))
        wkv_s.append(wkv)
        shift_s.append(sh)
        conv_s.append(cv)

        xp, xs = _ffn(x2p, x2s, p['ffn2_w_gate'], p['ffn2_w_up'], p['ffn2_w_down'], final_g,
                      final_norm=(l == depth - 1))

    return (xp.reshape(x_prompt.shape).astype(x_prompt.dtype), xs.reshape(x_sample.shape).astype(x_sample.dtype),
            jnp.stack(wkv_p), jnp.stack(shift_p), jnp.stack(conv_p), jnp.stack(mk_p), jnp.stack(mv_p),
            jnp.stack(wkv_s), jnp.stack(shift_s), jnp.stack(conv_s))
```

```python
import functools
import math

import jax
import jax.numpy as jnp
from jax import lax
from jax.experimental import pallas as pl
from jax.experimental.pallas import tpu as pltpu

F32 = jnp.float32
BF16 = jnp.bfloat16
HI = lax.Precision.HIGHEST

HEAD_DIM = 64
PAIR = 2 * HEAD_DIM
N_XATTN_HEADS = 4
CONV_WIDTH = 31
SUBLANES = 8
CONV_PAD = 32
LORA_DECAY = 64
LORA_A = 64
RMS_EPS = 1e-6
LN_EPS = 1e-5
GN_EPS = 64e-5
VMEM_LIMIT = 56 * 1024 * 1024


def _params(*sem):
    return pltpu.CompilerParams(dimension_semantics=sem, vmem_limit_bytes=VMEM_LIMIT)


def _rms(x, g):
    return x * lax.rsqrt(jnp.mean(x * x, axis=-1, keepdims=True) + RMS_EPS) * g


def _mm(a, b, precision=None):
    return jnp.dot(a, b, preferred_element_type=F32, precision=precision)


def _nt(a, b):
    return lax.dot_general(a, b, (((1,), (1,)), ((), ())), preferred_element_type=F32)


def _tn(a, b):
    return lax.dot_general(a, b, (((0,), (0,)), ((), ())), preferred_element_type=F32)


def _split(x):
    hi = x.astype(BF16)
    return hi, (x - hi.astype(F32)).astype(BF16)


def _exact_lhs_dot(sel, x):
    hi, lo = _split(x)
    return _mm(sel, hi) + _mm(sel, lo)


def _dot3(x, w):
    x_hi, x_lo = _split(x)
    w_hi, w_lo = _split(w)
    return _mm(x_hi, w_hi) + _mm(x_lo, w_hi) + _mm(x_hi, w_lo)


def _dot1(x, w):
    return _mm(x.astype(BF16), w.astype(BF16))


def _const_spec(shape):
    nd = len(shape)
    return pl.BlockSpec(shape, lambda *_: (0,) * nd)


def _ffn_body(xa_ref, xb_ref, wg_ref, wu_ref, wd_ref, fg_ref, oa_ref, ob_ref, *, final_norm, tiles_a):
    def tile(x_ref, o_ref):
        x = x_ref[...]
        xb = x.astype(BF16)
        inv = lax.rsqrt(jnp.mean(x * x, axis=-1, keepdims=True) + RMS_EPS)
        hg = _mm(xb, wg_ref[...]) * inv
        hu = _mm(xb, wu_ref[...]) * inv
        h = (hg * jax.nn.sigmoid(hg) * hu).astype(BF16)
        y = x + 0.5 * _mm(h, wd_ref[...])
        if final_norm:
            y = _rms(y, fg_ref[...])
        o_ref[...] = y

    in_a = pl.program_id(0) < tiles_a
    pl.when(in_a)(functools.partial(tile, xa_ref, oa_ref))
    pl.when(jnp.logical_not(in_a))(functools.partial(tile, xb_ref, ob_ref))


def _resident_spec(shape):
    nd = len(shape)
    return pl.BlockSpec(shape, lambda *_: (0,) * nd, pipeline_mode=pl.Buffered(1))


def _ffn(xa, xb, w_gate, w_up, w_down, final_g, *, final_norm, tm=512):
    (na, d), nb = xa.shape, xb.shape[0]
    tm = min(tm, na, nb)
    assert na % tm == 0 and nb % tm == 0
    ta, tb = na // tm, nb // tm
    spec_a = pl.BlockSpec((tm, d), lambda i: (jnp.minimum(i, ta - 1), 0))
    spec_b = pl.BlockSpec((tm, d), lambda i: (jnp.maximum(i - ta, 0), 0))
    return pl.pallas_call(
        functools.partial(_ffn_body, final_norm=final_norm, tiles_a=ta),
        grid=(ta + tb,),
        in_specs=[
            spec_a, spec_b,
            _resident_spec(w_gate.shape), _resident_spec(w_up.shape), _resident_spec(w_down.shape),
            _const_spec((1, d)),
        ],
        out_specs=[spec_a, spec_b],
        out_shape=[jax.ShapeDtypeStruct((na, d), F32), jax.ShapeDtypeStruct((nb, d), F32)],
        compiler_params=_params("arbitrary"),
        name="ffn_final" if final_norm else "ffn",
    )(xa, xb, w_gate, w_up, w_down, final_g)


def _memkv_body(m_ref, w_ref, wt_ref, k_ref, v_ref, kt_ref, vt_ref):
    mem = m_ref[0].astype(BF16)
    dk = k_ref.shape[-1]
    kv = _mm(mem, w_ref[...])
    k_ref[0] = kv[:, :dk]
    v_ref[0] = kv[:, dk:]
    kvt = _nt(wt_ref[...], mem)
    kt_ref[0] = kvt[:dk].astype(BF16)
    vt_ref[0] = kvt[dk:].astype(BF16)


def _memkv(mem, w, wt):
    b, n_mem, d = mem.shape
    dk = w.shape[1] // 2
    return pl.pallas_call(
        _memkv_body,
        grid=(b,),
        in_specs=[pl.BlockSpec((1, n_mem, d), lambda i: (i, 0, 0)), _const_spec(w.shape), _const_spec(wt.shape)],
        out_specs=[pl.BlockSpec((1, n_mem, dk), lambda i: (i, 0, 0))] * 2
        + [pl.BlockSpec((1, dk, n_mem), lambda i: (i, 0, 0))] * 2,
        out_shape=[jax.ShapeDtypeStruct((b, n_mem, dk), F32)] * 2
        + [jax.ShapeDtypeStruct((b, dk, n_mem), BF16)] * 2,
        compiler_params=_params("parallel"),
        name="memkv",
    )(mem, w, wt)


def _rwkv_body(x_ref, mixg_ref, ws_ref, sprev_ref, mu_ref, w0_ref, wdu_ref, a0_ref, wau_ref, wgu_ref,
               kk_ref, ka_ref, rk_ref, gng_ref, gnb_ref, s0_ref, hsum_ref, tri_ref, blk_ref,
               wc_ref, glub_ref, cprev_ref, cw_ref, cb_ref, lng_ref, lnb_ref,
               og_ref, sout_ref, wkv_ref, yc_ref, cnew_ref,
               carry, st, at_s, rt_s, bb_s, kb_s, bh_s, kh_s, v_s, wt_s, bonus_s, g_s, o_s,
               gm_s, hm_s, rp_s, oi_s, ubuf, ushift, yc_s, *, chunk, unroll1, unroll2):
    bb_n, tt, d = x_ref.shape
    m = bb_n * tt
    d_shift = ws_ref.shape[1]
    d_r = w0_ref.shape[1]
    dc = cw_ref.shape[1]
    n_pairs = d_r // PAIR
    n_ck = tt // chunk
    n_blk = m // chunk
    log_chunk = int(math.log2(chunk))
    c2 = 2 * chunk
    conv_lo = CONV_PAD - (CONV_WIDTH - 1)
    t = pl.program_id(1)

    @pl.when(t == 0)
    def _():
        carry[...] = sprev_ref[...]
        ubuf[:, conv_lo:CONV_PAD, :] = cprev_ref[...]
        zero = jnp.zeros((HEAD_DIM, HEAD_DIM), F32)
        for b_i in range(bb_n):
            for p in range(n_pairs):
                top = jnp.concatenate([s0_ref[b_i, 2 * p], zero], axis=1)
                bot = jnp.concatenate([zero, s0_ref[b_i, 2 * p + 1]], axis=1)
                st[b_i, p] = jnp.concatenate([top, bot], axis=0)

    n_part = 2 if (bb_n == 1 and m >= 4 * chunk) else 1
    pm_ = m // n_part
    parts = range(n_part)
    psl = [slice(i * pm_, (i + 1) * pm_) for i in parts]
    x2d = x_ref[...].reshape(m, d)
    xn = [_rms(x2d[psl[i]], mixg_ref[...]).astype(BF16) for i in parts]
    zc = [_mm(xn[i], wc_ref[...]) + glub_ref[...] for i in parts]
    zs = [_mm(xn[i], ws_ref[...]) for i in parts]
    u = [zc[i][:, :dc] * jax.nn.sigmoid(zc[i][:, dc:]) for i in parts]
    if n_part == 1:
        ubuf[:, CONV_PAD:CONV_PAD + tt, :] = u[0].reshape(bb_n, tt, dc)
    else:
        for i in parts:
            ubuf[0, CONV_PAD + i * pm_:CONV_PAD + (i + 1) * pm_, :] = u[i]
    for sh in range(1, SUBLANES):
        ushift[sh - 1] = ubuf[:, sh:sh + tt + CONV_PAD - SUBLANES, :]

    if n_part == 1:
        prev0 = jnp.broadcast_to(carry[...], (bb_n, tt, d_shift)).reshape(m, d_shift)
        row = lax.broadcasted_iota(jnp.int32, (m, d_shift), 0)
        prev = [jnp.where((row & (tt - 1)) == 0, prev0, pltpu.roll(zs[0], 1, axis=0))]
        last = zs[0].reshape(bb_n, tt, d_shift)[:, tt - 1:tt, :]
    else:
        row = lax.broadcasted_iota(jnp.int32, (pm_, d_shift), 0)
        first = [carry[0]] + [zs[i][pm_ - 1:pm_, :] for i in parts[:-1]]
        prev = [jnp.where(row == 0, first[i], pltpu.roll(zs[i], 1, axis=0)) for i in parts]
        last = zs[-1][pm_ - 1:pm_, :].reshape(1, 1, d_shift)
    carry[...] = last
    sout_ref[...] = last
    xm = [zs[i] + (prev[i] - zs[i]) * mu_ref[...] for i in parts]

    c1, c2_, c3 = d_r, 2 * d_r, 3 * d_r
    c4 = c3 + LORA_DECAY
    c5 = c4 + LORA_A
    r = [xm[i][:, :c1] for i in parts]
    k = [xm[i][:, c1:c2_] for i in parts]
    v = [xm[i][:, c2_:c3] for i in parts]
    dec_in = [w0_ref[...] + _dot3(jnp.tanh(xm[i][:, c3:c4]), wdu_ref[...]) for i in parts]
    a_in = [a0_ref[...] + _dot1(xm[i][:, c4:c5], wau_ref[...]) for i in parts]
    for i in parts:
        g_s[psl[i], :] = _dot1(jax.nn.sigmoid(xm[i][:, c5:]), wgu_ref[...])
    ld = [-jnp.exp(-(jnp.maximum(-dec_in[i], 0.0) + jnp.log(1.0 + jnp.exp(-jnp.abs(dec_in[i])))) - 0.5)
          for i in parts]
    a = [jax.nn.sigmoid(a_in[i]) for i in parts]
    kh = [k[i] * (1.0 + (a[i] - 1.0) * ka_ref[...]) for i in parts]

    head_sum = hsum_ref[...]
    kkraw = [k[i] * kk_ref[...] for i in parts]
    kk_ss = [_mm((kkraw[i] * kkraw[i]).astype(BF16), head_sum) for i in parts]
    rk_sum = [_mm((r[i] * kh[i] * rk_ref[...]).astype(BF16), head_sum) for i in parts]
    kk = [kkraw[i] / jnp.maximum(jnp.sqrt(kk_ss[i]), 1e-12) for i in parts]
    b = [kk[i] * a[i] for i in parts]
    tdt = at_s.dtype
    for i in parts:
        bonus_s[psl[i], :] = rk_sum[i] * v[i]
        v_s[psl[i], :] = v[i].astype(tdt)

    slab = tri_ref.shape[0]
    for s0 in range(0, pm_, slab):
        sl = slice(s0, s0 + slab)
        ld_c = [ld[i][sl] for i in parts]
        cum = [_exact_lhs_dot(tri_ref[...], ld_c[i]) for i in parts]
        tot = [_exact_lhs_dot(blk_ref[...], ld_c[i]) for i in parts]
        for i in parts:
            gl = slice(i * pm_ + s0, i * pm_ + s0 + slab)
            w_inv = jnp.exp(-cum[i])
            w_rem = jnp.exp(tot[i] - cum[i])
            at_s[gl, :] = (-kk[i][sl] * jnp.exp(cum[i] - ld_c[i])).astype(tdt)
            rt_s[gl, :] = (r[i][sl] * jnp.exp(cum[i])).astype(tdt)
            bb_s[gl, :] = (b[i][sl] * w_inv).astype(tdt)
            kb_s[gl, :] = (kh[i][sl] * w_inv).astype(tdt)
            bh_s[gl, :] = (b[i][sl] * w_rem).astype(tdt)
            kh_s[gl, :] = (kh[i][sl] * w_rem).astype(tdt)
            wt_s[gl, :] = jnp.exp(tot[i])

    lane_lo = lax.broadcasted_iota(jnp.int32, (chunk, PAIR), 1) < HEAD_DIM
    r2 = lax.broadcasted_iota(jnp.int32, (c2, c2), 0)
    q2 = lax.broadcasted_iota(jnp.int32, (c2, c2), 1)
    strict = (q2 & (chunk - 1)) < (r2 & (chunk - 1))
    incl = (q2 & (chunk - 1)) <= (r2 & (chunk - 1))
    eye2 = (r2 == q2).astype(F32)
    wide = c2 % PAIR == 0

    def stack(ref, rows, ls):
        x = ref[rows, ls]
        return jnp.concatenate([jnp.where(lane_lo, x, 0.0), jnp.where(lane_lo, 0.0, x)], axis=0).astype(BF16)

    def phase1_step(i, c):
        chains = [(unroll1 * i + j, p) for j in range(unroll1) for p in range(n_pairs)]
        n = range(len(chains))

        def stacks(ref):
            return [stack(ref, pl.ds(pl.multiple_of(blk * chunk, chunk), chunk), slice(p * PAIR, (p + 1) * PAIR))
                    for blk, p in chains]

        at, rt, bbm, kbm = stacks(at_s), stacks(rt_s), stacks(bb_s), stacks(kb_s)
        bhm, khm, vm = stacks(bh_s), stacks(kh_s), stacks(v_s)
        if wide:
            a4 = [_nt(jnp.concatenate([at[c_], rt[c_]], axis=0), jnp.concatenate([bbm[c_], kbm[c_]], axis=0))
                  for c_ in n]
            a_ab = [jnp.where(strict, a4[c_][:c2, :c2], 0.0) for c_ in n]
            a_ak = [jnp.where(strict, a4[c_][:c2, c2:], 0.0).astype(BF16) for c_ in n]
            a_rb = [jnp.where(incl, a4[c_][c2:, :c2], 0.0).astype(BF16) for c_ in n]
            a_rk = [jnp.where(incl, a4[c_][c2:, c2:], 0.0).astype(BF16) for c_ in n]
        else:
            a_ab = [jnp.where(strict, _nt(at[c_], bbm[c_]), 0.0) for c_ in n]
            a_ak = [jnp.where(strict, _nt(at[c_], kbm[c_]), 0.0).astype(BF16) for c_ in n]
            a_rb = [jnp.where(incl, _nt(rt[c_], bbm[c_]), 0.0).astype(BF16) for c_ in n]
            a_rk = [jnp.where(incl, _nt(rt[c_], kbm[c_]), 0.0).astype(BF16) for c_ in n]
        tinv = [eye2 + a_ab[c_] for c_ in n]
        ap = [a_ab[c_].astype(BF16) for c_ in n]
        ap = [_mm(ap[c_], ap[c_]).astype(BF16) for c_ in n]
        for i_sq in range(1, log_chunk):
            if i_sq == log_chunk - 1:
                tinv = [tinv[c_] + _mm(ap[c_], tinv[c_].astype(BF16)) for c_ in n]
            elif wide:
                x = [_mm(ap[c_], jnp.concatenate([ap[c_], tinv[c_].astype(BF16)], axis=1)) for c_ in n]
                ap = [x[c_][:, :c2].astype(BF16) for c_ in n]
                tinv = [tinv[c_] + x[c_][:, c2:] for c_ in n]
            else:
                tinv = [tinv[c_] + _mm(ap[c_], tinv[c_].astype(BF16)) for c_ in n]
                ap = [_mm(ap[c_], ap[c_]).astype(BF16) for c_ in n]
        av = [_mm(a_ak[c_], vm[c_]).astype(BF16) for c_ in n]
        pq = [_mm(tinv[c_].astype(BF16), jnp.concatenate([av[c_], at[c_]], axis=1)) for c_ in n]
        pm = [pq[c_][:, :PAIR].astype(BF16) for c_ in n]
        qm = [pq[c_][:, PAIR:].astype(BF16) for c_ in n]
        for c_, (blk, p) in enumerate(chains):
            gm_s[blk, p] = _tn(qm[c_], bhm[c_]).astype(BF16)
        for c_, (blk, p) in enumerate(chains):
            hm_s[blk, p] = _tn(jnp.concatenate([pm[c_], vm[c_]], axis=0),
                               jnp.concatenate([bhm[c_], khm[c_]], axis=0))
        rq = [_mm(a_rb[c_], jnp.concatenate([qm[c_], pm[c_]], axis=1)) for c_ in n]
        for c_, (blk, p) in enumerate(chains):
            rp_s[blk, p] = (rt[c_].astype(F32) + rq[c_][:, :PAIR]).astype(BF16)
        for c_, (blk, p) in enumerate(chains):
            oi_s[blk, p] = rq[c_][:, PAIR:] + _mm(a_rk[c_], vm[c_])
        return c

    lax.fori_loop(0, n_blk // unroll1, phase1_step, 0)

    def conv_block(blk):
        b_i, t0 = blk // n_ck, (blk % n_ck) * chunk
        acc = jnp.zeros((chunk, dc), F32) + cb_ref[...]
        for kx in range(CONV_WIDTH):
            off = conv_lo + kx
            sh, base = off % SUBLANES, off - off % SUBLANES
            rows = pl.ds(pl.multiple_of(t0 + base, SUBLANES), chunk)
            win = ubuf[b_i, rows, :] if sh == 0 else ushift[sh - 1, b_i, rows, :]
            acc = acc + win * cw_ref[kx:kx + 1, :]
        cm = jnp.mean(acc, axis=-1, keepdims=True)
        cc = acc - cm
        cv = jnp.mean(cc * cc, axis=-1, keepdims=True)
        cn = cc * lax.rsqrt(cv + LN_EPS) * lng_ref[...] + lnb_ref[...]
        yc_s[pl.ds(pl.multiple_of(blk * chunk, chunk), chunk), :] = cn * jax.nn.sigmoid(cn)

    def phase2_step(i, c):
        for j in range(unroll2):
            conv_block(unroll2 * i + j)
        chains = [(unroll2 * i + j, p) for j in range(unroll2) for p in range(n_pairs)]
        n = range(len(chains))
        row0 = [pl.multiple_of(blk * chunk, chunk) for blk, _ in chains]
        ls = [slice(p * PAIR, (p + 1) * PAIR) for _, p in chains]
        s = [st[blk // n_ck, p] for blk, p in chains]
        sb = [s[c_].astype(BF16) for c_ in n]
        s_new = [s[c_] * wt_s[pl.ds(row0[c_], 1), ls[c_]] + _mm(sb[c_], gm_s[blk, p]) + hm_s[blk, p]
                 for c_, (blk, p) in enumerate(chains)]
        o_bd = [oi_s[blk, p] + _nt(rp_s[blk, p], sb[c_]) for c_, (blk, p) in enumerate(chains)]
        for c_, (blk, p) in enumerate(chains):
            st[blk // n_ck, p] = s_new[c_]
        for c_ in n:
            o_s[pl.ds(row0[c_], chunk), ls[c_]] = o_bd[c_][:chunk] + o_bd[c_][chunk:]
        return c

    lax.fori_loop(0, n_blk // unroll2, phase2_step, 0)

    o = o_s[...]
    mean = _mm(o.astype(BF16), head_sum) * (1.0 / HEAD_DIM)
    oc = o - mean
    var = _mm((oc * oc).astype(BF16), head_sum) * (1.0 / HEAD_DIM)
    o = oc * lax.rsqrt(var + GN_EPS) * gng_ref[...] + gnb_ref[...] + bonus_s[...]
    og_ref[...] = (o * g_s[...]).astype(BF16).reshape(bb_n, tt, d_r)

    yc_ref[...] = yc_s[...].astype(BF16).reshape(bb_n, tt, dc)
    tail = ubuf[:, conv_lo + tt:CONV_PAD + tt, :]
    cnew_ref[...] = tail
    ubuf[:, conv_lo:CONV_PAD, :] = tail

    @pl.when(t == pl.num_programs(1) - 1)
    def _():
        for b_i in range(bb_n):
            for p in range(n_pairs):
                s = st[b_i, p]
                wkv_ref[b_i, 2 * p] = s[:HEAD_DIM, :HEAD_DIM]
                wkv_ref[b_i, 2 * p + 1] = s[HEAD_DIM:, HEAD_DIM:]


def _rwkv(x, mixg, ws, sprev, mu, w0, wdu, a0, wau, wgu, kk, ka, rk, gng, gnb, s0,
          wc, glub, cprev, cw, cb, lng, lnb, *, bb, tt, chunk, unroll1, unroll2):
    b, t_len, d = x.shape
    d_shift = ws.shape[1]
    d_r = w0.shape[1]
    dc = cw.shape[1]
    n_heads = d_r // HEAD_DIM
    n_pairs = d_r // PAIR
    m = bb * tt
    n_blk = m // chunk
    assert b % bb == 0 and t_len % tt == 0 and tt % chunk == 0
    assert n_blk % unroll1 == 0 and n_blk % unroll2 == 0 and (unroll2 == 1 or tt == chunk)
    assert chunk & (chunk - 1) == 0 and tt & (tt - 1) == 0 and chunk % 8 == 0
    tok = pltpu.VMEM((m, d_r), F32)
    tok_mm = pltpu.VMEM((m, d_r), BF16 if chunk % (2 * SUBLANES) == 0 else F32)
    lane = jnp.arange(d_r)
    head_sum = (lane[:, None] // HEAD_DIM == lane[None, :] // HEAD_DIM).astype(BF16)
    rows = jnp.arange(chunk if chunk >= HEAD_DIM else m)
    same_chunk = rows[:, None] // chunk == rows[None, :] // chunk
    tri = (same_chunk & (rows[None, :] <= rows[:, None])).astype(BF16)
    blk = same_chunk.astype(BF16)
    return pl.pallas_call(
        functools.partial(_rwkv_body, chunk=chunk, unroll1=unroll1, unroll2=unroll2),
        grid=(b // bb, t_len // tt),
        in_specs=[
            pl.BlockSpec((bb, tt, d), lambda i, j: (i, j, 0)),
            _const_spec(mixg.shape), _const_spec(ws.shape),
            pl.BlockSpec((bb, 1, d_shift), lambda i, j: (i, 0, 0)),
            _const_spec(mu.shape), _const_spec(w0.shape), _const_spec(wdu.shape), _const_spec(a0.shape),
            _const_spec(wau.shape), _const_spec(wgu.shape), _const_spec(kk.shape), _const_spec(ka.shape),
            _const_spec(rk.shape), _const_spec(gng.shape), _const_spec(gnb.shape),
            pl.BlockSpec((bb, n_heads, HEAD_DIM, HEAD_DIM), lambda i, j: (i, 0, 0, 0)),
            _const_spec(head_sum.shape), _const_spec(tri.shape), _const_spec(blk.shape),
            _const_spec(wc.shape), _const_spec(glub.shape),
            pl.BlockSpec((bb, CONV_WIDTH - 1, dc), lambda i, j: (i, 0, 0)),
            _const_spec(cw.shape), _const_spec(cb.shape), _const_spec(lng.shape), _const_spec(lnb.shape),
        ],
        out_specs=[
            pl.BlockSpec((bb, tt, d_r), lambda i, j: (i, j, 0)),
            pl.BlockSpec((bb, 1, d_shift), lambda i, j: (i, 0, 0)),
            pl.BlockSpec((bb, n_heads, HEAD_DIM, HEAD_DIM), lambda i, j: (i, 0, 0, 0)),
            pl.BlockSpec((bb, tt, dc), lambda i, j: (i, j, 0)),
            pl.BlockSpec((bb, CONV_WIDTH - 1, dc), lambda i, j: (i, 0, 0)),
        ],
        out_shape=[
            jax.ShapeDtypeStruct((b, t_len, d_r), BF16),
            jax.ShapeDtypeStruct((b, 1, d_shift), F32),
            jax.ShapeDtypeStruct((b, n_heads, HEAD_DIM, HEAD_DIM), F32),
            jax.ShapeDtypeStruct((b, t_len, dc), BF16),
            jax.ShapeDtypeStruct((b, CONV_WIDTH - 1, dc), F32),
        ],
        scratch_shapes=[
            pltpu.VMEM((bb, 1, d_shift), F32),
            pltpu.VMEM((bb, n_pairs, PAIR, PAIR), F32),
        ] + [tok_mm] * 7 + [tok] * 4 + [
            pltpu.VMEM((n_blk, n_pairs, PAIR, PAIR), BF16),
            pltpu.VMEM((n_blk, n_pairs, PAIR, PAIR), F32),
            pltpu.VMEM((n_blk, n_pairs, 2 * chunk, PAIR), BF16),
            pltpu.VMEM((n_blk, n_pairs, 2 * chunk, PAIR), F32),
            pltpu.VMEM((bb, CONV_PAD + tt, dc), F32),
            pltpu.VMEM((SUBLANES - 1, bb, CONV_PAD + tt - SUBLANES, dc), F32),
            pltpu.VMEM((m, dc), F32),
        ],
        compiler_params=_params("parallel", "arbitrary"),
        name="rwkv_conv",
    )(x, mixg, ws, sprev, mu, w0, wdu, a0, wau, wgu, kk, ka, rk, gng, gnb, s0, head_sum, tri, blk,
      wc, glub, cprev, cw, cb, lng, lnb)


def _merge_body(x_ref, wq_ref, wg_ref, kt_ref, vt_ref, ya_ref, yb_ref, wro_ref, wco_ref, wxo_ref, wo_ref, o_ref,
                *, t_len, rows, group):
    tm, d = x_ref.shape
    dh = kt_ref.shape[1] // N_XATTN_HEADS
    x = x_ref[...]
    xb = x.astype(BF16)
    inv = lax.rsqrt(jnp.mean(x * x, axis=-1, keepdims=True) + RMS_EPS)

    q = (_mm(xb, wq_ref[...]) * (inv * dh ** -0.5)).astype(BF16)
    n_blocks = tm // rows
    hs = lambda h: slice(h * dh, (h + 1) * dh)
    seq = lambda r: (r * rows) // t_len
    pieces = []
    for g0 in range(0, n_blocks, group):
        blocks = range(g0, min(g0 + group, n_blocks))
        chains = [(r, h) for r in blocks for h in range(N_XATTN_HEADS)]
        s = [_mm(q[r * rows:(r + 1) * rows, hs(h)], kt_ref[seq(r), hs(h), :].astype(BF16)) for r, h in chains]
        p = [jnp.exp(v - jnp.max(v, axis=-1, keepdims=True)) for v in s]
        l = [jnp.sum(v, axis=-1, keepdims=True) for v in p]
        o = [_nt(p[i].astype(BF16), vt_ref[seq(r), hs(h), :].astype(BF16)) / l[i] for i, (r, h) in enumerate(chains)]
        for j in range(len(blocks)):
            pieces.append(jnp.concatenate(o[j * N_XATTN_HEADS:(j + 1) * N_XATTN_HEADS], axis=-1))
    oc = jnp.concatenate(pieces, axis=0).astype(BF16)

    merged = jnp.zeros_like(x)
    for i, (y, w_ref) in enumerate(((ya_ref[...], wro_ref), (yb_ref[...], wco_ref), (oc, wxo_ref))):
        gate = jax.nn.sigmoid(_mm(xb, wg_ref[:, i * d:(i + 1) * d]) * inv)
        merged = merged + gate * _mm(y, w_ref[...])
    o_ref[...] = x + _mm(merged.astype(BF16), wo_ref[...])


def _merge(x, wq, wg, kt, vt, ya, yb, wro, wco, wxo, wo, *, t_len, tm, rows, group):
    n, d = x.shape
    dx, n_mem = kt.shape[1:]
    seqs = max(tm // t_len, 1)
    assert n % tm == 0 and tm % rows == 0 and (tm % t_len == 0 or t_len % tm == 0) and rows <= t_len
    row = lambda w: pl.BlockSpec((tm, w), lambda i: (i, 0))
    mem = pl.BlockSpec((seqs, dx, n_mem), lambda i: ((i * tm) // (t_len * seqs), 0, 0))
    return pl.pallas_call(
        functools.partial(_merge_body, t_len=t_len, rows=rows, group=group),
        grid=(n // tm,),
        in_specs=[row(d), _resident_spec(wq.shape), _resident_spec(wg.shape), mem, mem, row(ya.shape[1]),
                  row(yb.shape[1]), _resident_spec(wro.shape), _resident_spec(wco.shape),
                  _resident_spec(wxo.shape), _resident_spec(wo.shape)],
        out_specs=row(d),
        out_shape=jax.ShapeDtypeStruct((n, d), F32),
        compiler_params=_params("parallel"),
        name="merge",
    )(x, wq, wg, kt, vt, ya, yb, wro, wco, wxo, wo)


def _mix(x1, shift_prev, conv_prev, wkv0, mem_kt, mem_vt, p, *, bb, tt, rw, mg):
    b, t_len, d = x1.shape
    n = b * t_len
    og, shift, wkv, cb, conv = _rwkv(
        x1, p['mix_norm'], p['w_s'], shift_prev[:, None, :], p['mu_shift'], p['w0'], p['w_decay_up'], p['a0'],
        p['w_a_up'], p['w_g_up'], p['k_k'], p['k_a'], p['r_k'], p['gn_g'], p['gn_b'], wkv0,
        p['w_c'], p['glu_b'], conv_prev, p['conv_w'], p['conv_b'], p['conv_ln_g'], p['conv_ln_b'],
        bb=bb, tt=tt, **rw)
    x2 = _merge(x1.reshape(n, d), p['w_q'], p['w_g'], mem_kt, mem_vt, og.reshape(n, -1), cb.reshape(n, -1),
                p['w_rwkv_out'], p['w_conv_out'], p['w_xattn_out'], p['w_o'], t_len=t_len, **mg)
    return x2, wkv, shift[:, 0, :], conv


def kernel(x_prompt, mem_prompt, x_sample, state_wkv, state_shift, state_conv, cache_mem_k, cache_mem_v,
           ffn1_norm, ffn1_w_up, ffn1_w_down, mix_norm, w_in, mu_shift, w0, w_decay_up, a0, w_a_up, w_g_up,
           k_k, k_a, r_k, gn_g, gn_b, w_rwkv_out, glu_b, conv_w, conv_b, conv_ln_g, conv_ln_b, w_conv_out,
           w_mem_kv, w_xattn_out, w_o, ffn2_norm, ffn2_w_up, ffn2_w_down, final_norm):
    depth = w_in.shape[0]
    d_model = w_in.shape[1]
    d_r = w0.shape[1]
    d_shift = mu_shift.shape[1]
    d_conv = conv_w.shape[2]
    d_x = w_xattn_out.shape[1]
    n_heads = d_r // HEAD_DIM
    o1 = d_shift
    o2 = o1 + 2 * d_conv
    o3 = o2 + d_x
    d_ff = ffn1_w_down.shape[1]
    row = lambda a: a.astype(F32).reshape(1, -1)
    scaled = lambda g, w: (g.astype(F32)[:, None] * w.astype(F32)).astype(BF16)
    final_g = row(final_norm)

    layers = []
    for l in range(depth):
        layers.append({
            'ffn1_w_gate': scaled(ffn1_norm[l], ffn1_w_up[l][:, :d_ff]),
            'ffn1_w_up': scaled(ffn1_norm[l], ffn1_w_up[l][:, d_ff:]),
            'ffn1_w_down': ffn1_w_down[l].astype(BF16), 'mix_norm': row(mix_norm[l]),
            'w_s': w_in[l, :, :o1].astype(BF16), 'w_c': w_in[l, :, o1:o2].astype(BF16),
            'w_q': scaled(mix_norm[l], w_in[l, :, o2:o3]), 'w_g': scaled(mix_norm[l], w_in[l, :, o3:]),
            'mu_shift': row(mu_shift[l]), 'w0': row(w0[l]), 'w_decay_up': w_decay_up[l].astype(F32),
            'a0': row(a0[l]), 'w_a_up': w_a_up[l].astype(F32), 'w_g_up': w_g_up[l].astype(F32),
            'k_k': row(k_k[l]), 'k_a': row(k_a[l]), 'r_k': row(r_k[l]), 'gn_g': row(gn_g[l]),
            'gn_b': row(gn_b[l]), 'w_rwkv_out': w_rwkv_out[l].astype(BF16), 'glu_b': row(glu_b[l]),
            'conv_w': conv_w[l].astype(F32), 'conv_b': row(conv_b[l]), 'conv_ln_g': row(conv_ln_g[l]),
            'conv_ln_b': row(conv_ln_b[l]), 'w_conv_out': w_conv_out[l].astype(BF16),
            'w_mem_kv': w_mem_kv[l].astype(BF16), 'w_mem_kv_t': w_mem_kv[l].T.astype(BF16), 'w_xattn_out': w_xattn_out[l].astype(BF16),
            'w_o': w_o[l].astype(BF16),
            'ffn2_w_gate': scaled(ffn2_norm[l], ffn2_w_up[l][:, :d_ff]),
            'ffn2_w_up': scaled(ffn2_norm[l], ffn2_w_up[l][:, d_ff:]), 'ffn2_w_down': ffn2_w_down[l].astype(BF16),
        })

    bp, tp, _ = x_prompt.shape
    bs, ts, _ = x_sample.shape
    n_mem = mem_prompt.shape[1]
    xp = x_prompt.astype(F32).reshape(bp * tp, d_model)
    xs = x_sample.astype(F32).reshape(bs * ts, d_model)
    wkv_p, shift_p, conv_p, mk_p, mv_p = [], [], [], [], []
    wkv_s, shift_s, conv_s = [], [], []
    for l in range(depth):
        p = layers[l]
        x1p, x1s = _ffn(xp, xs, p['ffn1_w_gate'], p['ffn1_w_up'], p['ffn1_w_down'], final_g, final_norm=False)

        mk, mv, mkt, mvt = _memkv(mem_prompt.astype(F32), p['w_mem_kv'], p['w_mem_kv_t'])
        x2p, wkv, sh, cv = _mix(
            x1p.reshape(bp, tp, d_model), jnp.zeros((bp, d_shift), F32),
            jnp.zeros((bp, CONV_WIDTH - 1, d_conv), F32), jnp.zeros((bp, n_heads, HEAD_DIM, HEAD_DIM), F32),
            mkt, mvt, p, bb=1, tt=512, rw=dict(chunk=64, unroll1=4, unroll2=1), mg=dict(tm=512, rows=512, group=1))
        wkv_p.append(wkv)
        shift_p.append(sh)
        conv_p.append(cv)
        mk_p.append(mk.reshape(bp, n_mem, N_XATTN_HEADS, d_x // N_XATTN_HEADS))
        mv_p.append(mv.reshape(bp, n_mem, N_XATTN_HEADS, d_x // N_XATTN_HEADS))

        kt = jnp.transpose(cache_mem_k[l].astype(F32), (0, 2, 3, 1)).reshape(bs, d_x, n_mem)
        vt = jnp.transpose(cache_mem_v[l].astype(F32), (0, 2, 3, 1)).reshape(bs, d_x, n_mem)
        x2s, wkv, sh, cv = _mix(
            x1s.reshape(bs, ts, d_model), state_shift[l].astype(F32), state_conv[l].astype(F32),
            state_wkv[l].astype(F32), kt, vt, p, bb=16, tt=ts, rw=dict(chunk=ts, unroll1=4, unroll2=4),
            mg=dict(tm=16 * ts, rows=ts, group=4))
        wkv_s.append(wkv)
        shift_s.append(sh)
        conv_s.append(cv)

        xp, xs = _ffn(x2p, x2s, p['ffn2_w_gate'], p['ffn2_w_up'], p['ffn2_w_down'], final_g,
                      final_norm=(l == depth - 1))

    return (xp.reshape(x_prompt.shape).astype(x_prompt.dtype), xs.reshape(x_sample.shape).astype(x_sample.dtype),
            jnp.stack(wkv_p), jnp.stack(shift_p), jnp.stack(conv_p), jnp.stack(mk_p), jnp.stack(mv_p),
            jnp.stack(wkv_s), jnp.stack(shift_s), jnp.stack(conv_s))
```

```python
import functools
import math

import jax
import jax.numpy as jnp
from jax import lax
from jax.experimental import pallas as pl
from jax.experimental.pallas import tpu as pltpu

F32 = jnp.float32
BF16 = jnp.bfloat16
HI = lax.Precision.HIGHEST

HEAD_DIM = 64
PAIR = 2 * HEAD_DIM
N_XATTN_HEADS = 4
CONV_WIDTH = 31
SUBLANES = 8
CONV_PAD = 32
LORA_DECAY = 64
LORA_A = 64
RMS_EPS = 1e-6
LN_EPS = 1e-5
GN_EPS = 64e-5
VMEM_LIMIT = 56 * 1024 * 1024


def _params(*sem):
    return pltpu.CompilerParams(dimension_semantics=sem, vmem_limit_bytes=VMEM_LIMIT)


def _rms(x, g):
    return x * lax.rsqrt(jnp.mean(x * x, axis=-1, keepdims=True) + RMS_EPS) * g


def _mm(a, b, precision=None):
    return jnp.dot(a, b, preferred_element_type=F32, precision=precision)


def _nt(a, b):
    return lax.dot_general(a, b, (((1,), (1,)), ((), ())), preferred_element_type=F32)


def _tn(a, b):
    return lax.dot_general(a, b, (((0,), (0,)), ((), ())), preferred_element_type=F32)


def _split(x):
    hi = x.astype(BF16)
    return hi, (x - hi.astype(F32)).astype(BF16)


def _exact_lhs_dot(sel, x):
    hi, lo = _split(x)
    return _mm(sel, hi) + _mm(sel, lo)


def _dot3(x, w):
    x_hi, x_lo = _split(x)
    w_hi, w_lo = _split(w)
    return _mm(x_hi, w_hi) + _mm(x_lo, w_hi) + _mm(x_hi, w_lo)


def _dot1(x, w):
    return _mm(x.astype(BF16), w.astype(BF16))


def _const_spec(shape):
    nd = len(shape)
    return pl.BlockSpec(shape, lambda *_: (0,) * nd)


def _ffn_body(xa_ref, xb_ref, wg_ref, wu_ref, wd_ref, fg_ref, oa_ref, ob_ref, *, final_norm, tiles_a):
    def tile(x_ref, o_ref):
        x = x_ref[...]
        xb = x.astype(BF16)
        inv = lax.rsqrt(jnp.mean(x * x, axis=-1, keepdims=True) + RMS_EPS)
        hg = _mm(xb, wg_ref[...]) * inv
        hu = _mm(xb, wu_ref[...]) * inv
        h = (hg * jax.nn.sigmoid(hg) * hu).astype(BF16)
        y = x + 0.5 * _mm(h, wd_ref[...])
        if final_norm:
            y = _rms(y, fg_ref[...])
        o_ref[...] = y

    in_a = pl.program_id(0) < tiles_a
    pl.when(in_a)(functools.partial(tile, xa_ref, oa_ref))
    pl.when(jnp.logical_not(in_a))(functools.partial(tile, xb_ref, ob_ref))


def _resident_spec(shape):
    nd = len(shape)
    return pl.BlockSpec(shape, lambda *_: (0,) * nd, pipeline_mode=pl.Buffered(1))


def _ffn(xa, xb, w_gate, w_up, w_down, final_g, *, final_norm, tm=512):
    (na, d), nb = xa.shape, xb.shape[0]
    tm = min(tm, na, nb)
    assert na % tm == 0 and nb % tm == 0
    ta, tb = na // tm, nb // tm
    spec_a = pl.BlockSpec((tm, d), lambda i: (jnp.minimum(i, ta - 1), 0))
    spec_b = pl.BlockSpec((tm, d), lambda i: (jnp.maximum(i - ta, 0), 0))
    return pl.pallas_call(
        functools.partial(_ffn_body, final_norm=final_norm, tiles_a=ta),
        grid=(ta + tb,),
        in_specs=[
            spec_a, spec_b,
            _resident_spec(w_gate.shape), _resident_spec(w_up.shape), _resident_spec(w_down.shape),
            _const_spec((1, d)),
        ],
        out_specs=[spec_a, spec_b],
        out_shape=[jax.ShapeDtypeStruct((na, d), F32), jax.ShapeDtypeStruct((nb, d), F32)],
        compiler_params=_params("arbitrary"),
        name="ffn_final" if final_norm else "ffn",
    )(xa, xb, w_gate, w_up, w_down, final_g)


def _memkv_body(m_ref, w_ref, wt_ref, k_ref, v_ref, kt_ref, vt_ref):
    mem = m_ref[0].astype(BF16)
    dk = k_ref.shape[-1]
    kv = _mm(mem, w_ref[...])
    k_ref[0] = kv[:, :dk]
    v_ref[0] = kv[:, dk:]
    kvt = _nt(wt_ref[...], mem)
    kt_ref[0] = kvt[:dk].astype(BF16)
    vt_ref[0] = kvt[dk:].astype(BF16)


def _memkv(mem, w, wt):
    b, n_mem, d = mem.shape
    dk = w.shape[1] // 2
    return pl.pallas_call(
        _memkv_body,
        grid=(b,),
        in_specs=[pl.BlockSpec((1, n_mem, d), lambda i: (i, 0, 0)), _const_spec(w.shape), _const_spec(wt.shape)],
        out_specs=[pl.BlockSpec((1, n_mem, dk), lambda i: (i, 0, 0))] * 2
        + [pl.BlockSpec((1, dk, n_mem), lambda i: (i, 0, 0))] * 2,
        out_shape=[jax.ShapeDtypeStruct((b, n_mem, dk), F32)] * 2
        + [jax.ShapeDtypeStruct((b, dk, n_mem), BF16)] * 2,
        compiler_params=_params("parallel"),
        name="memkv",
    )(mem, w, wt)


def _rwkv_body(x_ref, ws_ref, sprev_ref, mu_ref, w0_ref, wdu_ref, a0_ref, wau_ref, wgu_ref,
               kk_ref, ka_ref, rk_ref, gng_ref, gnb_ref, s0_ref, hsum_ref, tri_ref, blk_ref,
               wc_ref, glub_ref, cprev_ref, cw_ref, cb_ref, lng_ref, lnb_ref,
               og_ref, sout_ref, wkv_ref, yc_ref, cnew_ref,
               carry, st, at_s, rt_s, bb_s, kb_s, bh_s, kh_s, v_s, wt_s, bonus_s, g_s, o_s,
               gm_s, hm_s, rp_s, oi_s, ubuf, ushift, yc_s, *, chunk, unroll1, unroll2):
    bb_n, tt, d = x_ref.shape
    m = bb_n * tt
    d_shift = ws_ref.shape[1]
    d_r = w0_ref.shape[1]
    dc = cw_ref.shape[1]
    n_pairs = d_r // PAIR
    n_ck = tt // chunk
    n_blk = m // chunk
    log_chunk = int(math.log2(chunk))
    c2 = 2 * chunk
    conv_lo = CONV_PAD - (CONV_WIDTH - 1)
    t = pl.program_id(1)

    @pl.when(t == 0)
    def _():
        carry[...] = sprev_ref[...]
        ubuf[:, conv_lo:CONV_PAD, :] = cprev_ref[...]
        zero = jnp.zeros((HEAD_DIM, HEAD_DIM), F32)
        for b_i in range(bb_n):
            for p in range(n_pairs):
                top = jnp.concatenate([s0_ref[b_i, 2 * p], zero], axis=1)
                bot = jnp.concatenate([zero, s0_ref[b_i, 2 * p + 1]], axis=1)
                st[b_i, p] = jnp.concatenate([top, bot], axis=0)

    n_part = 2 if (bb_n == 1 and m >= 4 * chunk) else 1
    pm_ = m // n_part
    parts = range(n_part)
    psl = [slice(i * pm_, (i + 1) * pm_) for i in parts]
    x2d = x_ref[...].reshape(m, d)
    xn = [x2d[psl[i]].astype(BF16) for i in parts]
    inv = [lax.rsqrt(jnp.mean(x2d[psl[i]] * x2d[psl[i]], axis=-1, keepdims=True) + RMS_EPS) for i in parts]
    zc = [_mm(xn[i], wc_ref[...]) * inv[i] + glub_ref[...] for i in parts]
    zs = [_mm(xn[i], ws_ref[...]) * inv[i] for i in parts]
    u = [zc[i][:, :dc] * jax.nn.sigmoid(zc[i][:, dc:]) for i in parts]
    if n_part == 1:
        ubuf[:, CONV_PAD:CONV_PAD + tt, :] = u[0].reshape(bb_n, tt, dc)
    else:
        for i in parts:
            ubuf[0, CONV_PAD + i * pm_:CONV_PAD + (i + 1) * pm_, :] = u[i]
    for sh in range(1, SUBLANES):
        ushift[sh - 1] = ubuf[:, sh:sh + tt + CONV_PAD - SUBLANES, :]

    if n_part == 1:
        prev0 = jnp.broadcast_to(carry[...], (bb_n, tt, d_shift)).reshape(m, d_shift)
        row = lax.broadcasted_iota(jnp.int32, (m, d_shift), 0)
        prev = [jnp.where((row & (tt - 1)) == 0, prev0, pltpu.roll(zs[0], 1, axis=0))]
        last = zs[0].reshape(bb_n, tt, d_shift)[:, tt - 1:tt, :]
    else:
        row = lax.broadcasted_iota(jnp.int32, (pm_, d_shift), 0)
        first = [carry[0]] + [zs[i][pm_ - 1:pm_, :] for i in parts[:-1]]
        prev = [jnp.where(row == 0, first[i], pltpu.roll(zs[i], 1, axis=0)) for i in parts]
        last = zs[-1][pm_ - 1:pm_, :].reshape(1, 1, d_shift)
    carry[...] = last
    sout_ref[...] = last
    xm = [zs[i] + (prev[i] - zs[i]) * mu_ref[...] for i in parts]

    c1, c2_, c3 = d_r, 2 * d_r, 3 * d_r
    c4 = c3 + LORA_DECAY
    c5 = c4 + LORA_A
    r = [xm[i][:, :c1] for i in parts]
    k = [xm[i][:, c1:c2_] for i in parts]
    v = [xm[i][:, c2_:c3] for i in parts]
    dec_in = [w0_ref[...] + _dot3(jnp.tanh(xm[i][:, c3:c4]), wdu_ref[...]) for i in parts]
    a_in = [a0_ref[...] + _dot1(xm[i][:, c4:c5], wau_ref[...]) for i in parts]
    for i in parts:
        g_s[psl[i], :] = _dot1(jax.nn.sigmoid(xm[i][:, c5:]), wgu_ref[...])
    ld = [-jnp.exp(-(jnp.maximum(-dec_in[i], 0.0) + jnp.log(1.0 + jnp.exp(-jnp.abs(dec_in[i])))) - 0.5)
          for i in parts]
    a = [jax.nn.sigmoid(a_in[i]) for i in parts]
    kh = [k[i] * (1.0 + (a[i] - 1.0) * ka_ref[...]) for i in parts]

    head_sum = hsum_ref[...]
    kkraw = [k[i] * kk_ref[...] for i in parts]
    kk_ss = [_mm((kkraw[i] * kkraw[i]).astype(BF16), head_sum) for i in parts]
    rk_sum = [_mm((r[i] * kh[i] * rk_ref[...]).astype(BF16), head_sum) for i in parts]
    kk = [kkraw[i] / jnp.maximum(jnp.sqrt(kk_ss[i]), 1e-12) for i in parts]
    b = [kk[i] * a[i] for i in parts]
    tdt = at_s.dtype
    for i in parts:
        bonus_s[psl[i], :] = rk_sum[i] * v[i]
        v_s[psl[i], :] = v[i].astype(tdt)

    slab = tri_ref.shape[0]
    for s0 in range(0, pm_, slab):
        sl = slice(s0, s0 + slab)
        ld_c = [ld[i][sl] for i in parts]
        cum = [_exact_lhs_dot(tri_ref[...], ld_c[i]) for i in parts]
        tot = [_exact_lhs_dot(blk_ref[...], ld_c[i]) for i in parts]
        for i in parts:
            gl = slice(i * pm_ + s0, i * pm_ + s0 + slab)
            w_inv = jnp.exp(-cum[i])
            w_rem = jnp.exp(tot[i] - cum[i])
            at_s[gl, :] = (-kk[i][sl] * jnp.exp(cum[i] - ld_c[i])).astype(tdt)
            rt_s[gl, :] = (r[i][sl] * jnp.exp(cum[i])).astype(tdt)
            bb_s[gl, :] = (b[i][sl] * w_inv).astype(tdt)
            kb_s[gl, :] = (kh[i][sl] * w_inv).astype(tdt)
            bh_s[gl, :] = (b[i][sl] * w_rem).astype(tdt)
            kh_s[gl, :] = (kh[i][sl] * w_rem).astype(tdt)
            wt_s[gl, :] = jnp.exp(tot[i])

    lane_lo = lax.broadcasted_iota(jnp.int32, (chunk, PAIR), 1) < HEAD_DIM
    r2 = lax.broadcasted_iota(jnp.int32, (c2, c2), 0)
    q2 = lax.broadcasted_iota(jnp.int32, (c2, c2), 1)
    strict = (q2 & (chunk - 1)) < (r2 & (chunk - 1))
    incl = (q2 & (chunk - 1)) <= (r2 & (chunk - 1))
    eye2 = (r2 == q2).astype(F32)
    wide = c2 % PAIR == 0

    def stack(ref, rows, ls):
        x = ref[rows, ls]
        return jnp.concatenate([jnp.where(lane_lo, x, 0.0), jnp.where(lane_lo, 0.0, x)], axis=0).astype(BF16)

    def phase1_step(i, c):
        chains = [(unroll1 * i + j, p) for j in range(unroll1) for p in range(n_pairs)]
        n = range(len(chains))

        def stacks(ref):
            return [stack(ref, pl.ds(pl.multiple_of(blk * chunk, chunk), chunk), slice(p * PAIR, (p + 1) * PAIR))
                    for blk, p in chains]

        at, rt, bbm, kbm = stacks(at_s), stacks(rt_s), stacks(bb_s), stacks(kb_s)
        bhm, khm, vm = stacks(bh_s), stacks(kh_s), stacks(v_s)
        if wide:
            a4 = [_nt(jnp.concatenate([at[c_], rt[c_]], axis=0), jnp.concatenate([bbm[c_], kbm[c_]], axis=0))
                  for c_ in n]
            a_ab = [jnp.where(strict, a4[c_][:c2, :c2], 0.0) for c_ in n]
            a_ak = [jnp.where(strict, a4[c_][:c2, c2:], 0.0).astype(BF16) for c_ in n]
            a_rb = [jnp.where(incl, a4[c_][c2:, :c2], 0.0).astype(BF16) for c_ in n]
            a_rk = [jnp.where(incl, a4[c_][c2:, c2:], 0.0).astype(BF16) for c_ in n]
        else:
            a_ab = [jnp.where(strict, _nt(at[c_], bbm[c_]), 0.0) for c_ in n]
            a_ak = [jnp.where(strict, _nt(at[c_], kbm[c_]), 0.0).astype(BF16) for c_ in n]
            a_rb = [jnp.where(incl, _nt(rt[c_], bbm[c_]), 0.0).astype(BF16) for c_ in n]
            a_rk = [jnp.where(incl, _nt(rt[c_], kbm[c_]), 0.0).astype(BF16) for c_ in n]
        tinv = [eye2 + a_ab[c_] for c_ in n]
        ap = [a_ab[c_].astype(BF16) for c_ in n]
        ap = [_mm(ap[c_], ap[c_]).astype(BF16) for c_ in n]
        for i_sq in range(1, log_chunk):
            if i_sq == log_chunk - 1:
                tinv = [tinv[c_] + _mm(ap[c_], tinv[c_].astype(BF16)) for c_ in n]
            elif wide:
                x = [_mm(ap[c_], jnp.concatenate([ap[c_], tinv[c_].astype(BF16)], axis=1)) for c_ in n]
                ap = [x[c_][:, :c2].astype(BF16) for c_ in n]
                tinv = [tinv[c_] + x[c_][:, c2:] for c_ in n]
            else:
                tinv = [tinv[c_] + _mm(ap[c_], tinv[c_].astype(BF16)) for c_ in n]
                ap = [_mm(ap[c_], ap[c_]).astype(BF16) for c_ in n]
        av = [_mm(a_ak[c_], vm[c_]).astype(BF16) for c_ in n]
        pq = [_mm(tinv[c_].astype(BF16), jnp.concatenate([av[c_], at[c_]], axis=1)) for c_ in n]
        pm = [pq[c_][:, :PAIR].astype(BF16) for c_ in n]
        qm = [pq[c_][:, PAIR:].astype(BF16) for c_ in n]
        for c_, (blk, p) in enumerate(chains):
            gm_s[blk, p] = _tn(qm[c_], bhm[c_]).astype(BF16)
        for c_, (blk, p) in enumerate(chains):
            hm_s[blk, p] = _tn(jnp.concatenate([pm[c_], vm[c_]], axis=0),
                               jnp.concatenate([bhm[c_], khm[c_]], axis=0))
        rq = [_mm(a_rb[c_], jnp.concatenate([qm[c_], pm[c_]], axis=1)) for c_ in n]
        for c_, (blk, p) in enumerate(chains):
            rp_s[blk, p] = (rt[c_].astype(F32) + rq[c_][:, :PAIR]).astype(BF16)
        for c_, (blk, p) in enumerate(chains):
            oi_s[blk, p] = rq[c_][:, PAIR:] + _mm(a_rk[c_], vm[c_])
        return c

    lax.fori_loop(0, n_blk // unroll1, phase1_step, 0)

    def conv_block(blk):
        b_i, t0 = blk // n_ck, (blk % n_ck) * chunk
        acc = jnp.zeros((chunk, dc), F32) + cb_ref[...]
        for kx in range(CONV_WIDTH):
            off = conv_lo + kx
            sh, base = off % SUBLANES, off - off % SUBLANES
            rows = pl.ds(pl.multiple_of(t0 + base, SUBLANES), chunk)
            win = ubuf[b_i, rows, :] if sh == 0 else ushift[sh - 1, b_i, rows, :]
            acc = acc + win * cw_ref[kx:kx + 1, :]
        cm = jnp.mean(acc, axis=-1, keepdims=True)
        cc = acc - cm
        cv = jnp.mean(cc * cc, axis=-1, keepdims=True)
        cn = cc * lax.rsqrt(cv + LN_EPS) * lng_ref[...] + lnb_ref[...]
        yc_s[pl.ds(pl.multiple_of(blk * chunk, chunk), chunk), :] = cn * jax.nn.sigmoid(cn)

    def phase2_step(i, c):
        for j in range(unroll2):
            conv_block(unroll2 * i + j)
        chains = [(unroll2 * i + j, p) for j in range(unroll2) for p in range(n_pairs)]
        n = range(len(chains))
        row0 = [pl.multiple_of(blk * chunk, chunk) for blk, _ in chains]
        ls = [slice(p * PAIR, (p + 1) * PAIR) for _, p in chains]
        s = [st[blk // n_ck, p] for blk, p in chains]
        sb = [s[c_].astype(BF16) for c_ in n]
        s_new = [s[c_] * wt_s[pl.ds(row0[c_], 1), ls[c_]] + _mm(sb[c_], gm_s[blk, p]) + hm_s[blk, p]
                 for c_, (blk, p) in enumerate(chains)]
        o_bd = [oi_s[blk, p] + _nt(rp_s[blk, p], sb[c_]) for c_, (blk, p) in enumerate(chains)]
        for c_, (blk, p) in enumerate(chains):
            st[blk // n_ck, p] = s_new[c_]
        for c_ in n:
            o_s[pl.ds(row0[c_], chunk), ls[c_]] = o_bd[c_][:chunk] + o_bd[c_][chunk:]
        return c

    lax.fori_loop(0, n_blk // unroll2, phase2_step, 0)

    o = o_s[...]
    mean = _mm(o.astype(BF16), head_sum) * (1.0 / HEAD_DIM)
    oc = o - mean
    var = _mm((oc * oc).astype(BF16), head_sum) * (1.0 / HEAD_DIM)
    o = oc * lax.rsqrt(var + GN_EPS) * gng_ref[...] + gnb_ref[...] + bonus_s[...]
    og_ref[...] = (o * g_s[...]).astype(BF16).reshape(bb_n, tt, d_r)

    yc_ref[...] = yc_s[...].astype(BF16).reshape(bb_n, tt, dc)
    tail = ubuf[:, conv_lo + tt:CONV_PAD + tt, :]
    cnew_ref[...] = tail
    ubuf[:, conv_lo:CONV_PAD, :] = tail

    @pl.when(t == pl.num_programs(1) - 1)
    def _():
        for b_i in range(bb_n):
            for p in range(n_pairs):
                s = st[b_i, p]
                wkv_ref[b_i, 2 * p] = s[:HEAD_DIM, :HEAD_DIM]
                wkv_ref[b_i, 2 * p + 1] = s[HEAD_DIM:, HEAD_DIM:]


def _rwkv(x, ws, sprev, mu, w0, wdu, a0, wau, wgu, kk, ka, rk, gng, gnb, s0,
          wc, glub, cprev, cw, cb, lng, lnb, *, bb, tt, chunk, unroll1, unroll2):
    b, t_len, d = x.shape
    d_shift = ws.shape[1]
    d_r = w0.shape[1]
    dc = cw.shape[1]
    n_heads = d_r // HEAD_DIM
    n_pairs = d_r // PAIR
    m = bb * tt
    n_blk = m // chunk
    assert b % bb == 0 and t_len % tt == 0 and tt % chunk == 0
    assert n_blk % unroll1 == 0 and n_blk % unroll2 == 0 and (unroll2 == 1 or tt == chunk)
    assert chunk & (chunk - 1) == 0 and tt & (tt - 1) == 0 and chunk % 8 == 0
    tok = pltpu.VMEM((m, d_r), F32)
    tok_mm = pltpu.VMEM((m, d_r), BF16 if chunk % (2 * SUBLANES) == 0 else F32)
    lane = jnp.arange(d_r)
    head_sum = (lane[:, None] // HEAD_DIM == lane[None, :] // HEAD_DIM).astype(BF16)
    rows = jnp.arange(chunk if chunk >= HEAD_DIM else m)
    same_chunk = rows[:, None] // chunk == rows[None, :] // chunk
    tri = (same_chunk & (rows[None, :] <= rows[:, None])).astype(BF16)
    blk = same_chunk.astype(BF16)
    return pl.pallas_call(
        functools.partial(_rwkv_body, chunk=chunk, unroll1=unroll1, unroll2=unroll2),
        grid=(b // bb, t_len // tt),
        in_specs=[
            pl.BlockSpec((bb, tt, d), lambda i, j: (i, j, 0)),
            _const_spec(ws.shape),
            pl.BlockSpec((bb, 1, d_shift), lambda i, j: (i, 0, 0)),
            _const_spec(mu.shape), _const_spec(w0.shape), _const_spec(wdu.shape), _const_spec(a0.shape),
            _const_spec(wau.shape), _const_spec(wgu.shape), _const_spec(kk.shape), _const_spec(ka.shape),
            _const_spec(rk.shape), _const_spec(gng.shape), _const_spec(gnb.shape),
            pl.BlockSpec((bb, n_heads, HEAD_DIM, HEAD_DIM), lambda i, j: (i, 0, 0, 0)),
            _const_spec(head_sum.shape), _const_spec(tri.shape), _const_spec(blk.shape),
            _const_spec(wc.shape), _const_spec(glub.shape),
            pl.BlockSpec((bb, CONV_WIDTH - 1, dc), lambda i, j: (i, 0, 0)),
            _const_spec(cw.shape), _const_spec(cb.shape), _const_spec(lng.shape), _const_spec(lnb.shape),
        ],
        out_specs=[
            pl.BlockSpec((bb, tt, d_r), lambda i, j: (i, j, 0)),
            pl.BlockSpec((bb, 1, d_shift), lambda i, j: (i, 0, 0)),
            pl.BlockSpec((bb, n_heads, HEAD_DIM, HEAD_DIM), lambda i, j: (i, 0, 0, 0)),
            pl.BlockSpec((bb, tt, dc), lambda i, j: (i, j, 0)),
            pl.BlockSpec((bb, CONV_WIDTH - 1, dc), lambda i, j: (i, 0, 0)),
        ],
        out_shape=[
            jax.ShapeDtypeStruct((b, t_len, d_r), BF16),
            jax.ShapeDtypeStruct((b, 1, d_shift), F32),
            jax.ShapeDtypeStruct((b, n_heads, HEAD_DIM, HEAD_DIM), F32),
            jax.ShapeDtypeStruct((b, t_len, dc), BF16),
            jax.ShapeDtypeStruct((b, CONV_WIDTH - 1, dc), F32),
        ],
        scratch_shapes=[
            pltpu.VMEM((bb, 1, d_shift), F32),
            pltpu.VMEM((bb, n_pairs, PAIR, PAIR), F32),
        ] + [tok_mm] * 7 + [tok] * 4 + [
            pltpu.VMEM((n_blk, n_pairs, PAIR, PAIR), BF16),
            pltpu.VMEM((n_blk, n_pairs, PAIR, PAIR), F32),
            pltpu.VMEM((n_blk, n_pairs, 2 * chunk, PAIR), BF16),
            pltpu.VMEM((n_blk, n_pairs, 2 * chunk, PAIR), F32),
            pltpu.VMEM((bb, CONV_PAD + tt, dc), F32),
            pltpu.VMEM((SUBLANES - 1, bb, CONV_PAD + tt - SUBLANES, dc), F32),
            pltpu.VMEM((m, dc), F32),
        ],
        compiler_params=_params("parallel", "arbitrary"),
        name="rwkv_conv",
    )(x, ws, sprev, mu, w0, wdu, a0, wau, wgu, kk, ka, rk, gng, gnb, s0, head_sum, tri, blk,
      wc, glub, cprev, cw, cb, lng, lnb)


def _merge_body(x_ref, wq_ref, wg_ref, kt_ref, vt_ref, ya_ref, yb_ref, wro_ref, wco_ref, wxo_ref, wo_ref, o_ref,
                *, t_len, rows, group):
    tm, d = x_ref.shape
    dh = kt_ref.shape[1] // N_XATTN_HEADS
    x = x_ref[...]
    xb = x.astype(BF16)
    inv = lax.rsqrt(jnp.mean(x * x, axis=-1, keepdims=True) + RMS_EPS)

    q = (_mm(xb, wq_ref[...]) * (inv * dh ** -0.5)).astype(BF16)
    n_blocks = tm // rows
    hs = lambda h: slice(h * dh, (h + 1) * dh)
    seq = lambda r: (r * rows) // t_len
    pieces = []
    for g0 in range(0, n_blocks, group):
        blocks = range(g0, min(g0 + group, n_blocks))
        chains = [(r, h) for r in blocks for h in range(N_XATTN_HEADS)]
        s = [_mm(q[r * rows:(r + 1) * rows, hs(h)], kt_ref[seq(r), hs(h), :].astype(BF16)) for r, h in chains]
        p = [jnp.exp(v - jnp.max(v, axis=-1, keepdims=True)) for v in s]
        l = [jnp.sum(v, axis=-1, keepdims=True) for v in p]
        o = [_nt(p[i].astype(BF16), vt_ref[seq(r), hs(h), :].astype(BF16)) / l[i] for i, (r, h) in enumerate(chains)]
        for j in range(len(blocks)):
            pieces.append(jnp.concatenate(o[j * N_XATTN_HEADS:(j + 1) * N_XATTN_HEADS], axis=-1))
    oc = jnp.concatenate(pieces, axis=0).astype(BF16)

    merged = jnp.zeros_like(x)
    for i, (y, w_ref) in enumerate(((ya_ref[...], wro_ref), (yb_ref[...], wco_ref), (oc, wxo_ref))):
        gate = jax.nn.sigmoid(_mm(xb, wg_ref[:, i * d:(i + 1) * d]) * inv)
        merged = merged + gate * _mm(y, w_ref[...])
    o_ref[...] = x + _mm(merged.astype(BF16), wo_ref[...])


def _merge(x, wq, wg, kt, vt, ya, yb, wro, wco, wxo, wo, *, t_len, tm, rows, group):
    n, d = x.shape
    dx, n_mem = kt.shape[1:]
    seqs = max(tm // t_len, 1)
    assert n % tm == 0 and tm % rows == 0 and (tm % t_len == 0 or t_len % tm == 0) and rows <= t_len
    row = lambda w: pl.BlockSpec((tm, w), lambda i: (i, 0))
    mem = pl.BlockSpec((seqs, dx, n_mem), lambda i: ((i * tm) // (t_len * seqs), 0, 0))
    return pl.pallas_call(
        functools.partial(_merge_body, t_len=t_len, rows=rows, group=group),
        grid=(n // tm,),
        in_specs=[row(d), _resident_spec(wq.shape), _resident_spec(wg.shape), mem, mem, row(ya.shape[1]),
                  row(yb.shape[1]), _resident_spec(wro.shape), _resident_spec(wco.shape),
                  _resident_spec(wxo.shape), _resident_spec(wo.shape)],
        out_specs=row(d),
        out_shape=jax.ShapeDtypeStruct((n, d), F32),
        compiler_params=_params("parallel"),
        name="merge",
    )(x, wq, wg, kt, vt, ya, yb, wro, wco, wxo, wo)


def _mix(x1, shift_prev, conv_prev, wkv0, mem_kt, mem_vt, p, *, bb, tt, rw, mg):
    b, t_len, d = x1.shape
    n = b * t_len
    og, shift, wkv, cb, conv = _rwkv(
        x1, p['w_s'], shift_prev[:, None, :], p['mu_shift'], p['w0'], p['w_decay_up'], p['a0'],
        p['w_a_up'], p['w_g_up'], p['k_k'], p['k_a'], p['r_k'], p['gn_g'], p['gn_b'], wkv0,
        p['w_c'], p['glu_b'], conv_prev, p['conv_w'], p['conv_b'], p['conv_ln_g'], p['conv_ln_b'],
        bb=bb, tt=tt, **rw)
    x2 = _merge(x1.reshape(n, d), p['w_q'], p['w_g'], mem_kt, mem_vt, og.reshape(n, -1), cb.reshape(n, -1),
                p['w_rwkv_out'], p['w_conv_out'], p['w_xattn_out'], p['w_o'], t_len=t_len, **mg)
    return x2, wkv, shift[:, 0, :], conv


def kernel(x_prompt, mem_prompt, x_sample, state_wkv, state_shift, state_conv, cache_mem_k, cache_mem_v,
           ffn1_norm, ffn1_w_up, ffn1_w_down, mix_norm, w_in, mu_shift, w0, w_decay_up, a0, w_a_up, w_g_up,
           k_k, k_a, r_k, gn_g, gn_b, w_rwkv_out, glu_b, conv_w, conv_b, conv_ln_g, conv_ln_b, w_conv_out,
           w_mem_kv, w_xattn_out, w_o, ffn2_norm, ffn2_w_up, ffn2_w_down, final_norm):
    depth = w_in.shape[0]
    d_model = w_in.shape[1]
    d_r = w0.shape[1]
    d_shift = mu_shift.shape[1]
    d_conv = conv_w.shape[2]
    d_x = w_xattn_out.shape[1]
    n_heads = d_r // HEAD_DIM
    o1 = d_shift
    o2 = o1 + 2 * d_conv
    o3 = o2 + d_x
    d_ff = ffn1_w_down.shape[1]
    row = lambda a: a.astype(F32).reshape(1, -1)
    scaled = lambda g, w: (g.astype(F32)[:, None] * w.astype(F32)).astype(BF16)
    final_g = row(final_norm)

    layers = []
    for l in range(depth):
        w_mix = scaled(mix_norm[l], w_in[l])
        layers.append({
            'ffn1_w_gate': scaled(ffn1_norm[l], ffn1_w_up[l][:, :d_ff]),
            'ffn1_w_up': scaled(ffn1_norm[l], ffn1_w_up[l][:, d_ff:]),
            'ffn1_w_down': ffn1_w_down[l].astype(BF16),
            'w_s': w_mix[:, :o1], 'w_c': w_mix[:, o1:o2], 'w_q': w_mix[:, o2:o3], 'w_g': w_mix[:, o3:],
            'mu_shift': row(mu_shift[l]), 'w0': row(w0[l]), 'w_decay_up': w_decay_up[l].astype(F32),
            'a0': row(a0[l]), 'w_a_up': w_a_up[l].astype(F32), 'w_g_up': w_g_up[l].astype(F32),
            'k_k': row(k_k[l]), 'k_a': row(k_a[l]), 'r_k': row(r_k[l]), 'gn_g': row(gn_g[l]),
            'gn_b': row(gn_b[l]), 'w_rwkv_out': w_rwkv_out[l].astype(BF16), 'glu_b': row(glu_b[l]),
            'conv_w': conv_w[l].astype(F32), 'conv_b': row(conv_b[l]), 'conv_ln_g': row(conv_ln_g[l]),
            'conv_ln_b': row(conv_ln_b[l]), 'w_conv_out': w_conv_out[l].astype(BF16),
            'w_mem_kv': w_mem_kv[l].astype(BF16), 'w_mem_kv_t': w_mem_kv[l].T.astype(BF16), 'w_xattn_out': w_xattn_out[l].astype(BF16),
            'w_o': w_o[l].astype(BF16),
            'ffn2_w_gate': scaled(ffn2_norm[l], ffn2_w_up[l][:, :d_ff]),
            'ffn2_w_up': scaled(ffn2_norm[l], ffn2_w_up[l][:, d_ff:]), 'ffn2_w_down': ffn2_w_down[l].astype(BF16),
        })

    bp, tp, _ = x_prompt.shape
    bs, ts, _ = x_sample.shape
    n_mem = mem_prompt.shape[1]
    xp = x_prompt.astype(F32).reshape(bp * tp, d_model)
    xs = x_sample.astype(F32).reshape(bs * ts, d_model)
    wkv_p, shift_p, conv_p, mk_p, mv_p = [], [], [], [], []
    wkv_s, shift_s, conv_s = [], [], []
    for l in range(depth):
        p = layers[l]
        x1p, x1s = _ffn(xp, xs, p['ffn1_w_gate'], p['ffn1_w_up'], p['ffn1_w_down'], final_g, final_norm=False)

        mk, mv, mkt, mvt = _memkv(mem_prompt.astype(F32), p['w_mem_kv'], p['w_mem_kv_t'])
        x2p, wkv, sh, cv = _mix(
            x1p.reshape(bp, tp, d_model), jnp.zeros((bp, d_shift), F32),
            jnp.zeros((bp, CONV_WIDTH - 1, d_conv), F32), jnp.zeros((bp, n_heads, HEAD_DIM, HEAD_DIM), F32),
            mkt, mvt, p, bb=1, tt=512, rw=dict(chunk=64, unroll1=4, unroll2=1), mg=dict(tm=512, rows=512, group=1))
        wkv_p.append(wkv)
        shift_p.append(sh)
        conv_p.append(cv)
        mk_p.append(mk.reshape(bp, n_mem, N_XATTN_HEADS, d_x // N_XATTN_HEADS))
        mv_p.append(mv.reshape(bp, n_mem, N_XATTN_HEADS, d_x // N_XATTN_HEADS))

        kt = jnp.transpose(cache_mem_k[l].astype(F32), (0, 2, 3, 1)).reshape(bs, d_x, n_mem)
        vt = jnp.transpose(cache_mem_v[l].astype(F32), (0, 2, 3, 1)).reshape(bs, d_x, n_mem)
        x2s, wkv, sh, cv = _mix(
            x1s.reshape(bs, ts, d_model), state_shift[l].astype(F32), state_conv[l].astype(F32),
            state_wkv[l].astype(F32), kt, vt, p, bb=16, tt=ts, rw=dict(chunk=ts, unroll1=4, unroll2=4),
            mg=dict(tm=16 * ts, rows=ts, group=4))
        wkv_s.append(wkv)
        shift_s.append(sh)
        conv_s.append(cv)

        xp, xs = _ffn(x2p, x2s, p['ffn2_w_gate'], p['ffn2_w_up'], p['ffn2_w_down'], final_g,
                      final_norm=(l == depth - 1))

    return (xp.reshape(x_prompt.shape).astype(x_prompt.dtype), xs.reshape(x_sample.shape).astype(x_sample.dtype),
            jnp.stack(wkv_p), jnp.stack(shift_p), jnp.stack(conv_p), jnp.stack(mk_p), jnp.stack(mv_p),
            jnp.stack(wkv_s), jnp.stack(shift_s), jnp.stack(conv_s))
```

```python
import functools
import math

import jax
import jax.numpy as jnp
from jax import lax
from jax.experimental import pallas as pl
from jax.experimental.pallas import tpu as pltpu

F32 = jnp.float32
BF16 = jnp.bfloat16
HI = lax.Precision.HIGHEST

HEAD_DIM = 64
PAIR = 2 * HEAD_DIM
N_XATTN_HEADS = 4
CONV_WIDTH = 31
SUBLANES = 8
CONV_PAD = 32
LORA_DECAY = 64
LORA_A = 64
RMS_EPS = 1e-6
LN_EPS = 1e-5
GN_EPS = 64e-5
VMEM_LIMIT = 56 * 1024 * 1024


def _params(*sem):
    return pltpu.CompilerParams(dimension_semantics=sem, vmem_limit_bytes=VMEM_LIMIT)


def _rms(x, g):
    return x * lax.rsqrt(jnp.mean(x * x, axis=-1, keepdims=True) + RMS_EPS) * g


def _mm(a, b, precision=None):
    return jnp.dot(a, b, preferred_element_type=F32, precision=precision)


def _nt(a, b):
    return lax.dot_general(a, b, (((1,), (1,)), ((), ())), preferred_element_type=F32)


def _tn(a, b):
    return lax.dot_general(a, b, (((0,), (0,)), ((), ())), preferred_element_type=F32)


def _split(x):
    hi = x.astype(BF16)
    return hi, (x - hi.astype(F32)).astype(BF16)


def _exact_lhs_dot(sel, x):
    hi, lo = _split(x)
    return _mm(sel, hi) + _mm(sel, lo)


def _dot3(x, w):
    x_hi, x_lo = _split(x)
    w_hi, w_lo = _split(w)
    return _mm(x_hi, w_hi) + _mm(x_lo, w_hi) + _mm(x_hi, w_lo)


def _dot1(x, w):
    return _mm(x.astype(BF16), w.astype(BF16))


def _const_spec(shape):
    nd = len(shape)
    return pl.BlockSpec(shape, lambda *_: (0,) * nd)


def _ffn_body(xa_ref, xb_ref, wg_ref, wu_ref, wd_ref, fg_ref, oa_ref, ob_ref, *, final_norm, tiles_a):
    def tile(x_ref, o_ref):
        x = x_ref[...]
        xb = x.astype(BF16)
        inv = lax.rsqrt(jnp.mean(x * x, axis=-1, keepdims=True) + RMS_EPS)
        hg = _mm(xb, wg_ref[...]) * inv
        hu = _mm(xb, wu_ref[...]) * inv
        h = (hg * jax.nn.sigmoid(hg) * hu).astype(BF16)
        y = x + 0.5 * _mm(h, wd_ref[...])
        if final_norm:
            y = _rms(y, fg_ref[...])
        o_ref[...] = y

    in_a = pl.program_id(0) < tiles_a
    pl.when(in_a)(functools.partial(tile, xa_ref, oa_ref))
    pl.when(jnp.logical_not(in_a))(functools.partial(tile, xb_ref, ob_ref))


def _resident_spec(shape):
    nd = len(shape)
    return pl.BlockSpec(shape, lambda *_: (0,) * nd, pipeline_mode=pl.Buffered(1))


def _ffn(xa, xb, w_gate, w_up, w_down, final_g, *, final_norm, tm=512):
    (na, d), nb = xa.shape, xb.shape[0]
    tm = min(tm, na, nb)
    assert na % tm == 0 and nb % tm == 0
    ta, tb = na // tm, nb // tm
    spec_a = pl.BlockSpec((tm, d), lambda i: (jnp.minimum(i, ta - 1), 0))
    spec_b = pl.BlockSpec((tm, d), lambda i: (jnp.maximum(i - ta, 0), 0))
    return pl.pallas_call(
        functools.partial(_ffn_body, final_norm=final_norm, tiles_a=ta),
        grid=(ta + tb,),
        in_specs=[
            spec_a, spec_b,
            _resident_spec(w_gate.shape), _resident_spec(w_up.shape), _resident_spec(w_down.shape),
            _const_spec((1, d)),
        ],
        out_specs=[spec_a, spec_b],
        out_shape=[jax.ShapeDtypeStruct((na, d), F32), jax.ShapeDtypeStruct((nb, d), F32)],
        compiler_params=_params("arbitrary"),
        name="ffn_final" if final_norm else "ffn",
    )(xa, xb, w_gate, w_up, w_down, final_g)


def _memkv_body(m_ref, w_ref, wt_ref, k_ref, v_ref, kt_ref, vt_ref):
    mem = m_ref[0].astype(BF16)
    dk = k_ref.shape[-1]
    kv = _mm(mem, w_ref[...])
    k_ref[0] = kv[:, :dk]
    v_ref[0] = kv[:, dk:]
    kvt = _nt(wt_ref[...], mem)
    kt_ref[0] = kvt[:dk].astype(BF16)
    vt_ref[0] = kvt[dk:].astype(BF16)


def _memkv(mem, w, wt):
    b, n_mem, d = mem.shape
    dk = w.shape[1] // 2
    return pl.pallas_call(
        _memkv_body,
        grid=(b,),
        in_specs=[pl.BlockSpec((1, n_mem, d), lambda i: (i, 0, 0)), _const_spec(w.shape), _const_spec(wt.shape)],
        out_specs=[pl.BlockSpec((1, n_mem, dk), lambda i: (i, 0, 0))] * 2
        + [pl.BlockSpec((1, dk, n_mem), lambda i: (i, 0, 0))] * 2,
        out_shape=[jax.ShapeDtypeStruct((b, n_mem, dk), F32)] * 2
        + [jax.ShapeDtypeStruct((b, dk, n_mem), BF16)] * 2,
        compiler_params=_params("parallel"),
        name="memkv",
    )(mem, w, wt)


def _rwkv_body(x_ref, mixg_ref, ws_ref, sprev_ref, mu_ref, w0_ref, wdu_ref, a0_ref, wau_ref, wgu_ref,
               kk_ref, ka_ref, rk_ref, gng_ref, gnb_ref, s0_ref, hsum_ref, tri_ref, blk_ref,
               wc_ref, glub_ref, cprev_ref, cw_ref, cb_ref, lng_ref, lnb_ref,
               og_ref, sout_ref, wkv_ref, yc_ref, cnew_ref,
               carry, st, at_s, rt_s, bb_s, kb_s, bh_s, kh_s, v_s, wt_s, bonus_s, g_s, o_s,
               gm_s, hm_s, rp_s, oi_s, ubuf, ushift, yc_s, *, chunk, unroll1, unroll2):
    bb_n, tt, d = x_ref.shape
    m = bb_n * tt
    d_shift = ws_ref.shape[0]
    d_r = w0_ref.shape[1]
    dc = cw_ref.shape[1]
    n_pairs = d_r // PAIR
    n_ck = tt // chunk
    n_blk = m // chunk
    log_chunk = int(math.log2(chunk))
    c2 = 2 * chunk
    conv_lo = CONV_PAD - (CONV_WIDTH - 1)
    t = pl.program_id(1)

    @pl.when(t == 0)
    def _():
        carry[...] = sprev_ref[...]
        ubuf[:, conv_lo:CONV_PAD, :] = cprev_ref[...]
        zero = jnp.zeros((HEAD_DIM, HEAD_DIM), F32)
        for b_i in range(bb_n):
            for p in range(n_pairs):
                top = jnp.concatenate([s0_ref[b_i, 2 * p], zero], axis=1)
                bot = jnp.concatenate([zero, s0_ref[b_i, 2 * p + 1]], axis=1)
                st[b_i, p] = jnp.concatenate([top, bot], axis=0)

    n_part = 2 if (bb_n == 1 and m >= 4 * chunk) else 1
    pm_ = m // n_part
    parts = range(n_part)
    psl = [slice(i * pm_, (i + 1) * pm_) for i in parts]
    x2d = x_ref[...].reshape(m, d)
    xn = [_rms(x2d[psl[i]], mixg_ref[...]).astype(BF16) for i in parts]
    zc = [_nt(xn[i], wc_ref[...]) + glub_ref[...] for i in parts]
    zs = [_nt(xn[i], ws_ref[...]) for i in parts]
    u = [zc[i][:, :dc] * jax.nn.sigmoid(zc[i][:, dc:]) for i in parts]
    if n_part == 1:
        ubuf[:, CONV_PAD:CONV_PAD + tt, :] = u[0].reshape(bb_n, tt, dc)
    else:
        for i in parts:
            ubuf[0, CONV_PAD + i * pm_:CONV_PAD + (i + 1) * pm_, :] = u[i]
    for sh in range(1, SUBLANES):
        ushift[sh - 1] = ubuf[:, sh:sh + tt + CONV_PAD - SUBLANES, :]

    if n_part == 1:
        prev0 = jnp.broadcast_to(carry[...], (bb_n, tt, d_shift)).reshape(m, d_shift)
        row = lax.broadcasted_iota(jnp.int32, (m, d_shift), 0)
        prev = [jnp.where((row & (tt - 1)) == 0, prev0, pltpu.roll(zs[0], 1, axis=0))]
        last = zs[0].reshape(bb_n, tt, d_shift)[:, tt - 1:tt, :]
    else:
        row = lax.broadcasted_iota(jnp.int32, (pm_, d_shift), 0)
        first = [carry[0]] + [zs[i][pm_ - 1:pm_, :] for i in parts[:-1]]
        prev = [jnp.where(row == 0, first[i], pltpu.roll(zs[i], 1, axis=0)) for i in parts]
        last = zs[-1][pm_ - 1:pm_, :].reshape(1, 1, d_shift)
    carry[...] = last
    sout_ref[...] = last
    xm = [zs[i] + (prev[i] - zs[i]) * mu_ref[...] for i in parts]

    c1, c2_, c3 = d_r, 2 * d_r, 3 * d_r
    c4 = c3 + LORA_DECAY
    c5 = c4 + LORA_A
    r = [xm[i][:, :c1] for i in parts]
    k = [xm[i][:, c1:c2_] for i in parts]
    v = [xm[i][:, c2_:c3] for i in parts]
    dec_in = [w0_ref[...] + _dot3(jnp.tanh(xm[i][:, c3:c4]), wdu_ref[...]) for i in parts]
    a_in = [a0_ref[...] + _dot1(xm[i][:, c4:c5], wau_ref[...]) for i in parts]
    for i in parts:
        g_s[psl[i], :] = _dot1(jax.nn.sigmoid(xm[i][:, c5:]), wgu_ref[...])
    ld = [-jnp.exp(-(jnp.maximum(-dec_in[i], 0.0) + jnp.log(1.0 + jnp.exp(-jnp.abs(dec_in[i])))) - 0.5)
          for i in parts]
    a = [jax.nn.sigmoid(a_in[i]) for i in parts]
    kh = [k[i] * (1.0 + (a[i] - 1.0) * ka_ref[...]) for i in parts]

    head_sum = hsum_ref[...]
    kkraw = [k[i] * kk_ref[...] for i in parts]
    kk_ss = [_mm((kkraw[i] * kkraw[i]).astype(BF16), head_sum) for i in parts]
    rk_sum = [_mm((r[i] * kh[i] * rk_ref[...]).astype(BF16), head_sum) for i in parts]
    kk = [kkraw[i] / jnp.maximum(jnp.sqrt(kk_ss[i]), 1e-12) for i in parts]
    b = [kk[i] * a[i] for i in parts]
    tdt = at_s.dtype
    for i in parts:
        bonus_s[psl[i], :] = rk_sum[i] * v[i]
        v_s[psl[i], :] = v[i].astype(tdt)

    slab = tri_ref.shape[0]
    for s0 in range(0, pm_, slab):
        sl = slice(s0, s0 + slab)
        ld_c = [ld[i][sl] for i in parts]
        cum = [_exact_lhs_dot(tri_ref[...], ld_c[i]) for i in parts]
        tot = [_exact_lhs_dot(blk_ref[...], ld_c[i]) for i in parts]
        for i in parts:
            gl = slice(i * pm_ + s0, i * pm_ + s0 + slab)
            w_inv = jnp.exp(-cum[i])
            w_rem = jnp.exp(tot[i] - cum[i])
            at_s[gl, :] = (-kk[i][sl] * jnp.exp(cum[i] - ld_c[i])).astype(tdt)
            rt_s[gl, :] = (r[i][sl] * jnp.exp(cum[i])).astype(tdt)
            bb_s[gl, :] = (b[i][sl] * w_inv).astype(tdt)
            kb_s[gl, :] = (kh[i][sl] * w_inv).astype(tdt)
            bh_s[gl, :] = (b[i][sl] * w_rem).astype(tdt)
            kh_s[gl, :] = (kh[i][sl] * w_rem).astype(tdt)
            wt_s[gl, :] = jnp.exp(tot[i])

    lane_lo = lax.broadcasted_iota(jnp.int32, (chunk, PAIR), 1) < HEAD_DIM
    r2 = lax.broadcasted_iota(jnp.int32, (c2, c2), 0)
    q2 = lax.broadcasted_iota(jnp.int32, (c2, c2), 1)
    strict = (q2 & (chunk - 1)) < (r2 & (chunk - 1))
    incl = (q2 & (chunk - 1)) <= (r2 & (chunk - 1))
    eye2 = (r2 == q2).astype(F32)
    wide = c2 % PAIR == 0

    def stack(ref, rows, ls):
        x = ref[rows, ls]
        return jnp.concatenate([jnp.where(lane_lo, x, 0.0), jnp.where(lane_lo, 0.0, x)], axis=0).astype(BF16)

    def phase1_step(i, c):
        chains = [(unroll1 * i + j, p) for j in range(unroll1) for p in range(n_pairs)]
        n = range(len(chains))

        def stacks(ref):
            return [stack(ref, pl.ds(pl.multiple_of(blk * chunk, chunk), chunk), slice(p * PAIR, (p + 1) * PAIR))
                    for blk, p in chains]

        at, rt, bbm, kbm = stacks(at_s), stacks(rt_s), stacks(bb_s), stacks(kb_s)
        bhm, khm, vm = stacks(bh_s), stacks(kh_s), stacks(v_s)
        if wide:
            a4 = [_nt(jnp.concatenate([at[c_], rt[c_]], axis=0), jnp.concatenate([bbm[c_], kbm[c_]], axis=0))
                  for c_ in n]
            a_ab = [jnp.where(strict, a4[c_][:c2, :c2], 0.0) for c_ in n]
            a_ak = [jnp.where(strict, a4[c_][:c2, c2:], 0.0).astype(BF16) for c_ in n]
            a_rb = [jnp.where(incl, a4[c_][c2:, :c2], 0.0).astype(BF16) for c_ in n]
            a_rk = [jnp.where(incl, a4[c_][c2:, c2:], 0.0).astype(BF16) for c_ in n]
        else:
            a_ab = [jnp.where(strict, _nt(at[c_], bbm[c_]), 0.0) for c_ in n]
            a_ak = [jnp.where(strict, _nt(at[c_], kbm[c_]), 0.0).astype(BF16) for c_ in n]
            a_rb = [jnp.where(incl, _nt(rt[c_], bbm[c_]), 0.0).astype(BF16) for c_ in n]
            a_rk = [jnp.where(incl, _nt(rt[c_], kbm[c_]), 0.0).astype(BF16) for c_ in n]
        tinv = [eye2 + a_ab[c_] for c_ in n]
        ap = [a_ab[c_].astype(BF16) for c_ in n]
        ap = [_mm(ap[c_], ap[c_]).astype(BF16) for c_ in n]
        for i_sq in range(1, log_chunk):
            if i_sq == log_chunk - 1:
                tinv = [tinv[c_] + _mm(ap[c_], tinv[c_].astype(BF16)) for c_ in n]
            elif wide:
                x = [_mm(ap[c_], jnp.concatenate([ap[c_], tinv[c_].astype(BF16)], axis=1)) for c_ in n]
                ap = [x[c_][:, :c2].astype(BF16) for c_ in n]
                tinv = [tinv[c_] + x[c_][:, c2:] for c_ in n]
            else:
                tinv = [tinv[c_] + _mm(ap[c_], tinv[c_].astype(BF16)) for c_ in n]
                ap = [_mm(ap[c_], ap[c_]).astype(BF16) for c_ in n]
        av = [_mm(a_ak[c_], vm[c_]).astype(BF16) for c_ in n]
        pq = [_mm(tinv[c_].astype(BF16), jnp.concatenate([av[c_], at[c_]], axis=1)) for c_ in n]
        pm = [pq[c_][:, :PAIR].astype(BF16) for c_ in n]
        qm = [pq[c_][:, PAIR:].astype(BF16) for c_ in n]
        for c_, (blk, p) in enumerate(chains):
            gm_s[blk, p] = _tn(qm[c_], bhm[c_]).astype(BF16)
        for c_, (blk, p) in enumerate(chains):
            hm_s[blk, p] = _tn(jnp.concatenate([pm[c_], vm[c_]], axis=0),
                               jnp.concatenate([bhm[c_], khm[c_]], axis=0))
        rq = [_mm(a_rb[c_], jnp.concatenate([qm[c_], pm[c_]], axis=1)) for c_ in n]
        for c_, (blk, p) in enumerate(chains):
            rp_s[blk, p] = (rt[c_].astype(F32) + rq[c_][:, :PAIR]).astype(BF16)
        for c_, (blk, p) in enumerate(chains):
            oi_s[blk, p] = rq[c_][:, PAIR:] + _mm(a_rk[c_], vm[c_])
        return c

    lax.fori_loop(0, n_blk // unroll1, phase1_step, 0)

    def conv_block(blk):
        b_i, t0 = blk // n_ck, (blk % n_ck) * chunk
        acc = jnp.zeros((chunk, dc), F32) + cb_ref[...]
        for kx in range(CONV_WIDTH):
            off = conv_lo + kx
            sh, base = off % SUBLANES, off - off % SUBLANES
            rows = pl.ds(pl.multiple_of(t0 + base, SUBLANES), chunk)
            win = ubuf[b_i, rows, :] if sh == 0 else ushift[sh - 1, b_i, rows, :]
            acc = acc + win * cw_ref[kx:kx + 1, :]
        cm = jnp.mean(acc, axis=-1, keepdims=True)
        cc = acc - cm
        cv = jnp.mean(cc * cc, axis=-1, keepdims=True)
        cn = cc * lax.rsqrt(cv + LN_EPS) * lng_ref[...] + lnb_ref[...]
        yc_s[pl.ds(pl.multiple_of(blk * chunk, chunk), chunk), :] = cn * jax.nn.sigmoid(cn)

    def phase2_step(i, c):
        for j in range(unroll2):
            conv_block(unroll2 * i + j)
        chains = [(unroll2 * i + j, p) for j in range(unroll2) for p in range(n_pairs)]
        n = range(len(chains))
        row0 = [pl.multiple_of(blk * chunk, chunk) for blk, _ in chains]
        ls = [slice(p * PAIR, (p + 1) * PAIR) for _, p in chains]
        s = [st[blk // n_ck, p] for blk, p in chains]
        sb = [s[c_].astype(BF16) for c_ in n]
        s_new = [s[c_] * wt_s[pl.ds(row0[c_], 1), ls[c_]] + _mm(sb[c_], gm_s[blk, p]) + hm_s[blk, p]
                 for c_, (blk, p) in enumerate(chains)]
        o_bd = [oi_s[blk, p] + _nt(rp_s[blk, p], sb[c_]) for c_, (blk, p) in enumerate(chains)]
        for c_, (blk, p) in enumerate(chains):
            st[blk // n_ck, p] = s_new[c_]
        for c_ in n:
            o_s[pl.ds(row0[c_], chunk), ls[c_]] = o_bd[c_][:chunk] + o_bd[c_][chunk:]
        return c

    lax.fori_loop(0, n_blk // unroll2, phase2_step, 0)

    o = o_s[...]
    mean = _mm(o.astype(BF16), head_sum) * (1.0 / HEAD_DIM)
    oc = o - mean
    var = _mm((oc * oc).astype(BF16), head_sum) * (1.0 / HEAD_DIM)
    o = oc * lax.rsqrt(var + GN_EPS) * gng_ref[...] + gnb_ref[...] + bonus_s[...]
    og_ref[...] = (o * g_s[...]).astype(BF16).reshape(bb_n, tt, d_r)

    yc_ref[...] = yc_s[...].astype(BF16).reshape(bb_n, tt, dc)
    tail = ubuf[:, conv_lo + tt:CONV_PAD + tt, :]
    cnew_ref[...] = tail
    ubuf[:, conv_lo:CONV_PAD, :] = tail

    @pl.when(t == pl.num_programs(1) - 1)
    def _():
        for b_i in range(bb_n):
            for p in range(n_pairs):
                s = st[b_i, p]
                wkv_ref[b_i, 2 * p] = s[:HEAD_DIM, :HEAD_DIM]
                wkv_ref[b_i, 2 * p + 1] = s[HEAD_DIM:, HEAD_DIM:]


def _rwkv(x, mixg, ws, sprev, mu, w0, wdu, a0, wau, wgu, kk, ka, rk, gng, gnb, s0,
          wc, glub, cprev, cw, cb, lng, lnb, *, bb, tt, chunk, unroll1, unroll2):
    b, t_len, d = x.shape
    d_shift = ws.shape[0]
    d_r = w0.shape[1]
    dc = cw.shape[1]
    n_heads = d_r // HEAD_DIM
    n_pairs = d_r // PAIR
    m = bb * tt
    n_blk = m // chunk
    assert b % bb == 0 and t_len % tt == 0 and tt % chunk == 0
    assert n_blk % unroll1 == 0 and n_blk % unroll2 == 0 and (unroll2 == 1 or tt == chunk)
    assert chunk & (chunk - 1) == 0 and tt & (tt - 1) == 0 and chunk % 8 == 0
    tok = pltpu.VMEM((m, d_r), F32)
    tok_mm = pltpu.VMEM((m, d_r), BF16 if chunk % (2 * SUBLANES) == 0 else F32)
    lane = jnp.arange(d_r)
    head_sum = (lane[:, None] // HEAD_DIM == lane[None, :] // HEAD_DIM).astype(BF16)
    rows = jnp.arange(chunk if chunk >= HEAD_DIM else m)
    same_chunk = rows[:, None] // chunk == rows[None, :] // chunk
    tri = (same_chunk & (rows[None, :] <= rows[:, None])).astype(BF16)
    blk = same_chunk.astype(BF16)
    return pl.pallas_call(
        functools.partial(_rwkv_body, chunk=chunk, unroll1=unroll1, unroll2=unroll2),
        grid=(b // bb, t_len // tt),
        in_specs=[
            pl.BlockSpec((bb, tt, d), lambda i, j: (i, j, 0)),
            _const_spec(mixg.shape), _const_spec(ws.shape),
            pl.BlockSpec((bb, 1, d_shift), lambda i, j: (i, 0, 0)),
            _const_spec(mu.shape), _const_spec(w0.shape), _const_spec(wdu.shape), _const_spec(a0.shape),
            _const_spec(wau.shape), _const_spec(wgu.shape), _const_spec(kk.shape), _const_spec(ka.shape),
            _const_spec(rk.shape), _const_spec(gng.shape), _const_spec(gnb.shape),
            pl.BlockSpec((bb, n_heads, HEAD_DIM, HEAD_DIM), lambda i, j: (i, 0, 0, 0)),
            _const_spec(head_sum.shape), _const_spec(tri.shape), _const_spec(blk.shape),
            _const_spec(wc.shape), _const_spec(glub.shape),
            pl.BlockSpec((bb, CONV_WIDTH - 1, dc), lambda i, j: (i, 0, 0)),
            _const_spec(cw.shape), _const_spec(cb.shape), _const_spec(lng.shape), _const_spec(lnb.shape),
        ],
        out_specs=[
            pl.BlockSpec((bb, tt, d_r), lambda i, j: (i, j, 0)),
            pl.BlockSpec((bb, 1, d_shift), lambda i, j: (i, 0, 0)),
            pl.BlockSpec((bb, n_heads, HEAD_DIM, HEAD_DIM), lambda i, j: (i, 0, 0, 0)),
            pl.BlockSpec((bb, tt, dc), lambda i, j: (i, j, 0)),
            pl.BlockSpec((bb, CONV_WIDTH - 1, dc), lambda i, j: (i, 0, 0)),
        ],
        out_shape=[
            jax.ShapeDtypeStruct((b, t_len, d_r), BF16),
            jax.ShapeDtypeStruct((b, 1, d_shift), F32),
            jax.ShapeDtypeStruct((b, n_heads, HEAD_DIM, HEAD_DIM), F32),
            jax.ShapeDtypeStruct((b, t_len, dc), BF16),
            jax.ShapeDtypeStruct((b, CONV_WIDTH - 1, dc), F32),
        ],
        scratch_shapes=[
            pltpu.VMEM((bb, 1, d_shift), F32),
            pltpu.VMEM((bb, n_pairs, PAIR, PAIR), F32),
        ] + [tok_mm] * 7 + [tok] * 4 + [
            pltpu.VMEM((n_blk, n_pairs, PAIR, PAIR), BF16),
            pltpu.VMEM((n_blk, n_pairs, PAIR, PAIR), F32),
            pltpu.VMEM((n_blk, n_pairs, 2 * chunk, PAIR), BF16),
            pltpu.VMEM((n_blk, n_pairs, 2 * chunk, PAIR), F32),
            pltpu.VMEM((bb, CONV_PAD + tt, dc), F32),
            pltpu.VMEM((SUBLANES - 1, bb, CONV_PAD + tt - SUBLANES, dc), F32),
            pltpu.VMEM((m, dc), F32),
        ],
        compiler_params=_params("parallel", "arbitrary"),
        name="rwkv_conv",
    )(x, mixg, ws, sprev, mu, w0, wdu, a0, wau, wgu, kk, ka, rk, gng, gnb, s0, head_sum, tri, blk,
      wc, glub, cprev, cw, cb, lng, lnb)


def _merge_body(x_ref, wq_ref, wg_ref, kt_ref, vt_ref, ya_ref, yb_ref, wro_ref, wco_ref, wxo_ref, wo_ref, o_ref,
                *, t_len, rows, group):
    tm, d = x_ref.shape
    dh = kt_ref.shape[1] // N_XATTN_HEADS
    x = x_ref[...]
    xb = x.astype(BF16)
    inv = lax.rsqrt(jnp.mean(x * x, axis=-1, keepdims=True) + RMS_EPS)

    q = (_nt(xb, wq_ref[...]) * (inv * dh ** -0.5)).astype(BF16)
    n_blocks = tm // rows
    hs = lambda h: slice(h * dh, (h + 1) * dh)
    seq = lambda r: (r * rows) // t_len
    pieces = []
    for g0 in range(0, n_blocks, group):
        blocks = range(g0, min(g0 + group, n_blocks))
        chains = [(r, h) for r in blocks for h in range(N_XATTN_HEADS)]
        s = [_mm(q[r * rows:(r + 1) * rows, hs(h)], kt_ref[seq(r), hs(h), :].astype(BF16)) for r, h in chains]
        p = [jnp.exp(v - jnp.max(v, axis=-1, keepdims=True)) for v in s]
        l = [jnp.sum(v, axis=-1, keepdims=True) for v in p]
        o = [_nt(p[i].astype(BF16), vt_ref[seq(r), hs(h), :].astype(BF16)) / l[i] for i, (r, h) in enumerate(chains)]
        for j in range(len(blocks)):
            pieces.append(jnp.concatenate(o[j * N_XATTN_HEADS:(j + 1) * N_XATTN_HEADS], axis=-1))
    oc = jnp.concatenate(pieces, axis=0).astype(BF16)

    merged = jnp.zeros_like(x)
    for i, (y, w_ref) in enumerate(((ya_ref[...], wro_ref), (yb_ref[...], wco_ref), (oc, wxo_ref))):
        gate = jax.nn.sigmoid(_nt(xb, wg_ref[i * d:(i + 1) * d, :]) * inv)
        merged = merged + gate * _mm(y, w_ref[...])
    o_ref[...] = x + _mm(merged.astype(BF16), wo_ref[...])


def _merge(x, wq, wg, kt, vt, ya, yb, wro, wco, wxo, wo, *, t_len, tm, rows, group):
    n, d = x.shape
    dx, n_mem = kt.shape[1:]
    seqs = max(tm // t_len, 1)
    assert n % tm == 0 and tm % rows == 0 and (tm % t_len == 0 or t_len % tm == 0) and rows <= t_len
    row = lambda w: pl.BlockSpec((tm, w), lambda i: (i, 0))
    mem = pl.BlockSpec((seqs, dx, n_mem), lambda i: ((i * tm) // (t_len * seqs), 0, 0))
    return pl.pallas_call(
        functools.partial(_merge_body, t_len=t_len, rows=rows, group=group),
        grid=(n // tm,),
        in_specs=[row(d), _resident_spec(wq.shape), _resident_spec(wg.shape), mem, mem, row(ya.shape[1]),
                  row(yb.shape[1]), _resident_spec(wro.shape), _resident_spec(wco.shape),
                  _resident_spec(wxo.shape), _resident_spec(wo.shape)],
        out_specs=row(d),
        out_shape=jax.ShapeDtypeStruct((n, d), F32),
        compiler_params=_params("parallel"),
        name="merge",
    )(x, wq, wg, kt, vt, ya, yb, wro, wco, wxo, wo)


def _mix(x1, shift_prev, conv_prev, wkv0, mem_kt, mem_vt, p, *, bb, tt, rw, mg):
    b, t_len, d = x1.shape
    n = b * t_len
    og, shift, wkv, cb, conv = _rwkv(
        x1, p['mix_norm'], p['w_s'], shift_prev[:, None, :], p['mu_shift'], p['w0'], p['w_decay_up'], p['a0'],
        p['w_a_up'], p['w_g_up'], p['k_k'], p['k_a'], p['r_k'], p['gn_g'], p['gn_b'], wkv0,
        p['w_c'], p['glu_b'], conv_prev, p['conv_w'], p['conv_b'], p['conv_ln_g'], p['conv_ln_b'],
        bb=bb, tt=tt, **rw)
    x2 = _merge(x1.reshape(n, d), p['w_q'], p['w_g'], mem_kt, mem_vt, og.reshape(n, -1), cb.reshape(n, -1),
                p['w_rwkv_out'], p['w_conv_out'], p['w_xattn_out'], p['w_o'], t_len=t_len, **mg)
    return x2, wkv, shift[:, 0, :], conv


def kernel(x_prompt, mem_prompt, x_sample, state_wkv, state_shift, state_conv, cache_mem_k, cache_mem_v,
           ffn1_norm, ffn1_w_up, ffn1_w_down, mix_norm, w_in, mu_shift, w0, w_decay_up, a0, w_a_up, w_g_up,
           k_k, k_a, r_k, gn_g, gn_b, w_rwkv_out, glu_b, conv_w, conv_b, conv_ln_g, conv_ln_b, w_conv_out,
           w_mem_kv, w_xattn_out, w_o, ffn2_norm, ffn2_w_up, ffn2_w_down, final_norm):
    depth = w_in.shape[0]
    d_model = w_in.shape[1]
    d_r = w0.shape[1]
    d_shift = mu_shift.shape[1]
    d_conv = conv_w.shape[2]
    d_x = w_xattn_out.shape[1]
    n_heads = d_r // HEAD_DIM
    o1 = d_shift
    o2 = o1 + 2 * d_conv
    o3 = o2 + d_x
    d_ff = ffn1_w_down.shape[1]
    row = lambda a: a.astype(F32).reshape(1, -1)
    scaled = lambda g, w: (g.astype(F32)[:, None] * w.astype(F32)).astype(BF16)
    final_g = row(final_norm)

    scaled_t = lambda g, w_t: (w_t.astype(F32) * g.astype(F32)[None, :]).astype(BF16)

    layers = []
    for l in range(depth):
        w_in_t = jnp.swapaxes(w_in[l], 0, 1)
        layers.append({
            'ffn1_w_gate': scaled(ffn1_norm[l], ffn1_w_up[l][:, :d_ff]),
            'ffn1_w_up': scaled(ffn1_norm[l], ffn1_w_up[l][:, d_ff:]),
            'ffn1_w_down': ffn1_w_down[l].astype(BF16), 'mix_norm': row(mix_norm[l]),
            'w_s': w_in_t[:o1].astype(BF16), 'w_c': w_in_t[o1:o2].astype(BF16),
            'w_q': scaled_t(mix_norm[l], w_in_t[o2:o3]), 'w_g': scaled_t(mix_norm[l], w_in_t[o3:]),
            'mu_shift': row(mu_shift[l]), 'w0': row(w0[l]), 'w_decay_up': w_decay_up[l].astype(F32),
            'a0': row(a0[l]), 'w_a_up': w_a_up[l].astype(F32), 'w_g_up': w_g_up[l].astype(F32),
            'k_k': row(k_k[l]), 'k_a': row(k_a[l]), 'r_k': row(r_k[l]), 'gn_g': row(gn_g[l]),
            'gn_b': row(gn_b[l]), 'w_rwkv_out': w_rwkv_out[l].astype(BF16), 'glu_b': row(glu_b[l]),
            'conv_w': conv_w[l].astype(F32), 'conv_b': row(conv_b[l]), 'conv_ln_g': row(conv_ln_g[l]),
            'conv_ln_b': row(conv_ln_b[l]), 'w_conv_out': w_conv_out[l].astype(BF16),
            'w_mem_kv': w_mem_kv[l].astype(BF16), 'w_mem_kv_t': w_mem_kv[l].T.astype(BF16), 'w_xattn_out': w_xattn_out[l].astype(BF16),
            'w_o': w_o[l].astype(BF16),
            'ffn2_w_gate': scaled(ffn2_norm[l], ffn2_w_up[l][:, :d_ff]),
            'ffn2_w_up': scaled(ffn2_norm[l], ffn2_w_up[l][:, d_ff:]), 'ffn2_w_down': ffn2_w_down[l].astype(BF16),
        })

    bp, tp, _ = x_prompt.shape
    bs, ts, _ = x_sample.shape
    n_mem = mem_prompt.shape[1]
    xp = x_prompt.astype(F32).reshape(bp * tp, d_model)
    xs = x_sample.astype(F32).reshape(bs * ts, d_model)
    wkv_p, shift_p, conv_p, mk_p, mv_p = [], [], [], [], []
    wkv_s, shift_s, conv_s = [], [], []
    for l in range(depth):
        p = layers[l]
        x1p, x1s = _ffn(xp, xs, p['ffn1_w_gate'], p['ffn1_w_up'], p['ffn1_w_down'], final_g, final_norm=False)

        mk, mv, mkt, mvt = _memkv(mem_prompt.astype(F32), p['w_mem_kv'], p['w_mem_kv_t'])
        x2p, wkv, sh, cv = _mix(
            x1p.reshape(bp, tp, d_model), jnp.zeros((bp, d_shift), F32),
            jnp.zeros((bp, CONV_WIDTH - 1, d_conv), F32), jnp.zeros((bp, n_heads, HEAD_DIM, HEAD_DIM), F32),
            mkt, mvt, p, bb=1, tt=512, rw=dict(chunk=64, unroll1=4, unroll2=1), mg=dict(tm=512, rows=512, group=1))
        wkv_p.append(wkv)
        shift_p.append(sh)
        conv_p.append(cv)
        mk_p.append(mk.reshape(bp, n_mem, N_XATTN_HEADS, d_x // N_XATTN_HEADS))
        mv_p.append(mv.reshape(bp, n_mem, N_XATTN_HEADS, d_x // N_XATTN_HEADS))

        kt = jnp.transpose(cache_mem_k[l].astype(F32), (0, 2, 3, 1)).reshape(bs, d_x, n_mem)
        vt = jnp.transpose(cache_mem_v[l].astype(F32), (0, 2, 3, 1)).reshape(bs, d_x, n_mem)
        x2s, wkv, sh, cv = _mix(
            x1s.reshape(bs, ts, d_model), state_shift[l].astype(F32), state_conv[l].astype(F32),
            state_wkv[l].astype(F32), kt, vt, p, bb=16, tt=ts, rw=dict(chunk=ts, unroll1=4, unroll2=4),
            mg=dict(tm=16 * ts, rows=ts, group=4))
        wkv_s.append(wkv)
        shift_s.append(sh)
        conv_s.append(cv)

        xp, xs = _ffn(x2p, x2s, p['ffn2_w_gate'], p['ffn2_w_up'], p['ffn2_w_down'], final_g,
                      final_norm=(l == depth - 1))

    return (xp.reshape(x_prompt.shape).astype(x_prompt.dtype), xs.reshape(x_sample.shape).astype(x_sample.dtype),
            jnp.stack(wkv_p), jnp.stack(shift_p), jnp.stack(conv_p), jnp.stack(mk_p), jnp.stack(mv_p),
            jnp.stack(wkv_s), jnp.stack(shift_s), jnp.stack(conv_s))
```

```python
import functools
import math

import jax
import jax.numpy as jnp
from jax import lax
from jax.experimental import pallas as pl
from jax.experimental.pallas import tpu as pltpu

F32 = jnp.float32
BF16 = jnp.bfloat16
HI = lax.Precision.HIGHEST

HEAD_DIM = 64
PAIR = 2 * HEAD_DIM
N_XATTN_HEADS = 4
CONV_WIDTH = 31
SUBLANES = 8
CONV_PAD = 32
LORA_DECAY = 64
LORA_A = 64
RMS_EPS = 1e-6
LN_EPS = 1e-5
GN_EPS = 64e-5
VMEM_LIMIT = 56 * 1024 * 1024


def _params(*sem):
    return pltpu.CompilerParams(dimension_semantics=sem, vmem_limit_bytes=VMEM_LIMIT)


def _rms(x, g):
    return x * lax.rsqrt(jnp.mean(x * x, axis=-1, keepdims=True) + RMS_EPS) * g


def _mm(a, b, precision=None):
    return jnp.dot(a, b, preferred_element_type=F32, precision=precision)


def _nt(a, b):
    return lax.dot_general(a, b, (((1,), (1,)), ((), ())), preferred_element_type=F32)


def _tn(a, b):
    return lax.dot_general(a, b, (((0,), (0,)), ((), ())), preferred_element_type=F32)


def _split(x):
    hi = x.astype(BF16)
    return hi, (x - hi.astype(F32)).astype(BF16)


def _exact_lhs_dot(sel, x):
    hi, lo = _split(x)
    return _mm(sel, hi) + _mm(sel, lo)


def _dot3(x, w):
    x_hi, x_lo = _split(x)
    w_hi, w_lo = _split(w)
    return _mm(x_hi, w_hi) + _mm(x_lo, w_hi) + _mm(x_hi, w_lo)


def _dot1(x, w):
    return _mm(x.astype(BF16), w.astype(BF16))


def _const_spec(shape):
    nd = len(shape)
    return pl.BlockSpec(shape, lambda *_: (0,) * nd)


def _ffn_body(xa_ref, xb_ref, wg_ref, wu_ref, wd_ref, fg_ref, oa_ref, ob_ref, *, final_norm, tiles_a):
    def tile(x_ref, o_ref):
        x = x_ref[...]
        xb = x.astype(BF16)
        inv = lax.rsqrt(jnp.mean(x * x, axis=-1, keepdims=True) + RMS_EPS)
        hg = _mm(xb, wg_ref[...]) * inv
        hu = _mm(xb, wu_ref[...]) * inv
        h = (hg * jax.nn.sigmoid(hg) * hu).astype(BF16)
        y = x + 0.5 * _mm(h, wd_ref[...])
        if final_norm:
            y = _rms(y, fg_ref[...])
        o_ref[...] = y

    in_a = pl.program_id(0) < tiles_a
    pl.when(in_a)(functools.partial(tile, xa_ref, oa_ref))
    pl.when(jnp.logical_not(in_a))(functools.partial(tile, xb_ref, ob_ref))


def _resident_spec(shape):
    nd = len(shape)
    return pl.BlockSpec(shape, lambda *_: (0,) * nd, pipeline_mode=pl.Buffered(1))


def _ffn(xa, xb, w_gate, w_up, w_down, final_g, *, final_norm, tm=512):
    (na, d), nb = xa.shape, xb.shape[0]
    tm = min(tm, na, nb)
    assert na % tm == 0 and nb % tm == 0
    ta, tb = na // tm, nb // tm
    spec_a = pl.BlockSpec((tm, d), lambda i: (jnp.minimum(i, ta - 1), 0))
    spec_b = pl.BlockSpec((tm, d), lambda i: (jnp.maximum(i - ta, 0), 0))
    return pl.pallas_call(
        functools.partial(_ffn_body, final_norm=final_norm, tiles_a=ta),
        grid=(ta + tb,),
        in_specs=[
            spec_a, spec_b,
            _resident_spec(w_gate.shape), _resident_spec(w_up.shape), _resident_spec(w_down.shape),
            _const_spec((1, d)),
        ],
        out_specs=[spec_a, spec_b],
        out_shape=[jax.ShapeDtypeStruct((na, d), F32), jax.ShapeDtypeStruct((nb, d), F32)],
        compiler_params=_params("arbitrary"),
        name="ffn_final" if final_norm else "ffn",
    )(xa, xb, w_gate, w_up, w_down, final_g)


def _memkv_body(m_ref, wt_ref, kt_ref, vt_ref):
    dk = kt_ref.shape[1]
    kvt = _nt(wt_ref[...], m_ref[0].astype(BF16))
    kt_ref[0] = kvt[:dk]
    vt_ref[0] = kvt[dk:]


def _memkv(mem, wt):
    b, n_mem, d = mem.shape
    dk = wt.shape[0] // 2
    return pl.pallas_call(
        _memkv_body,
        grid=(b,),
        in_specs=[pl.BlockSpec((1, n_mem, d), lambda i: (i, 0, 0)), _const_spec(wt.shape)],
        out_specs=[pl.BlockSpec((1, dk, n_mem), lambda i: (i, 0, 0))] * 2,
        out_shape=[jax.ShapeDtypeStruct((b, dk, n_mem), F32)] * 2,
        compiler_params=_params("parallel"),
        name="memkv",
    )(mem, wt)


def _rwkv_body(x_ref, mixg_ref, ws_ref, sprev_ref, mu_ref, w0_ref, wdu_ref, a0_ref, wau_ref, wgu_ref,
               kk_ref, ka_ref, rk_ref, gng_ref, gnb_ref, s0_ref, hsum_ref, tri_ref, blk_ref,
               wc_ref, glub_ref, cprev_ref, cw_ref, cb_ref, lng_ref, lnb_ref,
               og_ref, sout_ref, wkv_ref, yc_ref, cnew_ref,
               carry, st, at_s, rt_s, bb_s, kb_s, bh_s, kh_s, v_s, wt_s, bonus_s, g_s, o_s,
               gm_s, hm_s, rp_s, oi_s, ubuf, ushift, yc_s, *, chunk, unroll1, unroll2):
    bb_n, tt, d = x_ref.shape
    m = bb_n * tt
    d_shift = ws_ref.shape[1]
    d_r = w0_ref.shape[1]
    dc = cw_ref.shape[1]
    n_pairs = d_r // PAIR
    n_ck = tt // chunk
    n_blk = m // chunk
    log_chunk = int(math.log2(chunk))
    c2 = 2 * chunk
    conv_lo = CONV_PAD - (CONV_WIDTH - 1)
    t = pl.program_id(1)

    @pl.when(t == 0)
    def _():
        carry[...] = sprev_ref[...]
        ubuf[:, conv_lo:CONV_PAD, :] = cprev_ref[...]
        zero = jnp.zeros((HEAD_DIM, HEAD_DIM), F32)
        for b_i in range(bb_n):
            for p in range(n_pairs):
                top = jnp.concatenate([s0_ref[b_i, 2 * p], zero], axis=1)
                bot = jnp.concatenate([zero, s0_ref[b_i, 2 * p + 1]], axis=1)
                st[b_i, p] = jnp.concatenate([top, bot], axis=0)

    n_part = 2 if (bb_n == 1 and m >= 4 * chunk) else 1
    pm_ = m // n_part
    parts = range(n_part)
    psl = [slice(i * pm_, (i + 1) * pm_) for i in parts]
    x2d = x_ref[...].reshape(m, d)
    xn = [_rms(x2d[psl[i]], mixg_ref[...]).astype(BF16) for i in parts]
    zc = [_mm(xn[i], wc_ref[...]) + glub_ref[...] for i in parts]
    zs = [_mm(xn[i], ws_ref[...]) for i in parts]
    u = [zc[i][:, :dc] * jax.nn.sigmoid(zc[i][:, dc:]) for i in parts]
    if n_part == 1:
        ubuf[:, CONV_PAD:CONV_PAD + tt, :] = u[0].reshape(bb_n, tt, dc)
    else:
        for i in parts:
            ubuf[0, CONV_PAD + i * pm_:CONV_PAD + (i + 1) * pm_, :] = u[i]
    for sh in range(1, SUBLANES):
        ushift[sh - 1] = ubuf[:, sh:sh + tt + CONV_PAD - SUBLANES, :]

    if n_part == 1:
        prev0 = jnp.broadcast_to(carry[...], (bb_n, tt, d_shift)).reshape(m, d_shift)
        row = lax.broadcasted_iota(jnp.int32, (m, d_shift), 0)
        prev = [jnp.where((row & (tt - 1)) == 0, prev0, pltpu.roll(zs[0], 1, axis=0))]
        last = zs[0].reshape(bb_n, tt, d_shift)[:, tt - 1:tt, :]
    else:
        row = lax.broadcasted_iota(jnp.int32, (pm_, d_shift), 0)
        first = [carry[0]] + [zs[i][pm_ - 1:pm_, :] for i in parts[:-1]]
        prev = [jnp.where(row == 0, first[i], pltpu.roll(zs[i], 1, axis=0)) for i in parts]
        last = zs[-1][pm_ - 1:pm_, :].reshape(1, 1, d_shift)
    carry[...] = last
    sout_ref[...] = last
    xm = [zs[i] + (prev[i] - zs[i]) * mu_ref[...] for i in parts]

    c1, c2_, c3 = d_r, 2 * d_r, 3 * d_r
    c4 = c3 + LORA_DECAY
    c5 = c4 + LORA_A
    r = [xm[i][:, :c1] for i in parts]
    k = [xm[i][:, c1:c2_] for i in parts]
    v = [xm[i][:, c2_:c3] for i in parts]
    dec_in = [w0_ref[...] + _dot3(jnp.tanh(xm[i][:, c3:c4]), wdu_ref[...]) for i in parts]
    a_in = [a0_ref[...] + _dot1(xm[i][:, c4:c5], wau_ref[...]) for i in parts]
    for i in parts:
        g_s[psl[i], :] = _dot1(jax.nn.sigmoid(xm[i][:, c5:]), wgu_ref[...])
    ld = [-jnp.exp(-(jnp.maximum(-dec_in[i], 0.0) + jnp.log(1.0 + jnp.exp(-jnp.abs(dec_in[i])))) - 0.5)
          for i in parts]
    a = [jax.nn.sigmoid(a_in[i]) for i in parts]
    kh = [k[i] * (1.0 + (a[i] - 1.0) * ka_ref[...]) for i in parts]

    head_sum = hsum_ref[...]
    kkraw = [k[i] * kk_ref[...] for i in parts]
    kk_ss = [_mm((kkraw[i] * kkraw[i]).astype(BF16), head_sum) for i in parts]
    rk_sum = [_mm((r[i] * kh[i] * rk_ref[...]).astype(BF16), head_sum) for i in parts]
    kk = [kkraw[i] / jnp.maximum(jnp.sqrt(kk_ss[i]), 1e-12) for i in parts]
    b = [kk[i] * a[i] for i in parts]
    tdt = at_s.dtype
    for i in parts:
        bonus_s[psl[i], :] = rk_sum[i] * v[i]
        v_s[psl[i], :] = v[i].astype(tdt)

    slab = tri_ref.shape[0]
    for s0 in range(0, pm_, slab):
        sl = slice(s0, s0 + slab)
        ld_c = [ld[i][sl] for i in parts]
        cum = [_exact_lhs_dot(tri_ref[...], ld_c[i]) for i in parts]
        tot = [_exact_lhs_dot(blk_ref[...], ld_c[i]) for i in parts]
        for i in parts:
            gl = slice(i * pm_ + s0, i * pm_ + s0 + slab)
            w_inv = jnp.exp(-cum[i])
            w_rem = jnp.exp(tot[i] - cum[i])
            at_s[gl, :] = (-kk[i][sl] * jnp.exp(cum[i] - ld_c[i])).astype(tdt)
            rt_s[gl, :] = (r[i][sl] * jnp.exp(cum[i])).astype(tdt)
            bb_s[gl, :] = (b[i][sl] * w_inv).astype(tdt)
            kb_s[gl, :] = (kh[i][sl] * w_inv).astype(tdt)
            bh_s[gl, :] = (b[i][sl] * w_rem).astype(tdt)
            kh_s[gl, :] = (kh[i][sl] * w_rem).astype(tdt)
            wt_s[gl, :] = jnp.exp(tot[i])

    lane_lo = lax.broadcasted_iota(jnp.int32, (chunk, PAIR), 1) < HEAD_DIM
    r2 = lax.broadcasted_iota(jnp.int32, (c2, c2), 0)
    q2 = lax.broadcasted_iota(jnp.int32, (c2, c2), 1)
    strict = (q2 & (chunk - 1)) < (r2 & (chunk - 1))
    incl = (q2 & (chunk - 1)) <= (r2 & (chunk - 1))
    eye2 = (r2 == q2).astype(F32)
    wide = c2 % PAIR == 0

    def stack(ref, rows, ls):
        x = ref[rows, ls]
        return jnp.concatenate([jnp.where(lane_lo, x, 0.0), jnp.where(lane_lo, 0.0, x)], axis=0).astype(BF16)

    def phase1_step(i, c):
        chains = [(unroll1 * i + j, p) for j in range(unroll1) for p in range(n_pairs)]
        n = range(len(chains))

        def stacks(ref):
            return [stack(ref, pl.ds(pl.multiple_of(blk * chunk, chunk), chunk), slice(p * PAIR, (p + 1) * PAIR))
                    for blk, p in chains]

        at, rt, bbm, kbm = stacks(at_s), stacks(rt_s), stacks(bb_s), stacks(kb_s)
        bhm, khm, vm = stacks(bh_s), stacks(kh_s), stacks(v_s)
        if wide:
            a4 = [_nt(jnp.concatenate([at[c_], rt[c_]], axis=0), jnp.concatenate([bbm[c_], kbm[c_]], axis=0))
                  for c_ in n]
            a_ab = [jnp.where(strict, a4[c_][:c2, :c2], 0.0) for c_ in n]
            a_ak = [jnp.where(strict, a4[c_][:c2, c2:], 0.0).astype(BF16) for c_ in n]
            a_rb = [jnp.where(incl, a4[c_][c2:, :c2], 0.0).astype(BF16) for c_ in n]
            a_rk = [jnp.where(incl, a4[c_][c2:, c2:], 0.0).astype(BF16) for c_ in n]
        else:
            a_ab = [jnp.where(strict, _nt(at[c_], bbm[c_]), 0.0) for c_ in n]
            a_ak = [jnp.where(strict, _nt(at[c_], kbm[c_]), 0.0).astype(BF16) for c_ in n]
            a_rb = [jnp.where(incl, _nt(rt[c_], bbm[c_]), 0.0).astype(BF16) for c_ in n]
            a_rk = [jnp.where(incl, _nt(rt[c_], kbm[c_]), 0.0).astype(BF16) for c_ in n]
        tinv = [eye2 + a_ab[c_] for c_ in n]
        ap = [a_ab[c_].astype(BF16) for c_ in n]
        ap = [_mm(ap[c_], ap[c_]).astype(BF16) for c_ in n]
        for i_sq in range(1, log_chunk):
            if i_sq == log_chunk - 1:
                tinv = [tinv[c_] + _mm(ap[c_], tinv[c_].astype(BF16)) for c_ in n]
            elif wide:
                x = [_mm(ap[c_], jnp.concatenate([ap[c_], tinv[c_].astype(BF16)], axis=1)) for c_ in n]
                ap = [x[c_][:, :c2].astype(BF16) for c_ in n]
                tinv = [tinv[c_] + x[c_][:, c2:] for c_ in n]
            else:
                tinv = [tinv[c_] + _mm(ap[c_], tinv[c_].astype(BF16)) for c_ in n]
                ap = [_mm(ap[c_], ap[c_]).astype(BF16) for c_ in n]
        av = [_mm(a_ak[c_], vm[c_]).astype(BF16) for c_ in n]
        pq = [_mm(tinv[c_].astype(BF16), jnp.concatenate([av[c_], at[c_]], axis=1)) for c_ in n]
        pm = [pq[c_][:, :PAIR].astype(BF16) for c_ in n]
        qm = [pq[c_][:, PAIR:].astype(BF16) for c_ in n]
        for c_, (blk, p) in enumerate(chains):
            gm_s[blk, p] = _tn(qm[c_], bhm[c_]).astype(BF16)
        for c_, (blk, p) in enumerate(chains):
            hm_s[blk, p] = _tn(jnp.concatenate([pm[c_], vm[c_]], axis=0),
                               jnp.concatenate([bhm[c_], khm[c_]], axis=0))
        rq = [_mm(a_rb[c_], jnp.concatenate([qm[c_], pm[c_]], axis=1)) for c_ in n]
        for c_, (blk, p) in enumerate(chains):
            rp_s[blk, p] = (rt[c_].astype(F32) + rq[c_][:, :PAIR]).astype(BF16)
        for c_, (blk, p) in enumerate(chains):
            oi_s[blk, p] = rq[c_][:, PAIR:] + _mm(a_rk[c_], vm[c_])
        return c

    lax.fori_loop(0, n_blk // unroll1, phase1_step, 0)

    def conv_block(blk):
        b_i, t0 = blk // n_ck, (blk % n_ck) * chunk
        acc = jnp.zeros((chunk, dc), F32) + cb_ref[...]
        for kx in range(CONV_WIDTH):
            off = conv_lo + kx
            sh, base = off % SUBLANES, off - off % SUBLANES
            rows = pl.ds(pl.multiple_of(t0 + base, SUBLANES), chunk)
            win = ubuf[b_i, rows, :] if sh == 0 else ushift[sh - 1, b_i, rows, :]
            acc = acc + win * cw_ref[kx:kx + 1, :]
        cm = jnp.mean(acc, axis=-1, keepdims=True)
        cc = acc - cm
        cv = jnp.mean(cc * cc, axis=-1, keepdims=True)
        cn = cc * lax.rsqrt(cv + LN_EPS) * lng_ref[...] + lnb_ref[...]
        yc_s[pl.ds(pl.multiple_of(blk * chunk, chunk), chunk), :] = cn * jax.nn.sigmoid(cn)

    def phase2_step(i, c):
        for j in range(unroll2):
            conv_block(unroll2 * i + j)
        chains = [(unroll2 * i + j, p) for j in range(unroll2) for p in range(n_pairs)]
        n = range(len(chains))
        row0 = [pl.multiple_of(blk * chunk, chunk) for blk, _ in chains]
        ls = [slice(p * PAIR, (p + 1) * PAIR) for _, p in chains]
        s = [st[blk // n_ck, p] for blk, p in chains]
        sb = [s[c_].astype(BF16) for c_ in n]
        s_new = [s[c_] * wt_s[pl.ds(row0[c_], 1), ls[c_]] + _mm(sb[c_], gm_s[blk, p]) + hm_s[blk, p]
                 for c_, (blk, p) in enumerate(chains)]
        o_bd = [oi_s[blk, p] + _nt(rp_s[blk, p], sb[c_]) for c_, (blk, p) in enumerate(chains)]
        for c_, (blk, p) in enumerate(chains):
            st[blk // n_ck, p] = s_new[c_]
        for c_ in n:
            o_s[pl.ds(row0[c_], chunk), ls[c_]] = o_bd[c_][:chunk] + o_bd[c_][chunk:]
        return c

    lax.fori_loop(0, n_blk // unroll2, phase2_step, 0)

    o = o_s[...]
    mean = _mm(o.astype(BF16), head_sum) * (1.0 / HEAD_DIM)
    oc = o - mean
    var = _mm((oc * oc).astype(BF16), head_sum) * (1.0 / HEAD_DIM)
    o = oc * lax.rsqrt(var + GN_EPS) * gng_ref[...] + gnb_ref[...] + bonus_s[...]
    og_ref[...] = (o * g_s[...]).astype(BF16).reshape(bb_n, tt, d_r)

    yc_ref[...] = yc_s[...].astype(BF16).reshape(bb_n, tt, dc)
    tail = ubuf[:, conv_lo + tt:CONV_PAD + tt, :]
    cnew_ref[...] = tail
    ubuf[:, conv_lo:CONV_PAD, :] = tail

    @pl.when(t == pl.num_programs(1) - 1)
    def _():
        for b_i in range(bb_n):
            for p in range(n_pairs):
                s = st[b_i, p]
                wkv_ref[b_i, 2 * p] = s[:HEAD_DIM, :HEAD_DIM]
                wkv_ref[b_i, 2 * p + 1] = s[HEAD_DIM:, HEAD_DIM:]


def _rwkv(x, mixg, ws, sprev, mu, w0, wdu, a0, wau, wgu, kk, ka, rk, gng, gnb, s0,
          wc, glub, cprev, cw, cb, lng, lnb, *, bb, tt, chunk, unroll1, unroll2):
    b, t_len, d = x.shape
    d_shift = ws.shape[1]
    d_r = w0.shape[1]
    dc = cw.shape[1]
    n_heads = d_r // HEAD_DIM
    n_pairs = d_r // PAIR
    m = bb * tt
    n_blk = m // chunk
    assert b % bb == 0 and t_len % tt == 0 and tt % chunk == 0
    assert n_blk % unroll1 == 0 and n_blk % unroll2 == 0 and (unroll2 == 1 or tt == chunk)
    assert chunk & (chunk - 1) == 0 and tt & (tt - 1) == 0 and chunk % 8 == 0
    tok = pltpu.VMEM((m, d_r), F32)
    tok_mm = pltpu.VMEM((m, d_r), BF16 if chunk % (2 * SUBLANES) == 0 else F32)
    lane = jnp.arange(d_r)
    head_sum = (lane[:, None] // HEAD_DIM == lane[None, :] // HEAD_DIM).astype(BF16)
    rows = jnp.arange(chunk if chunk >= HEAD_DIM else m)
    same_chunk = rows[:, None] // chunk == rows[None, :] // chunk
    tri = (same_chunk & (rows[None, :] <= rows[:, None])).astype(BF16)
    blk = same_chunk.astype(BF16)
    return pl.pallas_call(
        functools.partial(_rwkv_body, chunk=chunk, unroll1=unroll1, unroll2=unroll2),
        grid=(b // bb, t_len // tt),
        in_specs=[
            pl.BlockSpec((bb, tt, d), lambda i, j: (i, j, 0)),
            _const_spec(mixg.shape), _const_spec(ws.shape),
            pl.BlockSpec((bb, 1, d_shift), lambda i, j: (i, 0, 0)),
            _const_spec(mu.shape), _const_spec(w0.shape), _const_spec(wdu.shape), _const_spec(a0.shape),
            _const_spec(wau.shape), _const_spec(wgu.shape), _const_spec(kk.shape), _const_spec(ka.shape),
            _const_spec(rk.shape), _const_spec(gng.shape), _const_spec(gnb.shape),
            pl.BlockSpec((bb, n_heads, HEAD_DIM, HEAD_DIM), lambda i, j: (i, 0, 0, 0)),
            _const_spec(head_sum.shape), _const_spec(tri.shape), _const_spec(blk.shape),
            _const_spec(wc.shape), _const_spec(glub.shape),
            pl.BlockSpec((bb, CONV_WIDTH - 1, dc), lambda i, j: (i, 0, 0)),
            _const_spec(cw.shape), _const_spec(cb.shape), _const_spec(lng.shape), _const_spec(lnb.shape),
        ],
        out_specs=[
            pl.BlockSpec((bb, tt, d_r), lambda i, j: (i, j, 0)),
            pl.BlockSpec((bb, 1, d_shift), lambda i, j: (i, 0, 0)),
            pl.BlockSpec((bb, n_heads, HEAD_DIM, HEAD_DIM), lambda i, j: (i, 0, 0, 0)),
            pl.BlockSpec((bb, tt, dc), lambda i, j: (i, j, 0)),
            pl.BlockSpec((bb, CONV_WIDTH - 1, dc), lambda i, j: (i, 0, 0)),
        ],
        out_shape=[
            jax.ShapeDtypeStruct((b, t_len, d_r), BF16),
            jax.ShapeDtypeStruct((b, 1, d_shift), F32),
            jax.ShapeDtypeStruct((b, n_heads, HEAD_DIM, HEAD_DIM), F32),
            jax.ShapeDtypeStruct((b, t_len, dc), BF16),
            jax.ShapeDtypeStruct((b, CONV_WIDTH - 1, dc), F32),
        ],
        scratch_shapes=[
            pltpu.VMEM((bb, 1, d_shift), F32),
            pltpu.VMEM((bb, n_pairs, PAIR, PAIR), F32),
        ] + [tok_mm] * 7 + [tok] * 4 + [
            pltpu.VMEM((n_blk, n_pairs, PAIR, PAIR), BF16),
            pltpu.VMEM((n_blk, n_pairs, PAIR, PAIR), F32),
            pltpu.VMEM((n_blk, n_pairs, 2 * chunk, PAIR), BF16),
            pltpu.VMEM((n_blk, n_pairs, 2 * chunk, PAIR), F32),
            pltpu.VMEM((bb, CONV_PAD + tt, dc), F32),
            pltpu.VMEM((SUBLANES - 1, bb, CONV_PAD + tt - SUBLANES, dc), F32),
            pltpu.VMEM((m, dc), F32),
        ],
        compiler_params=_params("parallel", "arbitrary"),
        name="rwkv_conv",
    )(x, mixg, ws, sprev, mu, w0, wdu, a0, wau, wgu, kk, ka, rk, gng, gnb, s0, head_sum, tri, blk,
      wc, glub, cprev, cw, cb, lng, lnb)


def _merge_body(x_ref, wq_ref, wg_ref, kt_ref, vt_ref, ya_ref, yb_ref, wro_ref, wco_ref, wxo_ref, wo_ref, o_ref,
                *, t_len, rows, group):
    tm, d = x_ref.shape
    dh = kt_ref.shape[1] // N_XATTN_HEADS
    x = x_ref[...]
    xb = x.astype(BF16)
    inv = lax.rsqrt(jnp.mean(x * x, axis=-1, keepdims=True) + RMS_EPS)

    q = (_mm(xb, wq_ref[...]) * (inv * dh ** -0.5)).astype(BF16)
    n_blocks = tm // rows
    hs = lambda h: slice(h * dh, (h + 1) * dh)
    seq = lambda r: (r * rows) // t_len
    pieces = []
    for g0 in range(0, n_blocks, group):
        blocks = range(g0, min(g0 + group, n_blocks))
        chains = [(r, h) for r in blocks for h in range(N_XATTN_HEADS)]
        s = [_mm(q[r * rows:(r + 1) * rows, hs(h)], kt_ref[seq(r), hs(h), :].astype(BF16)) for r, h in chains]
        p = [jnp.exp(v - jnp.max(v, axis=-1, keepdims=True)) for v in s]
        l = [jnp.sum(v, axis=-1, keepdims=True) for v in p]
        o = [_nt(p[i].astype(BF16), vt_ref[seq(r), hs(h), :].astype(BF16)) / l[i] for i, (r, h) in enumerate(chains)]
        for j in range(len(blocks)):
            pieces.append(jnp.concatenate(o[j * N_XATTN_HEADS:(j + 1) * N_XATTN_HEADS], axis=-1))
    oc = jnp.concatenate(pieces, axis=0).astype(BF16)

    merged = jnp.zeros_like(x)
    for i, (y, w_ref) in enumerate(((ya_ref[...], wro_ref), (yb_ref[...], wco_ref), (oc, wxo_ref))):
        gate = jax.nn.sigmoid(_mm(xb, wg_ref[:, i * d:(i + 1) * d]) * inv)
        merged = merged + gate * _mm(y, w_ref[...])
    o_ref[...] = x + _mm(merged.astype(BF16), wo_ref[...])


def _merge(x, wq, wg, kt, vt, ya, yb, wro, wco, wxo, wo, *, t_len, tm, rows, group):
    n, d = x.shape
    dx, n_mem = kt.shape[1:]
    seqs = max(tm // t_len, 1)
    assert n % tm == 0 and tm % rows == 0 and (tm % t_len == 0 or t_len % tm == 0) and rows <= t_len
    row = lambda w: pl.BlockSpec((tm, w), lambda i: (i, 0))
    mem = pl.BlockSpec((seqs, dx, n_mem), lambda i: ((i * tm) // (t_len * seqs), 0, 0))
    return pl.pallas_call(
        functools.partial(_merge_body, t_len=t_len, rows=rows, group=group),
        grid=(n // tm,),
        in_specs=[row(d), _resident_spec(wq.shape), _resident_spec(wg.shape), mem, mem, row(ya.shape[1]),
                  row(yb.shape[1]), _resident_spec(wro.shape), _resident_spec(wco.shape),
                  _resident_spec(wxo.shape), _resident_spec(wo.shape)],
        out_specs=row(d),
        out_shape=jax.ShapeDtypeStruct((n, d), F32),
        compiler_params=_params("parallel"),
        name="merge",
    )(x, wq, wg, kt, vt, ya, yb, wro, wco, wxo, wo)


def _mix(x1, shift_prev, conv_prev, wkv0, mem_kt, mem_vt, p, *, bb, tt, rw, mg):
    b, t_len, d = x1.shape
    n = b * t_len
    og, shift, wkv, cb, conv = _rwkv(
        x1, p['mix_norm'], p['w_s'], shift_prev[:, None, :], p['mu_shift'], p['w0'], p['w_decay_up'], p['a0'],
        p['w_a_up'], p['w_g_up'], p['k_k'], p['k_a'], p['r_k'], p['gn_g'], p['gn_b'], wkv0,
        p['w_c'], p['glu_b'], conv_prev, p['conv_w'], p['conv_b'], p['conv_ln_g'], p['conv_ln_b'],
        bb=bb, tt=tt, **rw)
    x2 = _merge(x1.reshape(n, d), p['w_q'], p['w_g'], mem_kt, mem_vt, og.reshape(n, -1), cb.reshape(n, -1),
                p['w_rwkv_out'], p['w_conv_out'], p['w_xattn_out'], p['w_o'], t_len=t_len, **mg)
    return x2, wkv, shift[:, 0, :], conv


def kernel(x_prompt, mem_prompt, x_sample, state_wkv, state_shift, state_conv, cache_mem_k, cache_mem_v,
           ffn1_norm, ffn1_w_up, ffn1_w_down, mix_norm, w_in, mu_shift, w0, w_decay_up, a0, w_a_up, w_g_up,
           k_k, k_a, r_k, gn_g, gn_b, w_rwkv_out, glu_b, conv_w, conv_b, conv_ln_g, conv_ln_b, w_conv_out,
           w_mem_kv, w_xattn_out, w_o, ffn2_norm, ffn2_w_up, ffn2_w_down, final_norm):
    depth = w_in.shape[0]
    d_model = w_in.shape[1]
    d_r = w0.shape[1]
    d_shift = mu_shift.shape[1]
    d_conv = conv_w.shape[2]
    d_x = w_xattn_out.shape[1]
    n_heads = d_r // HEAD_DIM
    o1 = d_shift
    o2 = o1 + 2 * d_conv
    o3 = o2 + d_x
    d_ff = ffn1_w_down.shape[1]
    row = lambda a: a.astype(F32).reshape(1, -1)
    scaled = lambda g, w: (g.astype(F32)[:, None] * w.astype(F32)).astype(BF16)
    final_g = row(final_norm)

    layers = []
    for l in range(depth):
        layers.append({
            'ffn1_w_gate': scaled(ffn1_norm[l], ffn1_w_up[l][:, :d_ff]),
            'ffn1_w_up': scaled(ffn1_norm[l], ffn1_w_up[l][:, d_ff:]),
            'ffn1_w_down': ffn1_w_down[l].astype(BF16), 'mix_norm': row(mix_norm[l]),
            'w_s': w_in[l, :, :o1].astype(BF16), 'w_c': w_in[l, :, o1:o2].astype(BF16),
            'w_q': scaled(mix_norm[l], w_in[l, :, o2:o3]), 'w_g': scaled(mix_norm[l], w_in[l, :, o3:]),
            'mu_shift': row(mu_shift[l]), 'w0': row(w0[l]), 'w_decay_up': w_decay_up[l].astype(F32),
            'a0': row(a0[l]), 'w_a_up': w_a_up[l].astype(F32), 'w_g_up': w_g_up[l].astype(F32),
            'k_k': row(k_k[l]), 'k_a': row(k_a[l]), 'r_k': row(r_k[l]), 'gn_g': row(gn_g[l]),
            'gn_b': row(gn_b[l]), 'w_rwkv_out': w_rwkv_out[l].astype(BF16), 'glu_b': row(glu_b[l]),
            'conv_w': conv_w[l].astype(F32), 'conv_b': row(conv_b[l]), 'conv_ln_g': row(conv_ln_g[l]),
            'conv_ln_b': row(conv_ln_b[l]), 'w_conv_out': w_conv_out[l].astype(BF16),
            'w_mem_kv_t': w_mem_kv[l].T.astype(BF16), 'w_xattn_out': w_xattn_out[l].astype(BF16),
            'w_o': w_o[l].astype(BF16),
            'ffn2_w_gate': scaled(ffn2_norm[l], ffn2_w_up[l][:, :d_ff]),
            'ffn2_w_up': scaled(ffn2_norm[l], ffn2_w_up[l][:, d_ff:]), 'ffn2_w_down': ffn2_w_down[l].astype(BF16),
        })

    bp, tp, _ = x_prompt.shape
    bs, ts, _ = x_sample.shape
    n_mem = mem_prompt.shape[1]
    xp = x_prompt.astype(F32).reshape(bp * tp, d_model)
    xs = x_sample.astype(F32).reshape(bs * ts, d_model)
    wkv_p, shift_p, conv_p, mk_p, mv_p = [], [], [], [], []
    wkv_s, shift_s, conv_s = [], [], []
    for l in range(depth):
        p = layers[l]
        x1p, x1s = _ffn(xp, xs, p['ffn1_w_gate'], p['ffn1_w_up'], p['ffn1_w_down'], final_g, final_norm=False)

        mkt, mvt = _memkv(mem_prompt.astype(F32), p['w_mem_kv_t'])
        x2p, wkv, sh, cv = _mix(
            x1p.reshape(bp, tp, d_model), jnp.zeros((bp, d_shift), F32),
            jnp.zeros((bp, CONV_WIDTH - 1, d_conv), F32), jnp.zeros((bp, n_heads, HEAD_DIM, HEAD_DIM), F32),
            mkt, mvt, p, bb=1, tt=512, rw=dict(chunk=64, unroll1=4, unroll2=1), mg=dict(tm=512, rows=512, group=1))
        wkv_p.append(wkv)
        shift_p.append(sh)
        conv_p.append(cv)
        heads_t = lambda a: jnp.transpose(a.reshape(bp, N_XATTN_HEADS, d_x // N_XATTN_HEADS, n_mem), (0, 3, 1, 2))
        mk_p.append(heads_t(mkt))
        mv_p.append(heads_t(mvt))

        kt = jnp.transpose(cache_mem_k[l].astype(F32), (0, 2, 3, 1)).reshape(bs, d_x, n_mem)
        vt = jnp.transpose(cache_mem_v[l].astype(F32), (0, 2, 3, 1)).reshape(bs, d_x, n_mem)
        x2s, wkv, sh, cv = _mix(
            x1s.reshape(bs, ts, d_model), state_shift[l].astype(F32), state_conv[l].astype(F32),
            state_wkv[l].astype(F32), kt, vt, p, bb=16, tt=ts, rw=dict(chunk=ts, unroll1=4, unroll2=4),
            mg=dict(tm=16 * ts, rows=ts, group=4))
        wkv_s.append(wkv)
        shift_s.append(sh)
        conv_s.append(cv)

        xp, xs = _ffn(x2p, x2s, p['ffn2_w_gate'], p['ffn2_w_up'], p['ffn2_w_down'], final_g,
                      final_norm=(l == depth - 1))

    return (xp.reshape(x_prompt.shape).astype(x_prompt.dtype), xs.reshape(x_sample.shape).astype(x_sample.dtype),
            jnp.stack(wkv_p), jnp.stack(shift_p), jnp.stack(conv_p), jnp.stack(mk_p), jnp.stack(mv_p),
            jnp.stack(wkv_s), jnp.stack(shift_s), jnp.stack(conv_s))
```

```python
import functools
import math

import jax
import jax.numpy as jnp
from jax import lax
from jax.experimental import pallas as pl
from jax.experimental.pallas import tpu as pltpu

F32 = jnp.float32
BF16 = jnp.bfloat16
HI = lax.Precision.HIGHEST

HEAD_DIM = 64
PAIR = 2 * HEAD_DIM
N_XATTN_HEADS = 4
CONV_WIDTH = 31
SUBLANES = 8
CONV_PAD = 32
LORA_DECAY = 64
LORA_A = 64
RMS_EPS = 1e-6
LN_EPS = 1e-5
GN_EPS = 64e-5
VMEM_LIMIT = 56 * 1024 * 1024


def _params(*sem):
    return pltpu.CompilerParams(dimension_semantics=sem, vmem_limit_bytes=VMEM_LIMIT)


def _rms(x, g):
    return x * lax.rsqrt(jnp.mean(x * x, axis=-1, keepdims=True) + RMS_EPS) * g


def _mm(a, b, precision=None):
    return jnp.dot(a, b, preferred_element_type=F32, precision=precision)


def _nt(a, b):
    return lax.dot_general(a, b, (((1,), (1,)), ((), ())), preferred_element_type=F32)


def _tn(a, b):
    return lax.dot_general(a, b, (((0,), (0,)), ((), ())), preferred_element_type=F32)


def _split(x):
    hi = x.astype(BF16)
    return hi, (x - hi.astype(F32)).astype(BF16)


def _exact_lhs_dot(sel, x):
    hi, lo = _split(x)
    return _mm(sel, hi) + _mm(sel, lo)


def _dot3(x, w):
    x_hi, x_lo = _split(x)
    w_hi, w_lo = _split(w)
    return _mm(x_hi, w_hi) + _mm(x_lo, w_hi) + _mm(x_hi, w_lo)


def _dot1(x, w):
    return _mm(x.astype(BF16), w.astype(BF16))


def _const_spec(shape):
    nd = len(shape)
    return pl.BlockSpec(shape, lambda *_: (0,) * nd)


def _ffn_body(xa_ref, xb_ref, wg_ref, wu_ref, wd_ref, fg_ref, oa_ref, ob_ref, *, final_norm, tiles_a):
    def tile(x_ref, o_ref):
        x = x_ref[...]
        xb = x.astype(BF16)
        inv = lax.rsqrt(jnp.mean(x * x, axis=-1, keepdims=True) + RMS_EPS)
        hg = _mm(xb, wg_ref[...]) * inv
        hu = _mm(xb, wu_ref[...]) * inv
        h = (hg * jax.nn.sigmoid(hg) * hu).astype(BF16)
        y = x + 0.5 * _mm(h, wd_ref[...])
        if final_norm:
            y = _rms(y, fg_ref[...])
        o_ref[...] = y

    in_a = pl.program_id(0) < tiles_a
    pl.when(in_a)(functools.partial(tile, xa_ref, oa_ref))
    pl.when(jnp.logical_not(in_a))(functools.partial(tile, xb_ref, ob_ref))


def _resident_spec(shape):
    nd = len(shape)
    return pl.BlockSpec(shape, lambda *_: (0,) * nd, pipeline_mode=pl.Buffered(1))


def _ffn(xa, xb, w_gate, w_up, w_down, final_g, *, final_norm, tm=512):
    (na, d), nb = xa.shape, xb.shape[0]
    tm = min(tm, na, nb)
    assert na % tm == 0 and nb % tm == 0
    ta, tb = na // tm, nb // tm
    spec_a = pl.BlockSpec((tm, d), lambda i: (jnp.minimum(i, ta - 1), 0))
    spec_b = pl.BlockSpec((tm, d), lambda i: (jnp.maximum(i - ta, 0), 0))
    return pl.pallas_call(
        functools.partial(_ffn_body, final_norm=final_norm, tiles_a=ta),
        grid=(ta + tb,),
        in_specs=[
            spec_a, spec_b,
            _resident_spec(w_gate.shape), _resident_spec(w_up.shape), _resident_spec(w_down.shape),
            _const_spec((1, d)),
        ],
        out_specs=[spec_a, spec_b],
        out_shape=[jax.ShapeDtypeStruct((na, d), F32), jax.ShapeDtypeStruct((nb, d), F32)],
        compiler_params=_params("arbitrary"),
        name="ffn_final" if final_norm else "ffn",
    )(xa, xb, w_gate, w_up, w_down, final_g)


def _memkv_body(m_ref, wt_ref, kt_ref, vt_ref):
    dk = kt_ref.shape[1]
    kvt = _nt(wt_ref[...], m_ref[0].astype(BF16))
    kt_ref[0] = kvt[:dk]
    vt_ref[0] = kvt[dk:]


def _memkv(mem, wt):
    b, n_mem, d = mem.shape
    dk = wt.shape[0] // 2
    return pl.pallas_call(
        _memkv_body,
        grid=(b,),
        in_specs=[pl.BlockSpec((1, n_mem, d), lambda i: (i, 0, 0)), _const_spec(wt.shape)],
        out_specs=[pl.BlockSpec((1, dk, n_mem), lambda i: (i, 0, 0))] * 2,
        out_shape=[jax.ShapeDtypeStruct((b, dk, n_mem), F32)] * 2,
        compiler_params=_params("parallel"),
        name="memkv",
    )(mem, wt)


def _rwkv_body(x_ref, mixg_ref, ws_ref, sprev_ref, mu_ref, w0_ref, wdu_ref, a0_ref, wau_ref, wgu_ref,
               kk_ref, ka_ref, rk_ref, gng_ref, gnb_ref, s0_ref, hsum_ref, tri_ref, blk_ref,
               wc_ref, glub_ref, cprev_ref, cw_ref, cb_ref, lng_ref, lnb_ref,
               og_ref, sout_ref, wkv_ref, yc_ref, cnew_ref,
               carry, st, at_s, rt_s, bb_s, kb_s, bh_s, kh_s, v_s, wt_s, bonus_s, g_s, o_s,
               gm_s, hm_s, rp_s, oi_s, ubuf, ushift, yc_s, *, chunk, unroll1):
    bb_n, tt, d = x_ref.shape
    m = bb_n * tt
    d_shift = ws_ref.shape[1]
    d_r = w0_ref.shape[1]
    dc = cw_ref.shape[1]
    n_pairs = d_r // PAIR
    n_ck = tt // chunk
    n_blk = m // chunk
    log_chunk = int(math.log2(chunk))
    c2 = 2 * chunk
    conv_lo = CONV_PAD - (CONV_WIDTH - 1)
    t = pl.program_id(1)

    @pl.when(t == 0)
    def _():
        carry[...] = sprev_ref[...]
        ubuf[:, conv_lo:CONV_PAD, :] = cprev_ref[...]
        zero = jnp.zeros((HEAD_DIM, HEAD_DIM), F32)
        for b_i in range(bb_n):
            for p in range(n_pairs):
                top = jnp.concatenate([s0_ref[b_i, 2 * p], zero], axis=1)
                bot = jnp.concatenate([zero, s0_ref[b_i, 2 * p + 1]], axis=1)
                st[b_i, p] = jnp.concatenate([top, bot], axis=0)

    n_part = 2 if (bb_n == 1 and m >= 4 * chunk) else 1
    pm_ = m // n_part
    parts = range(n_part)
    psl = [slice(i * pm_, (i + 1) * pm_) for i in parts]
    x2d = x_ref[...].reshape(m, d)
    xn = [_rms(x2d[psl[i]], mixg_ref[...]).astype(BF16) for i in parts]
    zc = [_mm(xn[i], wc_ref[...]) + glub_ref[...] for i in parts]
    zs = [_mm(xn[i], ws_ref[...]) for i in parts]
    u = [zc[i][:, :dc] * jax.nn.sigmoid(zc[i][:, dc:]) for i in parts]
    if n_part == 1:
        ubuf[:, CONV_PAD:CONV_PAD + tt, :] = u[0].reshape(bb_n, tt, dc)
    else:
        for i in parts:
            ubuf[0, CONV_PAD + i * pm_:CONV_PAD + (i + 1) * pm_, :] = u[i]
    for sh in range(1, SUBLANES):
        ushift[sh - 1] = ubuf[:, sh:sh + tt + CONV_PAD - SUBLANES, :]

    if n_part == 1:
        prev0 = jnp.broadcast_to(carry[...], (bb_n, tt, d_shift)).reshape(m, d_shift)
        row = lax.broadcasted_iota(jnp.int32, (m, d_shift), 0)
        prev = [jnp.where((row & (tt - 1)) == 0, prev0, pltpu.roll(zs[0], 1, axis=0))]
        last = zs[0].reshape(bb_n, tt, d_shift)[:, tt - 1:tt, :]
    else:
        row = lax.broadcasted_iota(jnp.int32, (pm_, d_shift), 0)
        first = [carry[0]] + [zs[i][pm_ - 1:pm_, :] for i in parts[:-1]]
        prev = [jnp.where(row == 0, first[i], pltpu.roll(zs[i], 1, axis=0)) for i in parts]
        last = zs[-1][pm_ - 1:pm_, :].reshape(1, 1, d_shift)
    carry[...] = last
    sout_ref[...] = last
    xm = [zs[i] + (prev[i] - zs[i]) * mu_ref[...] for i in parts]

    c1, c2_, c3 = d_r, 2 * d_r, 3 * d_r
    c4 = c3 + LORA_DECAY
    c5 = c4 + LORA_A
    r = [xm[i][:, :c1] for i in parts]
    k = [xm[i][:, c1:c2_] for i in parts]
    v = [xm[i][:, c2_:c3] for i in parts]
    dec_in = [w0_ref[...] + _dot3(jnp.tanh(xm[i][:, c3:c4]), wdu_ref[...]) for i in parts]
    a_in = [a0_ref[...] + _dot1(xm[i][:, c4:c5], wau_ref[...]) for i in parts]
    for i in parts:
        g_s[psl[i], :] = _dot1(jax.nn.sigmoid(xm[i][:, c5:]), wgu_ref[...])
    ld = [-jnp.exp(-(jnp.maximum(-dec_in[i], 0.0) + jnp.log(1.0 + jnp.exp(-jnp.abs(dec_in[i])))) - 0.5)
          for i in parts]
    a = [jax.nn.sigmoid(a_in[i]) for i in parts]
    kh = [k[i] * (1.0 + (a[i] - 1.0) * ka_ref[...]) for i in parts]

    head_sum = hsum_ref[...]
    kkraw = [k[i] * kk_ref[...] for i in parts]
    kk_ss = [_mm((kkraw[i] * kkraw[i]).astype(BF16), head_sum) for i in parts]
    rk_sum = [_mm((r[i] * kh[i] * rk_ref[...]).astype(BF16), head_sum) for i in parts]
    kk = [kkraw[i] / jnp.maximum(jnp.sqrt(kk_ss[i]), 1e-12) for i in parts]
    b = [kk[i] * a[i] for i in parts]
    tdt = at_s.dtype
    for i in parts:
        bonus_s[psl[i], :] = rk_sum[i] * v[i]
        v_s[psl[i], :] = v[i].astype(tdt)

    slab = tri_ref.shape[0]
    for s0 in range(0, pm_, slab):
        sl = slice(s0, s0 + slab)
        ld_c = [ld[i][sl] for i in parts]
        cum = [_exact_lhs_dot(tri_ref[...], ld_c[i]) for i in parts]
        tot = [_exact_lhs_dot(blk_ref[...], ld_c[i]) for i in parts]
        for i in parts:
            gl = slice(i * pm_ + s0, i * pm_ + s0 + slab)
            w_inv = jnp.exp(-cum[i])
            w_rem = jnp.exp(tot[i] - cum[i])
            at_s[gl, :] = (-kk[i][sl] * jnp.exp(cum[i] - ld_c[i])).astype(tdt)
            rt_s[gl, :] = (r[i][sl] * jnp.exp(cum[i])).astype(tdt)
            bb_s[gl, :] = (b[i][sl] * w_inv).astype(tdt)
            kb_s[gl, :] = (kh[i][sl] * w_inv).astype(tdt)
            bh_s[gl, :] = (b[i][sl] * w_rem).astype(tdt)
            kh_s[gl, :] = (kh[i][sl] * w_rem).astype(tdt)
            wt_s[gl, :] = jnp.exp(tot[i])

    lane_lo = lax.broadcasted_iota(jnp.int32, (chunk, PAIR), 1) < HEAD_DIM
    r2 = lax.broadcasted_iota(jnp.int32, (c2, c2), 0)
    q2 = lax.broadcasted_iota(jnp.int32, (c2, c2), 1)
    strict = (q2 & (chunk - 1)) < (r2 & (chunk - 1))
    incl = (q2 & (chunk - 1)) <= (r2 & (chunk - 1))
    eye2 = (r2 == q2).astype(F32)
    wide = c2 % PAIR == 0

    def stack(ref, rows, ls):
        x = ref[rows, ls]
        return jnp.concatenate([jnp.where(lane_lo, x, 0.0), jnp.where(lane_lo, 0.0, x)], axis=0).astype(BF16)

    def chain_mats(chains):
        n = range(len(chains))

        def stacks(ref):
            return [stack(ref, pl.ds(pl.multiple_of(blk * chunk, chunk), chunk), slice(p * PAIR, (p + 1) * PAIR))
                    for blk, p in chains]

        at, rt, bbm, kbm = stacks(at_s), stacks(rt_s), stacks(bb_s), stacks(kb_s)
        bhm, khm, vm = stacks(bh_s), stacks(kh_s), stacks(v_s)
        if wide:
            a4 = [_nt(jnp.concatenate([at[c_], rt[c_]], axis=0), jnp.concatenate([bbm[c_], kbm[c_]], axis=0))
                  for c_ in n]
            a_ab = [jnp.where(strict, a4[c_][:c2, :c2], 0.0) for c_ in n]
            a_ak = [jnp.where(strict, a4[c_][:c2, c2:], 0.0).astype(BF16) for c_ in n]
            a_rb = [jnp.where(incl, a4[c_][c2:, :c2], 0.0).astype(BF16) for c_ in n]
            a_rk = [jnp.where(incl, a4[c_][c2:, c2:], 0.0).astype(BF16) for c_ in n]
        else:
            a_ab = [jnp.where(strict, _nt(at[c_], bbm[c_]), 0.0) for c_ in n]
            a_ak = [jnp.where(strict, _nt(at[c_], kbm[c_]), 0.0).astype(BF16) for c_ in n]
            a_rb = [jnp.where(incl, _nt(rt[c_], bbm[c_]), 0.0).astype(BF16) for c_ in n]
            a_rk = [jnp.where(incl, _nt(rt[c_], kbm[c_]), 0.0).astype(BF16) for c_ in n]
        tinv = [eye2 + a_ab[c_] for c_ in n]
        ap = [a_ab[c_].astype(BF16) for c_ in n]
        ap = [_mm(ap[c_], ap[c_]).astype(BF16) for c_ in n]
        for i_sq in range(1, log_chunk):
            if i_sq == log_chunk - 1:
                tinv = [tinv[c_] + _mm(ap[c_], tinv[c_].astype(BF16)) for c_ in n]
            elif wide:
                x = [_mm(ap[c_], jnp.concatenate([ap[c_], tinv[c_].astype(BF16)], axis=1)) for c_ in n]
                ap = [x[c_][:, :c2].astype(BF16) for c_ in n]
                tinv = [tinv[c_] + x[c_][:, c2:] for c_ in n]
            else:
                tinv = [tinv[c_] + _mm(ap[c_], tinv[c_].astype(BF16)) for c_ in n]
                ap = [_mm(ap[c_], ap[c_]).astype(BF16) for c_ in n]
        av = [_mm(a_ak[c_], vm[c_]).astype(BF16) for c_ in n]
        pq = [_mm(tinv[c_].astype(BF16), jnp.concatenate([av[c_], at[c_]], axis=1)) for c_ in n]
        return rt, bhm, khm, vm, a_rb, a_rk, pq

    def phase1_step(i, c):
        chains = [(unroll1 * i + j, p) for j in range(unroll1) for p in range(n_pairs)]
        n = range(len(chains))
        rt, bhm, khm, vm, a_rb, a_rk, pq = chain_mats(chains)
        pm = [pq[c_][:, :PAIR].astype(BF16) for c_ in n]
        qm = [pq[c_][:, PAIR:].astype(BF16) for c_ in n]
        for c_, (blk, p) in enumerate(chains):
            gm_s[blk, p] = _tn(qm[c_], bhm[c_]).astype(BF16)
        for c_, (blk, p) in enumerate(chains):
            hm_s[blk, p] = _tn(jnp.concatenate([pm[c_], vm[c_]], axis=0),
                               jnp.concatenate([bhm[c_], khm[c_]], axis=0))
        rq = [_mm(a_rb[c_], jnp.concatenate([qm[c_], pm[c_]], axis=1)) for c_ in n]
        for c_, (blk, p) in enumerate(chains):
            rp_s[blk, p] = (rt[c_].astype(F32) + rq[c_][:, :PAIR]).astype(BF16)
        for c_, (blk, p) in enumerate(chains):
            oi_s[blk, p] = rq[c_][:, PAIR:] + _mm(a_rk[c_], vm[c_])
        return c

    def conv_block(blk):
        b_i, t0 = blk // n_ck, (blk % n_ck) * chunk
        acc = jnp.zeros((chunk, dc), F32) + cb_ref[...]
        for kx in range(CONV_WIDTH):
            off = conv_lo + kx
            sh, base = off % SUBLANES, off - off % SUBLANES
            rows = pl.ds(pl.multiple_of(t0 + base, SUBLANES), chunk)
            win = ubuf[b_i, rows, :] if sh == 0 else ushift[sh - 1, b_i, rows, :]
            acc = acc + win * cw_ref[kx:kx + 1, :]
        cm = jnp.mean(acc, axis=-1, keepdims=True)
        cc = acc - cm
        cv = jnp.mean(cc * cc, axis=-1, keepdims=True)
        cn = cc * lax.rsqrt(cv + LN_EPS) * lng_ref[...] + lnb_ref[...]
        yc_s[pl.ds(pl.multiple_of(blk * chunk, chunk), chunk), :] = cn * jax.nn.sigmoid(cn)

    def phase2_step(i, c):
        conv_block(i)
        chains = [(i, p) for p in range(n_pairs)]
        n = range(len(chains))
        row0 = [pl.multiple_of(blk * chunk, chunk) for blk, _ in chains]
        ls = [slice(p * PAIR, (p + 1) * PAIR) for _, p in chains]
        s = [st[blk // n_ck, p] for blk, p in chains]
        sb = [s[c_].astype(BF16) for c_ in n]
        s_new = [s[c_] * wt_s[pl.ds(row0[c_], 1), ls[c_]] + _mm(sb[c_], gm_s[blk, p]) + hm_s[blk, p]
                 for c_, (blk, p) in enumerate(chains)]
        o_bd = [oi_s[blk, p] + _nt(rp_s[blk, p], sb[c_]) for c_, (blk, p) in enumerate(chains)]
        for c_, (blk, p) in enumerate(chains):
            st[blk // n_ck, p] = s_new[c_]
        for c_ in n:
            o_s[pl.ds(row0[c_], chunk), ls[c_]] = o_bd[c_][:chunk] + o_bd[c_][chunk:]
        return c

    def single_chunk_step(i, c):
        for j in range(unroll1):
            conv_block(unroll1 * i + j)
        chains = [(unroll1 * i + j, p) for j in range(unroll1) for p in range(n_pairs)]
        n = range(len(chains))
        row0 = [pl.multiple_of(blk * chunk, chunk) for blk, _ in chains]
        ls = [slice(p * PAIR, (p + 1) * PAIR) for _, p in chains]
        rt, bhm, khm, vm, a_rb, a_rk, pq = chain_mats(chains)
        s = [st[blk, p] for blk, p in chains]
        sb = [s[c_].astype(BF16) for c_ in n]
        qr = [_nt(jnp.concatenate([pq[c_][:, PAIR:].astype(BF16), rt[c_]], axis=0), sb[c_]) for c_ in n]
        u = [(pq[c_][:, :PAIR] + qr[c_][:c2]).astype(BF16) for c_ in n]
        o_bd = [qr[c_][c2:] + _mm(a_rb[c_], u[c_]) + _mm(a_rk[c_], vm[c_]) for c_ in n]
        s_new = [s[c_] * wt_s[pl.ds(row0[c_], 1), ls[c_]]
                 + _tn(jnp.concatenate([u[c_], vm[c_]], axis=0), jnp.concatenate([bhm[c_], khm[c_]], axis=0))
                 for c_ in n]
        for c_, (blk, p) in enumerate(chains):
            st[blk, p] = s_new[c_]
        for c_ in n:
            o_s[pl.ds(row0[c_], chunk), ls[c_]] = o_bd[c_][:chunk] + o_bd[c_][chunk:]
        return c

    if n_ck == 1:
        lax.fori_loop(0, n_blk // unroll1, single_chunk_step, 0)
    else:
        lax.fori_loop(0, n_blk // unroll1, phase1_step, 0)
        lax.fori_loop(0, n_blk, phase2_step, 0)

    o = o_s[...]
    mean = _mm(o.astype(BF16), head_sum) * (1.0 / HEAD_DIM)
    oc = o - mean
    var = _mm((oc * oc).astype(BF16), head_sum) * (1.0 / HEAD_DIM)
    o = oc * lax.rsqrt(var + GN_EPS) * gng_ref[...] + gnb_ref[...] + bonus_s[...]
    og_ref[...] = (o * g_s[...]).astype(BF16).reshape(bb_n, tt, d_r)

    yc_ref[...] = yc_s[...].astype(BF16).reshape(bb_n, tt, dc)
    tail = ubuf[:, conv_lo + tt:CONV_PAD + tt, :]
    cnew_ref[...] = tail
    ubuf[:, conv_lo:CONV_PAD, :] = tail

    @pl.when(t == pl.num_programs(1) - 1)
    def _():
        for b_i in range(bb_n):
            for p in range(n_pairs):
                s = st[b_i, p]
                wkv_ref[b_i, 2 * p] = s[:HEAD_DIM, :HEAD_DIM]
                wkv_ref[b_i, 2 * p + 1] = s[HEAD_DIM:, HEAD_DIM:]


def _rwkv(x, mixg, ws, sprev, mu, w0, wdu, a0, wau, wgu, kk, ka, rk, gng, gnb, s0,
          wc, glub, cprev, cw, cb, lng, lnb, *, bb, tt, chunk, unroll1):
    b, t_len, d = x.shape
    d_shift = ws.shape[1]
    d_r = w0.shape[1]
    dc = cw.shape[1]
    n_heads = d_r // HEAD_DIM
    n_pairs = d_r // PAIR
    m = bb * tt
    n_blk = m // chunk
    n_kept = n_blk if tt > chunk else 1
    assert b % bb == 0 and t_len % tt == 0 and tt % chunk == 0
    assert n_blk % unroll1 == 0
    assert chunk & (chunk - 1) == 0 and tt & (tt - 1) == 0 and chunk % 8 == 0
    tok = pltpu.VMEM((m, d_r), F32)
    tok_mm = pltpu.VMEM((m, d_r), BF16 if chunk % (2 * SUBLANES) == 0 else F32)
    lane = jnp.arange(d_r)
    head_sum = (lane[:, None] // HEAD_DIM == lane[None, :] // HEAD_DIM).astype(BF16)
    rows = jnp.arange(chunk if chunk >= HEAD_DIM else m)
    same_chunk = rows[:, None] // chunk == rows[None, :] // chunk
    tri = (same_chunk & (rows[None, :] <= rows[:, None])).astype(BF16)
    blk = same_chunk.astype(BF16)
    return pl.pallas_call(
        functools.partial(_rwkv_body, chunk=chunk, unroll1=unroll1),
        grid=(b // bb, t_len // tt),
        in_specs=[
            pl.BlockSpec((bb, tt, d), lambda i, j: (i, j, 0)),
            _const_spec(mixg.shape), _const_spec(ws.shape),
            pl.BlockSpec((bb, 1, d_shift), lambda i, j: (i, 0, 0)),
            _const_spec(mu.shape), _const_spec(w0.shape), _const_spec(wdu.shape), _const_spec(a0.shape),
            _const_spec(wau.shape), _const_spec(wgu.shape), _const_spec(kk.shape), _const_spec(ka.shape),
            _const_spec(rk.shape), _const_spec(gng.shape), _const_spec(gnb.shape),
            pl.BlockSpec((bb, n_heads, HEAD_DIM, HEAD_DIM), lambda i, j: (i, 0, 0, 0)),
            _const_spec(head_sum.shape), _const_spec(tri.shape), _const_spec(blk.shape),
            _const_spec(wc.shape), _const_spec(glub.shape),
            pl.BlockSpec((bb, CONV_WIDTH - 1, dc), lambda i, j: (i, 0, 0)),
            _const_spec(cw.shape), _const_spec(cb.shape), _const_spec(lng.shape), _const_spec(lnb.shape),
        ],
        out_specs=[
            pl.BlockSpec((bb, tt, d_r), lambda i, j: (i, j, 0)),
            pl.BlockSpec((bb, 1, d_shift), lambda i, j: (i, 0, 0)),
            pl.BlockSpec((bb, n_heads, HEAD_DIM, HEAD_DIM), lambda i, j: (i, 0, 0, 0)),
            pl.BlockSpec((bb, tt, dc), lambda i, j: (i, j, 0)),
            pl.BlockSpec((bb, CONV_WIDTH - 1, dc), lambda i, j: (i, 0, 0)),
        ],
        out_shape=[
            jax.ShapeDtypeStruct((b, t_len, d_r), BF16),
            jax.ShapeDtypeStruct((b, 1, d_shift), F32),
            jax.ShapeDtypeStruct((b, n_heads, HEAD_DIM, HEAD_DIM), F32),
            jax.ShapeDtypeStruct((b, t_len, dc), BF16),
            jax.ShapeDtypeStruct((b, CONV_WIDTH - 1, dc), F32),
        ],
        scratch_shapes=[
            pltpu.VMEM((bb, 1, d_shift), F32),
            pltpu.VMEM((bb, n_pairs, PAIR, PAIR), F32),
        ] + [tok_mm] * 7 + [tok] * 4 + [
            pltpu.VMEM((n_kept, n_pairs, PAIR, PAIR), BF16),
            pltpu.VMEM((n_kept, n_pairs, PAIR, PAIR), F32),
            pltpu.VMEM((n_kept, n_pairs, 2 * chunk, PAIR), BF16),
            pltpu.VMEM((n_kept, n_pairs, 2 * chunk, PAIR), F32),
            pltpu.VMEM((bb, CONV_PAD + tt, dc), F32),
            pltpu.VMEM((SUBLANES - 1, bb, CONV_PAD + tt - SUBLANES, dc), F32),
            pltpu.VMEM((m, dc), F32),
        ],
        compiler_params=_params("parallel", "arbitrary"),
        name="rwkv_conv",
    )(x, mixg, ws, sprev, mu, w0, wdu, a0, wau, wgu, kk, ka, rk, gng, gnb, s0, head_sum, tri, blk,
      wc, glub, cprev, cw, cb, lng, lnb)


def _merge_body(x_ref, wq_ref, wg_ref, kt_ref, vt_ref, ya_ref, yb_ref, wro_ref, wco_ref, wxo_ref, wo_ref, o_ref,
                *, t_len, rows, group):
    tm, d = x_ref.shape
    dh = kt_ref.shape[1] // N_XATTN_HEADS
    x = x_ref[...]
    xb = x.astype(BF16)
    inv = lax.rsqrt(jnp.mean(x * x, axis=-1, keepdims=True) + RMS_EPS)

    q = (_mm(xb, wq_ref[...]) * (inv * dh ** -0.5)).astype(BF16)
    n_blocks = tm // rows
    hs = lambda h: slice(h * dh, (h + 1) * dh)
    seq = lambda r: (r * rows) // t_len
    pieces = []
    for g0 in range(0, n_blocks, group):
        blocks = range(g0, min(g0 + group, n_blocks))
        chains = [(r, h) for r in blocks for h in range(N_XATTN_HEADS)]
        s = [_mm(q[r * rows:(r + 1) * rows, hs(h)], kt_ref[seq(r), hs(h), :].astype(BF16)) for r, h in chains]
        p = [jnp.exp(v - jnp.max(v, axis=-1, keepdims=True)) for v in s]
        l = [jnp.sum(v, axis=-1, keepdims=True) for v in p]
        o = [_nt(p[i].astype(BF16), vt_ref[seq(r), hs(h), :].astype(BF16)) / l[i] for i, (r, h) in enumerate(chains)]
        for j in range(len(blocks)):
            pieces.append(jnp.concatenate(o[j * N_XATTN_HEADS:(j + 1) * N_XATTN_HEADS], axis=-1))
    oc = jnp.concatenate(pieces, axis=0).astype(BF16)

    merged = jnp.zeros_like(x)
    for i, (y, w_ref) in enumerate(((ya_ref[...], wro_ref), (yb_ref[...], wco_ref), (oc, wxo_ref))):
        gate = jax.nn.sigmoid(_mm(xb, wg_ref[:, i * d:(i + 1) * d]) * inv)
        merged = merged + gate * _mm(y, w_ref[...])
    o_ref[...] = x + _mm(merged.astype(BF16), wo_ref[...])


def _merge(x, wq, wg, kt, vt, ya, yb, wro, wco, wxo, wo, *, t_len, tm, rows, group):
    n, d = x.shape
    dx, n_mem = kt.shape[1:]
    seqs = max(tm // t_len, 1)
    assert n % tm == 0 and tm % rows == 0 and (tm % t_len == 0 or t_len % tm == 0) and rows <= t_len
    row = lambda w: pl.BlockSpec((tm, w), lambda i: (i, 0))
    mem = pl.BlockSpec((seqs, dx, n_mem), lambda i: ((i * tm) // (t_len * seqs), 0, 0))
    return pl.pallas_call(
        functools.partial(_merge_body, t_len=t_len, rows=rows, group=group),
        grid=(n // tm,),
        in_specs=[row(d), _resident_spec(wq.shape), _resident_spec(wg.shape), mem, mem, row(ya.shape[1]),
                  row(yb.shape[1]), _resident_spec(wro.shape), _resident_spec(wco.shape),
                  _resident_spec(wxo.shape), _resident_spec(wo.shape)],
        out_specs=row(d),
        out_shape=jax.ShapeDtypeStruct((n, d), F32),
        compiler_params=_params("parallel"),
        name="merge",
    )(x, wq, wg, kt, vt, ya, yb, wro, wco, wxo, wo)


def _mix(x1, shift_prev, conv_prev, wkv0, mem_kt, mem_vt, p, *, bb, tt, rw, mg):
    b, t_len, d = x1.shape
    n = b * t_len
    og, shift, wkv, cb, conv = _rwkv(
        x1, p['mix_norm'], p['w_s'], shift_prev[:, None, :], p['mu_shift'], p['w0'], p['w_decay_up'], p['a0'],
        p['w_a_up'], p['w_g_up'], p['k_k'], p['k_a'], p['r_k'], p['gn_g'], p['gn_b'], wkv0,
        p['w_c'], p['glu_b'], conv_prev, p['conv_w'], p['conv_b'], p['conv_ln_g'], p['conv_ln_b'],
        bb=bb, tt=tt, **rw)
    x2 = _merge(x1.reshape(n, d), p['w_q'], p['w_g'], mem_kt, mem_vt, og.reshape(n, -1), cb.reshape(n, -1),
                p['w_rwkv_out'], p['w_conv_out'], p['w_xattn_out'], p['w_o'], t_len=t_len, **mg)
    return x2, wkv, shift[:, 0, :], conv


def kernel(x_prompt, mem_prompt, x_sample, state_wkv, state_shift, state_conv, cache_mem_k, cache_mem_v,
           ffn1_norm, ffn1_w_up, ffn1_w_down, mix_norm, w_in, mu_shift, w0, w_decay_up, a0, w_a_up, w_g_up,
           k_k, k_a, r_k, gn_g, gn_b, w_rwkv_out, glu_b, conv_w, conv_b, conv_ln_g, conv_ln_b, w_conv_out,
           w_mem_kv, w_xattn_out, w_o, ffn2_norm, ffn2_w_up, ffn2_w_down, final_norm):
    depth = w_in.shape[0]
    d_model = w_in.shape[1]
    d_r = w0.shape[1]
    d_shift = mu_shift.shape[1]
    d_conv = conv_w.shape[2]
    d_x = w_xattn_out.shape[1]
    n_heads = d_r // HEAD_DIM
    o1 = d_shift
    o2 = o1 + 2 * d_conv
    o3 = o2 + d_x
    d_ff = ffn1_w_down.shape[1]
    row = lambda a: a.astype(F32).reshape(1, -1)
    scaled = lambda g, w: (g.astype(F32)[:, None] * w.astype(F32)).astype(BF16)
    final_g = row(final_norm)

    layers = []
    for l in range(depth):
        layers.append({
            'ffn1_w_gate': scaled(ffn1_norm[l], ffn1_w_up[l][:, :d_ff]),
            'ffn1_w_up': scaled(ffn1_norm[l], ffn1_w_up[l][:, d_ff:]),
            'ffn1_w_down': ffn1_w_down[l].astype(BF16), 'mix_norm': row(mix_norm[l]),
            'w_s': w_in[l, :, :o1].astype(BF16), 'w_c': w_in[l, :, o1:o2].astype(BF16),
            'w_q': scaled(mix_norm[l], w_in[l, :, o2:o3]), 'w_g': scaled(mix_norm[l], w_in[l, :, o3:]),
            'mu_shift': row(mu_shift[l]), 'w0': row(w0[l]), 'w_decay_up': w_decay_up[l].astype(F32),
            'a0': row(a0[l]), 'w_a_up': w_a_up[l].astype(F32), 'w_g_up': w_g_up[l].astype(F32),
            'k_k': row(k_k[l]), 'k_a': row(k_a[l]), 'r_k': row(r_k[l]), 'gn_g': row(gn_g[l]),
            'gn_b': row(gn_b[l]), 'w_rwkv_out': w_rwkv_out[l].astype(BF16), 'glu_b': row(glu_b[l]),
            'conv_w': conv_w[l].astype(F32), 'conv_b': row(conv_b[l]), 'conv_ln_g': row(conv_ln_g[l]),
            'conv_ln_b': row(conv_ln_b[l]), 'w_conv_out': w_conv_out[l].astype(BF16),
            'w_mem_kv_t': w_mem_kv[l].T.astype(BF16), 'w_xattn_out': w_xattn_out[l].astype(BF16),
            'w_o': w_o[l].astype(BF16),
            'ffn2_w_gate': scaled(ffn2_norm[l], ffn2_w_up[l][:, :d_ff]),
            'ffn2_w_up': scaled(ffn2_norm[l], ffn2_w_up[l][:, d_ff:]), 'ffn2_w_down': ffn2_w_down[l].astype(BF16),
        })

    bp, tp, _ = x_prompt.shape
    bs, ts, _ = x_sample.shape
    n_mem = mem_prompt.shape[1]
    xp = x_prompt.astype(F32).reshape(bp * tp, d_model)
    xs = x_sample.astype(F32).reshape(bs * ts, d_model)
    wkv_p, shift_p, conv_p, mk_p, mv_p = [], [], [], [], []
    wkv_s, shift_s, conv_s = [], [], []
    for l in range(depth):
        p = layers[l]
        x1p, x1s = _ffn(xp, xs, p['ffn1_w_gate'], p['ffn1_w_up'], p['ffn1_w_down'], final_g, final_norm=False)

        mkt, mvt = _memkv(mem_prompt.astype(F32), p['w_mem_kv_t'])
        x2p, wkv, sh, cv = _mix(
            x1p.reshape(bp, tp, d_model), jnp.zeros((bp, d_shift), F32),
            jnp.zeros((bp, CONV_WIDTH - 1, d_conv), F32), jnp.zeros((bp, n_heads, HEAD_DIM, HEAD_DIM), F32),
            mkt, mvt, p, bb=1, tt=512, rw=dict(chunk=64, unroll1=4), mg=dict(tm=512, rows=512, group=1))
        wkv_p.append(wkv)
        shift_p.append(sh)
        conv_p.append(cv)
        heads_t = lambda a: jnp.transpose(a.reshape(bp, N_XATTN_HEADS, d_x // N_XATTN_HEADS, n_mem), (0, 3, 1, 2))
        mk_p.append(heads_t(mkt))
        mv_p.append(heads_t(mvt))

        kt = jnp.transpose(cache_mem_k[l].astype(F32), (0, 2, 3, 1)).reshape(bs, d_x, n_mem)
        vt = jnp.transpose(cache_mem_v[l].astype(F32), (0, 2, 3, 1)).reshape(bs, d_x, n_mem)
        x2s, wkv, sh, cv = _mix(
            x1s.reshape(bs, ts, d_model), state_shift[l].astype(F32), state_conv[l].astype(F32),
            state_wkv[l].astype(F32), kt, vt, p, bb=16, tt=ts, rw=dict(chunk=ts, unroll1=4),
            mg=dict(tm=16 * ts, rows=ts, group=4))
        wkv_s.append(wkv)
        shift_s.append(sh)
        conv_s.append(cv)

        xp, xs = _ffn(x2p, x2s, p['ffn2_w_gate'], p['ffn2_w_up'], p['ffn2_w_down'], final_g,
                      final_norm=(l == depth - 1))

    return (xp.reshape(x_prompt.shape).astype(x_prompt.dtype), xs.reshape(x_sample.shape).astype(x_sample.dtype),
            jnp.stack(wkv_p), jnp.stack(shift_p), jnp.stack(conv_p), jnp.stack(mk_p), jnp.stack(mv_p),
            jnp.stack(wkv_s), jnp.stack(shift_s), jnp.stack(conv_s))
```

```python
import functools
import math

import jax
import jax.numpy as jnp
from jax import lax
from jax.experimental import pallas as pl
from jax.experimental.pallas import tpu as pltpu

F32 = jnp.float32
BF16 = jnp.bfloat16

HEAD_DIM = 64
PAIR = 2 * HEAD_DIM
N_XATTN_HEADS = 4
CONV_WIDTH = 31
SUBLANES = 8
CONV_PAD = 32
LORA_DECAY = 64
LORA_A = 64
RMS_EPS = 1e-6
LN_EPS = 1e-5
GN_EPS = 64e-5
VMEM_LIMIT = 56 * 1024 * 1024


def _params(*sem):
    return pltpu.CompilerParams(dimension_semantics=sem, vmem_limit_bytes=VMEM_LIMIT)


def _rms(x, g):
    return x * lax.rsqrt(jnp.mean(x * x, axis=-1, keepdims=True) + RMS_EPS) * g


def _mm(a, b):
    return jnp.dot(a, b, preferred_element_type=F32)


def _nt(a, b):
    return lax.dot_general(a, b, (((1,), (1,)), ((), ())), preferred_element_type=F32)


def _tn(a, b):
    return lax.dot_general(a, b, (((0,), (0,)), ((), ())), preferred_element_type=F32)


def _split(x):
    hi = x.astype(BF16)
    return hi, (x - hi.astype(F32)).astype(BF16)


def _exact_lhs_dot(sel, x):
    hi, lo = _split(x)
    return _mm(sel, hi) + _mm(sel, lo)


def _dot3(x, w):
    x_hi, x_lo = _split(x)
    w_hi, w_lo = _split(w)
    return _mm(x_hi, w_hi) + _mm(x_lo, w_hi) + _mm(x_hi, w_lo)


def _dot1(x, w):
    return _mm(x.astype(BF16), w.astype(BF16))


def _const_spec(shape):
    nd = len(shape)
    return pl.BlockSpec(shape, lambda *_: (0,) * nd)


def _ffn_body(xa_ref, xb_ref, wg_ref, wu_ref, wd_ref, fg_ref, oa_ref, ob_ref, *, final_norm, tiles_a):
    def tile(x_ref, o_ref):
        x = x_ref[...]
        xb = x.astype(BF16)
        inv = lax.rsqrt(jnp.mean(x * x, axis=-1, keepdims=True) + RMS_EPS)
        hg = _mm(xb, wg_ref[...]) * inv
        hu = _mm(xb, wu_ref[...]) * inv
        h = (hg * jax.nn.sigmoid(hg) * hu).astype(BF16)
        y = x + 0.5 * _mm(h, wd_ref[...])
        if final_norm:
            y = _rms(y, fg_ref[...])
        o_ref[...] = y

    in_a = pl.program_id(0) < tiles_a
    pl.when(in_a)(functools.partial(tile, xa_ref, oa_ref))
    pl.when(jnp.logical_not(in_a))(functools.partial(tile, xb_ref, ob_ref))


def _resident_spec(shape):
    nd = len(shape)
    return pl.BlockSpec(shape, lambda *_: (0,) * nd, pipeline_mode=pl.Buffered(1))


def _ffn(xa, xb, w_gate, w_up, w_down, final_g, *, final_norm, tm=512):
    (na, d), nb = xa.shape, xb.shape[0]
    tm = min(tm, na, nb)
    assert na % tm == 0 and nb % tm == 0
    ta, tb = na // tm, nb // tm
    spec_a = pl.BlockSpec((tm, d), lambda i: (jnp.minimum(i, ta - 1), 0))
    spec_b = pl.BlockSpec((tm, d), lambda i: (jnp.maximum(i - ta, 0), 0))
    return pl.pallas_call(
        functools.partial(_ffn_body, final_norm=final_norm, tiles_a=ta),
        grid=(ta + tb,),
        in_specs=[
            spec_a, spec_b,
            _resident_spec(w_gate.shape), _resident_spec(w_up.shape), _resident_spec(w_down.shape),
            _const_spec((1, d)),
        ],
        out_specs=[spec_a, spec_b],
        out_shape=[jax.ShapeDtypeStruct((na, d), F32), jax.ShapeDtypeStruct((nb, d), F32)],
        compiler_params=_params("arbitrary"),
        name="ffn_final" if final_norm else "ffn",
    )(xa, xb, w_gate, w_up, w_down, final_g)


def _memkv_body(m_ref, wt_ref, kt_ref, vt_ref):
    dk = kt_ref.shape[1]
    kvt = _nt(wt_ref[...], m_ref[0].astype(BF16))
    kt_ref[0] = kvt[:dk]
    vt_ref[0] = kvt[dk:]


def _memkv(mem, wt):
    b, n_mem, d = mem.shape
    dk = wt.shape[0] // 2
    return pl.pallas_call(
        _memkv_body,
        grid=(b,),
        in_specs=[pl.BlockSpec((1, n_mem, d), lambda i: (i, 0, 0)), _const_spec(wt.shape)],
        out_specs=[pl.BlockSpec((1, dk, n_mem), lambda i: (i, 0, 0))] * 2,
        out_shape=[jax.ShapeDtypeStruct((b, dk, n_mem), F32)] * 2,
        compiler_params=_params("parallel"),
        name="memkv",
    )(mem, wt)


def _rwkv_body(x_ref, mixg_ref, ws_ref, sprev_ref, mu_ref, w0_ref, wdu_ref, a0_ref, wau_ref, wgu_ref,
               kk_ref, ka_ref, rk_ref, gng_ref, gnb_ref, s0_ref, hsum_ref, tri_ref, blk_ref,
               wc_ref, glub_ref, cprev_ref, cw_ref, cb_ref, lng_ref, lnb_ref,
               og_ref, sout_ref, wkv_ref, yc_ref, cnew_ref,
               carry, st, at_s, rt_s, bb_s, kb_s, bh_s, kh_s, v_s, wt_s, bonus_s, g_s, o_s,
               gm_s, hm_s, rp_s, oi_s, ubuf, ushift, yc_s, *, chunk, unroll1):
    bb_n, tt, d = x_ref.shape
    m = bb_n * tt
    d_shift = ws_ref.shape[1]
    d_r = w0_ref.shape[1]
    dc = cw_ref.shape[1]
    n_pairs = d_r // PAIR
    n_ck = tt // chunk
    n_blk = m // chunk
    log_chunk = int(math.log2(chunk))
    c2 = 2 * chunk
    conv_lo = CONV_PAD - (CONV_WIDTH - 1)
    t = pl.program_id(1)

    @pl.when(t == 0)
    def _():
        carry[...] = sprev_ref[...]
        ubuf[:, conv_lo:CONV_PAD, :] = cprev_ref[...]
        zero = jnp.zeros((HEAD_DIM, HEAD_DIM), F32)
        for b_i in range(bb_n):
            for p in range(n_pairs):
                top = jnp.concatenate([s0_ref[b_i, 2 * p], zero], axis=1)
                bot = jnp.concatenate([zero, s0_ref[b_i, 2 * p + 1]], axis=1)
                st[b_i, p] = jnp.concatenate([top, bot], axis=0)

    n_part = 2 if (bb_n == 1 and m >= 4 * chunk) else 1
    pm_ = m // n_part
    parts = range(n_part)
    psl = [slice(i * pm_, (i + 1) * pm_) for i in parts]
    x2d = x_ref[...].reshape(m, d)
    xn = [_rms(x2d[psl[i]], mixg_ref[...]).astype(BF16) for i in parts]
    zc = [_mm(xn[i], wc_ref[...]) + glub_ref[...] for i in parts]
    zs = [_mm(xn[i], ws_ref[...]) for i in parts]
    u = [zc[i][:, :dc] * jax.nn.sigmoid(zc[i][:, dc:]) for i in parts]
    if n_part == 1:
        ubuf[:, CONV_PAD:CONV_PAD + tt, :] = u[0].reshape(bb_n, tt, dc)
    else:
        for i in parts:
            ubuf[0, CONV_PAD + i * pm_:CONV_PAD + (i + 1) * pm_, :] = u[i]
    for sh in range(1, SUBLANES):
        ushift[sh - 1] = ubuf[:, sh:sh + tt + CONV_PAD - SUBLANES, :]

    if n_part == 1:
        prev0 = jnp.broadcast_to(carry[...], (bb_n, tt, d_shift)).reshape(m, d_shift)
        row = lax.broadcasted_iota(jnp.int32, (m, d_shift), 0)
        prev = [jnp.where((row & (tt - 1)) == 0, prev0, pltpu.roll(zs[0], 1, axis=0))]
        last = zs[0].reshape(bb_n, tt, d_shift)[:, tt - 1:tt, :]
    else:
        row = lax.broadcasted_iota(jnp.int32, (pm_, d_shift), 0)
        first = [carry[0]] + [zs[i][pm_ - 1:pm_, :] for i in parts[:-1]]
        prev = [jnp.where(row == 0, first[i], pltpu.roll(zs[i], 1, axis=0)) for i in parts]
        last = zs[-1][pm_ - 1:pm_, :].reshape(1, 1, d_shift)
    carry[...] = last
    sout_ref[...] = last
    xm = [zs[i] + (prev[i] - zs[i]) * mu_ref[...] for i in parts]

    c1, c2_, c3 = d_r, 2 * d_r, 3 * d_r
    c4 = c3 + LORA_DECAY
    c5 = c4 + LORA_A
    r = [xm[i][:, :c1] for i in parts]
    k = [xm[i][:, c1:c2_] for i in parts]
    v = [xm[i][:, c2_:c3] for i in parts]
    dec_in = [w0_ref[...] + _dot3(jnp.tanh(xm[i][:, c3:c4]), wdu_ref[...]) for i in parts]
    a_in = [a0_ref[...] + _dot1(xm[i][:, c4:c5], wau_ref[...]) for i in parts]
    for i in parts:
        g_s[psl[i], :] = _dot1(jax.nn.sigmoid(xm[i][:, c5:]), wgu_ref[...])
    ld = [-jnp.exp(-(jnp.maximum(-dec_in[i], 0.0) + jnp.log(1.0 + jnp.exp(-jnp.abs(dec_in[i])))) - 0.5)
          for i in parts]
    a = [jax.nn.sigmoid(a_in[i]) for i in parts]
    kh = [k[i] * (1.0 + (a[i] - 1.0) * ka_ref[...]) for i in parts]

    head_sum = hsum_ref[...]
    kkraw = [k[i] * kk_ref[...] for i in parts]
    kk_ss = [_mm((kkraw[i] * kkraw[i]).astype(BF16), head_sum) for i in parts]
    rk_sum = [_mm((r[i] * kh[i] * rk_ref[...]).astype(BF16), head_sum) for i in parts]
    kk = [kkraw[i] / jnp.maximum(jnp.sqrt(kk_ss[i]), 1e-12) for i in parts]
    b = [kk[i] * a[i] for i in parts]
    tdt = at_s.dtype
    for i in parts:
        bonus_s[psl[i], :] = rk_sum[i] * v[i]
        v_s[psl[i], :] = v[i].astype(tdt)

    slab = tri_ref.shape[0]
    for s0 in range(0, pm_, slab):
        sl = slice(s0, s0 + slab)
        ld_c = [ld[i][sl] for i in parts]
        cum = [_exact_lhs_dot(tri_ref[...], ld_c[i]) for i in parts]
        tot = [_exact_lhs_dot(blk_ref[...], ld_c[i]) for i in parts]
        for i in parts:
            gl = slice(i * pm_ + s0, i * pm_ + s0 + slab)
            w_inv = jnp.exp(-cum[i])
            w_rem = jnp.exp(tot[i] - cum[i])
            at_s[gl, :] = (-kk[i][sl] * jnp.exp(cum[i] - ld_c[i])).astype(tdt)
            rt_s[gl, :] = (r[i][sl] * jnp.exp(cum[i])).astype(tdt)
            bb_s[gl, :] = (b[i][sl] * w_inv).astype(tdt)
            kb_s[gl, :] = (kh[i][sl] * w_inv).astype(tdt)
            bh_s[gl, :] = (b[i][sl] * w_rem).astype(tdt)
            kh_s[gl, :] = (kh[i][sl] * w_rem).astype(tdt)
            wt_s[gl, :] = jnp.exp(tot[i])

    lane_lo = lax.broadcasted_iota(jnp.int32, (chunk, PAIR), 1) < HEAD_DIM
    r2 = lax.broadcasted_iota(jnp.int32, (c2, c2), 0)
    q2 = lax.broadcasted_iota(jnp.int32, (c2, c2), 1)
    strict = (q2 & (chunk - 1)) < (r2 & (chunk - 1))
    incl = (q2 & (chunk - 1)) <= (r2 & (chunk - 1))
    eye2 = (r2 == q2).astype(F32)
    wide = c2 % PAIR == 0

    def stack(ref, rows, ls):
        x = ref[rows, ls]
        return jnp.concatenate([jnp.where(lane_lo, x, 0.0), jnp.where(lane_lo, 0.0, x)], axis=0).astype(BF16)

    def chain_mats(chains):
        n = range(len(chains))

        def stacks(ref):
            return [stack(ref, pl.ds(pl.multiple_of(blk * chunk, chunk), chunk), slice(p * PAIR, (p + 1) * PAIR))
                    for blk, p in chains]

        at, rt, bbm, kbm = stacks(at_s), stacks(rt_s), stacks(bb_s), stacks(kb_s)
        bhm, khm, vm = stacks(bh_s), stacks(kh_s), stacks(v_s)
        if wide:
            a4 = [_nt(jnp.concatenate([at[c_], rt[c_]], axis=0), jnp.concatenate([bbm[c_], kbm[c_]], axis=0))
                  for c_ in n]
            a_ab = [jnp.where(strict, a4[c_][:c2, :c2], 0.0) for c_ in n]
            a_ak = [jnp.where(strict, a4[c_][:c2, c2:], 0.0).astype(BF16) for c_ in n]
            a_rb = [jnp.where(incl, a4[c_][c2:, :c2], 0.0).astype(BF16) for c_ in n]
            a_rk = [jnp.where(incl, a4[c_][c2:, c2:], 0.0).astype(BF16) for c_ in n]
        else:
            a_ab = [jnp.where(strict, _nt(at[c_], bbm[c_]), 0.0) for c_ in n]
            a_ak = [jnp.where(strict, _nt(at[c_], kbm[c_]), 0.0).astype(BF16) for c_ in n]
            a_rb = [jnp.where(incl, _nt(rt[c_], bbm[c_]), 0.0).astype(BF16) for c_ in n]
            a_rk = [jnp.where(incl, _nt(rt[c_], kbm[c_]), 0.0).astype(BF16) for c_ in n]
        tinv = [eye2 + a_ab[c_] for c_ in n]
        ap = [a_ab[c_].astype(BF16) for c_ in n]
        ap = [_mm(ap[c_], ap[c_]).astype(BF16) for c_ in n]
        for i_sq in range(1, log_chunk):
            if i_sq == log_chunk - 1:
                tinv = [tinv[c_] + _mm(ap[c_], tinv[c_].astype(BF16)) for c_ in n]
            elif wide:
                x = [_mm(ap[c_], jnp.concatenate([ap[c_], tinv[c_].astype(BF16)], axis=1)) for c_ in n]
                ap = [x[c_][:, :c2].astype(BF16) for c_ in n]
                tinv = [tinv[c_] + x[c_][:, c2:] for c_ in n]
            else:
                tinv = [tinv[c_] + _mm(ap[c_], tinv[c_].astype(BF16)) for c_ in n]
                ap = [_mm(ap[c_], ap[c_]).astype(BF16) for c_ in n]
        av = [_mm(a_ak[c_], vm[c_]).astype(BF16) for c_ in n]
        pq = [_mm(tinv[c_].astype(BF16), jnp.concatenate([av[c_], at[c_]], axis=1)) for c_ in n]
        return rt, bhm, khm, vm, a_rb, a_rk, pq

    def phase1_step(i, c):
        chains = [(unroll1 * i + j, p) for j in range(unroll1) for p in range(n_pairs)]
        n = range(len(chains))
        rt, bhm, khm, vm, a_rb, a_rk, pq = chain_mats(chains)
        pm = [pq[c_][:, :PAIR].astype(BF16) for c_ in n]
        qm = [pq[c_][:, PAIR:].astype(BF16) for c_ in n]
        for c_, (blk, p) in enumerate(chains):
            gm_s[blk, p] = _tn(qm[c_], bhm[c_]).astype(BF16)
        for c_, (blk, p) in enumerate(chains):
            hm_s[blk, p] = _tn(jnp.concatenate([pm[c_], vm[c_]], axis=0),
                               jnp.concatenate([bhm[c_], khm[c_]], axis=0))
        rq = [_mm(a_rb[c_], jnp.concatenate([qm[c_], pm[c_]], axis=1)) for c_ in n]
        for c_, (blk, p) in enumerate(chains):
            rp_s[blk, p] = (rt[c_].astype(F32) + rq[c_][:, :PAIR]).astype(BF16)
        for c_, (blk, p) in enumerate(chains):
            oi_s[blk, p] = rq[c_][:, PAIR:] + _mm(a_rk[c_], vm[c_])
        return c

    def conv_block(blk):
        b_i, t0 = blk // n_ck, (blk % n_ck) * chunk
        acc = jnp.zeros((chunk, dc), F32) + cb_ref[...]
        for kx in range(CONV_WIDTH):
            off = conv_lo + kx
            sh, base = off % SUBLANES, off - off % SUBLANES
            rows = pl.ds(pl.multiple_of(t0 + base, SUBLANES), chunk)
            win = ubuf[b_i, rows, :] if sh == 0 else ushift[sh - 1, b_i, rows, :]
            acc = acc + win * cw_ref[kx:kx + 1, :]
        cm = jnp.mean(acc, axis=-1, keepdims=True)
        cc = acc - cm
        cv = jnp.mean(cc * cc, axis=-1, keepdims=True)
        cn = cc * lax.rsqrt(cv + LN_EPS) * lng_ref[...] + lnb_ref[...]
        yc_s[pl.ds(pl.multiple_of(blk * chunk, chunk), chunk), :] = cn * jax.nn.sigmoid(cn)

    def phase2_step(i, c):
        conv_block(i)
        chains = [(i, p) for p in range(n_pairs)]
        n = range(len(chains))
        row0 = [pl.multiple_of(blk * chunk, chunk) for blk, _ in chains]
        ls = [slice(p * PAIR, (p + 1) * PAIR) for _, p in chains]
        s = [st[blk // n_ck, p] for blk, p in chains]
        sb = [s[c_].astype(BF16) for c_ in n]
        s_new = [s[c_] * wt_s[pl.ds(row0[c_], 1), ls[c_]] + _mm(sb[c_], gm_s[blk, p]) + hm_s[blk, p]
                 for c_, (blk, p) in enumerate(chains)]
        o_bd = [oi_s[blk, p] + _nt(rp_s[blk, p], sb[c_]) for c_, (blk, p) in enumerate(chains)]
        for c_, (blk, p) in enumerate(chains):
            st[blk // n_ck, p] = s_new[c_]
        for c_ in n:
            o_s[pl.ds(row0[c_], chunk), ls[c_]] = o_bd[c_][:chunk] + o_bd[c_][chunk:]
        return c

    def single_chunk_step(i, c):
        for j in range(unroll1):
            conv_block(unroll1 * i + j)
        chains = [(unroll1 * i + j, p) for j in range(unroll1) for p in range(n_pairs)]
        n = range(len(chains))
        row0 = [pl.multiple_of(blk * chunk, chunk) for blk, _ in chains]
        ls = [slice(p * PAIR, (p + 1) * PAIR) for _, p in chains]
        rt, bhm, khm, vm, a_rb, a_rk, pq = chain_mats(chains)
        s = [st[blk, p] for blk, p in chains]
        sb = [s[c_].astype(BF16) for c_ in n]
        qr = [_nt(jnp.concatenate([pq[c_][:, PAIR:].astype(BF16), rt[c_]], axis=0), sb[c_]) for c_ in n]
        u = [(pq[c_][:, :PAIR] + qr[c_][:c2]).astype(BF16) for c_ in n]
        o_bd = [qr[c_][c2:] + _mm(a_rb[c_], u[c_]) + _mm(a_rk[c_], vm[c_]) for c_ in n]
        s_new = [s[c_] * wt_s[pl.ds(row0[c_], 1), ls[c_]]
                 + _tn(jnp.concatenate([u[c_], vm[c_]], axis=0), jnp.concatenate([bhm[c_], khm[c_]], axis=0))
                 for c_ in n]
        for c_, (blk, p) in enumerate(chains):
            st[blk, p] = s_new[c_]
        for c_ in n:
            o_s[pl.ds(row0[c_], chunk), ls[c_]] = o_bd[c_][:chunk] + o_bd[c_][chunk:]
        return c

    if n_ck == 1:
        lax.fori_loop(0, n_blk // unroll1, single_chunk_step, 0)
    else:
        lax.fori_loop(0, n_blk // unroll1, phase1_step, 0)
        lax.fori_loop(0, n_blk, phase2_step, 0)

    o = o_s[...]
    mean = _mm(o.astype(BF16), head_sum) * (1.0 / HEAD_DIM)
    oc = o - mean
    var = _mm((oc * oc).astype(BF16), head_sum) * (1.0 / HEAD_DIM)
    o = oc * lax.rsqrt(var + GN_EPS) * gng_ref[...] + gnb_ref[...] + bonus_s[...]
    og_ref[...] = (o * g_s[...]).astype(BF16).reshape(bb_n, tt, d_r)

    yc_ref[...] = yc_s[...].astype(BF16).reshape(bb_n, tt, dc)
    tail = ubuf[:, conv_lo + tt:CONV_PAD + tt, :]
    cnew_ref[...] = tail
    ubuf[:, conv_lo:CONV_PAD, :] = tail

    @pl.when(t == pl.num_programs(1) - 1)
    def _():
        for b_i in range(bb_n):
            for p in range(n_pairs):
                s = st[b_i, p]
                wkv_ref[b_i, 2 * p] = s[:HEAD_DIM, :HEAD_DIM]
                wkv_ref[b_i, 2 * p + 1] = s[HEAD_DIM:, HEAD_DIM:]


def _rwkv(x, mixg, ws, sprev, mu, w0, wdu, a0, wau, wgu, kk, ka, rk, gng, gnb, s0,
          wc, glub, cprev, cw, cb, lng, lnb, *, bb, tt, chunk, unroll1):
    b, t_len, d = x.shape
    d_shift = ws.shape[1]
    d_r = w0.shape[1]
    dc = cw.shape[1]
    n_heads = d_r // HEAD_DIM
    n_pairs = d_r // PAIR
    m = bb * tt
    n_blk = m // chunk
    n_kept = n_blk if tt > chunk else 1
    assert b % bb == 0 and t_len % tt == 0 and tt % chunk == 0
    assert n_blk % unroll1 == 0
    assert chunk & (chunk - 1) == 0 and tt & (tt - 1) == 0 and chunk % 8 == 0
    tok = pltpu.VMEM((m, d_r), F32)
    tok_mm = pltpu.VMEM((m, d_r), BF16 if chunk % (2 * SUBLANES) == 0 else F32)
    lane = jnp.arange(d_r)
    head_sum = (lane[:, None] // HEAD_DIM == lane[None, :] // HEAD_DIM).astype(BF16)
    rows = jnp.arange(chunk if chunk >= HEAD_DIM else m)
    same_chunk = rows[:, None] // chunk == rows[None, :] // chunk
    tri = (same_chunk & (rows[None, :] <= rows[:, None])).astype(BF16)
    blk = same_chunk.astype(BF16)
    return pl.pallas_call(
        functools.partial(_rwkv_body, chunk=chunk, unroll1=unroll1),
        grid=(b // bb, t_len // tt),
        in_specs=[
            pl.BlockSpec((bb, tt, d), lambda i, j: (i, j, 0)),
            _const_spec(mixg.shape), _const_spec(ws.shape),
            pl.BlockSpec((bb, 1, d_shift), lambda i, j: (i, 0, 0)),
            _const_spec(mu.shape), _const_spec(w0.shape), _const_spec(wdu.shape), _const_spec(a0.shape),
            _const_spec(wau.shape), _const_spec(wgu.shape), _const_spec(kk.shape), _const_spec(ka.shape),
            _const_spec(rk.shape), _const_spec(gng.shape), _const_spec(gnb.shape),
            pl.BlockSpec((bb, n_heads, HEAD_DIM, HEAD_DIM), lambda i, j: (i, 0, 0, 0)),
            _const_spec(head_sum.shape), _const_spec(tri.shape), _const_spec(blk.shape),
            _const_spec(wc.shape), _const_spec(glub.shape),
            pl.BlockSpec((bb, CONV_WIDTH - 1, dc), lambda i, j: (i, 0, 0)),
            _const_spec(cw.shape), _const_spec(cb.shape), _const_spec(lng.shape), _const_spec(lnb.shape),
        ],
        out_specs=[
            pl.BlockSpec((bb, tt, d_r), lambda i, j: (i, j, 0)),
            pl.BlockSpec((bb, 1, d_shift), lambda i, j: (i, 0, 0)),
            pl.BlockSpec((bb, n_heads, HEAD_DIM, HEAD_DIM), lambda i, j: (i, 0, 0, 0)),
            pl.BlockSpec((bb, tt, dc), lambda i, j: (i, j, 0)),
            pl.BlockSpec((bb, CONV_WIDTH - 1, dc), lambda i, j: (i, 0, 0)),
        ],
        out_shape=[
            jax.ShapeDtypeStruct((b, t_len, d_r), BF16),
            jax.ShapeDtypeStruct((b, 1, d_shift), F32),
            jax.ShapeDtypeStruct((b, n_heads, HEAD_DIM, HEAD_DIM), F32),
            jax.ShapeDtypeStruct((b, t_len, dc), BF16),
            jax.ShapeDtypeStruct((b, CONV_WIDTH - 1, dc), F32),
        ],
        scratch_shapes=[
            pltpu.VMEM((bb, 1, d_shift), F32),
            pltpu.VMEM((bb, n_pairs, PAIR, PAIR), F32),
        ] + [tok_mm] * 7 + [tok] * 4 + [
            pltpu.VMEM((n_kept, n_pairs, PAIR, PAIR), BF16),
            pltpu.VMEM((n_kept, n_pairs, PAIR, PAIR), F32),
            pltpu.VMEM((n_kept, n_pairs, 2 * chunk, PAIR), BF16),
            pltpu.VMEM((n_kept, n_pairs, 2 * chunk, PAIR), F32),
            pltpu.VMEM((bb, CONV_PAD + tt, dc), F32),
            pltpu.VMEM((SUBLANES - 1, bb, CONV_PAD + tt - SUBLANES, dc), F32),
            pltpu.VMEM((m, dc), F32),
        ],
        compiler_params=_params("parallel", "arbitrary"),
        name="rwkv_conv",
    )(x, mixg, ws, sprev, mu, w0, wdu, a0, wau, wgu, kk, ka, rk, gng, gnb, s0, head_sum, tri, blk,
      wc, glub, cprev, cw, cb, lng, lnb)


def _merge_body(x_ref, wq_ref, wg_ref, kt_ref, vt_ref, ya_ref, yb_ref, wro_ref, wco_ref, wxo_ref, wo_ref, o_ref,
                *, t_len, rows, group):
    tm, d = x_ref.shape
    dh = kt_ref.shape[1] // N_XATTN_HEADS
    x = x_ref[...]
    xb = x.astype(BF16)
    inv = lax.rsqrt(jnp.mean(x * x, axis=-1, keepdims=True) + RMS_EPS)

    q = (_mm(xb, wq_ref[...]) * (inv * dh ** -0.5)).astype(BF16)
    n_blocks = tm // rows
    hs = lambda h: slice(h * dh, (h + 1) * dh)
    seq = lambda r: (r * rows) // t_len
    pieces = []
    for g0 in range(0, n_blocks, group):
        blocks = range(g0, min(g0 + group, n_blocks))
        chains = [(r, h) for r in blocks for h in range(N_XATTN_HEADS)]
        s = [_mm(q[r * rows:(r + 1) * rows, hs(h)], kt_ref[seq(r), hs(h), :].astype(BF16)) for r, h in chains]
        p = [jnp.exp(v - jnp.max(v, axis=-1, keepdims=True)) for v in s]
        l = [jnp.sum(v, axis=-1, keepdims=True) for v in p]
        o = [_nt(p[i].astype(BF16), vt_ref[seq(r), hs(h), :].astype(BF16)) / l[i] for i, (r, h) in enumerate(chains)]
        for j in range(len(blocks)):
            pieces.append(jnp.concatenate(o[j * N_XATTN_HEADS:(j + 1) * N_XATTN_HEADS], axis=-1))
    oc = jnp.concatenate(pieces, axis=0).astype(BF16)

    merged = jnp.zeros_like(x)
    for i, (y, w_ref) in enumerate(((ya_ref[...], wro_ref), (yb_ref[...], wco_ref), (oc, wxo_ref))):
        gate = jax.nn.sigmoid(_mm(xb, wg_ref[:, i * d:(i + 1) * d]) * inv)
        merged = merged + gate * _mm(y, w_ref[...])
    o_ref[...] = x + _mm(merged.astype(BF16), wo_ref[...])


def _merge(x, wq, wg, kt, vt, ya, yb, wro, wco, wxo, wo, *, t_len, tm, rows, group):
    n, d = x.shape
    dx, n_mem = kt.shape[1:]
    seqs = max(tm // t_len, 1)
    assert n % tm == 0 and tm % rows == 0 and (tm % t_len == 0 or t_len % tm == 0) and rows <= t_len
    row = lambda w: pl.BlockSpec((tm, w), lambda i: (i, 0))
    mem = pl.BlockSpec((seqs, dx, n_mem), lambda i: ((i * tm) // (t_len * seqs), 0, 0))
    return pl.pallas_call(
        functools.partial(_merge_body, t_len=t_len, rows=rows, group=group),
        grid=(n // tm,),
        in_specs=[row(d), _resident_spec(wq.shape), _resident_spec(wg.shape), mem, mem, row(ya.shape[1]),
                  row(yb.shape[1]), _resident_spec(wro.shape), _resident_spec(wco.shape),
                  _resident_spec(wxo.shape), _resident_spec(wo.shape)],
        out_specs=row(d),
        out_shape=jax.ShapeDtypeStruct((n, d), F32),
        compiler_params=_params("parallel"),
        name="merge",
    )(x, wq, wg, kt, vt, ya, yb, wro, wco, wxo, wo)


def _mix(x1, shift_prev, conv_prev, wkv0, mem_kt, mem_vt, p, *, bb, tt, rw, mg):
    b, t_len, d = x1.shape
    n = b * t_len
    og, shift, wkv, cb, conv = _rwkv(
        x1, p['mix_norm'], p['w_s'], shift_prev[:, None, :], p['mu_shift'], p['w0'], p['w_decay_up'], p['a0'],
        p['w_a_up'], p['w_g_up'], p['k_k'], p['k_a'], p['r_k'], p['gn_g'], p['gn_b'], wkv0,
        p['w_c'], p['glu_b'], conv_prev, p['conv_w'], p['conv_b'], p['conv_ln_g'], p['conv_ln_b'],
        bb=bb, tt=tt, **rw)
    x2 = _merge(x1.reshape(n, d), p['w_q'], p['w_g'], mem_kt, mem_vt, og.reshape(n, -1), cb.reshape(n, -1),
                p['w_rwkv_out'], p['w_conv_out'], p['w_xattn_out'], p['w_o'], t_len=t_len, **mg)
    return x2, wkv, shift[:, 0, :], conv


def kernel(x_prompt, mem_prompt, x_sample, state_wkv, state_shift, state_conv, cache_mem_k, cache_mem_v,
           ffn1_norm, ffn1_w_up, ffn1_w_down, mix_norm, w_in, mu_shift, w0, w_decay_up, a0, w_a_up, w_g_up,
           k_k, k_a, r_k, gn_g, gn_b, w_rwkv_out, glu_b, conv_w, conv_b, conv_ln_g, conv_ln_b, w_conv_out,
           w_mem_kv, w_xattn_out, w_o, ffn2_norm, ffn2_w_up, ffn2_w_down, final_norm):
    depth = w_in.shape[0]
    d_model = w_in.shape[1]
    d_r = w0.shape[1]
    d_shift = mu_shift.shape[1]
    d_conv = conv_w.shape[2]
    d_x = w_xattn_out.shape[1]
    n_heads = d_r // HEAD_DIM
    o1 = d_shift
    o2 = o1 + 2 * d_conv
    o3 = o2 + d_x
    d_ff = ffn1_w_down.shape[1]
    row = lambda a: a.astype(F32).reshape(1, -1)
    scaled = lambda g, w: (g.astype(F32)[:, None] * w.astype(F32)).astype(BF16)
    final_g = row(final_norm)

    layers = []
    for l in range(depth):
        layers.append({
            'ffn1_w_gate': scaled(ffn1_norm[l], ffn1_w_up[l][:, :d_ff]),
            'ffn1_w_up': scaled(ffn1_norm[l], ffn1_w_up[l][:, d_ff:]),
            'ffn1_w_down': ffn1_w_down[l].astype(BF16), 'mix_norm': row(mix_norm[l]),
            'w_s': w_in[l, :, :o1].astype(BF16), 'w_c': w_in[l, :, o1:o2].astype(BF16),
            'w_q': scaled(mix_norm[l], w_in[l, :, o2:o3]), 'w_g': scaled(mix_norm[l], w_in[l, :, o3:]),
            'mu_shift': row(mu_shift[l]), 'w0': row(w0[l]), 'w_decay_up': w_decay_up[l].astype(F32),
            'a0': row(a0[l]), 'w_a_up': w_a_up[l].astype(F32), 'w_g_up': w_g_up[l].astype(F32),
            'k_k': row(k_k[l]), 'k_a': row(k_a[l]), 'r_k': row(r_k[l]), 'gn_g': row(gn_g[l]),
            'gn_b': row(gn_b[l]), 'w_rwkv_out': w_rwkv_out[l].astype(BF16), 'glu_b': row(glu_b[l]),
            'conv_w': conv_w[l].astype(F32), 'conv_b': row(conv_b[l]), 'conv_ln_g': row(conv_ln_g[l]),
            'conv_ln_b': row(conv_ln_b[l]), 'w_conv_out': w_conv_out[l].astype(BF16),
            'w_mem_kv_t': w_mem_kv[l].T.astype(BF16), 'w_xattn_out': w_xattn_out[l].astype(BF16),
            'w_o': w_o[l].astype(BF16),
            'ffn2_w_gate': scaled(ffn2_norm[l], ffn2_w_up[l][:, :d_ff]),
            'ffn2_w_up': scaled(ffn2_norm[l], ffn2_w_up[l][:, d_ff:]), 'ffn2_w_down': ffn2_w_down[l].astype(BF16),
        })

    bp, tp, _ = x_prompt.shape
    bs, ts, _ = x_sample.shape
    n_mem = mem_prompt.shape[1]
    xp = x_prompt.astype(F32).reshape(bp * tp, d_model)
    xs = x_sample.astype(F32).reshape(bs * ts, d_model)
    wkv_p, shift_p, conv_p, mk_p, mv_p = [], [], [], [], []
    wkv_s, shift_s, conv_s = [], [], []
    for l in range(depth):
        p = layers[l]
        x1p, x1s = _ffn(xp, xs, p['ffn1_w_gate'], p['ffn1_w_up'], p['ffn1_w_down'], final_g, final_norm=False)

        mkt, mvt = _memkv(mem_prompt.astype(F32), p['w_mem_kv_t'])
        x2p, wkv, sh, cv = _mix(
            x1p.reshape(bp, tp, d_model), jnp.zeros((bp, d_shift), F32),
            jnp.zeros((bp, CONV_WIDTH - 1, d_conv), F32), jnp.zeros((bp, n_heads, HEAD_DIM, HEAD_DIM), F32),
            mkt, mvt, p, bb=1, tt=512, rw=dict(chunk=64, unroll1=8), mg=dict(tm=512, rows=512, group=1))
        wkv_p.append(wkv)
        shift_p.append(sh)
        conv_p.append(cv)
        heads_t = lambda a: jnp.transpose(a.reshape(bp, N_XATTN_HEADS, d_x // N_XATTN_HEADS, n_mem), (0, 3, 1, 2))
        mk_p.append(heads_t(mkt))
        mv_p.append(heads_t(mvt))

        kt = jnp.transpose(cache_mem_k[l].astype(F32), (0, 2, 3, 1)).reshape(bs, d_x, n_mem)
        vt = jnp.transpose(cache_mem_v[l].astype(F32), (0, 2, 3, 1)).reshape(bs, d_x, n_mem)
        x2s, wkv, sh, cv = _mix(
            x1s.reshape(bs, ts, d_model), state_shift[l].astype(F32), state_conv[l].astype(F32),
            state_wkv[l].astype(F32), kt, vt, p, bb=16, tt=ts, rw=dict(chunk=ts, unroll1=16),
            mg=dict(tm=16 * ts, rows=ts, group=4))
        wkv_s.append(wkv)
        shift_s.append(sh)
        conv_s.append(cv)

        xp, xs = _ffn(x2p, x2s, p['ffn2_w_gate'], p['ffn2_w_up'], p['ffn2_w_down'], final_g,
                      final_norm=(l == depth - 1))

    return (xp.reshape(x_prompt.shape).astype(x_prompt.dtype), xs.reshape(x_sample.shape).astype(x_sample.dtype),
            jnp.stack(wkv_p), jnp.stack(shift_p), jnp.stack(conv_p), jnp.stack(mk_p), jnp.stack(mv_p),
            jnp.stack(wkv_s), jnp.stack(shift_s), jnp.stack(conv_s))
```

```python
import functools
import math

import jax
import jax.numpy as jnp
from jax import lax
from jax.experimental import pallas as pl
from jax.experimental.pallas import tpu as pltpu

F32 = jnp.float32
BF16 = jnp.bfloat16

HEAD_DIM = 64
PAIR = 2 * HEAD_DIM
N_XATTN_HEADS = 4
CONV_WIDTH = 31
SUBLANES = 8
CONV_PAD = 32
LORA_DECAY = 64
LORA_A = 64
RMS_EPS = 1e-6
LN_EPS = 1e-5
GN_EPS = 64e-5
VMEM_LIMIT = 56 * 1024 * 1024


def _params(*sem):
    return pltpu.CompilerParams(dimension_semantics=sem, vmem_limit_bytes=VMEM_LIMIT)


def _rms(x, g):
    return x * lax.rsqrt(jnp.mean(x * x, axis=-1, keepdims=True) + RMS_EPS) * g


def _mm(a, b):
    return jnp.dot(a, b, preferred_element_type=F32)


def _nt(a, b):
    return lax.dot_general(a, b, (((1,), (1,)), ((), ())), preferred_element_type=F32)


def _tn(a, b):
    return lax.dot_general(a, b, (((0,), (0,)), ((), ())), preferred_element_type=F32)


def _split(x):
    hi = x.astype(BF16)
    return hi, (x - hi.astype(F32)).astype(BF16)


def _exact_lhs_dot(sel, x):
    hi, lo = _split(x)
    return _mm(sel, hi) + _mm(sel, lo)


def _dot3(x, w):
    x_hi, x_lo = _split(x)
    w_hi, w_lo = _split(w)
    return _mm(x_hi, w_hi) + _mm(x_lo, w_hi) + _mm(x_hi, w_lo)


def _dot1(x, w):
    return _mm(x.astype(BF16), w.astype(BF16))


def _const_spec(shape):
    nd = len(shape)
    return pl.BlockSpec(shape, lambda *_: (0,) * nd)


def _ffn_body(xa_ref, xb_ref, wg_ref, wu_ref, wd_ref, fg_ref, oa_ref, ob_ref, *, final_norm, tiles_a):
    def tile(x_ref, o_ref):
        x = x_ref[...]
        xb = x.astype(BF16)
        inv = lax.rsqrt(jnp.mean(x * x, axis=-1, keepdims=True) + RMS_EPS)
        hg = _mm(xb, wg_ref[...]) * inv
        hu = _mm(xb, wu_ref[...]) * inv
        h = (hg * jax.nn.sigmoid(hg) * hu).astype(BF16)
        y = x + 0.5 * _mm(h, wd_ref[...])
        if final_norm:
            y = _rms(y, fg_ref[...])
        o_ref[...] = y

    in_a = pl.program_id(0) < tiles_a
    pl.when(in_a)(functools.partial(tile, xa_ref, oa_ref))
    pl.when(jnp.logical_not(in_a))(functools.partial(tile, xb_ref, ob_ref))


def _resident_spec(shape):
    nd = len(shape)
    return pl.BlockSpec(shape, lambda *_: (0,) * nd, pipeline_mode=pl.Buffered(1))


def _ffn(xa, xb, w_gate, w_up, w_down, final_g, *, final_norm, tm=512):
    (na, d), nb = xa.shape, xb.shape[0]
    tm = min(tm, na, nb)
    assert na % tm == 0 and nb % tm == 0
    ta, tb = na // tm, nb // tm
    spec_a = pl.BlockSpec((tm, d), lambda i: (jnp.minimum(i, ta - 1), 0))
    spec_b = pl.BlockSpec((tm, d), lambda i: (jnp.maximum(i - ta, 0), 0))
    return pl.pallas_call(
        functools.partial(_ffn_body, final_norm=final_norm, tiles_a=ta),
        grid=(ta + tb,),
        in_specs=[
            spec_a, spec_b,
            _resident_spec(w_gate.shape), _resident_spec(w_up.shape), _resident_spec(w_down.shape),
            _const_spec((1, d)),
        ],
        out_specs=[spec_a, spec_b],
        out_shape=[jax.ShapeDtypeStruct((na, d), F32), jax.ShapeDtypeStruct((nb, d), F32)],
        compiler_params=_params("arbitrary"),
        name="ffn_final" if final_norm else "ffn",
    )(xa, xb, w_gate, w_up, w_down, final_g)


def _memkv_body(m_ref, wt_ref, kt_ref, vt_ref):
    dk = kt_ref.shape[1]
    kvt = _nt(wt_ref[...], m_ref[0].astype(BF16))
    kt_ref[0] = kvt[:dk]
    vt_ref[0] = kvt[dk:]


def _memkv(mem, wt):
    b, n_mem, d = mem.shape
    dk = wt.shape[0] // 2
    return pl.pallas_call(
        _memkv_body,
        grid=(b,),
        in_specs=[pl.BlockSpec((1, n_mem, d), lambda i: (i, 0, 0)), _const_spec(wt.shape)],
        out_specs=[pl.BlockSpec((1, dk, n_mem), lambda i: (i, 0, 0))] * 2,
        out_shape=[jax.ShapeDtypeStruct((b, dk, n_mem), F32)] * 2,
        compiler_params=_params("parallel"),
        name="memkv",
    )(mem, wt)


def _rwkv_body(x_ref, mixg_ref, ws_ref, sprev_ref, mu_ref, w0_ref, wdu_ref, a0_ref, wau_ref, wgu_ref,
               kk_ref, ka_ref, rk_ref, gng_ref, gnb_ref, s0_ref, hsum_ref, tri_ref, blk_ref,
               wc_ref, glub_ref, cprev_ref, cw_ref, cb_ref, lng_ref, lnb_ref,
               og_ref, sout_ref, wkv_ref, yc_ref, cnew_ref,
               carry, st, at_s, rt_s, bb_s, kb_s, bh_s, kh_s, v_s, wt_s, bonus_s, g_s, o_s,
               gm_s, hm_s, rp_s, oi_s, ubuf, ushift, yc_s, *, chunk, unroll1):
    bb_n, tt, d = x_ref.shape
    m = bb_n * tt
    d_shift = ws_ref.shape[1]
    d_r = w0_ref.shape[1]
    dc = cw_ref.shape[1]
    n_pairs = d_r // PAIR
    n_ck = tt // chunk
    n_blk = m // chunk
    log_chunk = int(math.log2(chunk))
    c2 = 2 * chunk
    conv_lo = CONV_PAD - (CONV_WIDTH - 1)
    t = pl.program_id(1)

    @pl.when(t == 0)
    def _():
        carry[...] = sprev_ref[...]
        ubuf[:, conv_lo:CONV_PAD, :] = cprev_ref[...]
        zero = jnp.zeros((HEAD_DIM, HEAD_DIM), F32)
        for b_i in range(bb_n):
            for p in range(n_pairs):
                top = jnp.concatenate([s0_ref[b_i, 2 * p], zero], axis=1)
                bot = jnp.concatenate([zero, s0_ref[b_i, 2 * p + 1]], axis=1)
                st[b_i, p] = jnp.concatenate([top, bot], axis=0)

    n_part = 2 if (bb_n == 1 and m >= 4 * chunk) else 1
    pm_ = m // n_part
    parts = range(n_part)
    psl = [slice(i * pm_, (i + 1) * pm_) for i in parts]
    x2d = x_ref[...].reshape(m, d)
    xn = [_rms(x2d[psl[i]], mixg_ref[...]).astype(BF16) for i in parts]
    zc = [_mm(xn[i], wc_ref[...]) + glub_ref[...] for i in parts]
    zs = [_mm(xn[i], ws_ref[...]) for i in parts]
    u = [zc[i][:, :dc] * jax.nn.sigmoid(zc[i][:, dc:]) for i in parts]
    if n_part == 1:
        ubuf[:, CONV_PAD:CONV_PAD + tt, :] = u[0].reshape(bb_n, tt, dc)
    else:
        for i in parts:
            ubuf[0, CONV_PAD + i * pm_:CONV_PAD + (i + 1) * pm_, :] = u[i]
    for sh in range(1, SUBLANES):
        ushift[sh - 1] = ubuf[:, sh:sh + tt + CONV_PAD - SUBLANES, :]

    if n_part == 1:
        prev0 = jnp.broadcast_to(carry[...], (bb_n, tt, d_shift)).reshape(m, d_shift)
        row = lax.broadcasted_iota(jnp.int32, (m, d_shift), 0)
        prev = [jnp.where((row & (tt - 1)) == 0, prev0, pltpu.roll(zs[0], 1, axis=0))]
        last = zs[0].reshape(bb_n, tt, d_shift)[:, tt - 1:tt, :]
    else:
        row = lax.broadcasted_iota(jnp.int32, (pm_, d_shift), 0)
        first = [carry[0]] + [zs[i][pm_ - 1:pm_, :] for i in parts[:-1]]
        prev = [jnp.where(row == 0, first[i], pltpu.roll(zs[i], 1, axis=0)) for i in parts]
        last = zs[-1][pm_ - 1:pm_, :].reshape(1, 1, d_shift)
    carry[...] = last
    sout_ref[...] = last
    xm = [zs[i] + (prev[i] - zs[i]) * mu_ref[...] for i in parts]

    c1, c2_, c3 = d_r, 2 * d_r, 3 * d_r
    c4 = c3 + LORA_DECAY
    c5 = c4 + LORA_A
    r = [xm[i][:, :c1] for i in parts]
    k = [xm[i][:, c1:c2_] for i in parts]
    v = [xm[i][:, c2_:c3] for i in parts]
    dec_in = [w0_ref[...] + _dot3(jnp.tanh(xm[i][:, c3:c4]), wdu_ref[...]) for i in parts]
    a_in = [a0_ref[...] + _dot1(xm[i][:, c4:c5], wau_ref[...]) for i in parts]
    for i in parts:
        g_s[psl[i], :] = _dot1(jax.nn.sigmoid(xm[i][:, c5:]), wgu_ref[...])
    ld = [-jnp.exp(-(jnp.maximum(-dec_in[i], 0.0) + jnp.log(1.0 + jnp.exp(-jnp.abs(dec_in[i])))) - 0.5)
          for i in parts]
    a = [jax.nn.sigmoid(a_in[i]) for i in parts]
    kh = [k[i] * (1.0 + (a[i] - 1.0) * ka_ref[...]) for i in parts]

    head_sum = hsum_ref[...]
    kkraw = [k[i] * kk_ref[...] for i in parts]
    kk_ss = [_mm((kkraw[i] * kkraw[i]).astype(BF16), head_sum) for i in parts]
    rk_sum = [_mm((r[i] * kh[i] * rk_ref[...]).astype(BF16), head_sum) for i in parts]
    kk = [kkraw[i] / jnp.maximum(jnp.sqrt(kk_ss[i]), 1e-12) for i in parts]
    b = [kk[i] * a[i] for i in parts]
    tdt = at_s.dtype
    for i in parts:
        bonus_s[psl[i], :] = rk_sum[i] * v[i]
        v_s[psl[i], :] = v[i].astype(tdt)

    slab = tri_ref.shape[0]
    for s0 in range(0, pm_, slab):
        sl = slice(s0, s0 + slab)
        ld_c = [ld[i][sl] for i in parts]
        cum = [_exact_lhs_dot(tri_ref[...], ld_c[i]) for i in parts]
        tot = [_exact_lhs_dot(blk_ref[...], ld_c[i]) for i in parts]
        for i in parts:
            gl = slice(i * pm_ + s0, i * pm_ + s0 + slab)
            w_inv = jnp.exp(-cum[i])
            w_rem = jnp.exp(tot[i] - cum[i])
            at_s[gl, :] = (-kk[i][sl] * jnp.exp(cum[i] - ld_c[i])).astype(tdt)
            rt_s[gl, :] = (r[i][sl] * jnp.exp(cum[i])).astype(tdt)
            bb_s[gl, :] = (b[i][sl] * w_inv).astype(tdt)
            kb_s[gl, :] = (kh[i][sl] * w_inv).astype(tdt)
            bh_s[gl, :] = (b[i][sl] * w_rem).astype(tdt)
            kh_s[gl, :] = (kh[i][sl] * w_rem).astype(tdt)
            wt_s[gl, :] = jnp.exp(tot[i])

    lane_lo = lax.broadcasted_iota(jnp.int32, (chunk, PAIR), 1) < HEAD_DIM
    r2 = lax.broadcasted_iota(jnp.int32, (c2, c2), 0)
    q2 = lax.broadcasted_iota(jnp.int32, (c2, c2), 1)
    strict = (q2 & (chunk - 1)) < (r2 & (chunk - 1))
    incl = (q2 & (chunk - 1)) <= (r2 & (chunk - 1))
    eye2 = (r2 == q2).astype(F32)
    wide = c2 % PAIR == 0

    def stack(ref, rows, ls):
        x = ref[rows, ls]
        return jnp.concatenate([jnp.where(lane_lo, x, 0.0), jnp.where(lane_lo, 0.0, x)], axis=0).astype(BF16)

    def chain_mats(chains):
        n = range(len(chains))

        def stacks(ref):
            return [stack(ref, slice(blk * chunk, (blk + 1) * chunk), slice(p * PAIR, (p + 1) * PAIR))
                    for blk, p in chains]

        at, rt, bbm, kbm = stacks(at_s), stacks(rt_s), stacks(bb_s), stacks(kb_s)
        bhm, khm, vm = stacks(bh_s), stacks(kh_s), stacks(v_s)
        if wide:
            a4 = [_nt(jnp.concatenate([at[c_], rt[c_]], axis=0), jnp.concatenate([bbm[c_], kbm[c_]], axis=0))
                  for c_ in n]
            a_ab = [jnp.where(strict, a4[c_][:c2, :c2], 0.0) for c_ in n]
            a_ak = [jnp.where(strict, a4[c_][:c2, c2:], 0.0).astype(BF16) for c_ in n]
            a_rb = [jnp.where(incl, a4[c_][c2:, :c2], 0.0).astype(BF16) for c_ in n]
            a_rk = [jnp.where(incl, a4[c_][c2:, c2:], 0.0).astype(BF16) for c_ in n]
        else:
            a_ab = [jnp.where(strict, _nt(at[c_], bbm[c_]), 0.0) for c_ in n]
            a_ak = [jnp.where(strict, _nt(at[c_], kbm[c_]), 0.0).astype(BF16) for c_ in n]
            a_rb = [jnp.where(incl, _nt(rt[c_], bbm[c_]), 0.0).astype(BF16) for c_ in n]
            a_rk = [jnp.where(incl, _nt(rt[c_], kbm[c_]), 0.0).astype(BF16) for c_ in n]
        tinv = [eye2 + a_ab[c_] for c_ in n]
        ap = [a_ab[c_].astype(BF16) for c_ in n]
        ap = [_mm(ap[c_], ap[c_]).astype(BF16) for c_ in n]
        for i_sq in range(1, log_chunk):
            if i_sq == log_chunk - 1:
                tinv = [tinv[c_] + _mm(ap[c_], tinv[c_].astype(BF16)) for c_ in n]
            elif wide:
                x = [_mm(ap[c_], jnp.concatenate([ap[c_], tinv[c_].astype(BF16)], axis=1)) for c_ in n]
                ap = [x[c_][:, :c2].astype(BF16) for c_ in n]
                tinv = [tinv[c_] + x[c_][:, c2:] for c_ in n]
            else:
                tinv = [tinv[c_] + _mm(ap[c_], tinv[c_].astype(BF16)) for c_ in n]
                ap = [_mm(ap[c_], ap[c_]).astype(BF16) for c_ in n]
        av = [_mm(a_ak[c_], vm[c_]).astype(BF16) for c_ in n]
        pq = [_mm(tinv[c_].astype(BF16), jnp.concatenate([av[c_], at[c_]], axis=1)) for c_ in n]
        return rt, bhm, khm, vm, a_rb, a_rk, pq

    def phase1_step(i):
        chains = [(unroll1 * i + j, p) for j in range(unroll1) for p in range(n_pairs)]
        n = range(len(chains))
        rt, bhm, khm, vm, a_rb, a_rk, pq = chain_mats(chains)
        pm = [pq[c_][:, :PAIR].astype(BF16) for c_ in n]
        qm = [pq[c_][:, PAIR:].astype(BF16) for c_ in n]
        for c_, (blk, p) in enumerate(chains):
            gm_s[blk, p] = _tn(qm[c_], bhm[c_]).astype(BF16)
        for c_, (blk, p) in enumerate(chains):
            hm_s[blk, p] = _tn(jnp.concatenate([pm[c_], vm[c_]], axis=0),
                               jnp.concatenate([bhm[c_], khm[c_]], axis=0))
        rq = [_mm(a_rb[c_], jnp.concatenate([qm[c_], pm[c_]], axis=1)) for c_ in n]
        for c_, (blk, p) in enumerate(chains):
            rp_s[blk, p] = (rt[c_].astype(F32) + rq[c_][:, :PAIR]).astype(BF16)
        for c_, (blk, p) in enumerate(chains):
            oi_s[blk, p] = rq[c_][:, PAIR:] + _mm(a_rk[c_], vm[c_])

    def conv_block(blk):
        b_i, t0 = blk // n_ck, (blk % n_ck) * chunk
        acc = jnp.zeros((chunk, dc), F32) + cb_ref[...]
        for kx in range(CONV_WIDTH):
            off = conv_lo + kx
            sh, base = off % SUBLANES, off - off % SUBLANES
            rows = slice(t0 + base, t0 + base + chunk)
            win = ubuf[b_i, rows, :] if sh == 0 else ushift[sh - 1, b_i, rows, :]
            acc = acc + win * cw_ref[kx:kx + 1, :]
        cm = jnp.mean(acc, axis=-1, keepdims=True)
        cc = acc - cm
        cv = jnp.mean(cc * cc, axis=-1, keepdims=True)
        cn = cc * lax.rsqrt(cv + LN_EPS) * lng_ref[...] + lnb_ref[...]
        yc_s[blk * chunk:(blk + 1) * chunk, :] = cn * jax.nn.sigmoid(cn)

    def phase2_step(i):
        conv_block(i)
        chains = [(i, p) for p in range(n_pairs)]
        n = range(len(chains))
        row0 = [blk * chunk for blk, _ in chains]
        ls = [slice(p * PAIR, (p + 1) * PAIR) for _, p in chains]
        s = [st[blk // n_ck, p] for blk, p in chains]
        sb = [s[c_].astype(BF16) for c_ in n]
        s_new = [s[c_] * wt_s[row0[c_]:row0[c_] + 1, ls[c_]] + _mm(sb[c_], gm_s[blk, p]) + hm_s[blk, p]
                 for c_, (blk, p) in enumerate(chains)]
        o_bd = [oi_s[blk, p] + _nt(rp_s[blk, p], sb[c_]) for c_, (blk, p) in enumerate(chains)]
        for c_, (blk, p) in enumerate(chains):
            st[blk // n_ck, p] = s_new[c_]
        for c_ in n:
            o_s[row0[c_]:row0[c_] + chunk, ls[c_]] = o_bd[c_][:chunk] + o_bd[c_][chunk:]

    def single_chunk_step(i):
        for j in range(unroll1):
            conv_block(unroll1 * i + j)
        chains = [(unroll1 * i + j, p) for j in range(unroll1) for p in range(n_pairs)]
        n = range(len(chains))
        row0 = [blk * chunk for blk, _ in chains]
        ls = [slice(p * PAIR, (p + 1) * PAIR) for _, p in chains]
        rt, bhm, khm, vm, a_rb, a_rk, pq = chain_mats(chains)
        s = [st[blk, p] for blk, p in chains]
        sb = [s[c_].astype(BF16) for c_ in n]
        qr = [_nt(jnp.concatenate([pq[c_][:, PAIR:].astype(BF16), rt[c_]], axis=0), sb[c_]) for c_ in n]
        u = [(pq[c_][:, :PAIR] + qr[c_][:c2]).astype(BF16) for c_ in n]
        o_bd = [qr[c_][c2:] + _mm(a_rb[c_], u[c_]) + _mm(a_rk[c_], vm[c_]) for c_ in n]
        s_new = [s[c_] * wt_s[row0[c_]:row0[c_] + 1, ls[c_]]
                 + _tn(jnp.concatenate([u[c_], vm[c_]], axis=0), jnp.concatenate([bhm[c_], khm[c_]], axis=0))
                 for c_ in n]
        for c_, (blk, p) in enumerate(chains):
            st[blk, p] = s_new[c_]
        for c_ in n:
            o_s[row0[c_]:row0[c_] + chunk, ls[c_]] = o_bd[c_][:chunk] + o_bd[c_][chunk:]

    if n_ck == 1:
        for i in range(n_blk // unroll1):
            single_chunk_step(i)
    else:
        for i in range(n_blk // unroll1):
            phase1_step(i)
        for i in range(n_blk):
            phase2_step(i)

    o = o_s[...]
    mean = _mm(o.astype(BF16), head_sum) * (1.0 / HEAD_DIM)
    oc = o - mean
    var = _mm((oc * oc).astype(BF16), head_sum) * (1.0 / HEAD_DIM)
    o = oc * lax.rsqrt(var + GN_EPS) * gng_ref[...] + gnb_ref[...] + bonus_s[...]
    og_ref[...] = (o * g_s[...]).astype(BF16).reshape(bb_n, tt, d_r)

    yc_ref[...] = yc_s[...].astype(BF16).reshape(bb_n, tt, dc)
    tail = ubuf[:, conv_lo + tt:CONV_PAD + tt, :]
    cnew_ref[...] = tail
    ubuf[:, conv_lo:CONV_PAD, :] = tail

    @pl.when(t == pl.num_programs(1) - 1)
    def _():
        for b_i in range(bb_n):
            for p in range(n_pairs):
                s = st[b_i, p]
                wkv_ref[b_i, 2 * p] = s[:HEAD_DIM, :HEAD_DIM]
                wkv_ref[b_i, 2 * p + 1] = s[HEAD_DIM:, HEAD_DIM:]


def _rwkv(x, mixg, ws, sprev, mu, w0, wdu, a0, wau, wgu, kk, ka, rk, gng, gnb, s0,
          wc, glub, cprev, cw, cb, lng, lnb, *, bb, tt, chunk, unroll1):
    b, t_len, d = x.shape
    d_shift = ws.shape[1]
    d_r = w0.shape[1]
    dc = cw.shape[1]
    n_heads = d_r // HEAD_DIM
    n_pairs = d_r // PAIR
    m = bb * tt
    n_blk = m // chunk
    n_kept = n_blk if tt > chunk else 1
    assert b % bb == 0 and t_len % tt == 0 and tt % chunk == 0
    assert n_blk % unroll1 == 0
    assert chunk & (chunk - 1) == 0 and tt & (tt - 1) == 0 and chunk % 8 == 0
    tok = pltpu.VMEM((m, d_r), F32)
    tok_mm = pltpu.VMEM((m, d_r), BF16 if chunk % (2 * SUBLANES) == 0 else F32)
    lane = jnp.arange(d_r)
    head_sum = (lane[:, None] // HEAD_DIM == lane[None, :] // HEAD_DIM).astype(BF16)
    rows = jnp.arange(chunk if chunk >= HEAD_DIM else m)
    same_chunk = rows[:, None] // chunk == rows[None, :] // chunk
    tri = (same_chunk & (rows[None, :] <= rows[:, None])).astype(BF16)
    blk = same_chunk.astype(BF16)
    return pl.pallas_call(
        functools.partial(_rwkv_body, chunk=chunk, unroll1=unroll1),
        grid=(b // bb, t_len // tt),
        in_specs=[
            pl.BlockSpec((bb, tt, d), lambda i, j: (i, j, 0)),
            _const_spec(mixg.shape), _const_spec(ws.shape),
            pl.BlockSpec((bb, 1, d_shift), lambda i, j: (i, 0, 0)),
            _const_spec(mu.shape), _const_spec(w0.shape), _const_spec(wdu.shape), _const_spec(a0.shape),
            _const_spec(wau.shape), _const_spec(wgu.shape), _const_spec(kk.shape), _const_spec(ka.shape),
            _const_spec(rk.shape), _const_spec(gng.shape), _const_spec(gnb.shape),
            pl.BlockSpec((bb, n_heads, HEAD_DIM, HEAD_DIM), lambda i, j: (i, 0, 0, 0)),
            _const_spec(head_sum.shape), _const_spec(tri.shape), _const_spec(blk.shape),
            _const_spec(wc.shape), _const_spec(glub.shape),
            pl.BlockSpec((bb, CONV_WIDTH - 1, dc), lambda i, j: (i, 0, 0)),
            _const_spec(cw.shape), _const_spec(cb.shape), _const_spec(lng.shape), _const_spec(lnb.shape),
        ],
        out_specs=[
            pl.BlockSpec((bb, tt, d_r), lambda i, j: (i, j, 0)),
            pl.BlockSpec((bb, 1, d_shift), lambda i, j: (i, 0, 0)),
            pl.BlockSpec((bb, n_heads, HEAD_DIM, HEAD_DIM), lambda i, j: (i, 0, 0, 0)),
            pl.BlockSpec((bb, tt, dc), lambda i, j: (i, j, 0)),
            pl.BlockSpec((bb, CONV_WIDTH - 1, dc), lambda i, j: (i, 0, 0)),
        ],
        out_shape=[
            jax.ShapeDtypeStruct((b, t_len, d_r), BF16),
            jax.ShapeDtypeStruct((b, 1, d_shift), F32),
            jax.ShapeDtypeStruct((b, n_heads, HEAD_DIM, HEAD_DIM), F32),
            jax.ShapeDtypeStruct((b, t_len, dc), BF16),
            jax.ShapeDtypeStruct((b, CONV_WIDTH - 1, dc), F32),
        ],
        scratch_shapes=[
            pltpu.VMEM((bb, 1, d_shift), F32),
            pltpu.VMEM((bb, n_pairs, PAIR, PAIR), F32),
        ] + [tok_mm] * 7 + [tok] * 4 + [
            pltpu.VMEM((n_kept, n_pairs, PAIR, PAIR), BF16),
            pltpu.VMEM((n_kept, n_pairs, PAIR, PAIR), F32),
            pltpu.VMEM((n_kept, n_pairs, 2 * chunk, PAIR), BF16),
            pltpu.VMEM((n_kept, n_pairs, 2 * chunk, PAIR), F32),
            pltpu.VMEM((bb, CONV_PAD + tt, dc), F32),
            pltpu.VMEM((SUBLANES - 1, bb, CONV_PAD + tt - SUBLANES, dc), F32),
            pltpu.VMEM((m, dc), F32),
        ],
        compiler_params=_params("parallel", "arbitrary"),
        name="rwkv_conv",
    )(x, mixg, ws, sprev, mu, w0, wdu, a0, wau, wgu, kk, ka, rk, gng, gnb, s0, head_sum, tri, blk,
      wc, glub, cprev, cw, cb, lng, lnb)


def _merge_body(x_ref, wq_ref, wg_ref, kt_ref, vt_ref, ya_ref, yb_ref, wro_ref, wco_ref, wxo_ref, wo_ref, o_ref,
                *, t_len, rows, group):
    tm, d = x_ref.shape
    dh = kt_ref.shape[1] // N_XATTN_HEADS
    x = x_ref[...]
    xb = x.astype(BF16)
    inv = lax.rsqrt(jnp.mean(x * x, axis=-1, keepdims=True) + RMS_EPS)

    q = (_mm(xb, wq_ref[...]) * (inv * dh ** -0.5)).astype(BF16)
    n_blocks = tm // rows
    hs = lambda h: slice(h * dh, (h + 1) * dh)
    seq = lambda r: (r * rows) // t_len
    pieces = []
    for g0 in range(0, n_blocks, group):
        blocks = range(g0, min(g0 + group, n_blocks))
        chains = [(r, h) for r in blocks for h in range(N_XATTN_HEADS)]
        s = [_mm(q[r * rows:(r + 1) * rows, hs(h)], kt_ref[seq(r), hs(h), :].astype(BF16)) for r, h in chains]
        p = [jnp.exp(v - jnp.max(v, axis=-1, keepdims=True)) for v in s]
        l = [jnp.sum(v, axis=-1, keepdims=True) for v in p]
        o = [_nt(p[i].astype(BF16), vt_ref[seq(r), hs(h), :].astype(BF16)) / l[i] for i, (r, h) in enumerate(chains)]
        for j in range(len(blocks)):
            pieces.append(jnp.concatenate(o[j * N_XATTN_HEADS:(j + 1) * N_XATTN_HEADS], axis=-1))
    oc = jnp.concatenate(pieces, axis=0).astype(BF16)

    merged = jnp.zeros_like(x)
    for i, (y, w_ref) in enumerate(((ya_ref[...], wro_ref), (yb_ref[...], wco_ref), (oc, wxo_ref))):
        gate = jax.nn.sigmoid(_mm(xb, wg_ref[:, i * d:(i + 1) * d]) * inv)
        merged = merged + gate * _mm(y, w_ref[...])
    o_ref[...] = x + _mm(merged.astype(BF16), wo_ref[...])


def _merge(x, wq, wg, kt, vt, ya, yb, wro, wco, wxo, wo, *, t_len, tm, rows, group):
    n, d = x.shape
    dx, n_mem = kt.shape[1:]
    seqs = max(tm // t_len, 1)
    assert n % tm == 0 and tm % rows == 0 and (tm % t_len == 0 or t_len % tm == 0) and rows <= t_len
    row = lambda w: pl.BlockSpec((tm, w), lambda i: (i, 0))
    mem = pl.BlockSpec((seqs, dx, n_mem), lambda i: ((i * tm) // (t_len * seqs), 0, 0))
    return pl.pallas_call(
        functools.partial(_merge_body, t_len=t_len, rows=rows, group=group),
        grid=(n // tm,),
        in_specs=[row(d), _resident_spec(wq.shape), _resident_spec(wg.shape), mem, mem, row(ya.shape[1]),
                  row(yb.shape[1]), _resident_spec(wro.shape), _resident_spec(wco.shape),
                  _resident_spec(wxo.shape), _resident_spec(wo.shape)],
        out_specs=row(d),
        out_shape=jax.ShapeDtypeStruct((n, d), F32),
        compiler_params=_params("parallel"),
        name="merge",
    )(x, wq, wg, kt, vt, ya, yb, wro, wco, wxo, wo)


def _mix(x1, shift_prev, conv_prev, wkv0, mem_kt, mem_vt, p, *, bb, tt, rw, mg):
    b, t_len, d = x1.shape
    n = b * t_len
    og, shift, wkv, cb, conv = _rwkv(
        x1, p['mix_norm'], p['w_s'], shift_prev[:, None, :], p['mu_shift'], p['w0'], p['w_decay_up'], p['a0'],
        p['w_a_up'], p['w_g_up'], p['k_k'], p['k_a'], p['r_k'], p['gn_g'], p['gn_b'], wkv0,
        p['w_c'], p['glu_b'], conv_prev, p['conv_w'], p['conv_b'], p['conv_ln_g'], p['conv_ln_b'],
        bb=bb, tt=tt, **rw)
    x2 = _merge(x1.reshape(n, d), p['w_q'], p['w_g'], mem_kt, mem_vt, og.reshape(n, -1), cb.reshape(n, -1),
                p['w_rwkv_out'], p['w_conv_out'], p['w_xattn_out'], p['w_o'], t_len=t_len, **mg)
    return x2, wkv, shift[:, 0, :], conv


def kernel(x_prompt, mem_prompt, x_sample, state_wkv, state_shift, state_conv, cache_mem_k, cache_mem_v,
           ffn1_norm, ffn1_w_up, ffn1_w_down, mix_norm, w_in, mu_shift, w0, w_decay_up, a0, w_a_up, w_g_up,
           k_k, k_a, r_k, gn_g, gn_b, w_rwkv_out, glu_b, conv_w, conv_b, conv_ln_g, conv_ln_b, w_conv_out,
           w_mem_kv, w_xattn_out, w_o, ffn2_norm, ffn2_w_up, ffn2_w_down, final_norm):
    depth = w_in.shape[0]
    d_model = w_in.shape[1]
    d_r = w0.shape[1]
    d_shift = mu_shift.shape[1]
    d_conv = conv_w.shape[2]
    d_x = w_xattn_out.shape[1]
    n_heads = d_r // HEAD_DIM
    o1 = d_shift
    o2 = o1 + 2 * d_conv
    o3 = o2 + d_x
    d_ff = ffn1_w_down.shape[1]
    row = lambda a: a.astype(F32).reshape(1, -1)
    scaled = lambda g, w: (g.astype(F32)[:, None] * w.astype(F32)).astype(BF16)
    final_g = row(final_norm)

    layers = []
    for l in range(depth):
        layers.append({
            'ffn1_w_gate': scaled(ffn1_norm[l], ffn1_w_up[l][:, :d_ff]),
            'ffn1_w_up': scaled(ffn1_norm[l], ffn1_w_up[l][:, d_ff:]),
            'ffn1_w_down': ffn1_w_down[l].astype(BF16), 'mix_norm': row(mix_norm[l]),
            'w_s': w_in[l, :, :o1].astype(BF16), 'w_c': w_in[l, :, o1:o2].astype(BF16),
            'w_q': scaled(mix_norm[l], w_in[l, :, o2:o3]), 'w_g': scaled(mix_norm[l], w_in[l, :, o3:]),
            'mu_shift': row(mu_shift[l]), 'w0': row(w0[l]), 'w_decay_up': w_decay_up[l].astype(F32),
            'a0': row(a0[l]), 'w_a_up': w_a_up[l].astype(F32), 'w_g_up': w_g_up[l].astype(F32),
            'k_k': row(k_k[l]), 'k_a': row(k_a[l]), 'r_k': row(r_k[l]), 'gn_g': row(gn_g[l]),
            'gn_b': row(gn_b[l]), 'w_rwkv_out': w_rwkv_out[l].astype(BF16), 'glu_b': row(glu_b[l]),
            'conv_w': conv_w[l].astype(F32), 'conv_b': row(conv_b[l]), 'conv_ln_g': row(conv_ln_g[l]),
            'conv_ln_b': row(conv_ln_b[l]), 'w_conv_out': w_conv_out[l].astype(BF16),
            'w_mem_kv_t': w_mem_kv[l].T.astype(BF16), 'w_xattn_out': w_xattn_out[l].astype(BF16),
            'w_o': w_o[l].astype(BF16),
            'ffn2_w_gate': scaled(ffn2_norm[l], ffn2_w_up[l][:, :d_ff]),
            'ffn2_w_up': scaled(ffn2_norm[l], ffn2_w_up[l][:, d_ff:]), 'ffn2_w_down': ffn2_w_down[l].astype(BF16),
        })

    bp, tp, _ = x_prompt.shape
    bs, ts, _ = x_sample.shape
    n_mem = mem_prompt.shape[1]
    xp = x_prompt.astype(F32).reshape(bp * tp, d_model)
    xs = x_sample.astype(F32).reshape(bs * ts, d_model)
    wkv_p, shift_p, conv_p, mk_p, mv_p = [], [], [], [], []
    wkv_s, shift_s, conv_s = [], [], []
    for l in range(depth):
        p = layers[l]
        x1p, x1s = _ffn(xp, xs, p['ffn1_w_gate'], p['ffn1_w_up'], p['ffn1_w_down'], final_g, final_norm=False)

        mkt, mvt = _memkv(mem_prompt.astype(F32), p['w_mem_kv_t'])
        x2p, wkv, sh, cv = _mix(
            x1p.reshape(bp, tp, d_model), jnp.zeros((bp, d_shift), F32),
            jnp.zeros((bp, CONV_WIDTH - 1, d_conv), F32), jnp.zeros((bp, n_heads, HEAD_DIM, HEAD_DIM), F32),
            mkt, mvt, p, bb=1, tt=512, rw=dict(chunk=64, unroll1=4), mg=dict(tm=512, rows=512, group=1))
        wkv_p.append(wkv)
        shift_p.append(sh)
        conv_p.append(cv)
        heads_t = lambda a: jnp.transpose(a.reshape(bp, N_XATTN_HEADS, d_x // N_XATTN_HEADS, n_mem), (0, 3, 1, 2))
        mk_p.append(heads_t(mkt))
        mv_p.append(heads_t(mvt))

        kt = jnp.transpose(cache_mem_k[l].astype(F32), (0, 2, 3, 1)).reshape(bs, d_x, n_mem)
        vt = jnp.transpose(cache_mem_v[l].astype(F32), (0, 2, 3, 1)).reshape(bs, d_x, n_mem)
        x2s, wkv, sh, cv = _mix(
            x1s.reshape(bs, ts, d_model), state_shift[l].astype(F32), state_conv[l].astype(F32),
            state_wkv[l].astype(F32), kt, vt, p, bb=16, tt=ts, rw=dict(chunk=ts, unroll1=16),
            mg=dict(tm=16 * ts, rows=ts, group=4))
        wkv_s.append(wkv)
        shift_s.append(sh)
        conv_s.append(cv)

        xp, xs = _ffn(x2p, x2s, p['ffn2_w_gate'], p['ffn2_w_up'], p['ffn2_w_down'], final_g,
                      final_norm=(l == depth - 1))

    return (xp.reshape(x_prompt.shape).astype(x_prompt.dtype), xs.reshape(x_sample.shape).astype(x_sample.dtype),
            jnp.stack(wkv_p), jnp.stack(shift_p), jnp.stack(conv_p), jnp.stack(mk_p), jnp.stack(mv_p),
            jnp.stack(wkv_s), jnp.stack(shift_s), jnp.stack(conv_s))
```

```python
import functools
import math

import jax
import jax.numpy as jnp
from jax import lax
from jax.experimental import pallas as pl
from jax.experimental.pallas import tpu as pltpu

F32 = jnp.float32
BF16 = jnp.bfloat16

HEAD_DIM = 64
PAIR = 2 * HEAD_DIM
N_XATTN_HEADS = 4
CONV_WIDTH = 31
SUBLANES = 8
CONV_PAD = 32
LORA_DECAY = 64
LORA_A = 64
RMS_EPS = 1e-6
LN_EPS = 1e-5
GN_EPS = 64e-5
VMEM_LIMIT = 56 * 1024 * 1024


def _params(*sem):
    return pltpu.CompilerParams(dimension_semantics=sem, vmem_limit_bytes=VMEM_LIMIT)


def _rms(x, g):
    return x * lax.rsqrt(jnp.mean(x * x, axis=-1, keepdims=True) + RMS_EPS) * g


def _mm(a, b):
    return jnp.dot(a, b, preferred_element_type=F32)


def _nt(a, b):
    return lax.dot_general(a, b, (((1,), (1,)), ((), ())), preferred_element_type=F32)


def _tn(a, b):
    return lax.dot_general(a, b, (((0,), (0,)), ((), ())), preferred_element_type=F32)


def _split(x):
    hi = x.astype(BF16)
    return hi, (x - hi.astype(F32)).astype(BF16)


def _exact_lhs_dot(sel, x):
    hi, lo = _split(x)
    return _mm(sel, hi) + _mm(sel, lo)


def _dot3(x, w):
    x_hi, x_lo = _split(x)
    w_hi, w_lo = _split(w)
    return _mm(x_hi, w_hi) + _mm(x_lo, w_hi) + _mm(x_hi, w_lo)


def _dot1(x, w):
    return _mm(x.astype(BF16), w.astype(BF16))


def _const_spec(shape):
    nd = len(shape)
    return pl.BlockSpec(shape, lambda *_: (0,) * nd)


def _ffn_body(xa_ref, xb_ref, wg_ref, wu_ref, wd_ref, fg_ref, oa_ref, ob_ref, *, final_norm, tiles_a):
    def tile(x_ref, o_ref):
        x = x_ref[...]
        xb = x.astype(BF16)
        inv = lax.rsqrt(jnp.mean(x * x, axis=-1, keepdims=True) + RMS_EPS)
        hg = _mm(xb, wg_ref[...]) * inv
        hu = _mm(xb, wu_ref[...]) * inv
        h = (hg * jax.nn.sigmoid(hg) * hu).astype(BF16)
        y = x + 0.5 * _mm(h, wd_ref[...])
        if final_norm:
            y = _rms(y, fg_ref[...])
        o_ref[...] = y

    in_a = pl.program_id(0) < tiles_a
    pl.when(in_a)(functools.partial(tile, xa_ref, oa_ref))
    pl.when(jnp.logical_not(in_a))(functools.partial(tile, xb_ref, ob_ref))


def _resident_spec(shape):
    nd = len(shape)
    return pl.BlockSpec(shape, lambda *_: (0,) * nd, pipeline_mode=pl.Buffered(1))


def _ffn(xa, xb, w_gate, w_up, w_down, final_g, *, final_norm, tm=512):
    (na, d), nb = xa.shape, xb.shape[0]
    tm = min(tm, na, nb)
    assert na % tm == 0 and nb % tm == 0
    ta, tb = na // tm, nb // tm
    spec_a = pl.BlockSpec((tm, d), lambda i: (jnp.minimum(i, ta - 1), 0))
    spec_b = pl.BlockSpec((tm, d), lambda i: (jnp.maximum(i - ta, 0), 0))
    return pl.pallas_call(
        functools.partial(_ffn_body, final_norm=final_norm, tiles_a=ta),
        grid=(ta + tb,),
        in_specs=[
            spec_a, spec_b,
            _resident_spec(w_gate.shape), _resident_spec(w_up.shape), _resident_spec(w_down.shape),
            _const_spec((1, d)),
        ],
        out_specs=[spec_a, spec_b],
        out_shape=[jax.ShapeDtypeStruct((na, d), F32), jax.ShapeDtypeStruct((nb, d), F32)],
        compiler_params=_params("arbitrary"),
        name="ffn_final" if final_norm else "ffn",
    )(xa, xb, w_gate, w_up, w_down, final_g)


def _memkv_body(m_ref, wt_ref, kt_ref, vt_ref):
    dk = kt_ref.shape[1]
    kvt = _nt(wt_ref[...], m_ref[0].astype(BF16))
    kt_ref[0] = kvt[:dk]
    vt_ref[0] = kvt[dk:]


def _memkv(mem, wt):
    b, n_mem, d = mem.shape
    dk = wt.shape[0] // 2
    return pl.pallas_call(
        _memkv_body,
        grid=(b,),
        in_specs=[pl.BlockSpec((1, n_mem, d), lambda i: (i, 0, 0)), _const_spec(wt.shape)],
        out_specs=[pl.BlockSpec((1, dk, n_mem), lambda i: (i, 0, 0))] * 2,
        out_shape=[jax.ShapeDtypeStruct((b, dk, n_mem), F32)] * 2,
        compiler_params=_params("parallel"),
        name="memkv",
    )(mem, wt)


def _rwkv_body(x_ref, mixg_ref, ws_ref, sprev_ref, mu_ref, w0_ref, wdu_ref, a0_ref, wau_ref, wgu_ref,
               kk_ref, ka_ref, rk_ref, gng_ref, gnb_ref, s0_ref, hsum_ref, tri_ref, blk_ref,
               wc_ref, glub_ref, cprev_ref, cw_ref, cb_ref, lng_ref, lnb_ref,
               og_ref, sout_ref, wkv_ref, yc_ref, cnew_ref,
               carry, st, at_s, rt_s, bb_s, kb_s, bh_s, kh_s, v_s, wt_s, bonus_s, g_s, o_s,
               gm_s, hm_s, rp_s, oi_s, ubuf, ushift, yc_s, *, chunk, unroll1):
    bb_n, tt, d = x_ref.shape
    m = bb_n * tt
    d_shift = ws_ref.shape[1]
    d_r = w0_ref.shape[1]
    dc = cw_ref.shape[1]
    n_pairs = d_r // PAIR
    n_ck = tt // chunk
    n_blk = m // chunk
    log_chunk = int(math.log2(chunk))
    c2 = 2 * chunk
    conv_lo = CONV_PAD - (CONV_WIDTH - 1)
    t = pl.program_id(1)

    @pl.when(t == 0)
    def _():
        carry[...] = sprev_ref[...]
        ubuf[:, conv_lo:CONV_PAD, :] = cprev_ref[...]
        zero = jnp.zeros((HEAD_DIM, HEAD_DIM), F32)
        for b_i in range(bb_n):
            for p in range(n_pairs):
                top = jnp.concatenate([s0_ref[b_i, 2 * p], zero], axis=1)
                bot = jnp.concatenate([zero, s0_ref[b_i, 2 * p + 1]], axis=1)
                st[b_i, p] = jnp.concatenate([top, bot], axis=0)

    n_part = 2 if (bb_n == 1 and m >= 4 * chunk) else 1
    pm_ = m // n_part
    parts = range(n_part)
    psl = [slice(i * pm_, (i + 1) * pm_) for i in parts]
    x2d = x_ref[...].reshape(m, d)
    xn = [_rms(x2d[psl[i]], mixg_ref[...]).astype(BF16) for i in parts]
    zc = [_mm(xn[i], wc_ref[...]) + glub_ref[...] for i in parts]
    zs = [_mm(xn[i], ws_ref[...]) for i in parts]
    u = [zc[i][:, :dc] * jax.nn.sigmoid(zc[i][:, dc:]) for i in parts]
    if n_part == 1:
        ubuf[:, CONV_PAD:CONV_PAD + tt, :] = u[0].reshape(bb_n, tt, dc)
    else:
        for i in parts:
            ubuf[0, CONV_PAD + i * pm_:CONV_PAD + (i + 1) * pm_, :] = u[i]
    for sh in range(1, SUBLANES):
        ushift[sh - 1] = ubuf[:, sh:sh + tt + CONV_PAD - SUBLANES, :]

    if n_part == 1:
        prev0 = jnp.broadcast_to(carry[...], (bb_n, tt, d_shift)).reshape(m, d_shift)
        row = lax.broadcasted_iota(jnp.int32, (m, d_shift), 0)
        prev = [jnp.where((row & (tt - 1)) == 0, prev0, pltpu.roll(zs[0], 1, axis=0))]
        last = zs[0].reshape(bb_n, tt, d_shift)[:, tt - 1:tt, :]
    else:
        row = lax.broadcasted_iota(jnp.int32, (pm_, d_shift), 0)
        first = [carry[0]] + [zs[i][pm_ - 1:pm_, :] for i in parts[:-1]]
        prev = [jnp.where(row == 0, first[i], pltpu.roll(zs[i], 1, axis=0)) for i in parts]
        last = zs[-1][pm_ - 1:pm_, :].reshape(1, 1, d_shift)
    carry[...] = last
    sout_ref[...] = last
    xm = [zs[i] + (prev[i] - zs[i]) * mu_ref[...] for i in parts]

    c1, c2_, c3 = d_r, 2 * d_r, 3 * d_r
    c4 = c3 + LORA_DECAY
    c5 = c4 + LORA_A
    r = [xm[i][:, :c1] for i in parts]
    k = [xm[i][:, c1:c2_] for i in parts]
    v = [xm[i][:, c2_:c3] for i in parts]
    dec_in = [w0_ref[...] + _dot3(jnp.tanh(xm[i][:, c3:c4]), wdu_ref[...]) for i in parts]
    a_in = [a0_ref[...] + _dot1(xm[i][:, c4:c5], wau_ref[...]) for i in parts]
    for i in parts:
        g_s[psl[i], :] = _dot1(jax.nn.sigmoid(xm[i][:, c5:]), wgu_ref[...])
    ld = [-jnp.exp(-(jnp.maximum(-dec_in[i], 0.0) + jnp.log(1.0 + jnp.exp(-jnp.abs(dec_in[i])))) - 0.5)
          for i in parts]
    a = [jax.nn.sigmoid(a_in[i]) for i in parts]
    kh = [k[i] * (1.0 + (a[i] - 1.0) * ka_ref[...]) for i in parts]

    head_sum = hsum_ref[...]
    kkraw = [k[i] * kk_ref[...] for i in parts]
    kk_ss = [_mm((kkraw[i] * kkraw[i]).astype(BF16), head_sum) for i in parts]
    rk_sum = [_mm((r[i] * kh[i] * rk_ref[...]).astype(BF16), head_sum) for i in parts]
    kk = [kkraw[i] / jnp.maximum(jnp.sqrt(kk_ss[i]), 1e-12) for i in parts]
    b = [kk[i] * a[i] for i in parts]
    tdt = at_s.dtype
    for i in parts:
        bonus_s[psl[i], :] = rk_sum[i] * v[i]
        v_s[psl[i], :] = v[i].astype(tdt)

    slab = tri_ref.shape[0]
    for s0 in range(0, pm_, slab):
        sl = slice(s0, s0 + slab)
        ld_c = [ld[i][sl] for i in parts]
        cum = [_exact_lhs_dot(tri_ref[...], ld_c[i]) for i in parts]
        tot = [_exact_lhs_dot(blk_ref[...], ld_c[i]) for i in parts]
        for i in parts:
            gl = slice(i * pm_ + s0, i * pm_ + s0 + slab)
            w_inv = jnp.exp(-cum[i])
            w_rem = jnp.exp(tot[i] - cum[i])
            at_s[gl, :] = (-kk[i][sl] * jnp.exp(cum[i] - ld_c[i])).astype(tdt)
            rt_s[gl, :] = (r[i][sl] * jnp.exp(cum[i])).astype(tdt)
            bb_s[gl, :] = (b[i][sl] * w_inv).astype(tdt)
            kb_s[gl, :] = (kh[i][sl] * w_inv).astype(tdt)
            bh_s[gl, :] = (b[i][sl] * w_rem).astype(tdt)
            kh_s[gl, :] = (kh[i][sl] * w_rem).astype(tdt)
            wt_s[gl, :] = jnp.exp(tot[i])

    lane_lo = lax.broadcasted_iota(jnp.int32, (chunk, PAIR), 1) < HEAD_DIM
    r2 = lax.broadcasted_iota(jnp.int32, (c2, c2), 0)
    q2 = lax.broadcasted_iota(jnp.int32, (c2, c2), 1)
    strict = (q2 & (chunk - 1)) < (r2 & (chunk - 1))
    incl = (q2 & (chunk - 1)) <= (r2 & (chunk - 1))
    eye2 = (r2 == q2).astype(F32)
    wide = c2 % PAIR == 0

    def stack(ref, rows, ls):
        x = ref[rows, ls]
        return jnp.concatenate([jnp.where(lane_lo, x, 0.0), jnp.where(lane_lo, 0.0, x)], axis=0).astype(BF16)

    def chain_mats(chains):
        n = range(len(chains))

        def stacks(ref):
            return [stack(ref, slice(blk * chunk, (blk + 1) * chunk), slice(p * PAIR, (p + 1) * PAIR))
                    for blk, p in chains]

        at, rt, bbm, kbm = stacks(at_s), stacks(rt_s), stacks(bb_s), stacks(kb_s)
        bhm, khm, vm = stacks(bh_s), stacks(kh_s), stacks(v_s)
        if wide:
            a4 = [_nt(jnp.concatenate([at[c_], rt[c_]], axis=0), jnp.concatenate([bbm[c_], kbm[c_]], axis=0))
                  for c_ in n]
            a_ab = [jnp.where(strict, a4[c_][:c2, :c2], 0.0) for c_ in n]
            a_ak = [jnp.where(strict, a4[c_][:c2, c2:], 0.0).astype(BF16) for c_ in n]
            a_rb = [jnp.where(incl, a4[c_][c2:, :c2], 0.0).astype(BF16) for c_ in n]
            a_rk = [jnp.where(incl, a4[c_][c2:, c2:], 0.0).astype(BF16) for c_ in n]
        else:
            a_ab = [jnp.where(strict, _nt(at[c_], bbm[c_]), 0.0) for c_ in n]
            a_ak = [jnp.where(strict, _nt(at[c_], kbm[c_]), 0.0).astype(BF16) for c_ in n]
            a_rb = [jnp.where(incl, _nt(rt[c_], bbm[c_]), 0.0).astype(BF16) for c_ in n]
            a_rk = [jnp.where(incl, _nt(rt[c_], kbm[c_]), 0.0).astype(BF16) for c_ in n]
        tinv = [eye2 + a_ab[c_] for c_ in n]
        ap = [a_ab[c_].astype(BF16) for c_ in n]
        ap = [_mm(ap[c_], ap[c_]).astype(BF16) for c_ in n]
        for i_sq in range(1, log_chunk):
            if i_sq == log_chunk - 1:
                tinv = [tinv[c_] + _mm(ap[c_], tinv[c_].astype(BF16)) for c_ in n]
            elif wide:
                x = [_mm(ap[c_], jnp.concatenate([ap[c_], tinv[c_].astype(BF16)], axis=1)) for c_ in n]
                ap = [x[c_][:, :c2].astype(BF16) for c_ in n]
                tinv = [tinv[c_] + x[c_][:, c2:] for c_ in n]
            else:
                tinv = [tinv[c_] + _mm(ap[c_], tinv[c_].astype(BF16)) for c_ in n]
                ap = [_mm(ap[c_], ap[c_]).astype(BF16) for c_ in n]
        av = [_mm(a_ak[c_], vm[c_]).astype(BF16) for c_ in n]
        pq = [_mm(tinv[c_].astype(BF16), jnp.concatenate([av[c_], at[c_]], axis=1)) for c_ in n]
        return rt, bhm, khm, vm, a_rb, a_rk, pq

    def phase1_step(i):
        chains = [(unroll1 * i + j, p) for j in range(unroll1) for p in range(n_pairs)]
        n = range(len(chains))
        rt, bhm, khm, vm, a_rb, a_rk, pq = chain_mats(chains)
        pm = [pq[c_][:, :PAIR].astype(BF16) for c_ in n]
        qm = [pq[c_][:, PAIR:].astype(BF16) for c_ in n]
        for c_, (blk, p) in enumerate(chains):
            gm_s[blk, p] = _tn(qm[c_], bhm[c_]).astype(BF16)
        for c_, (blk, p) in enumerate(chains):
            hm_s[blk, p] = _tn(jnp.concatenate([pm[c_], vm[c_]], axis=0),
                               jnp.concatenate([bhm[c_], khm[c_]], axis=0))
        rq = [_mm(a_rb[c_], jnp.concatenate([qm[c_], pm[c_]], axis=1)) for c_ in n]
        for c_, (blk, p) in enumerate(chains):
            rp_s[blk, p] = (rt[c_].astype(F32) + rq[c_][:, :PAIR]).astype(BF16)
        for c_, (blk, p) in enumerate(chains):
            oi_s[blk, p] = rq[c_][:, PAIR:] + _mm(a_rk[c_], vm[c_])

    def conv_block(blk):
        b_i, t0 = blk // n_ck, (blk % n_ck) * chunk
        acc = jnp.zeros((chunk, dc), F32) + cb_ref[...]
        for kx in range(CONV_WIDTH):
            off = conv_lo + kx
            sh, base = off % SUBLANES, off - off % SUBLANES
            rows = slice(t0 + base, t0 + base + chunk)
            win = ubuf[b_i, rows, :] if sh == 0 else ushift[sh - 1, b_i, rows, :]
            acc = acc + win * cw_ref[kx:kx + 1, :]
        cm = jnp.mean(acc, axis=-1, keepdims=True)
        cc = acc - cm
        cv = jnp.mean(cc * cc, axis=-1, keepdims=True)
        cn = cc * lax.rsqrt(cv + LN_EPS) * lng_ref[...] + lnb_ref[...]
        yc_s[blk * chunk:(blk + 1) * chunk, :] = cn * jax.nn.sigmoid(cn)

    def phase2_step(i):
        conv_block(i)
        chains = [(i, p) for p in range(n_pairs)]
        n = range(len(chains))
        row0 = [blk * chunk for blk, _ in chains]
        ls = [slice(p * PAIR, (p + 1) * PAIR) for _, p in chains]
        s = [st[blk // n_ck, p] for blk, p in chains]
        sb = [s[c_].astype(BF16) for c_ in n]
        s_new = [s[c_] * wt_s[row0[c_]:row0[c_] + 1, ls[c_]] + _mm(sb[c_], gm_s[blk, p]) + hm_s[blk, p]
                 for c_, (blk, p) in enumerate(chains)]
        o_bd = [oi_s[blk, p] + _nt(rp_s[blk, p], sb[c_]) for c_, (blk, p) in enumerate(chains)]
        for c_, (blk, p) in enumerate(chains):
            st[blk // n_ck, p] = s_new[c_]
        for c_ in n:
            o_s[row0[c_]:row0[c_] + chunk, ls[c_]] = o_bd[c_][:chunk] + o_bd[c_][chunk:]

    def single_chunk_step(i):
        for j in range(unroll1):
            conv_block(unroll1 * i + j)
        chains = [(unroll1 * i + j, p) for j in range(unroll1) for p in range(n_pairs)]
        n = range(len(chains))
        row0 = [blk * chunk for blk, _ in chains]
        ls = [slice(p * PAIR, (p + 1) * PAIR) for _, p in chains]
        rt, bhm, khm, vm, a_rb, a_rk, pq = chain_mats(chains)
        s = [st[blk, p] for blk, p in chains]
        sb = [s[c_].astype(BF16) for c_ in n]
        qr = [_nt(jnp.concatenate([pq[c_][:, PAIR:].astype(BF16), rt[c_]], axis=0), sb[c_]) for c_ in n]
        u = [(pq[c_][:, :PAIR] + qr[c_][:c2]).astype(BF16) for c_ in n]
        o_bd = [qr[c_][c2:] + _mm(a_rb[c_], u[c_]) + _mm(a_rk[c_], vm[c_]) for c_ in n]
        s_new = [s[c_] * wt_s[row0[c_]:row0[c_] + 1, ls[c_]]
                 + _tn(jnp.concatenate([u[c_], vm[c_]], axis=0), jnp.concatenate([bhm[c_], khm[c_]], axis=0))
                 for c_ in n]
        for c_, (blk, p) in enumerate(chains):
            st[blk, p] = s_new[c_]
        for c_ in n:
            o_s[row0[c_]:row0[c_] + chunk, ls[c_]] = o_bd[c_][:chunk] + o_bd[c_][chunk:]

    if n_ck == 1:
        for i in range(n_blk // unroll1):
            single_chunk_step(i)
    else:
        for i in range(n_blk // unroll1):
            phase1_step(i)
        for i in range(n_blk):
            phase2_step(i)

    o = o_s[...]
    mean = _mm(o.astype(BF16), head_sum) * (1.0 / HEAD_DIM)
    oc = o - mean
    var = _mm((oc * oc).astype(BF16), head_sum) * (1.0 / HEAD_DIM)
    o = oc * lax.rsqrt(var + GN_EPS) * gng_ref[...] + gnb_ref[...] + bonus_s[...]
    og_ref[...] = (o * g_s[...]).astype(BF16).reshape(bb_n, tt, d_r)

    yc_ref[...] = yc_s[...].astype(BF16).reshape(bb_n, tt, dc)
    tail = ubuf[:, conv_lo + tt:CONV_PAD + tt, :]
    cnew_ref[...] = tail
    ubuf[:, conv_lo:CONV_PAD, :] = tail

    @pl.when(t == pl.num_programs(1) - 1)
    def _():
        for b_i in range(bb_n):
            for p in range(n_pairs):
                s = st[b_i, p]
                wkv_ref[b_i, 2 * p] = s[:HEAD_DIM, :HEAD_DIM]
                wkv_ref[b_i, 2 * p + 1] = s[HEAD_DIM:, HEAD_DIM:]


def _rwkv(x, mixg, ws, sprev, mu, w0, wdu, a0, wau, wgu, kk, ka, rk, gng, gnb, s0,
          wc, glub, cprev, cw, cb, lng, lnb, *, bb, tt, chunk, unroll1):
    b, t_len, d = x.shape
    d_shift = ws.shape[1]
    d_r = w0.shape[1]
    dc = cw.shape[1]
    n_heads = d_r // HEAD_DIM
    n_pairs = d_r // PAIR
    m = bb * tt
    n_blk = m // chunk
    n_kept = n_blk if tt > chunk else 1
    assert b % bb == 0 and t_len % tt == 0 and tt % chunk == 0
    assert n_blk % unroll1 == 0
    assert chunk & (chunk - 1) == 0 and tt & (tt - 1) == 0 and chunk % 8 == 0
    tok = pltpu.VMEM((m, d_r), F32)
    tok_mm = pltpu.VMEM((m, d_r), BF16 if chunk % (2 * SUBLANES) == 0 else F32)
    lane = jnp.arange(d_r)
    head_sum = (lane[:, None] // HEAD_DIM == lane[None, :] // HEAD_DIM).astype(BF16)
    rows = jnp.arange(chunk if chunk >= HEAD_DIM else m)
    same_chunk = rows[:, None] // chunk == rows[None, :] // chunk
    tri = (same_chunk & (rows[None, :] <= rows[:, None])).astype(BF16)
    blk = same_chunk.astype(BF16)
    return pl.pallas_call(
        functools.partial(_rwkv_body, chunk=chunk, unroll1=unroll1),
        grid=(b // bb, t_len // tt),
        in_specs=[
            pl.BlockSpec((bb, tt, d), lambda i, j: (i, j, 0)),
            _const_spec(mixg.shape), _const_spec(ws.shape),
            pl.BlockSpec((bb, 1, d_shift), lambda i, j: (i, 0, 0)),
            _const_spec(mu.shape), _const_spec(w0.shape), _const_spec(wdu.shape), _const_spec(a0.shape),
            _const_spec(wau.shape), _const_spec(wgu.shape), _const_spec(kk.shape), _const_spec(ka.shape),
            _const_spec(rk.shape), _const_spec(gng.shape), _const_spec(gnb.shape),
            pl.BlockSpec((bb, n_heads, HEAD_DIM, HEAD_DIM), lambda i, j: (i, 0, 0, 0)),
            _const_spec(head_sum.shape), _const_spec(tri.shape), _const_spec(blk.shape),
            _const_spec(wc.shape), _const_spec(glub.shape),
            pl.BlockSpec((bb, CONV_WIDTH - 1, dc), lambda i, j: (i, 0, 0)),
            _const_spec(cw.shape), _const_spec(cb.shape), _const_spec(lng.shape), _const_spec(lnb.shape),
        ],
        out_specs=[
            pl.BlockSpec((bb, tt, d_r), lambda i, j: (i, j, 0)),
            pl.BlockSpec((bb, 1, d_shift), lambda i, j: (i, 0, 0)),
            pl.BlockSpec((bb, n_heads, HEAD_DIM, HEAD_DIM), lambda i, j: (i, 0, 0, 0)),
            pl.BlockSpec((bb, tt, dc), lambda i, j: (i, j, 0)),
            pl.BlockSpec((bb, CONV_WIDTH - 1, dc), lambda i, j: (i, 0, 0)),
        ],
        out_shape=[
            jax.ShapeDtypeStruct((b, t_len, d_r), BF16),
            jax.ShapeDtypeStruct((b, 1, d_shift), F32),
            jax.ShapeDtypeStruct((b, n_heads, HEAD_DIM, HEAD_DIM), F32),
            jax.ShapeDtypeStruct((b, t_len, dc), BF16),
            jax.ShapeDtypeStruct((b, CONV_WIDTH - 1, dc), F32),
        ],
        scratch_shapes=[
            pltpu.VMEM((bb, 1, d_shift), F32),
            pltpu.VMEM((bb, n_pairs, PAIR, PAIR), F32),
        ] + [tok_mm] * 7 + [tok] * 4 + [
            pltpu.VMEM((n_kept, n_pairs, PAIR, PAIR), BF16),
            pltpu.VMEM((n_kept, n_pairs, PAIR, PAIR), F32),
            pltpu.VMEM((n_kept, n_pairs, 2 * chunk, PAIR), BF16),
            pltpu.VMEM((n_kept, n_pairs, 2 * chunk, PAIR), F32),
            pltpu.VMEM((bb, CONV_PAD + tt, dc), F32),
            pltpu.VMEM((SUBLANES - 1, bb, CONV_PAD + tt - SUBLANES, dc), F32),
            pltpu.VMEM((m, dc), F32),
        ],
        compiler_params=_params("parallel", "arbitrary"),
        name="rwkv_conv",
    )(x, mixg, ws, sprev, mu, w0, wdu, a0, wau, wgu, kk, ka, rk, gng, gnb, s0, head_sum, tri, blk,
      wc, glub, cprev, cw, cb, lng, lnb)


def _merge_body(x_ref, wq_ref, wg_ref, kt_ref, vt_ref, ya_ref, yb_ref, wro_ref, wco_ref, wxo_ref, wo_ref, o_ref,
                *, t_len, rows, group):
    tm, d = x_ref.shape
    dh = kt_ref.shape[1] // N_XATTN_HEADS
    x = x_ref[...]
    xb = x.astype(BF16)
    inv = lax.rsqrt(jnp.mean(x * x, axis=-1, keepdims=True) + RMS_EPS)

    q = (_mm(xb, wq_ref[...]) * (inv * dh ** -0.5)).astype(BF16)
    n_blocks = tm // rows
    hs = lambda h: slice(h * dh, (h + 1) * dh)
    seq = lambda r: (r * rows) // t_len
    pieces = []
    for g0 in range(0, n_blocks, group):
        blocks = range(g0, min(g0 + group, n_blocks))
        chains = [(r, h) for r in blocks for h in range(N_XATTN_HEADS)]
        s = [_mm(q[r * rows:(r + 1) * rows, hs(h)], kt_ref[seq(r), hs(h), :].astype(BF16)) for r, h in chains]
        p = [jnp.exp(v - jnp.max(v, axis=-1, keepdims=True)) for v in s]
        l = [jnp.sum(v, axis=-1, keepdims=True) for v in p]
        o = [_nt(p[i].astype(BF16), vt_ref[seq(r), hs(h), :].astype(BF16)) / l[i] for i, (r, h) in enumerate(chains)]
        for j in range(len(blocks)):
            pieces.append(jnp.concatenate(o[j * N_XATTN_HEADS:(j + 1) * N_XATTN_HEADS], axis=-1))
    oc = jnp.concatenate(pieces, axis=0).astype(BF16)

    merged = jnp.zeros_like(x)
    for i, (y, w_ref) in enumerate(((ya_ref[...], wro_ref), (yb_ref[...], wco_ref), (oc, wxo_ref))):
        gate = jax.nn.sigmoid(_mm(xb, wg_ref[:, i * d:(i + 1) * d]) * inv)
        merged = merged + gate * _mm(y, w_ref[...])
    o_ref[...] = x + _mm(merged.astype(BF16), wo_ref[...])


def _merge(x, wq, wg, kt, vt, ya, yb, wro, wco, wxo, wo, *, t_len, tm, rows, group):
    n, d = x.shape
    dx, n_mem = kt.shape[1:]
    seqs = max(tm // t_len, 1)
    assert n % tm == 0 and tm % rows == 0 and (tm % t_len == 0 or t_len % tm == 0) and rows <= t_len
    row = lambda w: pl.BlockSpec((tm, w), lambda i: (i, 0))
    mem = pl.BlockSpec((seqs, dx, n_mem), lambda i: ((i * tm) // (t_len * seqs), 0, 0))
    return pl.pallas_call(
        functools.partial(_merge_body, t_len=t_len, rows=rows, group=group),
        grid=(n // tm,),
        in_specs=[row(d), _resident_spec(wq.shape), _resident_spec(wg.shape), mem, mem, row(ya.shape[1]),
                  row(yb.shape[1]), _resident_spec(wro.shape), _resident_spec(wco.shape),
                  _resident_spec(wxo.shape), _resident_spec(wo.shape)],
        out_specs=row(d),
        out_shape=jax.ShapeDtypeStruct((n, d), F32),
        compiler_params=_params("parallel"),
        name="merge",
    )(x, wq, wg, kt, vt, ya, yb, wro, wco, wxo, wo)


def _mix(x1, shift_prev, conv_prev, wkv0, mem_kt, mem_vt, p, *, bb, tt, rw, mg):
    b, t_len, d = x1.shape
    n = b * t_len
    og, shift, wkv, cb, conv = _rwkv(
        x1, p['mix_norm'], p['w_s'], shift_prev[:, None, :], p['mu_shift'], p['w0'], p['w_decay_up'], p['a0'],
        p['w_a_up'], p['w_g_up'], p['k_k'], p['k_a'], p['r_k'], p['gn_g'], p['gn_b'], wkv0,
        p['w_c'], p['glu_b'], conv_prev, p['conv_w'], p['conv_b'], p['conv_ln_g'], p['conv_ln_b'],
        bb=bb, tt=tt, **rw)
    x2 = _merge(x1.reshape(n, d), p['w_q'], p['w_g'], mem_kt, mem_vt, og.reshape(n, -1), cb.reshape(n, -1),
                p['w_rwkv_out'], p['w_conv_out'], p['w_xattn_out'], p['w_o'], t_len=t_len, **mg)
    return x2, wkv, shift[:, 0, :], conv


def kernel(x_prompt, mem_prompt, x_sample, state_wkv, state_shift, state_conv, cache_mem_k, cache_mem_v,
           ffn1_norm, ffn1_w_up, ffn1_w_down, mix_norm, w_in, mu_shift, w0, w_decay_up, a0, w_a_up, w_g_up,
           k_k, k_a, r_k, gn_g, gn_b, w_rwkv_out, glu_b, conv_w, conv_b, conv_ln_g, conv_ln_b, w_conv_out,
           w_mem_kv, w_xattn_out, w_o, ffn2_norm, ffn2_w_up, ffn2_w_down, final_norm):
    depth = w_in.shape[0]
    d_model = w_in.shape[1]
    d_r = w0.shape[1]
    d_shift = mu_shift.shape[1]
    d_conv = conv_w.shape[2]
    d_x = w_xattn_out.shape[1]
    n_heads = d_r // HEAD_DIM
    o1 = d_shift
    o2 = o1 + 2 * d_conv
    o3 = o2 + d_x
    d_ff = ffn1_w_down.shape[1]
    row = lambda a: a.astype(F32).reshape(1, -1)
    scaled = lambda g, w: (g.astype(F32)[:, None] * w.astype(F32)).astype(BF16)
    final_g = row(final_norm)

    layers = []
    for l in range(depth):
        layers.append({
            'ffn1_w_gate': scaled(ffn1_norm[l], ffn1_w_up[l][:, :d_ff]),
            'ffn1_w_up': scaled(ffn1_norm[l], ffn1_w_up[l][:, d_ff:]),
            'ffn1_w_down': ffn1_w_down[l].astype(BF16), 'mix_norm': row(mix_norm[l]),
            'w_s': w_in[l, :, :o1].astype(BF16), 'w_c': w_in[l, :, o1:o2].astype(BF16),
            'w_q': scaled(mix_norm[l], w_in[l, :, o2:o3]), 'w_g': scaled(mix_norm[l], w_in[l, :, o3:]),
            'mu_shift': row(mu_shift[l]), 'w0': row(w0[l]), 'w_decay_up': w_decay_up[l].astype(F32),
            'a0': row(a0[l]), 'w_a_up': w_a_up[l].astype(F32), 'w_g_up': w_g_up[l].astype(F32),
            'k_k': row(k_k[l]), 'k_a': row(k_a[l]), 'r_k': row(r_k[l]), 'gn_g': row(gn_g[l]),
            'gn_b': row(gn_b[l]), 'w_rwkv_out': w_rwkv_out[l].astype(BF16), 'glu_b': row(glu_b[l]),
            'conv_w': conv_w[l].astype(F32), 'conv_b': row(conv_b[l]), 'conv_ln_g': row(conv_ln_g[l]),
            'conv_ln_b': row(conv_ln_b[l]), 'w_conv_out': w_conv_out[l].astype(BF16),
            'w_mem_kv_t': w_mem_kv[l].T.astype(BF16), 'w_xattn_out': w_xattn_out[l].astype(BF16),
            'w_o': w_o[l].astype(BF16),
            'ffn2_w_gate': scaled(ffn2_norm[l], ffn2_w_up[l][:, :d_ff]),
            'ffn2_w_up': scaled(ffn2_norm[l], ffn2_w_up[l][:, d_ff:]), 'ffn2_w_down': ffn2_w_down[l].astype(BF16),
        })

    bp, tp, _ = x_prompt.shape
    bs, ts, _ = x_sample.shape
    n_mem = mem_prompt.shape[1]
    xp = x_prompt.astype(F32).reshape(bp * tp, d_model)
    xs = x_sample.astype(F32).reshape(bs * ts, d_model)
    wkv_p, shift_p, conv_p, mk_p, mv_p = [], [], [], [], []
    wkv_s, shift_s, conv_s = [], [], []
    for l in range(depth):
        p = layers[l]
        x1p, x1s = _ffn(xp, xs, p['ffn1_w_gate'], p['ffn1_w_up'], p['ffn1_w_down'], final_g, final_norm=False)

        mkt, mvt = _memkv(mem_prompt.astype(F32), p['w_mem_kv_t'])
        x2p, wkv, sh, cv = _mix(
            x1p.reshape(bp, tp, d_model), jnp.zeros((bp, d_shift), F32),
            jnp.zeros((bp, CONV_WIDTH - 1, d_conv), F32), jnp.zeros((bp, n_heads, HEAD_DIM, HEAD_DIM), F32),
            mkt, mvt, p, bb=1, tt=512, rw=dict(chunk=64, unroll1=8), mg=dict(tm=512, rows=512, group=1))
        wkv_p.append(wkv)
        shift_p.append(sh)
        conv_p.append(cv)
        heads_t = lambda a: jnp.transpose(a.reshape(bp, N_XATTN_HEADS, d_x // N_XATTN_HEADS, n_mem), (0, 3, 1, 2))
        mk_p.append(heads_t(mkt))
        mv_p.append(heads_t(mvt))

        kt = jnp.transpose(cache_mem_k[l].astype(F32), (0, 2, 3, 1)).reshape(bs, d_x, n_mem)
        vt = jnp.transpose(cache_mem_v[l].astype(F32), (0, 2, 3, 1)).reshape(bs, d_x, n_mem)
        x2s, wkv, sh, cv = _mix(
            x1s.reshape(bs, ts, d_model), state_shift[l].astype(F32), state_conv[l].astype(F32),
            state_wkv[l].astype(F32), kt, vt, p, bb=16, tt=ts, rw=dict(chunk=ts, unroll1=16),
            mg=dict(tm=16 * ts, rows=ts, group=4))
        wkv_s.append(wkv)
        shift_s.append(sh)
        conv_s.append(cv)

        xp, xs = _ffn(x2p, x2s, p['ffn2_w_gate'], p['ffn2_w_up'], p['ffn2_w_down'], final_g,
                      final_norm=(l == depth - 1))

    return (xp.reshape(x_prompt.shape).astype(x_prompt.dtype), xs.reshape(x_sample.shape).astype(x_sample.dtype),
            jnp.stack(wkv_p), jnp.stack(shift_p), jnp.stack(conv_p), jnp.stack(mk_p), jnp.stack(mv_p),
            jnp.stack(wkv_s), jnp.stack(shift_s), jnp.stack(conv_s))
```

```python
import functools
import math

import jax
import jax.numpy as jnp
from jax import lax
from jax.experimental import pallas as pl
from jax.experimental.pallas import tpu as pltpu

F32 = jnp.float32
BF16 = jnp.bfloat16

HEAD_DIM = 64
PAIR = 2 * HEAD_DIM
N_XATTN_HEADS = 4
CONV_WIDTH = 31
SUBLANES = 8
CONV_PAD = 32
LORA_DECAY = 64
LORA_A = 64
RMS_EPS = 1e-6
LN_EPS = 1e-5
GN_EPS = 64e-5
VMEM_LIMIT = 56 * 1024 * 1024


def _params(*sem):
    return pltpu.CompilerParams(dimension_semantics=sem, vmem_limit_bytes=VMEM_LIMIT)


def _rms(x, g):
    return x * lax.rsqrt(jnp.mean(x * x, axis=-1, keepdims=True) + RMS_EPS) * g


def _mm(a, b):
    return jnp.dot(a, b, preferred_element_type=F32)


def _nt(a, b):
    return lax.dot_general(a, b, (((1,), (1,)), ((), ())), preferred_element_type=F32)


def _tn(a, b):
    return lax.dot_general(a, b, (((0,), (0,)), ((), ())), preferred_element_type=F32)


def _split(x):
    hi = x.astype(BF16)
    return hi, (x - hi.astype(F32)).astype(BF16)


def _exact_lhs_dot(sel, x):
    hi, lo = _split(x)
    return _mm(sel, hi) + _mm(sel, lo)


def _dot3(x, w):
    x_hi, x_lo = _split(x)
    w_hi, w_lo = _split(w)
    return _mm(x_hi, w_hi) + _mm(x_lo, w_hi) + _mm(x_hi, w_lo)


def _dot1(x, w):
    return _mm(x.astype(BF16), w.astype(BF16))


def _const_spec(shape):
    nd = len(shape)
    return pl.BlockSpec(shape, lambda *_: (0,) * nd)


def _ffn_body(xa_ref, xb_ref, wg_ref, wu_ref, wd_ref, fg_ref, oa_ref, ob_ref, *, final_norm, tiles_a):
    def tile(x_ref, o_ref):
        x = x_ref[...]
        xb = x.astype(BF16)
        inv = lax.rsqrt(jnp.mean(x * x, axis=-1, keepdims=True) + RMS_EPS)
        hg = _mm(xb, wg_ref[...]) * inv
        hu = _mm(xb, wu_ref[...]) * inv
        h = (hg * jax.nn.sigmoid(hg) * hu).astype(BF16)
        y = x + 0.5 * _mm(h, wd_ref[...])
        if final_norm:
            y = _rms(y, fg_ref[...])
        o_ref[...] = y

    in_a = pl.program_id(0) < tiles_a
    pl.when(in_a)(functools.partial(tile, xa_ref, oa_ref))
    pl.when(jnp.logical_not(in_a))(functools.partial(tile, xb_ref, ob_ref))


def _resident_spec(shape):
    nd = len(shape)
    return pl.BlockSpec(shape, lambda *_: (0,) * nd, pipeline_mode=pl.Buffered(1))


def _ffn(xa, xb, w_gate, w_up, w_down, final_g, *, final_norm, tm=512):
    (na, d), nb = xa.shape, xb.shape[0]
    tm = min(tm, na, nb)
    assert na % tm == 0 and nb % tm == 0
    ta, tb = na // tm, nb // tm
    spec_a = pl.BlockSpec((tm, d), lambda i: (jnp.minimum(i, ta - 1), 0))
    spec_b = pl.BlockSpec((tm, d), lambda i: (jnp.maximum(i - ta, 0), 0))
    return pl.pallas_call(
        functools.partial(_ffn_body, final_norm=final_norm, tiles_a=ta),
        grid=(ta + tb,),
        in_specs=[
            spec_a, spec_b,
            _resident_spec(w_gate.shape), _resident_spec(w_up.shape), _resident_spec(w_down.shape),
            _const_spec((1, d)),
        ],
        out_specs=[spec_a, spec_b],
        out_shape=[jax.ShapeDtypeStruct((na, d), F32), jax.ShapeDtypeStruct((nb, d), F32)],
        compiler_params=_params("arbitrary"),
        name="ffn_final" if final_norm else "ffn",
    )(xa, xb, w_gate, w_up, w_down, final_g)


def _memkv_body(m_ref, wt_ref, kt_ref, vt_ref):
    dk = kt_ref.shape[1]
    kvt = _nt(wt_ref[...], m_ref[0].astype(BF16))
    kt_ref[0] = kvt[:dk]
    vt_ref[0] = kvt[dk:]


def _memkv(mem, wt):
    b, n_mem, d = mem.shape
    dk = wt.shape[0] // 2
    return pl.pallas_call(
        _memkv_body,
        grid=(b,),
        in_specs=[pl.BlockSpec((1, n_mem, d), lambda i: (i, 0, 0)), _const_spec(wt.shape)],
        out_specs=[pl.BlockSpec((1, dk, n_mem), lambda i: (i, 0, 0))] * 2,
        out_shape=[jax.ShapeDtypeStruct((b, dk, n_mem), F32)] * 2,
        compiler_params=_params("parallel"),
        name="memkv",
    )(mem, wt)


def _rwkv_body(x_ref, mixg_ref, ws_ref, sprev_ref, mu_ref, w0_ref, wdu_ref, a0_ref, wau_ref, wgu_ref,
               kk_ref, ka_ref, rk_ref, gng_ref, gnb_ref, s0_ref, hsum_ref, tri_ref, blk_ref,
               wc_ref, glub_ref, cprev_ref, cw_ref, cb_ref, lng_ref, lnb_ref,
               og_ref, sout_ref, wkv_ref, yc_ref, cnew_ref,
               carry, st, at_s, rt_s, bb_s, kb_s, bh_s, kh_s, v_s, wt_s, bonus_s, g_s, o_s,
               gm_s, hm_s, rp_s, oi_s, ubuf, ushift, yc_s, *, chunk, unroll1):
    bb_n, tt, d = x_ref.shape
    m = bb_n * tt
    d_shift = ws_ref.shape[1]
    d_r = w0_ref.shape[1]
    dc = cw_ref.shape[1]
    n_pairs = d_r // PAIR
    n_ck = tt // chunk
    n_blk = m // chunk
    log_chunk = int(math.log2(chunk))
    c2 = 2 * chunk
    conv_lo = CONV_PAD - (CONV_WIDTH - 1)
    t = pl.program_id(1)

    @pl.when(t == 0)
    def _():
        carry[...] = sprev_ref[...]
        ubuf[:, conv_lo:CONV_PAD, :] = cprev_ref[...]
        zero = jnp.zeros((HEAD_DIM, HEAD_DIM), F32)
        for b_i in range(bb_n):
            for p in range(n_pairs):
                top = jnp.concatenate([s0_ref[b_i, 2 * p], zero], axis=1)
                bot = jnp.concatenate([zero, s0_ref[b_i, 2 * p + 1]], axis=1)
                st[b_i, p] = jnp.concatenate([top, bot], axis=0)

    n_part = 2 if (bb_n == 1 and m >= 4 * chunk) else 1
    pm_ = m // n_part
    parts = range(n_part)
    psl = [slice(i * pm_, (i + 1) * pm_) for i in parts]
    x2d = x_ref[...].reshape(m, d)
    xn = [_rms(x2d[psl[i]], mixg_ref[...]).astype(BF16) for i in parts]
    zc = [_mm(xn[i], wc_ref[...]) + glub_ref[...] for i in parts]
    zs = [_mm(xn[i], ws_ref[...]) for i in parts]
    u = [zc[i][:, :dc] * jax.nn.sigmoid(zc[i][:, dc:]) for i in parts]
    if n_part == 1:
        ubuf[:, CONV_PAD:CONV_PAD + tt, :] = u[0].reshape(bb_n, tt, dc)
    else:
        for i in parts:
            ubuf[0, CONV_PAD + i * pm_:CONV_PAD + (i + 1) * pm_, :] = u[i]
    for sh in range(1, SUBLANES):
        ushift[sh - 1] = ubuf[:, sh:sh + tt + CONV_PAD - SUBLANES, :]

    if n_part == 1:
        prev0 = jnp.broadcast_to(carry[...], (bb_n, tt, d_shift)).reshape(m, d_shift)
        row = lax.broadcasted_iota(jnp.int32, (m, d_shift), 0)
        prev = [jnp.where((row & (tt - 1)) == 0, prev0, pltpu.roll(zs[0], 1, axis=0))]
        last = zs[0].reshape(bb_n, tt, d_shift)[:, tt - 1:tt, :]
    else:
        row = lax.broadcasted_iota(jnp.int32, (pm_, d_shift), 0)
        first = [carry[0]] + [zs[i][pm_ - 1:pm_, :] for i in parts[:-1]]
        prev = [jnp.where(row == 0, first[i], pltpu.roll(zs[i], 1, axis=0)) for i in parts]
        last = zs[-1][pm_ - 1:pm_, :].reshape(1, 1, d_shift)
    carry[...] = last
    sout_ref[...] = last
    xm = [zs[i] + (prev[i] - zs[i]) * mu_ref[...] for i in parts]

    c1, c2_, c3 = d_r, 2 * d_r, 3 * d_r
    c4 = c3 + LORA_DECAY
    c5 = c4 + LORA_A
    r = [xm[i][:, :c1] for i in parts]
    k = [xm[i][:, c1:c2_] for i in parts]
    v = [xm[i][:, c2_:c3] for i in parts]
    dec_in = [w0_ref[...] + _dot3(jnp.tanh(xm[i][:, c3:c4]), wdu_ref[...]) for i in parts]
    a_in = [a0_ref[...] + _dot1(xm[i][:, c4:c5], wau_ref[...]) for i in parts]
    for i in parts:
        g_s[psl[i], :] = _dot1(jax.nn.sigmoid(xm[i][:, c5:]), wgu_ref[...])
    ld = [-jnp.exp(-(jnp.maximum(-dec_in[i], 0.0) + jnp.log(1.0 + jnp.exp(-jnp.abs(dec_in[i])))) - 0.5)
          for i in parts]
    a = [jax.nn.sigmoid(a_in[i]) for i in parts]
    kh = [k[i] * (1.0 + (a[i] - 1.0) * ka_ref[...]) for i in parts]

    head_sum = hsum_ref[...]
    kkraw = [k[i] * kk_ref[...] for i in parts]
    kk_ss = [_mm((kkraw[i] * kkraw[i]).astype(BF16), head_sum) for i in parts]
    rk_sum = [_mm((r[i] * kh[i] * rk_ref[...]).astype(BF16), head_sum) for i in parts]
    kk = [kkraw[i] * lax.rsqrt(jnp.maximum(kk_ss[i], 1e-24)) for i in parts]
    b = [kk[i] * a[i] for i in parts]
    tdt = at_s.dtype
    for i in parts:
        bonus_s[psl[i], :] = rk_sum[i] * v[i]
        v_s[psl[i], :] = v[i].astype(tdt)

    slab = tri_ref.shape[0]
    for s0 in range(0, pm_, slab):
        sl = slice(s0, s0 + slab)
        ld_c = [ld[i][sl] for i in parts]
        cum = [_exact_lhs_dot(tri_ref[...], ld_c[i]) for i in parts]
        tot = [_exact_lhs_dot(blk_ref[...], ld_c[i]) for i in parts]
        for i in parts:
            gl = slice(i * pm_ + s0, i * pm_ + s0 + slab)
            w_inv = jnp.exp(-cum[i])
            w_rem = jnp.exp(tot[i] - cum[i])
            at_s[gl, :] = (-kk[i][sl] * jnp.exp(cum[i] - ld_c[i])).astype(tdt)
            rt_s[gl, :] = (r[i][sl] * jnp.exp(cum[i])).astype(tdt)
            bb_s[gl, :] = (b[i][sl] * w_inv).astype(tdt)
            kb_s[gl, :] = (kh[i][sl] * w_inv).astype(tdt)
            bh_s[gl, :] = (b[i][sl] * w_rem).astype(tdt)
            kh_s[gl, :] = (kh[i][sl] * w_rem).astype(tdt)
            wt_s[gl, :] = jnp.exp(tot[i])

    lane_lo = lax.broadcasted_iota(jnp.int32, (chunk, PAIR), 1) < HEAD_DIM
    r2 = lax.broadcasted_iota(jnp.int32, (c2, c2), 0)
    q2 = lax.broadcasted_iota(jnp.int32, (c2, c2), 1)
    strict = (q2 & (chunk - 1)) < (r2 & (chunk - 1))
    incl = (q2 & (chunk - 1)) <= (r2 & (chunk - 1))
    eye2 = (r2 == q2).astype(F32)
    wide = c2 % PAIR == 0

    def stack(ref, rows, ls):
        x = ref[rows, ls]
        return jnp.concatenate([jnp.where(lane_lo, x, 0.0), jnp.where(lane_lo, 0.0, x)], axis=0).astype(BF16)

    def chain_mats(chains):
        n = range(len(chains))

        def stacks(ref):
            return [stack(ref, slice(blk * chunk, (blk + 1) * chunk), slice(p * PAIR, (p + 1) * PAIR))
                    for blk, p in chains]

        at, rt, bbm, kbm = stacks(at_s), stacks(rt_s), stacks(bb_s), stacks(kb_s)
        bhm, khm, vm = stacks(bh_s), stacks(kh_s), stacks(v_s)
        if wide:
            a4 = [_nt(jnp.concatenate([at[c_], rt[c_]], axis=0), jnp.concatenate([bbm[c_], kbm[c_]], axis=0))
                  for c_ in n]
            a_ab = [jnp.where(strict, a4[c_][:c2, :c2], 0.0) for c_ in n]
            a_ak = [jnp.where(strict, a4[c_][:c2, c2:], 0.0).astype(BF16) for c_ in n]
            a_rb = [jnp.where(incl, a4[c_][c2:, :c2], 0.0).astype(BF16) for c_ in n]
            a_rk = [jnp.where(incl, a4[c_][c2:, c2:], 0.0).astype(BF16) for c_ in n]
        else:
            a_ab = [jnp.where(strict, _nt(at[c_], bbm[c_]), 0.0) for c_ in n]
            a_ak = [jnp.where(strict, _nt(at[c_], kbm[c_]), 0.0).astype(BF16) for c_ in n]
            a_rb = [jnp.where(incl, _nt(rt[c_], bbm[c_]), 0.0).astype(BF16) for c_ in n]
            a_rk = [jnp.where(incl, _nt(rt[c_], kbm[c_]), 0.0).astype(BF16) for c_ in n]
        tinv = [eye2 + a_ab[c_] for c_ in n]
        ap = [a_ab[c_].astype(BF16) for c_ in n]
        ap = [_mm(ap[c_], ap[c_]).astype(BF16) for c_ in n]
        for i_sq in range(1, log_chunk):
            if i_sq == log_chunk - 1:
                tinv = [tinv[c_] + _mm(ap[c_], tinv[c_].astype(BF16)) for c_ in n]
            elif wide:
                x = [_mm(ap[c_], jnp.concatenate([ap[c_], tinv[c_].astype(BF16)], axis=1)) for c_ in n]
                ap = [x[c_][:, :c2].astype(BF16) for c_ in n]
                tinv = [tinv[c_] + x[c_][:, c2:] for c_ in n]
            else:
                tinv = [tinv[c_] + _mm(ap[c_], tinv[c_].astype(BF16)) for c_ in n]
                ap = [_mm(ap[c_], ap[c_]).astype(BF16) for c_ in n]
        av = [_mm(a_ak[c_], vm[c_]).astype(BF16) for c_ in n]
        pq = [_mm(tinv[c_].astype(BF16), jnp.concatenate([av[c_], at[c_]], axis=1)) for c_ in n]
        return rt, bhm, khm, vm, a_rb, a_rk, pq

    def phase1_step(i):
        chains = [(unroll1 * i + j, p) for j in range(unroll1) for p in range(n_pairs)]
        n = range(len(chains))
        rt, bhm, khm, vm, a_rb, a_rk, pq = chain_mats(chains)
        pm = [pq[c_][:, :PAIR].astype(BF16) for c_ in n]
        qm = [pq[c_][:, PAIR:].astype(BF16) for c_ in n]
        for c_, (blk, p) in enumerate(chains):
            gm_s[blk, p] = _tn(qm[c_], bhm[c_]).astype(BF16)
        for c_, (blk, p) in enumerate(chains):
            hm_s[blk, p] = _tn(jnp.concatenate([pm[c_], vm[c_]], axis=0),
                               jnp.concatenate([bhm[c_], khm[c_]], axis=0))
        rq = [_mm(a_rb[c_], jnp.concatenate([qm[c_], pm[c_]], axis=1)) for c_ in n]
        for c_, (blk, p) in enumerate(chains):
            rp_s[blk, p] = (rt[c_].astype(F32) + rq[c_][:, :PAIR]).astype(BF16)
        for c_, (blk, p) in enumerate(chains):
            oi_s[blk, p] = rq[c_][:, PAIR:] + _mm(a_rk[c_], vm[c_])

    def conv_block(blk):
        b_i, t0 = blk // n_ck, (blk % n_ck) * chunk
        acc = jnp.zeros((chunk, dc), F32) + cb_ref[...]
        for kx in range(CONV_WIDTH):
            off = conv_lo + kx
            sh, base = off % SUBLANES, off - off % SUBLANES
            rows = slice(t0 + base, t0 + base + chunk)
            win = ubuf[b_i, rows, :] if sh == 0 else ushift[sh - 1, b_i, rows, :]
            acc = acc + win * cw_ref[kx:kx + 1, :]
        cm = jnp.mean(acc, axis=-1, keepdims=True)
        cc = acc - cm
        cv = jnp.mean(cc * cc, axis=-1, keepdims=True)
        cn = cc * lax.rsqrt(cv + LN_EPS) * lng_ref[...] + lnb_ref[...]
        yc_s[blk * chunk:(blk + 1) * chunk, :] = cn * jax.nn.sigmoid(cn)

    def phase2_step(i):
        conv_block(i)
        chains = [(i, p) for p in range(n_pairs)]
        n = range(len(chains))
        row0 = [blk * chunk for blk, _ in chains]
        ls = [slice(p * PAIR, (p + 1) * PAIR) for _, p in chains]
        s = [st[blk // n_ck, p] for blk, p in chains]
        sb = [s[c_].astype(BF16) for c_ in n]
        s_new = [s[c_] * wt_s[row0[c_]:row0[c_] + 1, ls[c_]] + _mm(sb[c_], gm_s[blk, p]) + hm_s[blk, p]
                 for c_, (blk, p) in enumerate(chains)]
        o_bd = [oi_s[blk, p] + _nt(rp_s[blk, p], sb[c_]) for c_, (blk, p) in enumerate(chains)]
        for c_, (blk, p) in enumerate(chains):
            st[blk // n_ck, p] = s_new[c_]
        for c_ in n:
            o_s[row0[c_]:row0[c_] + chunk, ls[c_]] = o_bd[c_][:chunk] + o_bd[c_][chunk:]

    def single_chunk_step(i):
        for j in range(unroll1):
            conv_block(unroll1 * i + j)
        chains = [(unroll1 * i + j, p) for j in range(unroll1) for p in range(n_pairs)]
        n = range(len(chains))
        row0 = [blk * chunk for blk, _ in chains]
        ls = [slice(p * PAIR, (p + 1) * PAIR) for _, p in chains]
        rt, bhm, khm, vm, a_rb, a_rk, pq = chain_mats(chains)
        s = [st[blk, p] for blk, p in chains]
        sb = [s[c_].astype(BF16) for c_ in n]
        qr = [_nt(jnp.concatenate([pq[c_][:, PAIR:].astype(BF16), rt[c_]], axis=0), sb[c_]) for c_ in n]
        u = [(pq[c_][:, :PAIR] + qr[c_][:c2]).astype(BF16) for c_ in n]
        o_bd = [qr[c_][c2:] + _mm(a_rb[c_], u[c_]) + _mm(a_rk[c_], vm[c_]) for c_ in n]
        s_new = [s[c_] * wt_s[row0[c_]:row0[c_] + 1, ls[c_]]
                 + _tn(jnp.concatenate([u[c_], vm[c_]], axis=0), jnp.concatenate([bhm[c_], khm[c_]], axis=0))
                 for c_ in n]
        for c_, (blk, p) in enumerate(chains):
            st[blk, p] = s_new[c_]
        for c_ in n:
            o_s[row0[c_]:row0[c_] + chunk, ls[c_]] = o_bd[c_][:chunk] + o_bd[c_][chunk:]

    if n_ck == 1:
        for i in range(n_blk // unroll1):
            single_chunk_step(i)
    else:
        for i in range(n_blk // unroll1):
            phase1_step(i)
        for i in range(n_blk):
            phase2_step(i)

    o = o_s[...]
    mean = _mm(o.astype(BF16), head_sum) * (1.0 / HEAD_DIM)
    oc = o - mean
    var = _mm((oc * oc).astype(BF16), head_sum) * (1.0 / HEAD_DIM)
    o = oc * lax.rsqrt(var + GN_EPS) * gng_ref[...] + gnb_ref[...] + bonus_s[...]
    og_ref[...] = (o * g_s[...]).astype(BF16).reshape(bb_n, tt, d_r)

    yc_ref[...] = yc_s[...].astype(BF16).reshape(bb_n, tt, dc)
    tail = ubuf[:, conv_lo + tt:CONV_PAD + tt, :]
    cnew_ref[...] = tail
    ubuf[:, conv_lo:CONV_PAD, :] = tail

    @pl.when(t == pl.num_programs(1) - 1)
    def _():
        for b_i in range(bb_n):
            for p in range(n_pairs):
                s = st[b_i, p]
                wkv_ref[b_i, 2 * p] = s[:HEAD_DIM, :HEAD_DIM]
                wkv_ref[b_i, 2 * p + 1] = s[HEAD_DIM:, HEAD_DIM:]


def _rwkv(x, mixg, ws, sprev, mu, w0, wdu, a0, wau, wgu, kk, ka, rk, gng, gnb, s0,
          wc, glub, cprev, cw, cb, lng, lnb, *, bb, tt, chunk, unroll1):
    b, t_len, d = x.shape
    d_shift = ws.shape[1]
    d_r = w0.shape[1]
    dc = cw.shape[1]
    n_heads = d_r // HEAD_DIM
    n_pairs = d_r // PAIR
    m = bb * tt
    n_blk = m // chunk
    n_kept = n_blk if tt > chunk else 1
    assert b % bb == 0 and t_len % tt == 0 and tt % chunk == 0
    assert n_blk % unroll1 == 0
    assert chunk & (chunk - 1) == 0 and tt & (tt - 1) == 0 and chunk % 8 == 0
    tok = pltpu.VMEM((m, d_r), F32)
    tok_mm = pltpu.VMEM((m, d_r), BF16 if chunk % (2 * SUBLANES) == 0 else F32)
    lane = jnp.arange(d_r)
    head_sum = (lane[:, None] // HEAD_DIM == lane[None, :] // HEAD_DIM).astype(BF16)
    rows = jnp.arange(chunk if chunk >= HEAD_DIM else m)
    same_chunk = rows[:, None] // chunk == rows[None, :] // chunk
    tri = (same_chunk & (rows[None, :] <= rows[:, None])).astype(BF16)
    blk = same_chunk.astype(BF16)
    return pl.pallas_call(
        functools.partial(_rwkv_body, chunk=chunk, unroll1=unroll1),
        grid=(b // bb, t_len // tt),
        in_specs=[
            pl.BlockSpec((bb, tt, d), lambda i, j: (i, j, 0)),
            _const_spec(mixg.shape), _const_spec(ws.shape),
            pl.BlockSpec((bb, 1, d_shift), lambda i, j: (i, 0, 0)),
            _const_spec(mu.shape), _const_spec(w0.shape), _const_spec(wdu.shape), _const_spec(a0.shape),
            _const_spec(wau.shape), _const_spec(wgu.shape), _const_spec(kk.shape), _const_spec(ka.shape),
            _const_spec(rk.shape), _const_spec(gng.shape), _const_spec(gnb.shape),
            pl.BlockSpec((bb, n_heads, HEAD_DIM, HEAD_DIM), lambda i, j: (i, 0, 0, 0)),
            _const_spec(head_sum.shape), _const_spec(tri.shape), _const_spec(blk.shape),
            _const_spec(wc.shape), _const_spec(glub.shape),
            pl.BlockSpec((bb, CONV_WIDTH - 1, dc), lambda i, j: (i, 0, 0)),
            _const_spec(cw.shape), _const_spec(cb.shape), _const_spec(lng.shape), _const_spec(lnb.shape),
        ],
        out_specs=[
            pl.BlockSpec((bb, tt, d_r), lambda i, j: (i, j, 0)),
            pl.BlockSpec((bb, 1, d_shift), lambda i, j: (i, 0, 0)),
            pl.BlockSpec((bb, n_heads, HEAD_DIM, HEAD_DIM), lambda i, j: (i, 0, 0, 0)),
            pl.BlockSpec((bb, tt, dc), lambda i, j: (i, j, 0)),
            pl.BlockSpec((bb, CONV_WIDTH - 1, dc), lambda i, j: (i, 0, 0)),
        ],
        out_shape=[
            jax.ShapeDtypeStruct((b, t_len, d_r), BF16),
            jax.ShapeDtypeStruct((b, 1, d_shift), F32),
            jax.ShapeDtypeStruct((b, n_heads, HEAD_DIM, HEAD_DIM), F32),
            jax.ShapeDtypeStruct((b, t_len, dc), BF16),
            jax.ShapeDtypeStruct((b, CONV_WIDTH - 1, dc), F32),
        ],
        scratch_shapes=[
            pltpu.VMEM((bb, 1, d_shift), F32),
            pltpu.VMEM((bb, n_pairs, PAIR, PAIR), F32),
        ] + [tok_mm] * 7 + [tok] * 4 + [
            pltpu.VMEM((n_kept, n_pairs, PAIR, PAIR), BF16),
            pltpu.VMEM((n_kept, n_pairs, PAIR, PAIR), F32),
            pltpu.VMEM((n_kept, n_pairs, 2 * chunk, PAIR), BF16),
            pltpu.VMEM((n_kept, n_pairs, 2 * chunk, PAIR), F32),
            pltpu.VMEM((bb, CONV_PAD + tt, dc), F32),
            pltpu.VMEM((SUBLANES - 1, bb, CONV_PAD + tt - SUBLANES, dc), F32),
            pltpu.VMEM((m, dc), F32),
        ],
        compiler_params=_params("parallel", "arbitrary"),
        name="rwkv_conv",
    )(x, mixg, ws, sprev, mu, w0, wdu, a0, wau, wgu, kk, ka, rk, gng, gnb, s0, head_sum, tri, blk,
      wc, glub, cprev, cw, cb, lng, lnb)


def _merge_body(x_ref, wq_ref, wg_ref, kt_ref, vt_ref, ya_ref, yb_ref, wro_ref, wco_ref, wxo_ref, wo_ref, o_ref,
                *, t_len, rows, group):
    tm, d = x_ref.shape
    dh = kt_ref.shape[1] // N_XATTN_HEADS
    x = x_ref[...]
    xb = x.astype(BF16)
    inv = lax.rsqrt(jnp.mean(x * x, axis=-1, keepdims=True) + RMS_EPS)

    q = (_mm(xb, wq_ref[...]) * (inv * dh ** -0.5)).astype(BF16)
    n_blocks = tm // rows
    hs = lambda h: slice(h * dh, (h + 1) * dh)
    seq = lambda r: (r * rows) // t_len
    pieces = []
    for g0 in range(0, n_blocks, group):
        blocks = range(g0, min(g0 + group, n_blocks))
        chains = [(r, h) for r in blocks for h in range(N_XATTN_HEADS)]
        s = [_mm(q[r * rows:(r + 1) * rows, hs(h)], kt_ref[seq(r), hs(h), :].astype(BF16)) for r, h in chains]
        p = [jnp.exp(v - jnp.max(v, axis=-1, keepdims=True)) for v in s]
        l = [jnp.sum(v, axis=-1, keepdims=True) for v in p]
        o = [_nt(p[i].astype(BF16), vt_ref[seq(r), hs(h), :].astype(BF16)) / l[i] for i, (r, h) in enumerate(chains)]
        for j in range(len(blocks)):
            pieces.append(jnp.concatenate(o[j * N_XATTN_HEADS:(j + 1) * N_XATTN_HEADS], axis=-1))
    oc = jnp.concatenate(pieces, axis=0).astype(BF16)

    merged = jnp.zeros_like(x)
    for i, (y, w_ref) in enumerate(((ya_ref[...], wro_ref), (yb_ref[...], wco_ref), (oc, wxo_ref))):
        gate = jax.nn.sigmoid(_mm(xb, wg_ref[:, i * d:(i + 1) * d]) * inv)
        merged = merged + gate * _mm(y, w_ref[...])
    o_ref[...] = x + _mm(merged.astype(BF16), wo_ref[...])


def _merge(x, wq, wg, kt, vt, ya, yb, wro, wco, wxo, wo, *, t_len, tm, rows, group):
    n, d = x.shape
    dx, n_mem = kt.shape[1:]
    seqs = max(tm // t_len, 1)
    assert n % tm == 0 and tm % rows == 0 and (tm % t_len == 0 or t_len % tm == 0) and rows <= t_len
    row = lambda w: pl.BlockSpec((tm, w), lambda i: (i, 0))
    mem = pl.BlockSpec((seqs, dx, n_mem), lambda i: ((i * tm) // (t_len * seqs), 0, 0))
    return pl.pallas_call(
        functools.partial(_merge_body, t_len=t_len, rows=rows, group=group),
        grid=(n // tm,),
        in_specs=[row(d), _resident_spec(wq.shape), _resident_spec(wg.shape), mem, mem, row(ya.shape[1]),
                  row(yb.shape[1]), _resident_spec(wro.shape), _resident_spec(wco.shape),
                  _resident_spec(wxo.shape), _resident_spec(wo.shape)],
        out_specs=row(d),
        out_shape=jax.ShapeDtypeStruct((n, d), F32),
        compiler_params=_params("parallel"),
        name="merge",
    )(x, wq, wg, kt, vt, ya, yb, wro, wco, wxo, wo)


def _mix(x1, shift_prev, conv_prev, wkv0, mem_kt, mem_vt, p, *, bb, tt, rw, mg):
    b, t_len, d = x1.shape
    n = b * t_len
    og, shift, wkv, cb, conv = _rwkv(
        x1, p['mix_norm'], p['w_s'], shift_prev[:, None, :], p['mu_shift'], p['w0'], p['w_decay_up'], p['a0'],
        p['w_a_up'], p['w_g_up'], p['k_k'], p['k_a'], p['r_k'], p['gn_g'], p['gn_b'], wkv0,
        p['w_c'], p['glu_b'], conv_prev, p['conv_w'], p['conv_b'], p['conv_ln_g'], p['conv_ln_b'],
        bb=bb, tt=tt, **rw)
    x2 = _merge(x1.reshape(n, d), p['w_q'], p['w_g'], mem_kt, mem_vt, og.reshape(n, -1), cb.reshape(n, -1),
                p['w_rwkv_out'], p['w_conv_out'], p['w_xattn_out'], p['w_o'], t_len=t_len, **mg)
    return x2, wkv, shift[:, 0, :], conv


def kernel(x_prompt, mem_prompt, x_sample, state_wkv, state_shift, state_conv, cache_mem_k, cache_mem_v,
           ffn1_norm, ffn1_w_up, ffn1_w_down, mix_norm, w_in, mu_shift, w0, w_decay_up, a0, w_a_up, w_g_up,
           k_k, k_a, r_k, gn_g, gn_b, w_rwkv_out, glu_b, conv_w, conv_b, conv_ln_g, conv_ln_b, w_conv_out,
           w_mem_kv, w_xattn_out, w_o, ffn2_norm, ffn2_w_up, ffn2_w_down, final_norm):
    depth = w_in.shape[0]
    d_model = w_in.shape[1]
    d_r = w0.shape[1]
    d_shift = mu_shift.shape[1]
    d_conv = conv_w.shape[2]
    d_x = w_xattn_out.shape[1]
    n_heads = d_r // HEAD_DIM
    o1 = d_shift
    o2 = o1 + 2 * d_conv
    o3 = o2 + d_x
    d_ff = ffn1_w_down.shape[1]
    row = lambda a: a.astype(F32).reshape(1, -1)
    scaled = lambda g, w: (g.astype(F32)[:, None] * w.astype(F32)).astype(BF16)
    final_g = row(final_norm)

    layers = []
    for l in range(depth):
        layers.append({
            'ffn1_w_gate': scaled(ffn1_norm[l], ffn1_w_up[l][:, :d_ff]),
            'ffn1_w_up': scaled(ffn1_norm[l], ffn1_w_up[l][:, d_ff:]),
            'ffn1_w_down': ffn1_w_down[l].astype(BF16), 'mix_norm': row(mix_norm[l]),
            'w_s': w_in[l, :, :o1].astype(BF16), 'w_c': w_in[l, :, o1:o2].astype(BF16),
            'w_q': scaled(mix_norm[l], w_in[l, :, o2:o3]), 'w_g': scaled(mix_norm[l], w_in[l, :, o3:]),
            'mu_shift': row(mu_shift[l]), 'w0': row(w0[l]), 'w_decay_up': w_decay_up[l].astype(F32),
            'a0': row(a0[l]), 'w_a_up': w_a_up[l].astype(F32), 'w_g_up': w_g_up[l].astype(F32),
            'k_k': row(k_k[l]), 'k_a': row(k_a[l]), 'r_k': row(r_k[l]), 'gn_g': row(gn_g[l]),
            'gn_b': row(gn_b[l]), 'w_rwkv_out': w_rwkv_out[l].astype(BF16), 'glu_b': row(glu_b[l]),
            'conv_w': conv_w[l].astype(F32), 'conv_b': row(conv_b[l]), 'conv_ln_g': row(conv_ln_g[l]),
            'conv_ln_b': row(conv_ln_b[l]), 'w_conv_out': w_conv_out[l].astype(BF16),
            'w_mem_kv_t': w_mem_kv[l].T.astype(BF16), 'w_xattn_out': w_xattn_out[l].astype(BF16),
            'w_o': w_o[l].astype(BF16),
            'ffn2_w_gate': scaled(ffn2_norm[l], ffn2_w_up[l][:, :d_ff]),
            'ffn2_w_up': scaled(ffn2_norm[l], ffn2_w_up[l][:, d_ff:]), 'ffn2_w_down': ffn2_w_down[l].astype(BF16),
        })

    bp, tp, _ = x_prompt.shape
    bs, ts, _ = x_sample.shape
    n_mem = mem_prompt.shape[1]
    xp = x_prompt.astype(F32).reshape(bp * tp, d_model)
    xs = x_sample.astype(F32).reshape(bs * ts, d_model)
    wkv_p, shift_p, conv_p, mk_p, mv_p = [], [], [], [], []
    wkv_s, shift_s, conv_s = [], [], []
    for l in range(depth):
        p = layers[l]
        x1p, x1s = _ffn(xp, xs, p['ffn1_w_gate'], p['ffn1_w_up'], p['ffn1_w_down'], final_g, final_norm=False)

        mkt, mvt = _memkv(mem_prompt.astype(F32), p['w_mem_kv_t'])
        x2p, wkv, sh, cv = _mix(
            x1p.reshape(bp, tp, d_model), jnp.zeros((bp, d_shift), F32),
            jnp.zeros((bp, CONV_WIDTH - 1, d_conv), F32), jnp.zeros((bp, n_heads, HEAD_DIM, HEAD_DIM), F32),
            mkt, mvt, p, bb=1, tt=512, rw=dict(chunk=64, unroll1=8), mg=dict(tm=512, rows=512, group=1))
        wkv_p.append(wkv)
        shift_p.append(sh)
        conv_p.append(cv)
        heads_t = lambda a: jnp.transpose(a.reshape(bp, N_XATTN_HEADS, d_x // N_XATTN_HEADS, n_mem), (0, 3, 1, 2))
        mk_p.append(heads_t(mkt))
        mv_p.append(heads_t(mvt))

        kt = jnp.transpose(cache_mem_k[l].astype(F32), (0, 2, 3, 1)).reshape(bs, d_x, n_mem)
        vt = jnp.transpose(cache_mem_v[l].astype(F32), (0, 2, 3, 1)).reshape(bs, d_x, n_mem)
        x2s, wkv, sh, cv = _mix(
            x1s.reshape(bs, ts, d_model), state_shift[l].astype(F32), state_conv[l].astype(F32),
            state_wkv[l].astype(F32), kt, vt, p, bb=16, tt=ts, rw=dict(chunk=ts, unroll1=16),
            mg=dict(tm=16 * ts, rows=ts, group=16))
        wkv_s.append(wkv)
        shift_s.append(sh)
        conv_s.append(cv)

        xp, xs = _ffn(x2p, x2s, p['ffn2_w_gate'], p['ffn2_w_up'], p['ffn2_w_down'], final_g,
                      final_norm=(l == depth - 1))

    return (xp.reshape(x_prompt.shape).astype(x_prompt.dtype), xs.reshape(x_sample.shape).astype(x_sample.dtype),
            jnp.stack(wkv_p), jnp.stack(shift_p), jnp.stack(conv_p), jnp.stack(mk_p), jnp.stack(mv_p),
            jnp.stack(wkv_s), jnp.stack(shift_s), jnp.stack(conv_s))
```

```python
import functools
import math

import jax
import jax.numpy as jnp
from jax import lax
from jax.experimental import pallas as pl
from jax.experimental.pallas import tpu as pltpu

F32 = jnp.float32
BF16 = jnp.bfloat16

HEAD_DIM = 64
PAIR = 2 * HEAD_DIM
N_XATTN_HEADS = 4
CONV_WIDTH = 31
SUBLANES = 8
CONV_PAD = 32
LORA_DECAY = 64
LORA_A = 64
RMS_EPS = 1e-6
LN_EPS = 1e-5
GN_EPS = 64e-5
LOG2E = math.log2(math.e)
VMEM_LIMIT = 56 * 1024 * 1024


def _params(*sem):
    return pltpu.CompilerParams(dimension_semantics=sem, vmem_limit_bytes=VMEM_LIMIT)


def _rms(x, g):
    return x * lax.rsqrt(jnp.mean(x * x, axis=-1, keepdims=True) + RMS_EPS) * g


def _mm(a, b):
    return jnp.dot(a, b, preferred_element_type=F32)


def _nt(a, b):
    return lax.dot_general(a, b, (((1,), (1,)), ((), ())), preferred_element_type=F32)


def _tn(a, b):
    return lax.dot_general(a, b, (((0,), (0,)), ((), ())), preferred_element_type=F32)


def _split(x):
    hi = x.astype(BF16)
    return hi, (x - hi.astype(F32)).astype(BF16)


def _exact_lhs_dot(sel, x):
    hi, lo = _split(x)
    return _mm(sel, hi) + _mm(sel, lo)


def _dot3(x, w):
    x_hi, x_lo = _split(x)
    w_hi, w_lo = _split(w)
    return _mm(x_hi, w_hi) + _mm(x_lo, w_hi) + _mm(x_hi, w_lo)


def _dot1(x, w):
    return _mm(x.astype(BF16), w.astype(BF16))


def _const_spec(shape):
    nd = len(shape)
    return pl.BlockSpec(shape, lambda *_: (0,) * nd)


def _ffn_body(xa_ref, xb_ref, wg_ref, wu_ref, wd_ref, fg_ref, oa_ref, ob_ref, *, final_norm, tiles_a):
    def tile(x_ref, o_ref):
        x = x_ref[...]
        xb = x.astype(BF16)
        inv = lax.rsqrt(jnp.mean(x * x, axis=-1, keepdims=True) + RMS_EPS)
        hg = _mm(xb, wg_ref[...]) * inv
        hu = _mm(xb, wu_ref[...]) * inv
        h = (hg * jax.nn.sigmoid(hg) * hu).astype(BF16)
        y = x + 0.5 * _mm(h, wd_ref[...])
        if final_norm:
            y = _rms(y, fg_ref[...])
        o_ref[...] = y

    in_a = pl.program_id(0) < tiles_a
    pl.when(in_a)(functools.partial(tile, xa_ref, oa_ref))
    pl.when(jnp.logical_not(in_a))(functools.partial(tile, xb_ref, ob_ref))


def _resident_spec(shape):
    nd = len(shape)
    return pl.BlockSpec(shape, lambda *_: (0,) * nd, pipeline_mode=pl.Buffered(1))


def _ffn(xa, xb, w_gate, w_up, w_down, final_g, *, final_norm, tm=512):
    (na, d), nb = xa.shape, xb.shape[0]
    tm = min(tm, na, nb)
    assert na % tm == 0 and nb % tm == 0
    ta, tb = na // tm, nb // tm
    spec_a = pl.BlockSpec((tm, d), lambda i: (jnp.minimum(i, ta - 1), 0))
    spec_b = pl.BlockSpec((tm, d), lambda i: (jnp.maximum(i - ta, 0), 0))
    return pl.pallas_call(
        functools.partial(_ffn_body, final_norm=final_norm, tiles_a=ta),
        grid=(ta + tb,),
        in_specs=[
            spec_a, spec_b,
            _resident_spec(w_gate.shape), _resident_spec(w_up.shape), _resident_spec(w_down.shape),
            _const_spec((1, d)),
        ],
        out_specs=[spec_a, spec_b],
        out_shape=[jax.ShapeDtypeStruct((na, d), F32), jax.ShapeDtypeStruct((nb, d), F32)],
        compiler_params=_params("arbitrary"),
        name="ffn_final" if final_norm else "ffn",
    )(xa, xb, w_gate, w_up, w_down, final_g)


def _memkv_body(m_ref, wt_ref, kt_ref, vt_ref):
    dk = kt_ref.shape[1]
    kvt = _nt(wt_ref[...], m_ref[0].astype(BF16))
    kt_ref[0] = kvt[:dk]
    vt_ref[0] = kvt[dk:]


def _memkv(mem, wt):
    b, n_mem, d = mem.shape
    dk = wt.shape[0] // 2
    return pl.pallas_call(
        _memkv_body,
        grid=(b,),
        in_specs=[pl.BlockSpec((1, n_mem, d), lambda i: (i, 0, 0)), _const_spec(wt.shape)],
        out_specs=[pl.BlockSpec((1, dk, n_mem), lambda i: (i, 0, 0))] * 2,
        out_shape=[jax.ShapeDtypeStruct((b, dk, n_mem), F32)] * 2,
        compiler_params=_params("parallel"),
        name="memkv",
    )(mem, wt)


def _rwkv_body(x_ref, mixg_ref, ws_ref, sprev_ref, mu_ref, w0_ref, wdu_ref, a0_ref, wau_ref, wgu_ref,
               kk_ref, ka_ref, rk_ref, gng_ref, gnb_ref, s0_ref, hsum_ref, tri_ref, blk_ref,
               wc_ref, glub_ref, cprev_ref, cw_ref, cb_ref, lng_ref, lnb_ref,
               og_ref, sout_ref, wkv_ref, yc_ref, cnew_ref,
               carry, st, at_s, rt_s, bb_s, kb_s, bh_s, kh_s, v_s, wt_s, bonus_s, g_s, o_s,
               gm_s, hm_s, rp_s, oi_s, ubuf, ushift, yc_s, *, chunk, unroll1):
    bb_n, tt, d = x_ref.shape
    m = bb_n * tt
    d_shift = ws_ref.shape[1]
    d_r = w0_ref.shape[1]
    dc = cw_ref.shape[1]
    n_pairs = d_r // PAIR
    n_ck = tt // chunk
    n_blk = m // chunk
    log_chunk = int(math.log2(chunk))
    c2 = 2 * chunk
    conv_lo = CONV_PAD - (CONV_WIDTH - 1)
    t = pl.program_id(1)

    @pl.when(t == 0)
    def _():
        carry[...] = sprev_ref[...]
        ubuf[:, conv_lo:CONV_PAD, :] = cprev_ref[...]
        zero = jnp.zeros((HEAD_DIM, HEAD_DIM), F32)
        for b_i in range(bb_n):
            for p in range(n_pairs):
                top = jnp.concatenate([s0_ref[b_i, 2 * p], zero], axis=1)
                bot = jnp.concatenate([zero, s0_ref[b_i, 2 * p + 1]], axis=1)
                st[b_i, p] = jnp.concatenate([top, bot], axis=0)

    n_part = 2 if (bb_n == 1 and m >= 4 * chunk) else 1
    pm_ = m // n_part
    parts = range(n_part)
    psl = [slice(i * pm_, (i + 1) * pm_) for i in parts]
    x2d = x_ref[...].reshape(m, d)
    xn = [_rms(x2d[psl[i]], mixg_ref[...]).astype(BF16) for i in parts]
    zc = [_mm(xn[i], wc_ref[...]) + glub_ref[...] for i in parts]
    zs = [_mm(xn[i], ws_ref[...]) for i in parts]
    u = [zc[i][:, :dc] * jax.nn.sigmoid(zc[i][:, dc:]) for i in parts]
    if n_part == 1:
        ubuf[:, CONV_PAD:CONV_PAD + tt, :] = u[0].reshape(bb_n, tt, dc)
    else:
        for i in parts:
            ubuf[0, CONV_PAD + i * pm_:CONV_PAD + (i + 1) * pm_, :] = u[i]
    for sh in range(1, SUBLANES):
        ushift[sh - 1] = ubuf[:, sh:sh + tt + CONV_PAD - SUBLANES, :]

    if n_part == 1:
        prev0 = jnp.broadcast_to(carry[...], (bb_n, tt, d_shift)).reshape(m, d_shift)
        row = lax.broadcasted_iota(jnp.int32, (m, d_shift), 0)
        prev = [jnp.where((row & (tt - 1)) == 0, prev0, pltpu.roll(zs[0], 1, axis=0))]
        last = zs[0].reshape(bb_n, tt, d_shift)[:, tt - 1:tt, :]
    else:
        row = lax.broadcasted_iota(jnp.int32, (pm_, d_shift), 0)
        first = [carry[0]] + [zs[i][pm_ - 1:pm_, :] for i in parts[:-1]]
        prev = [jnp.where(row == 0, first[i], pltpu.roll(zs[i], 1, axis=0)) for i in parts]
        last = zs[-1][pm_ - 1:pm_, :].reshape(1, 1, d_shift)
    carry[...] = last
    sout_ref[...] = last
    xm = [zs[i] + (prev[i] - zs[i]) * mu_ref[...] for i in parts]

    c1, c2_, c3 = d_r, 2 * d_r, 3 * d_r
    c4 = c3 + LORA_DECAY
    c5 = c4 + LORA_A
    r = [xm[i][:, :c1] for i in parts]
    k = [xm[i][:, c1:c2_] for i in parts]
    v = [xm[i][:, c2_:c3] for i in parts]
    dec_in = [w0_ref[...] + _dot3(jnp.tanh(xm[i][:, c3:c4]), wdu_ref[...]) for i in parts]
    a_in = [a0_ref[...] + _dot1(xm[i][:, c4:c5], wau_ref[...]) for i in parts]
    for i in parts:
        g_s[psl[i], :] = _dot1(jax.nn.sigmoid(xm[i][:, c5:]), wgu_ref[...])
    ld = [-jnp.exp(-(jnp.maximum(-dec_in[i], 0.0) + jnp.log(1.0 + jnp.exp(-jnp.abs(dec_in[i])))) - 0.5)
          for i in parts]
    a = [jax.nn.sigmoid(a_in[i]) for i in parts]
    kh = [k[i] * (1.0 + (a[i] - 1.0) * ka_ref[...]) for i in parts]

    head_sum = hsum_ref[...]
    kkraw = [k[i] * kk_ref[...] for i in parts]
    kk_ss = [_mm((kkraw[i] * kkraw[i]).astype(BF16), head_sum) for i in parts]
    rk_sum = [_mm((r[i] * kh[i] * rk_ref[...]).astype(BF16), head_sum) for i in parts]
    kk = [kkraw[i] * lax.rsqrt(jnp.maximum(kk_ss[i], 1e-24)) for i in parts]
    b = [kk[i] * a[i] for i in parts]
    tdt = at_s.dtype
    for i in parts:
        bonus_s[psl[i], :] = rk_sum[i] * v[i]
        v_s[psl[i], :] = v[i].astype(tdt)

    slab = tri_ref.shape[0]
    for s0 in range(0, pm_, slab):
        sl = slice(s0, s0 + slab)
        ld_c = [ld[i][sl] * LOG2E for i in parts]
        cum = [_exact_lhs_dot(tri_ref[...], ld_c[i]) for i in parts]
        tot = [_exact_lhs_dot(blk_ref[...], ld_c[i]) for i in parts]
        for i in parts:
            gl = slice(i * pm_ + s0, i * pm_ + s0 + slab)
            w_inv = jnp.exp2(-cum[i])
            w_rem = jnp.exp2(tot[i] - cum[i])
            at_s[gl, :] = (-kk[i][sl] * jnp.exp2(cum[i] - ld_c[i])).astype(tdt)
            rt_s[gl, :] = (r[i][sl] * jnp.exp2(cum[i])).astype(tdt)
            bb_s[gl, :] = (b[i][sl] * w_inv).astype(tdt)
            kb_s[gl, :] = (kh[i][sl] * w_inv).astype(tdt)
            bh_s[gl, :] = (b[i][sl] * w_rem).astype(tdt)
            kh_s[gl, :] = (kh[i][sl] * w_rem).astype(tdt)
            wt_s[gl, :] = jnp.exp2(tot[i])

    lane_lo = lax.broadcasted_iota(jnp.int32, (chunk, PAIR), 1) < HEAD_DIM
    r2 = lax.broadcasted_iota(jnp.int32, (c2, c2), 0)
    q2 = lax.broadcasted_iota(jnp.int32, (c2, c2), 1)
    strict = (q2 & (chunk - 1)) < (r2 & (chunk - 1))
    incl = (q2 & (chunk - 1)) <= (r2 & (chunk - 1))
    eye2 = (r2 == q2).astype(F32)
    wide = c2 % PAIR == 0

    def stack(ref, rows, ls):
        x = ref[rows, ls]
        return jnp.concatenate([jnp.where(lane_lo, x, 0.0), jnp.where(lane_lo, 0.0, x)], axis=0).astype(BF16)

    def chain_mats(chains):
        n = range(len(chains))

        def stacks(ref):
            return [stack(ref, slice(blk * chunk, (blk + 1) * chunk), slice(p * PAIR, (p + 1) * PAIR))
                    for blk, p in chains]

        at, rt, bbm, kbm = stacks(at_s), stacks(rt_s), stacks(bb_s), stacks(kb_s)
        bhm, khm, vm = stacks(bh_s), stacks(kh_s), stacks(v_s)
        if wide:
            a4 = [_nt(jnp.concatenate([at[c_], rt[c_]], axis=0), jnp.concatenate([bbm[c_], kbm[c_]], axis=0))
                  for c_ in n]
            a_ab = [jnp.where(strict, a4[c_][:c2, :c2], 0.0) for c_ in n]
            a_ak = [jnp.where(strict, a4[c_][:c2, c2:], 0.0).astype(BF16) for c_ in n]
            a_rb = [jnp.where(incl, a4[c_][c2:, :c2], 0.0).astype(BF16) for c_ in n]
            a_rk = [jnp.where(incl, a4[c_][c2:, c2:], 0.0).astype(BF16) for c_ in n]
        else:
            a_ab = [jnp.where(strict, _nt(at[c_], bbm[c_]), 0.0) for c_ in n]
            a_ak = [jnp.where(strict, _nt(at[c_], kbm[c_]), 0.0).astype(BF16) for c_ in n]
            a_rb = [jnp.where(incl, _nt(rt[c_], bbm[c_]), 0.0).astype(BF16) for c_ in n]
            a_rk = [jnp.where(incl, _nt(rt[c_], kbm[c_]), 0.0).astype(BF16) for c_ in n]
        tinv = [eye2 + a_ab[c_] for c_ in n]
        ap = [a_ab[c_].astype(BF16) for c_ in n]
        ap = [_mm(ap[c_], ap[c_]).astype(BF16) for c_ in n]
        for i_sq in range(1, log_chunk):
            if i_sq == log_chunk - 1:
                tinv = [tinv[c_] + _mm(ap[c_], tinv[c_].astype(BF16)) for c_ in n]
            elif wide:
                x = [_mm(ap[c_], jnp.concatenate([ap[c_], tinv[c_].astype(BF16)], axis=1)) for c_ in n]
                ap = [x[c_][:, :c2].astype(BF16) for c_ in n]
                tinv = [tinv[c_] + x[c_][:, c2:] for c_ in n]
            else:
                tinv = [tinv[c_] + _mm(ap[c_], tinv[c_].astype(BF16)) for c_ in n]
                ap = [_mm(ap[c_], ap[c_]).astype(BF16) for c_ in n]
        av = [_mm(a_ak[c_], vm[c_]).astype(BF16) for c_ in n]
        pq = [_mm(tinv[c_].astype(BF16), jnp.concatenate([av[c_], at[c_]], axis=1)) for c_ in n]
        return rt, bhm, khm, vm, a_rb, a_rk, pq

    def phase1_step(i):
        chains = [(unroll1 * i + j, p) for j in range(unroll1) for p in range(n_pairs)]
        n = range(len(chains))
        rt, bhm, khm, vm, a_rb, a_rk, pq = chain_mats(chains)
        pm = [pq[c_][:, :PAIR].astype(BF16) for c_ in n]
        qm = [pq[c_][:, PAIR:].astype(BF16) for c_ in n]
        for c_, (blk, p) in enumerate(chains):
            gm_s[blk, p] = _tn(qm[c_], bhm[c_]).astype(BF16)
        for c_, (blk, p) in enumerate(chains):
            hm_s[blk, p] = _tn(jnp.concatenate([pm[c_], vm[c_]], axis=0),
                               jnp.concatenate([bhm[c_], khm[c_]], axis=0))
        rq = [_mm(a_rb[c_], jnp.concatenate([qm[c_], pm[c_]], axis=1)) for c_ in n]
        for c_, (blk, p) in enumerate(chains):
            rp_s[blk, p] = (rt[c_].astype(F32) + rq[c_][:, :PAIR]).astype(BF16)
        for c_, (blk, p) in enumerate(chains):
            oi_s[blk, p] = rq[c_][:, PAIR:] + _mm(a_rk[c_], vm[c_])

    def conv_block(blk):
        b_i, t0 = blk // n_ck, (blk % n_ck) * chunk
        acc = jnp.zeros((chunk, dc), F32) + cb_ref[...]
        for kx in range(CONV_WIDTH):
            off = conv_lo + kx
            sh, base = off % SUBLANES, off - off % SUBLANES
            rows = slice(t0 + base, t0 + base + chunk)
            win = ubuf[b_i, rows, :] if sh == 0 else ushift[sh - 1, b_i, rows, :]
            acc = acc + win * cw_ref[kx:kx + 1, :]
        cm = jnp.mean(acc, axis=-1, keepdims=True)
        cc = acc - cm
        cv = jnp.mean(cc * cc, axis=-1, keepdims=True)
        cn = cc * lax.rsqrt(cv + LN_EPS) * lng_ref[...] + lnb_ref[...]
        yc_s[blk * chunk:(blk + 1) * chunk, :] = cn * jax.nn.sigmoid(cn)

    def phase2_step(i):
        conv_block(i)
        chains = [(i, p) for p in range(n_pairs)]
        n = range(len(chains))
        row0 = [blk * chunk for blk, _ in chains]
        ls = [slice(p * PAIR, (p + 1) * PAIR) for _, p in chains]
        s = [st[blk // n_ck, p] for blk, p in chains]
        sb = [s[c_].astype(BF16) for c_ in n]
        s_new = [s[c_] * wt_s[row0[c_]:row0[c_] + 1, ls[c_]] + _mm(sb[c_], gm_s[blk, p]) + hm_s[blk, p]
                 for c_, (blk, p) in enumerate(chains)]
        o_bd = [oi_s[blk, p] + _nt(rp_s[blk, p], sb[c_]) for c_, (blk, p) in enumerate(chains)]
        for c_, (blk, p) in enumerate(chains):
            st[blk // n_ck, p] = s_new[c_]
        for c_ in n:
            o_s[row0[c_]:row0[c_] + chunk, ls[c_]] = o_bd[c_][:chunk] + o_bd[c_][chunk:]

    def single_chunk_step(i):
        for j in range(unroll1):
            conv_block(unroll1 * i + j)
        chains = [(unroll1 * i + j, p) for j in range(unroll1) for p in range(n_pairs)]
        n = range(len(chains))
        row0 = [blk * chunk for blk, _ in chains]
        ls = [slice(p * PAIR, (p + 1) * PAIR) for _, p in chains]
        rt, bhm, khm, vm, a_rb, a_rk, pq = chain_mats(chains)
        s = [st[blk, p] for blk, p in chains]
        sb = [s[c_].astype(BF16) for c_ in n]
        qr = [_nt(jnp.concatenate([pq[c_][:, PAIR:].astype(BF16), rt[c_]], axis=0), sb[c_]) for c_ in n]
        u = [(pq[c_][:, :PAIR] + qr[c_][:c2]).astype(BF16) for c_ in n]
        o_bd = [qr[c_][c2:] + _mm(a_rb[c_], u[c_]) + _mm(a_rk[c_], vm[c_]) for c_ in n]
        s_new = [s[c_] * wt_s[row0[c_]:row0[c_] + 1, ls[c_]]
                 + _tn(jnp.concatenate([u[c_], vm[c_]], axis=0), jnp.concatenate([bhm[c_], khm[c_]], axis=0))
                 for c_ in n]
        for c_, (blk, p) in enumerate(chains):
            st[blk, p] = s_new[c_]
        for c_ in n:
            o_s[row0[c_]:row0[c_] + chunk, ls[c_]] = o_bd[c_][:chunk] + o_bd[c_][chunk:]

    if n_ck == 1:
        for i in range(n_blk // unroll1):
            single_chunk_step(i)
    else:
        for i in range(n_blk // unroll1):
            phase1_step(i)
        for i in range(n_blk):
            phase2_step(i)

    o = o_s[...]
    mean = _mm(o.astype(BF16), head_sum) * (1.0 / HEAD_DIM)
    oc = o - mean
    var = _mm((oc * oc).astype(BF16), head_sum) * (1.0 / HEAD_DIM)
    o = oc * lax.rsqrt(var + GN_EPS) * gng_ref[...] + gnb_ref[...] + bonus_s[...]
    og_ref[...] = (o * g_s[...]).astype(BF16).reshape(bb_n, tt, d_r)

    yc_ref[...] = yc_s[...].astype(BF16).reshape(bb_n, tt, dc)
    tail = ubuf[:, conv_lo + tt:CONV_PAD + tt, :]
    cnew_ref[...] = tail
    ubuf[:, conv_lo:CONV_PAD, :] = tail

    @pl.when(t == pl.num_programs(1) - 1)
    def _():
        for b_i in range(bb_n):
            for p in range(n_pairs):
                s = st[b_i, p]
                wkv_ref[b_i, 2 * p] = s[:HEAD_DIM, :HEAD_DIM]
                wkv_ref[b_i, 2 * p + 1] = s[HEAD_DIM:, HEAD_DIM:]


def _rwkv(x, mixg, ws, sprev, mu, w0, wdu, a0, wau, wgu, kk, ka, rk, gng, gnb, s0,
          wc, glub, cprev, cw, cb, lng, lnb, *, bb, tt, chunk, unroll1):
    b, t_len, d = x.shape
    d_shift = ws.shape[1]
    d_r = w0.shape[1]
    dc = cw.shape[1]
    n_heads = d_r // HEAD_DIM
    n_pairs = d_r // PAIR
    m = bb * tt
    n_blk = m // chunk
    n_kept = n_blk if tt > chunk else 1
    assert b % bb == 0 and t_len % tt == 0 and tt % chunk == 0
    assert n_blk % unroll1 == 0
    assert chunk & (chunk - 1) == 0 and tt & (tt - 1) == 0 and chunk % 8 == 0
    tok = pltpu.VMEM((m, d_r), F32)
    tok_mm = pltpu.VMEM((m, d_r), BF16 if chunk % (2 * SUBLANES) == 0 else F32)
    lane = jnp.arange(d_r)
    head_sum = (lane[:, None] // HEAD_DIM == lane[None, :] // HEAD_DIM).astype(BF16)
    rows = jnp.arange(chunk if chunk >= HEAD_DIM else m)
    same_chunk = rows[:, None] // chunk == rows[None, :] // chunk
    tri = (same_chunk & (rows[None, :] <= rows[:, None])).astype(BF16)
    blk = same_chunk.astype(BF16)
    return pl.pallas_call(
        functools.partial(_rwkv_body, chunk=chunk, unroll1=unroll1),
        grid=(b // bb, t_len // tt),
        in_specs=[
            pl.BlockSpec((bb, tt, d), lambda i, j: (i, j, 0)),
            _const_spec(mixg.shape), _const_spec(ws.shape),
            pl.BlockSpec((bb, 1, d_shift), lambda i, j: (i, 0, 0)),
            _const_spec(mu.shape), _const_spec(w0.shape), _const_spec(wdu.shape), _const_spec(a0.shape),
            _const_spec(wau.shape), _const_spec(wgu.shape), _const_spec(kk.shape), _const_spec(ka.shape),
            _const_spec(rk.shape), _const_spec(gng.shape), _const_spec(gnb.shape),
            pl.BlockSpec((bb, n_heads, HEAD_DIM, HEAD_DIM), lambda i, j: (i, 0, 0, 0)),
            _const_spec(head_sum.shape), _const_spec(tri.shape), _const_spec(blk.shape),
            _const_spec(wc.shape), _const_spec(glub.shape),
            pl.BlockSpec((bb, CONV_WIDTH - 1, dc), lambda i, j: (i, 0, 0)),
            _const_spec(cw.shape), _const_spec(cb.shape), _const_spec(lng.shape), _const_spec(lnb.shape),
        ],
        out_specs=[
            pl.BlockSpec((bb, tt, d_r), lambda i, j: (i, j, 0)),
            pl.BlockSpec((bb, 1, d_shift), lambda i, j: (i, 0, 0)),
            pl.BlockSpec((bb, n_heads, HEAD_DIM, HEAD_DIM), lambda i, j: (i, 0, 0, 0)),
            pl.BlockSpec((bb, tt, dc), lambda i, j: (i, j, 0)),
            pl.BlockSpec((bb, CONV_WIDTH - 1, dc), lambda i, j: (i, 0, 0)),
        ],
        out_shape=[
            jax.ShapeDtypeStruct((b, t_len, d_r), BF16),
            jax.ShapeDtypeStruct((b, 1, d_shift), F32),
            jax.ShapeDtypeStruct((b, n_heads, HEAD_DIM, HEAD_DIM), F32),
            jax.ShapeDtypeStruct((b, t_len, dc), BF16),
            jax.ShapeDtypeStruct((b, CONV_WIDTH - 1, dc), F32),
        ],
        scratch_shapes=[
            pltpu.VMEM((bb, 1, d_shift), F32),
            pltpu.VMEM((bb, n_pairs, PAIR, PAIR), F32),
        ] + [tok_mm] * 7 + [tok] * 4 + [
            pltpu.VMEM((n_kept, n_pairs, PAIR, PAIR), BF16),
            pltpu.VMEM((n_kept, n_pairs, PAIR, PAIR), F32),
            pltpu.VMEM((n_kept, n_pairs, 2 * chunk, PAIR), BF16),
            pltpu.VMEM((n_kept, n_pairs, 2 * chunk, PAIR), F32),
            pltpu.VMEM((bb, CONV_PAD + tt, dc), F32),
            pltpu.VMEM((SUBLANES - 1, bb, CONV_PAD + tt - SUBLANES, dc), F32),
            pltpu.VMEM((m, dc), F32),
        ],
        compiler_params=_params("parallel", "arbitrary"),
        name="rwkv_conv",
    )(x, mixg, ws, sprev, mu, w0, wdu, a0, wau, wgu, kk, ka, rk, gng, gnb, s0, head_sum, tri, blk,
      wc, glub, cprev, cw, cb, lng, lnb)


def _merge_body(x_ref, wq_ref, wg_ref, kt_ref, vt_ref, ya_ref, yb_ref, wro_ref, wco_ref, wxo_ref, wo_ref, o_ref,
                *, t_len, rows, group):
    tm, d = x_ref.shape
    dh = kt_ref.shape[1] // N_XATTN_HEADS
    x = x_ref[...]
    xb = x.astype(BF16)
    inv = lax.rsqrt(jnp.mean(x * x, axis=-1, keepdims=True) + RMS_EPS)

    q = (_mm(xb, wq_ref[...]) * (inv * dh ** -0.5)).astype(BF16)
    n_blocks = tm // rows
    hs = lambda h: slice(h * dh, (h + 1) * dh)
    seq = lambda r: (r * rows) // t_len
    pieces = []
    for g0 in range(0, n_blocks, group):
        blocks = range(g0, min(g0 + group, n_blocks))
        chains = [(r, h) for r in blocks for h in range(N_XATTN_HEADS)]
        s = [_mm(q[r * rows:(r + 1) * rows, hs(h)], kt_ref[seq(r), hs(h), :].astype(BF16)) for r, h in chains]
        p = [jnp.exp(v - jnp.max(v, axis=-1, keepdims=True)) for v in s]
        l = [jnp.sum(v, axis=-1, keepdims=True) for v in p]
        o = [_nt(p[i].astype(BF16), vt_ref[seq(r), hs(h), :].astype(BF16)) / l[i] for i, (r, h) in enumerate(chains)]
        for j in range(len(blocks)):
            pieces.append(jnp.concatenate(o[j * N_XATTN_HEADS:(j + 1) * N_XATTN_HEADS], axis=-1))
    oc = jnp.concatenate(pieces, axis=0).astype(BF16)

    merged = jnp.zeros_like(x)
    for i, (y, w_ref) in enumerate(((ya_ref[...], wro_ref), (yb_ref[...], wco_ref), (oc, wxo_ref))):
        gate = jax.nn.sigmoid(_mm(xb, wg_ref[:, i * d:(i + 1) * d]) * inv)
        merged = merged + gate * _mm(y, w_ref[...])
    o_ref[...] = x + _mm(merged.astype(BF16), wo_ref[...])


def _merge(x, wq, wg, kt, vt, ya, yb, wro, wco, wxo, wo, *, t_len, tm, rows, group):
    n, d = x.shape
    dx, n_mem = kt.shape[1:]
    seqs = max(tm // t_len, 1)
    assert n % tm == 0 and tm % rows == 0 and (tm % t_len == 0 or t_len % tm == 0) and rows <= t_len
    row = lambda w: pl.BlockSpec((tm, w), lambda i: (i, 0))
    mem = pl.BlockSpec((seqs, dx, n_mem), lambda i: ((i * tm) // (t_len * seqs), 0, 0))
    return pl.pallas_call(
        functools.partial(_merge_body, t_len=t_len, rows=rows, group=group),
        grid=(n // tm,),
        in_specs=[row(d), _resident_spec(wq.shape), _resident_spec(wg.shape), mem, mem, row(ya.shape[1]),
                  row(yb.shape[1]), _resident_spec(wro.shape), _resident_spec(wco.shape),
                  _resident_spec(wxo.shape), _resident_spec(wo.shape)],
        out_specs=row(d),
        out_shape=jax.ShapeDtypeStruct((n, d), F32),
        compiler_params=_params("parallel"),
        name="merge",
    )(x, wq, wg, kt, vt, ya, yb, wro, wco, wxo, wo)


def _mix(x1, shift_prev, conv_prev, wkv0, mem_kt, mem_vt, p, *, bb, tt, rw, mg):
    b, t_len, d = x1.shape
    n = b * t_len
    og, shift, wkv, cb, conv = _rwkv(
        x1, p['mix_norm'], p['w_s'], shift_prev[:, None, :], p['mu_shift'], p['w0'], p['w_decay_up'], p['a0'],
        p['w_a_up'], p['w_g_up'], p['k_k'], p['k_a'], p['r_k'], p['gn_g'], p['gn_b'], wkv0,
        p['w_c'], p['glu_b'], conv_prev, p['conv_w'], p['conv_b'], p['conv_ln_g'], p['conv_ln_b'],
        bb=bb, tt=tt, **rw)
    x2 = _merge(x1.reshape(n, d), p['w_q'], p['w_g'], mem_kt, mem_vt, og.reshape(n, -1), cb.reshape(n, -1),
                p['w_rwkv_out'], p['w_conv_out'], p['w_xattn_out'], p['w_o'], t_len=t_len, **mg)
    return x2, wkv, shift[:, 0, :], conv


def kernel(x_prompt, mem_prompt, x_sample, state_wkv, state_shift, state_conv, cache_mem_k, cache_mem_v,
           ffn1_norm, ffn1_w_up, ffn1_w_down, mix_norm, w_in, mu_shift, w0, w_decay_up, a0, w_a_up, w_g_up,
           k_k, k_a, r_k, gn_g, gn_b, w_rwkv_out, glu_b, conv_w, conv_b, conv_ln_g, conv_ln_b, w_conv_out,
           w_mem_kv, w_xattn_out, w_o, ffn2_norm, ffn2_w_up, ffn2_w_down, final_norm):
    depth = w_in.shape[0]
    d_model = w_in.shape[1]
    d_r = w0.shape[1]
    d_shift = mu_shift.shape[1]
    d_conv = conv_w.shape[2]
    d_x = w_xattn_out.shape[1]
    n_heads = d_r // HEAD_DIM
    o1 = d_shift
    o2 = o1 + 2 * d_conv
    o3 = o2 + d_x
    d_ff = ffn1_w_down.shape[1]
    row = lambda a: a.astype(F32).reshape(1, -1)
    scaled = lambda g, w: (g.astype(F32)[:, None] * w.astype(F32)).astype(BF16)
    final_g = row(final_norm)

    layers = []
    for l in range(depth):
        layers.append({
            'ffn1_w_gate': scaled(ffn1_norm[l], ffn1_w_up[l][:, :d_ff]),
            'ffn1_w_up': scaled(ffn1_norm[l], ffn1_w_up[l][:, d_ff:]),
            'ffn1_w_down': ffn1_w_down[l].astype(BF16), 'mix_norm': row(mix_norm[l]),
            'w_s': w_in[l, :, :o1].astype(BF16), 'w_c': w_in[l, :, o1:o2].astype(BF16),
            'w_q': scaled(mix_norm[l], w_in[l, :, o2:o3]), 'w_g': scaled(mix_norm[l], w_in[l, :, o3:]),
            'mu_shift': row(mu_shift[l]), 'w0': row(w0[l]), 'w_decay_up': w_decay_up[l].astype(F32),
            'a0': row(a0[l]), 'w_a_up': w_a_up[l].astype(F32), 'w_g_up': w_g_up[l].astype(F32),
            'k_k': row(k_k[l]), 'k_a': row(k_a[l]), 'r_k': row(r_k[l]), 'gn_g': row(gn_g[l]),
            'gn_b': row(gn_b[l]), 'w_rwkv_out': w_rwkv_out[l].astype(BF16), 'glu_b': row(glu_b[l]),
            'conv_w': conv_w[l].astype(F32), 'conv_b': row(conv_b[l]), 'conv_ln_g': row(conv_ln_g[l]),
            'conv_ln_b': row(conv_ln_b[l]), 'w_conv_out': w_conv_out[l].astype(BF16),
            'w_mem_kv_t': w_mem_kv[l].T.astype(BF16), 'w_xattn_out': w_xattn_out[l].astype(BF16),
            'w_o': w_o[l].astype(BF16),
            'ffn2_w_gate': scaled(ffn2_norm[l], ffn2_w_up[l][:, :d_ff]),
            'ffn2_w_up': scaled(ffn2_norm[l], ffn2_w_up[l][:, d_ff:]), 'ffn2_w_down': ffn2_w_down[l].astype(BF16),
        })

    bp, tp, _ = x_prompt.shape
    bs, ts, _ = x_sample.shape
    n_mem = mem_prompt.shape[1]
    xp = x_prompt.astype(F32).reshape(bp * tp, d_model)
    xs = x_sample.astype(F32).reshape(bs * ts, d_model)
    wkv_p, shift_p, conv_p, mk_p, mv_p = [], [], [], [], []
    wkv_s, shift_s, conv_s = [], [], []
    for l in range(depth):
        p = layers[l]
        x1p, x1s = _ffn(xp, xs, p['ffn1_w_gate'], p['ffn1_w_up'], p['ffn1_w_down'], final_g, final_norm=False)

        mkt, mvt = _memkv(mem_prompt.astype(F32), p['w_mem_kv_t'])
        x2p, wkv, sh, cv = _mix(
            x1p.reshape(bp, tp, d_model), jnp.zeros((bp, d_shift), F32),
            jnp.zeros((bp, CONV_WIDTH - 1, d_conv), F32), jnp.zeros((bp, n_heads, HEAD_DIM, HEAD_DIM), F32),
            mkt, mvt, p, bb=1, tt=512, rw=dict(chunk=64, unroll1=8), mg=dict(tm=512, rows=512, group=1))
        wkv_p.append(wkv)
        shift_p.append(sh)
        conv_p.append(cv)
        heads_t = lambda a: jnp.transpose(a.reshape(bp, N_XATTN_HEADS, d_x // N_XATTN_HEADS, n_mem), (0, 3, 1, 2))
        mk_p.append(heads_t(mkt))
        mv_p.append(heads_t(mvt))

        kt = jnp.transpose(cache_mem_k[l].astype(F32), (0, 2, 3, 1)).reshape(bs, d_x, n_mem)
        vt = jnp.transpose(cache_mem_v[l].astype(F32), (0, 2, 3, 1)).reshape(bs, d_x, n_mem)
        x2s, wkv, sh, cv = _mix(
            x1s.reshape(bs, ts, d_model), state_shift[l].astype(F32), state_conv[l].astype(F32),
            state_wkv[l].astype(F32), kt, vt, p, bb=16, tt=ts, rw=dict(chunk=ts, unroll1=16),
            mg=dict(tm=16 * ts, rows=ts, group=16))
        wkv_s.append(wkv)
        shift_s.append(sh)
        conv_s.append(cv)

        xp, xs = _ffn(x2p, x2s, p['ffn2_w_gate'], p['ffn2_w_up'], p['ffn2_w_down'], final_g,
                      final_norm=(l == depth - 1))

    return (xp.reshape(x_prompt.shape).astype(x_prompt.dtype), xs.reshape(x_sample.shape).astype(x_sample.dtype),
            jnp.stack(wkv_p), jnp.stack(shift_p), jnp.stack(conv_p), jnp.stack(mk_p), jnp.stack(mv_p),
            jnp.stack(wkv_s), jnp.stack(shift_s), jnp.stack(conv_s))
```
